```python
import math
import jax, jax.numpy as jnp
from jax import lax
import numpy as np

D_MODEL = 2048
BATCH = 8
SEQ = 8192
DEPTH = 1

HEAD_DIM = 64
N_Q_HEADS = 16
N_KV_HEADS = 2
GROUP = N_Q_HEADS // N_KV_HEADS
WINDOW = 128
BLOCK = 128
ROT_DIM = HEAD_DIM // 4
ROPE_THETA = 500000.0
Q_W = N_Q_HEADS * HEAD_DIM
KV_W = N_KV_HEADS * HEAD_DIM

SSM_W = D_MODEL // 2
SSM_GC = 16
SSM_G = SSM_W // SSM_GC
SSM_P = 64

D_FF = 4 * D_MODEL

MIX_W = Q_W + SSM_W
IN_W = Q_W + 2 * KV_W + SSM_W + 2 * D_MODEL
EPS = 1e-6

kernel_name = "hybrid_swa_sink_s5_gated_block"


def rms_norm(x, g):
    xf = x.astype(jnp.float32)
    y = xf * lax.rsqrt(jnp.mean(xf * xf, axis=-1, keepdims=True) + EPS)
    return (y * g.astype(jnp.float32)).astype(x.dtype)


def partial_rope(x, pos):
    half = ROT_DIM // 2
    inv = ROPE_THETA ** (-jnp.arange(half, dtype=jnp.float32) * 2.0 / ROT_DIM)
    ang = pos.astype(jnp.float32)[:, None] * inv[None, :]
    cos = jnp.cos(ang)[None, :, None, :]
    sin = jnp.sin(ang)[None, :, None, :]
    xr = x[..., :ROT_DIM].astype(jnp.float32)
    x1, x2 = xr[..., :half], xr[..., half:]
    rot = jnp.concatenate([x1 * cos - x2 * sin, x2 * cos + x1 * sin], axis=-1)
    return jnp.concatenate([rot.astype(x.dtype), x[..., ROT_DIM:]], axis=-1)


def sliding_window_gqa_sinks(q, k, v, sinks):
    B, L = q.shape[0], q.shape[1]
    nb = L // BLOCK
    qb = q.reshape(B, nb, BLOCK, N_KV_HEADS, GROUP, HEAD_DIM)
    pad = jnp.zeros((B, BLOCK, N_KV_HEADS, HEAD_DIM), k.dtype)
    kp = jnp.concatenate([pad, k], axis=1)
    vp = jnp.concatenate([pad, v], axis=1)
    shp = (B, nb, BLOCK, N_KV_HEADS, HEAD_DIM)
    kb = jnp.concatenate([kp[:, :-BLOCK].reshape(shp), kp[:, BLOCK:].reshape(shp)], axis=2)
    vb = jnp.concatenate([vp[:, :-BLOCK].reshape(shp), vp[:, BLOCK:].reshape(shp)], axis=2)
    s = jnp.einsum('bnqhgd,bnkhd->bnhgqk', qb, kb).astype(jnp.float32) / math.sqrt(HEAD_DIM)
    qi = jnp.arange(BLOCK)[:, None]
    kj = jnp.arange(2 * BLOCK)[None, :]
    rel = qi + BLOCK - kj
    kpos = jnp.arange(nb)[:, None, None] * BLOCK - BLOCK + kj[None]
    mask = (rel >= 0)[None] & (rel < WINDOW)[None] & (kpos >= 0)
    s = jnp.where(mask[None, :, None, None], s, jnp.finfo(jnp.float32).min)
    sink = jnp.broadcast_to(
        sinks.astype(jnp.float32).reshape(N_KV_HEADS, GROUP)[None, None, :, :, None, None],
        s.shape[:-1] + (1,))
    p = jax.nn.softmax(jnp.concatenate([s, sink], axis=-1), axis=-1)[..., :-1]
    o = jnp.einsum('bnhgqk,bnkhd->bnqhgd', p.astype(v.dtype), vb)
    return o.reshape(B, L, Q_W)


def s5_mixer(u, lam_re, lam_im, log_dt, b_re, b_im, c_re, c_im, d_skip, w_glu):
    B, L = u.shape[0], u.shape[1]
    f32 = jnp.float32
    ug = u.reshape(B, L, SSM_G, SSM_GC).astype(f32)
    lr, li = lam_re.astype(f32), lam_im.astype(f32)
    dt = jnp.exp(log_dt.astype(f32))[:, None]
    mag = jnp.exp(lr * dt)
    a_re, a_im = mag * jnp.cos(li * dt), mag * jnp.sin(li * dt)
    den = lr * lr + li * li
    nr, ni = a_re - 1.0, a_im
    coef_re = (nr * lr + ni * li) / den
    coef_im = (ni * lr - nr * li) / den
    br, bi = b_re.astype(f32), b_im.astype(f32)
    bb_re = coef_re[..., None] * br - coef_im[..., None] * bi
    bb_im = coef_re[..., None] * bi + coef_im[..., None] * br
    bu_re = jnp.einsum('blgc,gpc->blgp', ug, bb_re)
    bu_im = jnp.einsum('blgc,gpc->blgp', ug, bb_im)
    at_re = jnp.broadcast_to(a_re, bu_re.shape)
    at_im = jnp.broadcast_to(a_im, bu_im.shape)

    def combine(e1, e2):
        ar1, ai1, br1, bi1 = e1
        ar2, ai2, br2, bi2 = e2
        return (ar2 * ar1 - ai2 * ai1,
                ar2 * ai1 + ai2 * ar1,
                ar2 * br1 - ai2 * bi1 + br2,
                ar2 * bi1 + ai2 * br1 + bi2)

    _, _, xs_re, xs_im = lax.associative_scan(combine, (at_re, at_im, bu_re, bu_im), axis=1)
    y = (jnp.einsum('blgp,gcp->blgc', xs_re, c_re.astype(f32))
         - jnp.einsum('blgp,gcp->blgc', xs_im, c_im.astype(f32))
         + d_skip.astype(f32)[None, None] * ug)
    y = jax.nn.gelu(y.reshape(B, L, SSM_W)).astype(u.dtype)
    zg = y @ w_glu
    return zg[..., :SSM_W] * jax.nn.sigmoid(zg[..., SSM_W:])


def _fwd_setup_inputs(seed: int = 0) -> dict:
    key = jax.random.key(seed)
    ks = jax.random.split(key, 24)
    f32 = jnp.float32
    nrm = lambda k, shp, s: jax.random.normal(k, shp, f32) * s
    x = jax.random.normal(ks[0], (BATCH, SEQ, D_MODEL), f32)
    gain = lambda k: 1.0 + nrm(k, (DEPTH, D_MODEL), 0.02)
    n_idx = jnp.arange(SSM_P, dtype=f32)
    lam_re = -0.5 + nrm(ks[9], (DEPTH, SSM_G, SSM_P), 0.01)
    lam_im = jnp.pi * n_idx[None, None, :] + nrm(ks[10], (DEPTH, SSM_G, SSM_P), 0.01)
    log_dt = jax.random.uniform(ks[11], (DEPTH, SSM_G), f32, math.log(1e-3), math.log(1e-1))
    return {
        "x": x,
        "norm_mix_pre": gain(ks[1]),
        "norm_mix_post": gain(ks[2]),
        "norm_mlp_pre": gain(ks[3]),
        "norm_mlp_post": gain(ks[4]),
        "w_in": nrm(ks[5], (DEPTH, D_MODEL, IN_W), D_MODEL ** -0.5),
        "sinks": nrm(ks[6], (DEPTH, N_Q_HEADS), 0.5),
        "lam_re": lam_re,
        "lam_im": lam_im,
        "log_dt": log_dt,
        "b_re": nrm(ks[12], (DEPTH, SSM_G, SSM_P, SSM_GC), (2 * SSM_GC) ** -0.5),
        "b_im": nrm(ks[13], (DEPTH, SSM_G, SSM_P, SSM_GC), (2 * SSM_GC) ** -0.5),
        "c_re": nrm(ks[14], (DEPTH, SSM_G, SSM_GC, SSM_P), (2 * SSM_P) ** -0.5),
        "c_im": nrm(ks[15], (DEPTH, SSM_G, SSM_GC, SSM_P), (2 * SSM_P) ** -0.5),
        "d_skip": nrm(ks[16], (DEPTH, SSM_G, SSM_GC), 1.0),
        "w_glu": nrm(ks[17], (DEPTH, SSM_W, 2 * SSM_W), SSM_W ** -0.5),
        "w_branch": nrm(ks[18], (DEPTH, MIX_W, D_MODEL), (MIX_W // 2) ** -0.5),
        "w_out": nrm(ks[19], (DEPTH, D_MODEL, D_MODEL), D_MODEL ** -0.5),
        "w_up": nrm(ks[20], (DEPTH, D_MODEL, D_FF), D_MODEL ** -0.5),
        "w_down": nrm(ks[21], (DEPTH, D_FF, D_MODEL), D_FF ** -0.5),
    }


def _fwd_reference(x, norm_mix_pre, norm_mix_post, norm_mlp_pre, norm_mlp_post, w_in, sinks,
              lam_re, lam_im, log_dt, b_re, b_im, c_re, c_im, d_skip, w_glu,
              w_branch, w_out, w_up, w_down):
    B, L, _ = x.shape
    pos = jnp.arange(L)
    o1 = Q_W
    o2 = o1 + KV_W
    o3 = o2 + KV_W
    o4 = o3 + SSM_W
    o5 = o4 + D_MODEL
    for l in range(DEPTH):
        h = rms_norm(x, norm_mix_pre[l])
        z = h @ w_in[l]
        q = z[..., :o1].reshape(B, L, N_Q_HEADS, HEAD_DIM)
        k = z[..., o1:o2].reshape(B, L, N_KV_HEADS, HEAD_DIM)
        v = z[..., o2:o3].reshape(B, L, N_KV_HEADS, HEAD_DIM)
        u = z[..., o3:o4]
        g_attn = jax.nn.sigmoid(z[..., o4:o5].astype(jnp.float32)).astype(x.dtype)
        g_ssm = jax.nn.sigmoid(z[..., o5:].astype(jnp.float32)).astype(x.dtype)
        q = partial_rope(q, pos)
        k = partial_rope(k, pos)
        o_attn = sliding_window_gqa_sinks(q, k, v, sinks[l])
        o_ssm = s5_mixer(u, lam_re[l], lam_im[l], log_dt[l], b_re[l], b_im[l],
                         c_re[l], c_im[l], d_skip[l], w_glu[l])
        y_attn = o_attn @ w_branch[l][:Q_W]
        y_ssm = o_ssm @ w_branch[l][Q_W:]
        mixed = (g_attn * y_attn + g_ssm * y_ssm) @ w_out[l]
        x = x + rms_norm(mixed, norm_mix_post[l])
        h2 = rms_norm(x, norm_mlp_pre[l])
        a = jax.nn.relu(h2 @ w_up[l])
        x = x + rms_norm((a * a) @ w_down[l], norm_mlp_post[l])
    return x


import jax as _jax
import jax.numpy as _jnp

TWIN_FORMAT = 'train_step'
FWD_PARAMS = ['x', 'norm_mix_pre', 'norm_mix_post', 'norm_mlp_pre', 'norm_mlp_post', 'w_in', 'sinks', 'lam_re', 'lam_im', 'log_dt', 'b_re', 'b_im', 'c_re', 'c_im', 'd_skip', 'w_glu', 'w_branch', 'w_out', 'w_up', 'w_down']
TWIN_WEIGHTS = ['norm_mix_pre', 'norm_mix_post', 'norm_mlp_pre', 'norm_mlp_post', 'w_in', 'sinks', 'lam_re', 'lam_im', 'log_dt', 'b_re', 'b_im', 'c_re', 'c_im', 'd_skip', 'w_glu', 'w_branch', 'w_out', 'w_up', 'w_down']
TWIN_DIFF_INPUT = 'x'
TWIN_INPUTS = ['x', 'norm_mix_pre', 'norm_mix_post', 'norm_mlp_pre', 'norm_mlp_post', 'w_in', 'sinks', 'lam_re', 'lam_im', 'log_dt', 'b_re', 'b_im', 'c_re', 'c_im', 'd_skip', 'w_glu', 'w_branch', 'w_out', 'w_up', 'w_down', 'loss_target', 'm_norm_mix_pre', 'm_norm_mix_post', 'm_norm_mlp_pre', 'm_norm_mlp_post', 'm_w_in', 'm_sinks', 'm_lam_re', 'm_lam_im', 'm_log_dt', 'm_b_re', 'm_b_im', 'm_c_re', 'm_c_im', 'm_d_skip', 'm_w_glu', 'm_w_branch', 'm_w_out', 'm_w_up', 'm_w_down', 'v_norm_mix_pre', 'v_norm_mix_post', 'v_norm_mlp_pre', 'v_norm_mlp_post', 'v_w_in', 'v_sinks', 'v_lam_re', 'v_lam_im', 'v_log_dt', 'v_b_re', 'v_b_im', 'v_c_re', 'v_c_im', 'v_d_skip', 'v_w_glu', 'v_w_branch', 'v_w_out', 'v_w_up', 'v_w_down']
TWIN_OUTPUTS = ['loss', 'grad_x', 'grad_norm_mix_pre', 'grad_norm_mix_post', 'grad_norm_mlp_pre', 'grad_norm_mlp_post', 'grad_w_in', 'grad_sinks', 'grad_lam_re', 'grad_lam_im', 'grad_log_dt', 'grad_b_re', 'grad_b_im', 'grad_c_re', 'grad_c_im', 'grad_d_skip', 'grad_w_glu', 'grad_w_branch', 'grad_w_out', 'grad_w_up', 'grad_w_down', 'delta_norm_mix_pre', 'delta_norm_mix_post', 'delta_norm_mlp_pre', 'delta_norm_mlp_post', 'delta_w_in', 'delta_sinks', 'delta_lam_re', 'delta_lam_im', 'delta_log_dt', 'delta_b_re', 'delta_b_im', 'delta_c_re', 'delta_c_im', 'delta_d_skip', 'delta_w_glu', 'delta_w_branch', 'delta_w_out', 'delta_w_up', 'delta_w_down', 'new_m_norm_mix_pre', 'new_m_norm_mix_post', 'new_m_norm_mlp_pre', 'new_m_norm_mlp_post', 'new_m_w_in', 'new_m_sinks', 'new_m_lam_re', 'new_m_lam_im', 'new_m_log_dt', 'new_m_b_re', 'new_m_b_im', 'new_m_c_re', 'new_m_c_im', 'new_m_d_skip', 'new_m_w_glu', 'new_m_w_branch', 'new_m_w_out', 'new_m_w_up', 'new_m_w_down', 'new_v_norm_mix_pre', 'new_v_norm_mix_post', 'new_v_norm_mlp_pre', 'new_v_norm_mlp_post', 'new_v_w_in', 'new_v_sinks', 'new_v_lam_re', 'new_v_lam_im', 'new_v_log_dt', 'new_v_b_re', 'new_v_b_im', 'new_v_c_re', 'new_v_c_im', 'new_v_d_skip', 'new_v_w_glu', 'new_v_w_branch', 'new_v_w_out', 'new_v_w_up', 'new_v_w_down']
TWIN_LEAF_KINDS = {'loss': 'loss', 'grad_x': 'grad_x', 'grad_norm_mix_pre': 'grad_w', 'grad_norm_mix_post': 'grad_w', 'grad_norm_mlp_pre': 'grad_w', 'grad_norm_mlp_post': 'grad_w', 'grad_w_in': 'grad_w', 'grad_sinks': 'grad_w', 'grad_lam_re': 'grad_w', 'grad_lam_im': 'grad_w', 'grad_log_dt': 'grad_w', 'grad_b_re': 'grad_w', 'grad_b_im': 'grad_w', 'grad_c_re': 'grad_w', 'grad_c_im': 'grad_w', 'grad_d_skip': 'grad_w', 'grad_w_glu': 'grad_w', 'grad_w_branch': 'grad_w', 'grad_w_out': 'grad_w', 'grad_w_up': 'grad_w', 'grad_w_down': 'grad_w', 'delta_norm_mix_pre': 'delta_w', 'delta_norm_mix_post': 'delta_w', 'delta_norm_mlp_pre': 'delta_w', 'delta_norm_mlp_post': 'delta_w', 'delta_w_in': 'delta_w', 'delta_sinks': 'delta_w', 'delta_lam_re': 'delta_w', 'delta_lam_im': 'delta_w', 'delta_log_dt': 'delta_w', 'delta_b_re': 'delta_w', 'delta_b_im': 'delta_w', 'delta_c_re': 'delta_w', 'delta_c_im': 'delta_w', 'delta_d_skip': 'delta_w', 'delta_w_glu': 'delta_w', 'delta_w_branch': 'delta_w', 'delta_w_out': 'delta_w', 'delta_w_up': 'delta_w', 'delta_w_down': 'delta_w', 'new_m_norm_mix_pre': 'new_m', 'new_m_norm_mix_post': 'new_m', 'new_m_norm_mlp_pre': 'new_m', 'new_m_norm_mlp_post': 'new_m', 'new_m_w_in': 'new_m', 'new_m_sinks': 'new_m', 'new_m_lam_re': 'new_m', 'new_m_lam_im': 'new_m', 'new_m_log_dt': 'new_m', 'new_m_b_re': 'new_m', 'new_m_b_im': 'new_m', 'new_m_c_re': 'new_m', 'new_m_c_im': 'new_m', 'new_m_d_skip': 'new_m', 'new_m_w_glu': 'new_m', 'new_m_w_branch': 'new_m', 'new_m_w_out': 'new_m', 'new_m_w_up': 'new_m', 'new_m_w_down': 'new_m', 'new_v_norm_mix_pre': 'new_v', 'new_v_norm_mix_post': 'new_v', 'new_v_norm_mlp_pre': 'new_v', 'new_v_norm_mlp_post': 'new_v', 'new_v_w_in': 'new_v', 'new_v_sinks': 'new_v', 'new_v_lam_re': 'new_v', 'new_v_lam_im': 'new_v', 'new_v_log_dt': 'new_v', 'new_v_b_re': 'new_v', 'new_v_b_im': 'new_v', 'new_v_c_re': 'new_v', 'new_v_c_im': 'new_v', 'new_v_d_skip': 'new_v', 'new_v_w_glu': 'new_v', 'new_v_w_branch': 'new_v', 'new_v_w_out': 'new_v', 'new_v_w_up': 'new_v', 'new_v_w_down': 'new_v'}


def _forward(args):
    return _fwd_reference(*[args[k] for k in FWD_PARAMS])


def _output_shape():
    def fwd():
        inp = _fwd_setup_inputs(0)
        return _fwd_reference(*[inp[k] for k in FWD_PARAMS])
    out = _jax.eval_shape(fwd)
    return out.shape, out.dtype

N_MICROBATCH = 1
ADAM_LR = 0.001
ADAM_B1 = 0.9
ADAM_B2 = 0.999
ADAM_EPS = 1e-08
ADAM_WD = 0.01
ADAM_STEP = 10
PER_EXAMPLE_BATCH_AXIS = {'x': 0, 'loss_target': 0}
SHARED_INPUTS = []
_WEIGHT_DTYPES = {'norm_mix_pre': _jnp.float32, 'norm_mix_post': _jnp.float32, 'norm_mlp_pre': _jnp.float32, 'norm_mlp_post': _jnp.float32, 'w_in': _jnp.float32, 'sinks': _jnp.float32, 'lam_re': _jnp.float32, 'lam_im': _jnp.float32, 'log_dt': _jnp.float32, 'b_re': _jnp.float32, 'b_im': _jnp.float32, 'c_re': _jnp.float32, 'c_im': _jnp.float32, 'd_skip': _jnp.float32, 'w_glu': _jnp.float32, 'w_branch': _jnp.float32, 'w_out': _jnp.float32, 'w_up': _jnp.float32, 'w_down': _jnp.float32}
MOMENT_SCALE = {'norm_mix_pre': 5.492693e-01, 'norm_mix_post': 3.340019e+01, 'norm_mlp_pre': 2.679757e+00, 'norm_mlp_post': 3.372816e+01, 'w_in': 2.843047e-01, 'sinks': 1.196918e-01, 'lam_re': 2.802726e-02, 'lam_im': 2.512668e-02, 'log_dt': 2.171067e+01, 'b_re': 1.734578e-02, 'b_im': 1.629005e-02, 'c_re': 3.506695e-02, 'c_im': 3.296423e-02, 'd_skip': 9.376862e+00, 'w_glu': 6.124016e+00, 'w_branch': 4.477419e+00, 'w_out': 6.204850e+00, 'w_up': 1.273339e+00, 'w_down': 5.682437e+00}


def _to_microbatches(a, axis):
    t = _jnp.moveaxis(a, axis, 0)
    t = t.reshape((N_MICROBATCH, t.shape[0] // N_MICROBATCH) + t.shape[1:])
    return _jnp.moveaxis(t, 1, axis + 1)


def setup_inputs(seed: int = 0) -> dict:
    inp = _fwd_setup_inputs(seed)
    key = _jax.random.fold_in(_jax.random.key(seed), 7919)
    shape, _ = _output_shape()
    out = dict(inp)
    out["loss_target"] = _jax.random.normal(_jax.random.fold_in(key, 0), shape, _jnp.float32)
    for i, name in enumerate(TWIN_WEIGHTS):
        w = inp[name].astype(_jnp.float32)
        if MOMENT_SCALE is None:
            s = _jnp.sqrt(_jnp.mean(_jnp.square(w)) + 1e-30)
        else:
            s = MOMENT_SCALE[name]
        km, kv = _jax.random.split(_jax.random.fold_in(key, i + 1))
        out[name] = w
        out["m_" + name] = s * _jax.random.normal(km, w.shape, _jnp.float32)
        out["v_" + name] = (s * s) * _jax.random.uniform(kv, w.shape, _jnp.float32, 0.5, 1.5)
    if N_MICROBATCH > 1:
        for name, axis in PER_EXAMPLE_BATCH_AXIS.items():
            out[name] = _to_microbatches(out[name], axis)
    return {'x': out['x'], 'norm_mix_pre': out['norm_mix_pre'], 'norm_mix_post': out['norm_mix_post'], 'norm_mlp_pre': out['norm_mlp_pre'], 'norm_mlp_post': out['norm_mlp_post'], 'w_in': out['w_in'], 'sinks': out['sinks'], 'lam_re': out['lam_re'], 'lam_im': out['lam_im'], 'log_dt': out['log_dt'], 'b_re': out['b_re'], 'b_im': out['b_im'], 'c_re': out['c_re'], 'c_im': out['c_im'], 'd_skip': out['d_skip'], 'w_glu': out['w_glu'], 'w_branch': out['w_branch'], 'w_out': out['w_out'], 'w_up': out['w_up'], 'w_down': out['w_down'], 'loss_target': out['loss_target'], 'm_norm_mix_pre': out['m_norm_mix_pre'], 'm_norm_mix_post': out['m_norm_mix_post'], 'm_norm_mlp_pre': out['m_norm_mlp_pre'], 'm_norm_mlp_post': out['m_norm_mlp_post'], 'm_w_in': out['m_w_in'], 'm_sinks': out['m_sinks'], 'm_lam_re': out['m_lam_re'], 'm_lam_im': out['m_lam_im'], 'm_log_dt': out['m_log_dt'], 'm_b_re': out['m_b_re'], 'm_b_im': out['m_b_im'], 'm_c_re': out['m_c_re'], 'm_c_im': out['m_c_im'], 'm_d_skip': out['m_d_skip'], 'm_w_glu': out['m_w_glu'], 'm_w_branch': out['m_w_branch'], 'm_w_out': out['m_w_out'], 'm_w_up': out['m_w_up'], 'm_w_down': out['m_w_down'], 'v_norm_mix_pre': out['v_norm_mix_pre'], 'v_norm_mix_post': out['v_norm_mix_post'], 'v_norm_mlp_pre': out['v_norm_mlp_pre'], 'v_norm_mlp_post': out['v_norm_mlp_post'], 'v_w_in': out['v_w_in'], 'v_sinks': out['v_sinks'], 'v_lam_re': out['v_lam_re'], 'v_lam_im': out['v_lam_im'], 'v_log_dt': out['v_log_dt'], 'v_b_re': out['v_b_re'], 'v_b_im': out['v_b_im'], 'v_c_re': out['v_c_re'], 'v_c_im': out['v_c_im'], 'v_d_skip': out['v_d_skip'], 'v_w_glu': out['v_w_glu'], 'v_w_branch': out['v_w_branch'], 'v_w_out': out['v_w_out'], 'v_w_up': out['v_w_up'], 'v_w_down': out['v_w_down']}


def _loss(weights, diff, rest, loss_target):
    with _jax.named_scope("forward"):
        args = {**rest, TWIN_DIFF_INPUT: diff, **{k: w.astype(_WEIGHT_DTYPES[k]) for k, w in weights.items()}}
        y = _forward(args)
    with _jax.named_scope("loss_head"):
        err = _jnp.square(y.astype(_jnp.float32) - loss_target)
        return 0.5 * _jnp.sum(_jnp.mean(err, axis=-1)) if err.ndim else 0.5 * err


def _adamw(w, g, m, v):
    m = ADAM_B1 * m + (1.0 - ADAM_B1) * g
    v = ADAM_B2 * v + (1.0 - ADAM_B2) * _jnp.square(g)
    m_hat = m / (1.0 - ADAM_B1 ** ADAM_STEP)
    v_hat = v / (1.0 - ADAM_B2 ** ADAM_STEP)
    delta = -ADAM_LR * (m_hat / (_jnp.sqrt(v_hat) + ADAM_EPS) + ADAM_WD * w)
    return delta, m, v


def reference(x, norm_mix_pre, norm_mix_post, norm_mlp_pre, norm_mlp_post, w_in, sinks, lam_re, lam_im, log_dt, b_re, b_im, c_re, c_im, d_skip, w_glu, w_branch, w_out, w_up, w_down, loss_target, m_norm_mix_pre, m_norm_mix_post, m_norm_mlp_pre, m_norm_mlp_post, m_w_in, m_sinks, m_lam_re, m_lam_im, m_log_dt, m_b_re, m_b_im, m_c_re, m_c_im, m_d_skip, m_w_glu, m_w_branch, m_w_out, m_w_up, m_w_down, v_norm_mix_pre, v_norm_mix_post, v_norm_mlp_pre, v_norm_mlp_post, v_w_in, v_sinks, v_lam_re, v_lam_im, v_log_dt, v_b_re, v_b_im, v_c_re, v_c_im, v_d_skip, v_w_glu, v_w_branch, v_w_out, v_w_up, v_w_down):
    given = dict(x=x, norm_mix_pre=norm_mix_pre, norm_mix_post=norm_mix_post, norm_mlp_pre=norm_mlp_pre, norm_mlp_post=norm_mlp_post, w_in=w_in, sinks=sinks, lam_re=lam_re, lam_im=lam_im, log_dt=log_dt, b_re=b_re, b_im=b_im, c_re=c_re, c_im=c_im, d_skip=d_skip, w_glu=w_glu, w_branch=w_branch, w_out=w_out, w_up=w_up, w_down=w_down, loss_target=loss_target, m_norm_mix_pre=m_norm_mix_pre, m_norm_mix_post=m_norm_mix_post, m_norm_mlp_pre=m_norm_mlp_pre, m_norm_mlp_post=m_norm_mlp_post, m_w_in=m_w_in, m_sinks=m_sinks, m_lam_re=m_lam_re, m_lam_im=m_lam_im, m_log_dt=m_log_dt, m_b_re=m_b_re, m_b_im=m_b_im, m_c_re=m_c_re, m_c_im=m_c_im, m_d_skip=m_d_skip, m_w_glu=m_w_glu, m_w_branch=m_w_branch, m_w_out=m_w_out, m_w_up=m_w_up, m_w_down=m_w_down, v_norm_mix_pre=v_norm_mix_pre, v_norm_mix_post=v_norm_mix_post, v_norm_mlp_pre=v_norm_mlp_pre, v_norm_mlp_post=v_norm_mlp_post, v_w_in=v_w_in, v_sinks=v_sinks, v_lam_re=v_lam_re, v_lam_im=v_lam_im, v_log_dt=v_log_dt, v_b_re=v_b_re, v_b_im=v_b_im, v_c_re=v_c_re, v_c_im=v_c_im, v_d_skip=v_d_skip, v_w_glu=v_w_glu, v_w_branch=v_w_branch, v_w_out=v_w_out, v_w_up=v_w_up, v_w_down=v_w_down)
    weights = {n: given[n] for n in TWIN_WEIGHTS}
    shared = {n: given[n] for n in SHARED_INPUTS}
    per_example = {n: given[n] for n in ['x']}
    grad_fn = _jax.value_and_grad(_loss, argnums=(0, 1))

    def one_microbatch(ex, loss_target):
        ex = dict(ex)
        diff = ex.pop(TWIN_DIFF_INPUT)
        return grad_fn(weights, diff, {**shared, **ex}, loss_target)

    if N_MICROBATCH == 1:
        loss, (grad_w, grad_x) = one_microbatch(per_example, given["loss_target"])
    else:
        def body(carry, xs):
            loss_sum, grad_sum = carry
            l_k, (gw_k, gx_k) = one_microbatch(xs[0], xs[1])
            with _jax.named_scope("update"):
                return (loss_sum + l_k, _jax.tree.map(_jnp.add, grad_sum, gw_k)), gx_k

        init = (_jnp.zeros((), _jnp.float32), _jax.tree.map(_jnp.zeros_like, weights))
        (loss, grad_w), grad_x = _jax.lax.scan(body, init, (per_example, given["loss_target"]))
    with _jax.named_scope("update"):
        delta_w, new_m, new_v = {}, {}, {}
        for n in TWIN_WEIGHTS:
            delta_w[n], new_m[n], new_v[n] = _adamw(weights[n], grad_w[n], given["m_" + n], given["v_" + n])
    return (loss, grad_x, *[grad_w[n] for n in TWIN_WEIGHTS], *[delta_w[n] for n in TWIN_WEIGHTS],
            *[new_m[n] for n in TWIN_WEIGHTS], *[new_v[n] for n in TWIN_WEIGHTS])
```

```python
import functools
import math

import jax
import jax.numpy as jnp
from jax import lax
from jax.experimental import pallas as pl
from jax.experimental.pallas import tpu as pltpu

F32 = jnp.float32
BF16 = jnp.bfloat16
MESH = pl.DeviceIdType.MESH

LANES = 128
SUBLANES = 8
VMEM_LIMIT = 56 * 1024 * 1024

HEAD_DIM = 64
N_Q_HEADS = 16
N_KV_HEADS = 2
Q_W = N_Q_HEADS * HEAD_DIM
KV_W = N_KV_HEADS * HEAD_DIM
BLOCK = 128
ROT_DIM = HEAD_DIM // 4
ROPE_THETA = 500000.0
SSM_GC = 16
SSM_P = 64
GROUPS_PER_BLOCK = 8
NW = GROUPS_PER_BLOCK * SSM_P // LANES
EPS = 1e-6
N_DEV = 8

ADAM_LR = 0.001
ADAM_B1 = 0.9
ADAM_B2 = 0.999
ADAM_EPS = 1e-08
ADAM_WD = 0.01
ADAM_STEP = 10


def _params(sem=None):
    return pltpu.CompilerParams(dimension_semantics=sem, vmem_limit_bytes=VMEM_LIMIT)


def _pick(dim, prefs):
    for p in prefs:
        if dim % p == 0:
            return p
    return dim


def _sigmoid(x):
    return 1.0 / (1.0 + jnp.exp(-x))


_GELU_C = math.sqrt(2.0 / math.pi)


def _gelu(x):
    return 0.5 * x * (1.0 + jnp.tanh(_GELU_C * (x + 0.044715 * x * x * x)))


def _gelu_grad(x):
    t = jnp.tanh(_GELU_C * (x + 0.044715 * x * x * x))
    return 0.5 * (1.0 + t) + 0.5 * x * (1.0 - t * t) * _GELU_C * (1.0 + 3.0 * 0.044715 * x * x)


_DIMS = {"nn": (((1,), (0,)), ((), ())), "nt": (((1,), (1,)), ((), ())), "tn": (((0,), (0,)), ((), ()))}


def _mm(a, b, *, mode, name, out_dtypes=(F32,), epi=None, extras=()):
    if mode == "nn":
        (M, K), (K2, N) = a.shape, b.shape
    elif mode == "nt":
        (M, K), (N, K2) = a.shape, b.shape
    else:
        (K, M), (K2, N) = a.shape, b.shape
    assert K == K2, (a.shape, b.shape, mode)
    tm = _pick(M, (1024, 512, 256, 128))
    tn = _pick(N, (1024, 1280, 640, 512, 384, 256, 128))
    tk = _pick(K, (512, 256, 128))
    nk = K // tk
    n_ex = len(extras)
    n_out = len(out_dtypes)

    def body(*refs):
        a_ref, b_ref = refs[0], refs[1]
        ex_refs = refs[2:2 + n_ex]
        o_refs = refs[2 + n_ex:2 + n_ex + n_out]
        acc = refs[-1]
        k = pl.program_id(2)

        @pl.when(k == 0)
        def _():
            acc[...] = jnp.zeros_like(acc)

        acc[...] += lax.dot_general(a_ref[...].astype(BF16), b_ref[...].astype(BF16), _DIMS[mode],
                                    preferred_element_type=F32)

        @pl.when(k == nk - 1)
        def _():
            r = acc[...]
            outs = (r,) if epi is None else epi(r, *[e[...] for e in ex_refs])
            for o_ref, o in zip(o_refs, outs):
                o_ref[...] = o.astype(o_ref.dtype)

    if mode == "nn":
        a_spec = pl.BlockSpec((tm, tk), lambda i, j, k: (i, k))
        b_spec = pl.BlockSpec((tk, tn), lambda i, j, k: (k, j))
    elif mode == "nt":
        a_spec = pl.BlockSpec((tm, tk), lambda i, j, k: (i, k))
        b_spec = pl.BlockSpec((tn, tk), lambda i, j, k: (j, k))
    else:
        a_spec = pl.BlockSpec((tk, tm), lambda i, j, k: (k, i))
        b_spec = pl.BlockSpec((tk, tn), lambda i, j, k: (k, j))
    o_spec = pl.BlockSpec((tm, tn), lambda i, j, k: (i, j))
    res = pl.pallas_call(
        body,
        grid=(M // tm, N // tn, nk),
        in_specs=[a_spec, b_spec] + [o_spec] * n_ex,
        out_specs=tuple([o_spec] * n_out),
        out_shape=tuple(jax.ShapeDtypeStruct((M, N), d) for d in out_dtypes),
        scratch_shapes=[pltpu.VMEM((tm, tn), F32)],
        compiler_params=_params(("parallel", "parallel", "arbitrary")),
        name=name,
    )(a, b, *extras)
    return res[0] if n_out == 1 else res


def _ew(fn, ins, out_dtypes, *, rows, ncols, name):
    g = ncols
    for _, off in ins:
        g = math.gcd(g, off)
    tc = _pick(g, (512, 256, 128))
    tr = _pick(rows, (1024, 512, 256, 128))
    n_in = len(ins)

    def body(*refs):
        outs = fn(*[r[...] for r in refs[:n_in]])
        for o_ref, o in zip(refs[n_in:], outs):
            o_ref[...] = o.astype(o_ref.dtype)

    def in_spec(off):
        ob = off // tc
        return pl.BlockSpec((tr, tc), lambda i, j: (i, j + ob))

    o_spec = pl.BlockSpec((tr, tc), lambda i, j: (i, j))
    res = pl.pallas_call(
        body,
        grid=(rows // tr, ncols // tc),
        in_specs=[in_spec(off) for _, off in ins],
        out_specs=tuple([o_spec] * len(out_dtypes)),
        out_shape=tuple(jax.ShapeDtypeStruct((rows, ncols), d) for d in out_dtypes),
        compiler_params=_params(("parallel", "parallel")),
        name=name,
    )(*[arr for arr, _ in ins])
    return res[0] if len(out_dtypes) == 1 else res


def _rstd(x):
    return lax.rsqrt(jnp.mean(x * x, axis=-1, keepdims=True) + EPS)


def _norm_bwd(x, r, g, dy):
    t = dy * g
    dx = r * t - x * (r * r * r) * jnp.mean(t * x, axis=-1, keepdims=True)
    return dx, dy * x * r


def _row_call(body, ins, row_ins, outs, acc_outs, *, rows, width, name):
    tr = _pick(rows, (256, 128))
    t_spec = pl.BlockSpec((tr, width), lambda i: (i, 0))
    r_spec = pl.BlockSpec((1, width), lambda i: (0, 0))
    return pl.pallas_call(
        body,
        grid=(rows // tr,),
        in_specs=[t_spec] * len(ins) + [r_spec] * len(row_ins),
        out_specs=tuple([t_spec] * len(outs) + [pl.BlockSpec(s, lambda i: (0, 0)) for s in acc_outs]),
        out_shape=tuple([jax.ShapeDtypeStruct((rows, width), d) for d in outs]
                        + [jax.ShapeDtypeStruct(s, F32) for s in acc_outs]),
        compiler_params=_params(("arbitrary",)),
        name=name,
    )(*ins, *row_ins)


def _rms_pre(x, g):
    L, D = x.shape

    def body(x_ref, g_ref, h_ref):
        xv = x_ref[...]
        h_ref[...] = (xv * _rstd(xv) * g_ref[...]).astype(BF16)

    return _row_call(body, [x], [g], [BF16], [], rows=L, width=D, name="rms_pre")[0]


def _post_pre(x, mixed, g_post, g_pre):
    L, D = x.shape

    def body(x_ref, m_ref, gp_ref, gq_ref, x1_ref, h2_ref):
        mv = m_ref[...]
        x1 = x_ref[...] + mv * _rstd(mv) * gp_ref[...]
        x1_ref[...] = x1
        h2_ref[...] = (x1 * _rstd(x1) * gq_ref[...]).astype(BF16)

    return _row_call(body, [x, mixed], [g_post, g_pre], [F32, BF16], [], rows=L, width=D, name="post_pre")


def _loss_bwd(x1, dn, g_post, target):
    L, D = x1.shape

    def body(x1_ref, dn_ref, t_ref, g_ref, dx2_ref, ddn_ref, dg_ref, loss_ref):
        @pl.when(pl.program_id(0) == 0)
        def _():
            dg_ref[...] = jnp.zeros_like(dg_ref)
            loss_ref[...] = jnp.zeros_like(loss_ref)

        dnv = dn_ref[...]
        g = g_ref[...]
        r = _rstd(dnv)
        err = x1_ref[...] + dnv * r * g - t_ref[...]
        loss_ref[...] += 0.5 * jnp.sum(jnp.mean(err * err, axis=-1, keepdims=True), axis=0, keepdims=True)
        dx2 = err * (1.0 / D)
        dx2_ref[...] = dx2
        ddn, dgr = _norm_bwd(dnv, r, g, dx2)
        ddn_ref[...] = ddn.astype(BF16)
        dg_ref[...] += jnp.sum(dgr, axis=0, keepdims=True)

    return _row_call(body, [x1, dn, target], [g_post], [F32, BF16], [(1, D), (1, 1)],
                     rows=L, width=D, name="loss_bwd")


def _norm_bwd_pair(x1, dh2, dx2, mixed, g_pre, g_post):
    L, D = x1.shape

    def body(x1_ref, dh_ref, dx2_ref, m_ref, gq_ref, gp_ref, dx1_ref, dm_ref, dgq_ref, dgp_ref):
        @pl.when(pl.program_id(0) == 0)
        def _():
            dgq_ref[...] = jnp.zeros_like(dgq_ref)
            dgp_ref[...] = jnp.zeros_like(dgp_ref)

        x1v = x1_ref[...]
        d1, dgq = _norm_bwd(x1v, _rstd(x1v), gq_ref[...], dh_ref[...])
        dx1 = dx2_ref[...] + d1
        dx1_ref[...] = dx1
        mv = m_ref[...]
        dm, dgp = _norm_bwd(mv, _rstd(mv), gp_ref[...], dx1)
        dm_ref[...] = dm.astype(BF16)
        dgq_ref[...] += jnp.sum(dgq, axis=0, keepdims=True)
        dgp_ref[...] += jnp.sum(dgp, axis=0, keepdims=True)

    return _row_call(body, [x1, dh2, dx2, mixed], [g_pre, g_post], [F32, BF16], [(1, D), (1, D)],
                     rows=L, width=D, name="norm_bwd_pair")


def _final_bwd(x, dh, dx1, g_pre):
    L, D = x.shape

    def body(x_ref, dh_ref, dx1_ref, g_ref, gx_ref, dg_ref):
        @pl.when(pl.program_id(0) == 0)
        def _():
            dg_ref[...] = jnp.zeros_like(dg_ref)

        xv = x_ref[...]
        d0, dg = _norm_bwd(xv, _rstd(xv), g_ref[...], dh_ref[...])
        gx_ref[...] = dx1_ref[...] + d0
        dg_ref[...] += jnp.sum(dg, axis=0, keepdims=True)

    return _row_call(body, [x, dh, dx1], [g_pre], [F32], [(1, D)], rows=L, width=D, name="final_bwd")


def _rope_tables(L):
    half = ROT_DIM // 2
    inv = ROPE_THETA ** (-jnp.arange(half, dtype=F32) * 2.0 / ROT_DIM)
    ang = jnp.arange(L, dtype=F32)[:, None] * inv[None, :]
    d = jnp.arange(LANES) % HEAD_DIM
    a = ang[:, d % half]
    cos_t = jnp.where(d[None, :] < ROT_DIM, jnp.cos(a), 1.0)
    sin_t = jnp.where(d[None, :] < half, -jnp.sin(a), jnp.where(d[None, :] < ROT_DIM, jnp.sin(a), 0.0))
    return cos_t.astype(F32), sin_t.astype(F32)


def _lane_lo(shape):
    return lax.broadcasted_iota(jnp.int32, shape, 1) < HEAD_DIM


def _rope(x, cos_t, sin_t):
    d = lax.broadcasted_iota(jnp.int32, x.shape, 1) % HEAD_DIM
    partner = jnp.where(d < ROT_DIM // 2, pltpu.roll(x, LANES - ROT_DIM // 2, 1), pltpu.roll(x, ROT_DIM // 2, 1))
    return x * cos_t + partner * sin_t


def _dup(kv, g):
    sw = pltpu.roll(kv, HEAD_DIM, 1)
    lo = _lane_lo(kv.shape)
    return jnp.where(lo, kv, sw) if g == 0 else jnp.where(lo, sw, kv)


def _attn_mask(n):
    qi = lax.broadcasted_iota(jnp.int32, (BLOCK, 2 * BLOCK), 0)
    kj = lax.broadcasted_iota(jnp.int32, (BLOCK, 2 * BLOCK), 1)
    rel = qi + BLOCK - kj
    return (rel >= 0) & (rel < BLOCK) & ((kj >= BLOCK) | (n > 0))


def _softmax_sink(s, mask, sink):
    s = jnp.where(mask, s, -1e30)
    m = jnp.maximum(jnp.max(s, axis=-1, keepdims=True), sink)
    e = jnp.where(mask, jnp.exp(s - m), 0.0)
    es = jnp.exp(sink - m)
    inv = 1.0 / (jnp.sum(e, axis=-1, keepdims=True) + es)
    return e * inv, es * inv


_NT = (((1,), (1,)), ((), ()))
_TN = (((0,), (0,)), ((), ()))


def _dot(a, b):
    return jnp.dot(a.astype(BF16), b.astype(BF16), preferred_element_type=F32)


def _dot_nt(a, b):
    return lax.dot_general(a.astype(BF16), b.astype(BF16), _NT, preferred_element_type=F32)


def _dot_tn(a, b):
    return lax.dot_general(a.astype(BF16), b.astype(BF16), _TN, preferred_element_type=F32)


def _attn_specs(nb):
    kcol, vcol = Q_W // LANES, Q_W // LANES + 1
    prev = lambda n: jnp.maximum(n - 1, 0)
    return [
        pl.BlockSpec((BLOCK, Q_W), lambda n: (n, 0)),
        pl.BlockSpec((BLOCK, LANES), lambda n: (n, kcol)),
        pl.BlockSpec((BLOCK, LANES), lambda n: (prev(n), kcol)),
        pl.BlockSpec((BLOCK, LANES), lambda n: (n, vcol)),
        pl.BlockSpec((BLOCK, LANES), lambda n: (prev(n), vcol)),
        pl.BlockSpec((BLOCK, LANES), lambda n: (n, 0)),
        pl.BlockSpec((BLOCK, LANES), lambda n: (prev(n), 0)),
        pl.BlockSpec((BLOCK, LANES), lambda n: (n, 0)),
        pl.BlockSpec((BLOCK, LANES), lambda n: (prev(n), 0)),
        pl.BlockSpec(memory_space=pltpu.SMEM),
    ]


def _attn_prep(refs):
    q_ref, kc_ref, kp_ref, vc_ref, vp_ref, cc_ref, cp_ref, sc_ref, sp_ref = refs
    cos_c, sin_c, cos_p, sin_p = cc_ref[...], sc_ref[...], cp_ref[...], sp_ref[...]
    k2 = jnp.concatenate([_rope(kp_ref[...], cos_p, sin_p), _rope(kc_ref[...], cos_c, sin_c)], axis=0)
    v2 = jnp.concatenate([vp_ref[...], vc_ref[...]], axis=0)
    lo2 = _lane_lo(v2.shape)
    kd = [_dup(k2, g).astype(BF16) for g in range(N_KV_HEADS)]
    vd = [_dup(v2, g) for g in range(N_KV_HEADS)]
    va = [jnp.where(lo2, v, 0.0).astype(BF16) for v in vd]
    vb = [jnp.where(lo2, 0.0, v).astype(BF16) for v in vd]
    return cos_c, sin_c, cos_p, sin_p, kd, va, vb


_SCALE = 1.0 / math.sqrt(HEAD_DIM)
_TILES = Q_W // LANES
_TILES_PER_KV = _TILES // N_KV_HEADS


def _attn_fwd(z, cos_t, sin_t, sinks):
    L = z.shape[0]
    nb = L // BLOCK

    def body(q_ref, kc_ref, kp_ref, vc_ref, vp_ref, cc_ref, cp_ref, sc_ref, sp_ref, sink_ref, o_ref):
        n = pl.program_id(0)
        cos_c, sin_c, _, _, kd, va, vb = _attn_prep((q_ref, kc_ref, kp_ref, vc_ref, vp_ref, cc_ref, cp_ref, sc_ref, sp_ref))
        mask = _attn_mask(n)
        lo = _lane_lo((BLOCK, LANES))
        for t in range(_TILES):
            g = t // _TILES_PER_KV
            qt = _rope(q_ref[:, t * LANES:(t + 1) * LANES], cos_c, sin_c) * _SCALE
            pa, _ = _softmax_sink(_dot_nt(jnp.where(lo, qt, 0.0), kd[g]), mask, sink_ref[0, 2 * t])
            pb, _ = _softmax_sink(_dot_nt(jnp.where(lo, 0.0, qt), kd[g]), mask, sink_ref[0, 2 * t + 1])
            o_ref[:, t * LANES:(t + 1) * LANES] = (_dot(pa, va[g]) + _dot(pb, vb[g])).astype(BF16)

    return pl.pallas_call(
        body,
        grid=(nb,),
        in_specs=_attn_specs(nb),
        out_specs=pl.BlockSpec((BLOCK, Q_W), lambda n: (n, 0)),
        out_shape=jax.ShapeDtypeStruct((L, Q_W), BF16),
        compiler_params=_params(("parallel",)),
        name="attn_fwd",
    )(z, z, z, z, z, cos_t, cos_t, sin_t, sin_t, sinks)


def _attn_bwd(z, d_o, cos_t, sin_t, sinks):
    L = z.shape[0]
    nb = L // BLOCK

    def body(q_ref, kc_ref, kp_ref, vc_ref, vp_ref, cc_ref, cp_ref, sc_ref, sp_ref, sink_ref, do_ref,
             dq_ref, dkc_ref, dkp_ref, dvc_ref, dvp_ref, ds_ref):
        n = pl.program_id(0)

        @pl.when(n == 0)
        def _():
            ds_ref[...] = jnp.zeros_like(ds_ref)

        cos_c, sin_c, cos_p, sin_p, kd, va, vb = _attn_prep(
            (q_ref, kc_ref, kp_ref, vc_ref, vp_ref, cc_ref, cp_ref, sc_ref, sp_ref))
        mask = _attn_mask(n)
        lo = _lane_lo((BLOCK, LANES))
        lo2 = _lane_lo((2 * BLOCK, LANES))
        acc_k = [jnp.zeros((2 * BLOCK, LANES), F32) for _ in range(N_KV_HEADS)]
        acc_v = [jnp.zeros((2 * BLOCK, LANES), F32) for _ in range(N_KV_HEADS)]
        for t in range(_TILES):
            g = t // _TILES_PER_KV
            sl = slice(t * LANES, (t + 1) * LANES)
            qt = _rope(q_ref[:, sl], cos_c, sin_c) * _SCALE
            qa, qb = jnp.where(lo, qt, 0.0), jnp.where(lo, 0.0, qt)
            pa, psa = _softmax_sink(_dot_nt(qa, kd[g]), mask, sink_ref[0, 2 * t])
            pb, psb = _softmax_sink(_dot_nt(qb, kd[g]), mask, sink_ref[0, 2 * t + 1])
            dot_ = do_ref[:, sl]
            ot = _dot(pa, va[g]) + _dot(pb, vb[g])
            prod = dot_ * ot
            da = jnp.sum(jnp.where(lo, prod, 0.0), axis=-1, keepdims=True)
            db = jnp.sum(jnp.where(lo, 0.0, prod), axis=-1, keepdims=True)
            dsa = pa * (_dot_nt(dot_, va[g]) - da)
            dsb = pb * (_dot_nt(dot_, vb[g]) - db)
            dqt = jnp.where(lo, _dot(dsa, kd[g]), _dot(dsb, kd[g])) * _SCALE
            dq_ref[:, sl] = _rope(dqt, cos_c, -sin_c).astype(BF16)
            acc_k[g] = acc_k[g] + _dot_tn(dsa, qa) + _dot_tn(dsb, qb)
            acc_v[g] = acc_v[g] + _dot_tn(pa, jnp.where(lo, dot_, 0.0)) + _dot_tn(pb, jnp.where(lo, 0.0, dot_))
            row = jnp.where(_lane_lo((1, LANES)), -jnp.sum(psa * da, axis=0, keepdims=True),
                            -jnp.sum(psb * db, axis=0, keepdims=True))
            ds_ref[t:t + 1, :] += row
        fk = [a + pltpu.roll(a, HEAD_DIM, 1) for a in acc_k]
        fv = [a + pltpu.roll(a, HEAD_DIM, 1) for a in acc_v]
        dk2 = jnp.where(lo2, fk[0], fk[1])
        dv2 = jnp.where(lo2, fv[0], fv[1])
        dkp_ref[...] = _rope(dk2[:BLOCK], cos_p, -sin_p)
        dkc_ref[...] = _rope(dk2[BLOCK:], cos_c, -sin_c)
        dvp_ref[...] = dv2[:BLOCK]
        dvc_ref[...] = dv2[BLOCK:]

    blk = pl.BlockSpec((BLOCK, LANES), lambda n: (n, 0))
    kv = jax.ShapeDtypeStruct((L, LANES), F32)
    return pl.pallas_call(
        body,
        grid=(nb,),
        in_specs=_attn_specs(nb) + [pl.BlockSpec((BLOCK, Q_W), lambda n: (n, 0))],
        out_specs=(pl.BlockSpec((BLOCK, Q_W), lambda n: (n, 0)), blk, blk, blk, blk,
                   pl.BlockSpec((_TILES, LANES), lambda n: (0, 0))),
        out_shape=(jax.ShapeDtypeStruct((L, Q_W), BF16), kv, kv, kv, kv,
                   jax.ShapeDtypeStruct((_TILES, LANES), F32)),
        compiler_params=_params(("arbitrary",)),
        name="attn_bwd",
    )(z, z, z, z, z, cos_t, cos_t, sin_t, sin_t, sinks, d_o)


def _kv_combine(dkc, dkp, dvc, dvp):
    L = dkc.shape[0]
    nb = L // BLOCK

    def body(kc_ref, kp_ref, vc_ref, vp_ref, dk_ref, dv_ref):
        live = jnp.where(pl.program_id(0) + 1 < nb, 1.0, 0.0)
        dk_ref[...] = (kc_ref[...] + live * kp_ref[...]).astype(BF16)
        dv_ref[...] = (vc_ref[...] + live * vp_ref[...]).astype(BF16)

    cur = pl.BlockSpec((BLOCK, LANES), lambda n: (n, 0))
    nxt = pl.BlockSpec((BLOCK, LANES), lambda n: (jnp.minimum(n + 1, nb - 1), 0))
    o = jax.ShapeDtypeStruct((L, LANES), BF16)
    return pl.pallas_call(body, grid=(nb,), in_specs=[cur, nxt, cur, nxt], out_specs=(cur, cur),
                          out_shape=(o, o), compiler_params=_params(("parallel",)), name="kv_combine")(dkc, dkp, dvc, dvp)


def _discretise(lr, li, ldt, br, bi):
    dt = jnp.exp(ldt)
    mag = jnp.exp(lr * dt)
    a_re, a_im = mag * jnp.cos(li * dt), mag * jnp.sin(li * dt)
    den = lr * lr + li * li
    nr, ni = a_re - 1.0, a_im
    coef_re = (nr * lr + ni * li) / den
    coef_im = (ni * lr - nr * li) / den
    return a_re, a_im, coef_re * br - coef_im * bi, coef_re * bi + coef_im * br


def _disc_specs(n):
    tr = _pick(n, (512,))
    cs = pl.BlockSpec((tr, 1), lambda i: (i, 0))
    ms = pl.BlockSpec((tr, SSM_GC), lambda i: (i, 0))
    return tr, cs, ms


def _disc_fwd(lr, li, ldt, br, bi):
    n = lr.shape[0]
    tr, cs, ms = _disc_specs(n)

    def body(lr_ref, li_ref, dt_ref, br_ref, bi_ref, o1, o2, o3, o4):
        r = _discretise(lr_ref[...], li_ref[...], dt_ref[...], br_ref[...], bi_ref[...])
        o1[...], o2[...], o3[...], o4[...] = r

    col = jax.ShapeDtypeStruct((n, 1), F32)
    mat = jax.ShapeDtypeStruct((n, SSM_GC), F32)
    return pl.pallas_call(body, grid=(n // tr,), in_specs=[cs, cs, cs, ms, ms], out_specs=(cs, cs, ms, ms),
                          out_shape=(col, col, mat, mat), compiler_params=_params(("parallel",)), name="disc_fwd")(
        lr, li, ldt, br, bi)


def _disc_bwd(lr, li, ldt, br, bi, gar, gai, gbr, gbi):
    n = lr.shape[0]
    tr, cs, ms = _disc_specs(n)

    def body(lr_ref, li_ref, dt_ref, br_ref, bi_ref, gar_ref, gai_ref, gbr_ref, gbi_ref, o_lr, o_li, o_dt, o_br, o_bi):
        _, vjp = jax.vjp(_discretise, lr_ref[...], li_ref[...], dt_ref[...], br_ref[...], bi_ref[...])
        g = vjp((gar_ref[...], gai_ref[...], gbr_ref[...], gbi_ref[...]))
        o_lr[...] = g[0]
        o_li[...] = g[1]
        o_dt[...] = jnp.sum(g[2].reshape(tr // SSM_P, SSM_P, 1), axis=1)
        o_br[...] = g[3]
        o_bi[...] = g[4]

    col = jax.ShapeDtypeStruct((n, 1), F32)
    mat = jax.ShapeDtypeStruct((n, SSM_GC), F32)
    return pl.pallas_call(
        body, grid=(n // tr,), in_specs=[cs, cs, cs, ms, ms, cs, cs, ms, ms],
        out_specs=(cs, cs, pl.BlockSpec((tr // SSM_P, 1), lambda i: (i, 0)), ms, ms),
        out_shape=(col, col, jax.ShapeDtypeStruct((n // SSM_P, 1), F32), mat, mat),
        compiler_params=_params(("parallel",)), name="disc_bwd")(lr, li, ldt, br, bi, gar, gai, gbr, gbi)


def _cpow(ar, ai, nsq):
    for _ in range(nsq):
        ar, ai = ar * ar - ai * ai, 2.0 * ar * ai
    return ar, ai


def _ssm_dims(L):
    tc = min(1024, L)
    seg = tc // SUBLANES
    assert seg & (seg - 1) == 0
    return tc, seg, L // tc, seg.bit_length() - 1


def _ssm_fwd(z, u_off, br_m, bi_m, cr_m, ci_m, a_re3, a_im3, d_row):
    L = z.shape[0]
    ngb = br_m.shape[0]
    tc, seg, nc, nsq = _ssm_dims(L)
    ucol = u_off // LANES

    def body(u_ref, br_ref, bi_ref, cr_ref, ci_ref, ar_ref, ai_ref, d_ref, y_ref, xr_ref, xi_ref,
             bur, bui, car_r, car_i, ini_r, ini_i):
        @pl.when(pl.program_id(1) == 0)
        def _():
            car_r[...] = jnp.zeros_like(car_r)
            car_i[...] = jnp.zeros_like(car_i)

        u = u_ref[...]
        pr = _dot(u, br_ref[0])
        pi = _dot(u, bi_ref[0])
        for w in range(NW):
            bur[w] = pr[:, w * LANES:(w + 1) * LANES]
            bui[w] = pi[:, w * LANES:(w + 1) * LANES]
        ar = [jnp.broadcast_to(ar_ref[w], (SUBLANES, LANES)) for w in range(NW)]
        ai = [jnp.broadcast_to(ai_ref[w], (SUBLANES, LANES)) for w in range(NW)]

        def step(i, carry, store):
            xr, xi = carry
            rows = pl.ds(i, SUBLANES, stride=seg)
            nr, ni = [], []
            for w in range(NW):
                r = ar[w] * xr[w] - ai[w] * xi[w] + bur[w, rows, :]
                m = ar[w] * xi[w] + ai[w] * xr[w] + bui[w, rows, :]
                if store:
                    xr_ref[w, rows, :] = r
                    xi_ref[w, rows, :] = m
                nr.append(r)
                ni.append(m)
            return tuple(nr), tuple(ni)

        zero = tuple(jnp.zeros((SUBLANES, LANES), F32) for _ in range(NW))
        er, ei = lax.fori_loop(0, seg, functools.partial(step, store=False), (zero, zero), unroll=2)
        for w in range(NW):
            pr_, pi_ = _cpow(ar[w][0:1], ai[w][0:1], nsq)
            sr, si = car_r[w, 0:1, :], car_i[w, 0:1, :]
            for j in range(SUBLANES):
                ini_r[w, j:j + 1, :] = sr
                ini_i[w, j:j + 1, :] = si
                sr, si = (pr_ * sr - pi_ * si + er[w][j:j + 1], pr_ * si + pi_ * sr + ei[w][j:j + 1])
            car_r[w, 0:1, :] = sr
            car_i[w, 0:1, :] = si
        init = (tuple(ini_r[w] for w in range(NW)), tuple(ini_i[w] for w in range(NW)))
        lax.fori_loop(0, seg, functools.partial(step, store=True), init, unroll=2)
        acc = d_ref[...] * u
        for w in range(NW):
            sl = slice(w * LANES, (w + 1) * LANES)
            acc = acc + _dot(xr_ref[w], cr_ref[0, sl, :]) - _dot(xi_ref[w], ci_ref[0, sl, :])
        y_ref[...] = acc

    mat_b = pl.BlockSpec((1, LANES, NW * LANES), lambda b, k: (b, 0, 0))
    mat_c = pl.BlockSpec((1, NW * LANES, LANES), lambda b, k: (b, 0, 0))
    a_spec = pl.BlockSpec((NW, 1, LANES), lambda b, k: (b, 0, 0))
    x_spec = pl.BlockSpec((NW, tc, LANES), lambda b, k: (b, k, 0))
    xs = jax.ShapeDtypeStruct((ngb * NW, L, LANES), F32)
    st = pltpu.VMEM((NW, SUBLANES, LANES), F32)
    return pl.pallas_call(
        body,
        grid=(ngb, nc),
        in_specs=[pl.BlockSpec((tc, LANES), lambda b, k: (k, b + ucol)), mat_b, mat_b, mat_c, mat_c, a_spec, a_spec,
                  pl.BlockSpec((1, LANES), lambda b, k: (0, b))],
        out_specs=(pl.BlockSpec((tc, LANES), lambda b, k: (k, b)), x_spec, x_spec),
        out_shape=(jax.ShapeDtypeStruct((L, ngb * LANES), F32), xs, xs),
        scratch_shapes=[pltpu.VMEM((NW, tc, LANES), F32), pltpu.VMEM((NW, tc, LANES), F32), st, st, st, st],
        compiler_params=_params(("arbitrary", "arbitrary")),
        name="ssm_fwd",
    )(z, br_m, bi_m, cr_m, ci_m, a_re3, a_im3, d_row)


def _ssm_bwd(dy, z, u_off, xs_r, xs_i, br_m, bi_m, cr_m, ci_m, a_re3, a_im3, d_row):
    L = z.shape[0]
    ngb = br_m.shape[0]
    tc, seg, nc, nsq = _ssm_dims(L)
    ucol = u_off // LANES

    def body(dy_ref, u_ref, xr_ref, xi_ref, br_ref, bi_ref, cr_ref, ci_ref, ar_ref, ai_ref, d_ref,
             du_ref, gd_ref, gar_ref, gai_ref, gbr_ref, gbi_ref, gcr_ref, gci_ref,
             gr_s, gi_s, car_r, car_i, ini_r, ini_i, acc_r, acc_i):
        k = pl.program_id(1)

        @pl.when(k == 0)
        def _():
            for ref in (car_r, car_i, acc_r, acc_i, gd_ref, gbr_ref, gbi_ref, gcr_ref, gci_ref):
                ref[...] = jnp.zeros_like(ref)

        dy_v = dy_ref[...]
        u = u_ref[...]
        g_re = _dot_nt(dy_v, cr_ref[0])
        g_im = -_dot_nt(dy_v, ci_ref[0])
        for w in range(NW):
            gr_s[w] = g_re[:, w * LANES:(w + 1) * LANES]
            gi_s[w] = g_im[:, w * LANES:(w + 1) * LANES]
        ar = [jnp.broadcast_to(ar_ref[w], (SUBLANES, LANES)) for w in range(NW)]
        ai = [jnp.broadcast_to(ai_ref[w], (SUBLANES, LANES)) for w in range(NW)]

        def step1(ii, carry):
            xr, xi = carry
            rows = pl.ds(seg - 1 - ii, SUBLANES, stride=seg)
            nr = tuple(ar[w] * xr[w] + ai[w] * xi[w] + gr_s[w, rows, :] for w in range(NW))
            ni = tuple(ar[w] * xi[w] - ai[w] * xr[w] + gi_s[w, rows, :] for w in range(NW))
            return nr, ni

        zero = tuple(jnp.zeros((SUBLANES, LANES), F32) for _ in range(NW))
        er, ei = lax.fori_loop(0, seg, step1, (zero, zero), unroll=2)
        for w in range(NW):
            pr_, pi_ = _cpow(ar[w][0:1], ai[w][0:1], nsq)
            sr, si = car_r[w, 0:1, :], car_i[w, 0:1, :]
            for j in reversed(range(SUBLANES)):
                ini_r[w, j:j + 1, :] = sr
                ini_i[w, j:j + 1, :] = si
                sr, si = (pr_ * sr + pi_ * si + er[w][j:j + 1], pr_ * si - pi_ * sr + ei[w][j:j + 1])
            car_r[w, 0:1, :] = sr
            car_i[w, 0:1, :] = si

        def step2(ii, carry):
            gxr, gxi, acr, aci = carry
            rows = pl.ds(seg - 1 - ii, SUBLANES, stride=seg)
            nr, ni, nar, nai = [], [], [], []
            for w in range(NW):
                xr_t, xi_t = xr_ref[w, rows, :], xi_ref[w, rows, :]
                nar.append(acr[w] + gxr[w] * xr_t + gxi[w] * xi_t)
                nai.append(aci[w] + gxi[w] * xr_t - gxr[w] * xi_t)
                r = ar[w] * gxr[w] + ai[w] * gxi[w] + gr_s[w, rows, :]
                m = ar[w] * gxi[w] - ai[w] * gxr[w] + gi_s[w, rows, :]
                gr_s[w, rows, :] = r
                gi_s[w, rows, :] = m
                nr.append(r)
                ni.append(m)
            return tuple(nr), tuple(ni), tuple(nar), tuple(nai)

        init = (tuple(ini_r[w] for w in range(NW)), tuple(ini_i[w] for w in range(NW)),
                tuple(acc_r[w] for w in range(NW)), tuple(acc_i[w] for w in range(NW)))
        _, _, acr, aci = lax.fori_loop(0, seg, step2, init, unroll=2)
        du = d_ref[...] * dy_v
        for w in range(NW):
            sl = slice(w * LANES, (w + 1) * LANES)
            acc_r[w] = acr[w]
            acc_i[w] = aci[w]
            gxr_w, gxi_w = gr_s[w], gi_s[w]
            du = du + _dot_nt(gxr_w, br_ref[0, :, sl]) + _dot_nt(gxi_w, bi_ref[0, :, sl])
            gbr_ref[0, :, sl] += _dot_tn(u, gxr_w)
            gbi_ref[0, :, sl] += _dot_tn(u, gxi_w)
            gcr_ref[0, sl, :] += _dot_tn(xr_ref[w], dy_v)
            gci_ref[0, sl, :] += _dot_tn(-xi_ref[w], dy_v)
        du_ref[...] = du.astype(BF16)
        gd_ref[...] += jnp.sum(dy_v * u, axis=0, keepdims=True)

        @pl.when(k == nc - 1)
        def _():
            for w in range(NW):
                gar_ref[w] = jnp.sum(acc_r[w], axis=0, keepdims=True)
                gai_ref[w] = jnp.sum(acc_i[w], axis=0, keepdims=True)

    rk = lambda k: nc - 1 - k
    mat_b = pl.BlockSpec((1, LANES, NW * LANES), lambda b, k: (b, 0, 0))
    mat_c = pl.BlockSpec((1, NW * LANES, LANES), lambda b, k: (b, 0, 0))
    a_spec = pl.BlockSpec((NW, 1, LANES), lambda b, k: (b, 0, 0))
    x_spec = pl.BlockSpec((NW, tc, LANES), lambda b, k: (b, rk(k), 0))
    st = pltpu.VMEM((NW, SUBLANES, LANES), F32)
    big = pltpu.VMEM((NW, tc, LANES), F32)
    return pl.pallas_call(
        body,
        grid=(ngb, nc),
        in_specs=[pl.BlockSpec((tc, LANES), lambda b, k: (rk(k), b)),
                  pl.BlockSpec((tc, LANES), lambda b, k: (rk(k), b + ucol)),
                  x_spec, x_spec, mat_b, mat_b, mat_c, mat_c, a_spec, a_spec,
                  pl.BlockSpec((1, LANES), lambda b, k: (0, b))],
        out_specs=(pl.BlockSpec((tc, LANES), lambda b, k: (rk(k), b)),
                   pl.BlockSpec((1, LANES), lambda b, k: (0, b)), a_spec, a_spec, mat_b, mat_b, mat_c, mat_c),
        out_shape=(jax.ShapeDtypeStruct((L, ngb * LANES), BF16),
                   jax.ShapeDtypeStruct((1, ngb * LANES), F32),
                   jax.ShapeDtypeStruct((ngb * NW, 1, LANES), F32), jax.ShapeDtypeStruct((ngb * NW, 1, LANES), F32),
                   jax.ShapeDtypeStruct(br_m.shape, F32), jax.ShapeDtypeStruct(br_m.shape, F32),
                   jax.ShapeDtypeStruct(cr_m.shape, F32), jax.ShapeDtypeStruct(cr_m.shape, F32)),
        scratch_shapes=[big, big, st, st, st, st, st, st],
        compiler_params=_params(("arbitrary", "arbitrary")),
        name="ssm_bwd",
    )(dy, z, xs_r, xs_i, br_m, bi_m, cr_m, ci_m, a_re3, a_im3, d_row)


def _block_diag_in(bb, ngb):
    t = bb.reshape(ngb, GROUPS_PER_BLOCK, SSM_P, SSM_GC).transpose(0, 1, 3, 2)
    eye = jnp.eye(GROUPS_PER_BLOCK, dtype=F32)
    m = t[:, :, :, None, :] * eye[None, :, None, :, None]
    return m.reshape(ngb, GROUPS_PER_BLOCK * SSM_GC, GROUPS_PER_BLOCK * SSM_P)


def _block_diag_out(c, ngb):
    t = c.reshape(ngb, GROUPS_PER_BLOCK, SSM_GC, SSM_P).transpose(0, 1, 3, 2)
    eye = jnp.eye(GROUPS_PER_BLOCK, dtype=F32)
    m = t[:, :, :, None, :] * eye[None, :, None, :, None]
    return m.reshape(ngb, GROUPS_PER_BLOCK * SSM_P, GROUPS_PER_BLOCK * SSM_GC)


def _diag_in(m, ngb):
    m5 = m.reshape(ngb, GROUPS_PER_BLOCK, SSM_GC, GROUPS_PER_BLOCK, SSM_P)
    d = jnp.diagonal(m5, axis1=1, axis2=3)
    return d.transpose(0, 3, 2, 1).reshape(ngb * GROUPS_PER_BLOCK * SSM_P, SSM_GC)


def _diag_out(m, ngb):
    m5 = m.reshape(ngb, GROUPS_PER_BLOCK, SSM_P, GROUPS_PER_BLOCK, SSM_GC)
    d = jnp.diagonal(m5, axis1=1, axis2=3)
    return d.transpose(0, 3, 2, 1).reshape(ngb * GROUPS_PER_BLOCK, SSM_GC, SSM_P)


_ANY = pl.BlockSpec(memory_space=pl.ANY)


def _all_gather(shards, name):
    n = len(shards)

    def body(*refs):
        ins, outs = refs[:n], refs[n:2 * n]
        send_sems, recv_sems, local_sems = refs[2 * n:]
        x, y, c = lax.axis_index("x"), lax.axis_index("y"), lax.axis_index("c")
        me, sibling = (x, y, c), (x, y, 1 - c)
        chips = [(1 - x, y), (x, 1 - y), (1 - x, 1 - y)]

        def copy(a, k, block, to, src=None):
            s = 4 * block[0] + 2 * block[1] + block[2]
            return pltpu.make_async_remote_copy(
                src_ref=outs[a].at[s] if src is None else src, dst_ref=outs[a].at[s],
                send_sem=send_sems.at[7 * a + k], recv_sem=recv_sems.at[7 * a + k],
                device_id=to, device_id_type=MESH)

        started = []
        for a in range(n):
            mine = pltpu.make_async_copy(ins[a], outs[a].at[4 * x + 2 * y + c], local_sems.at[a])
            mine.start()
            started.append(mine)
        sends = []
        for a in range(n):
            first = [copy(a, 0, me, sibling, src=ins[a])]
            first += [copy(a, 1 + j, me, (*chip, c), src=ins[a]) for j, chip in enumerate(chips)]
            for cp in first:
                cp.start()
            sends += first
        for a in range(n):
            for j, chip in enumerate(chips):
                copy(a, 1 + j, (*chip, c), me).wait_recv()
                fwd = copy(a, 4 + j, (*chip, c), sibling)
                fwd.start()
                sends.append(fwd)
        for a in range(n):
            copy(a, 0, sibling, me).wait_recv()
            for j, chip in enumerate(chips):
                copy(a, 4 + j, (*chip, 1 - c), me).wait_recv()
        for cp in sends:
            cp.wait_send()
        for mine in started:
            mine.wait()

    return pl.pallas_call(
        body,
        in_specs=[_ANY] * n,
        out_specs=tuple([_ANY] * n),
        out_shape=tuple(jax.ShapeDtypeStruct((N_DEV,) + s.shape, s.dtype) for s in shards),
        scratch_shapes=[pltpu.SemaphoreType.DMA((7 * n,)), pltpu.SemaphoreType.DMA((7 * n,)),
                        pltpu.SemaphoreType.DMA((n,))],
        compiler_params=pltpu.CompilerParams(has_side_effects=True),
        name=name,
    )(*shards)


def _exchange_sibling(parts, name):
    n = len(parts)

    def body(*refs):
        ins, outs = refs[:n], refs[n:2 * n]
        send_sems, recv_sems = refs[2 * n:]
        x, y, c = lax.axis_index("x"), lax.axis_index("y"), lax.axis_index("c")
        copies = []
        for a in range(n):
            for q in range(4):
                cp = pltpu.make_async_remote_copy(
                    src_ref=ins[a].at[2 * q + (1 - c)], dst_ref=outs[a].at[q],
                    send_sem=send_sems.at[4 * a + q], recv_sem=recv_sems.at[4 * a + q],
                    device_id=(x, y, 1 - c), device_id_type=MESH)
                cp.start()
                copies.append(cp)
        for cp in copies:
            cp.wait()

    return pl.pallas_call(
        body,
        in_specs=[_ANY] * n,
        out_specs=tuple([_ANY] * n),
        out_shape=tuple(jax.ShapeDtypeStruct((4,) + p.shape[1:], p.dtype) for p in parts),
        scratch_shapes=[pltpu.SemaphoreType.DMA((4 * n,)), pltpu.SemaphoreType.DMA((4 * n,))],
        compiler_params=pltpu.CompilerParams(has_side_effects=True),
        name=name,
    )(*parts)


def _exchange_chips(parts, name):
    n = len(parts)

    def body(*refs):
        ins, outs = refs[:n], refs[n:2 * n]
        send_sems, recv_sems = refs[2 * n:]
        x, y, c = lax.axis_index("x"), lax.axis_index("y"), lax.axis_index("c")
        chips = [(1 - x, y), (x, 1 - y), (1 - x, 1 - y)]
        copies = []
        for a in range(n):
            for j, (px, py) in enumerate(chips):
                cp = pltpu.make_async_remote_copy(
                    src_ref=ins[a].at[2 * px + py], dst_ref=outs[a].at[j],
                    send_sem=send_sems.at[3 * a + j], recv_sem=recv_sems.at[3 * a + j],
                    device_id=(px, py, c), device_id_type=MESH)
                cp.start()
                copies.append(cp)
        for cp in copies:
            cp.wait()

    return pl.pallas_call(
        body,
        in_specs=[_ANY] * n,
        out_specs=tuple([_ANY] * n),
        out_shape=tuple(jax.ShapeDtypeStruct((3,) + p.shape[1:], p.dtype) for p in parts),
        scratch_shapes=[pltpu.SemaphoreType.DMA((3 * n,)), pltpu.SemaphoreType.DMA((3 * n,))],
        compiler_params=pltpu.CompilerParams(has_side_effects=True),
        name=name,
    )(*parts)


def _sibling_add(part, recv, name):
    _, R, C = part.shape
    tr = _pick(R, (256, 128))
    c = lax.axis_index("c")

    def body(c_ref, p_ref, r_ref, o_ref):
        o_ref[...] = p_ref[...] + r_ref[...]

    return pl.pallas_call(
        body,
        grid_spec=pltpu.PrefetchScalarGridSpec(
            num_scalar_prefetch=1,
            grid=(4, R // tr),
            in_specs=[pl.BlockSpec((1, tr, C), lambda q, i, c_ref: (2 * q + c_ref[0], i, 0)),
                      pl.BlockSpec((1, tr, C), lambda q, i, c_ref: (q, i, 0))],
            out_specs=pl.BlockSpec((1, tr, C), lambda q, i, c_ref: (q, i, 0)),
        ),
        out_shape=jax.ShapeDtypeStruct((4, R, C), F32),
        compiler_params=_params(("parallel", "parallel")),
        name=name,
    )(jnp.reshape(c, (1,)).astype(jnp.int32), part, recv)


def _adamw(w, g, m, v):
    m = ADAM_B1 * m + (1.0 - ADAM_B1) * g
    v = ADAM_B2 * v + (1.0 - ADAM_B2) * (g * g)
    m_hat = m / (1.0 - ADAM_B1 ** ADAM_STEP)
    v_hat = v / (1.0 - ADAM_B2 ** ADAM_STEP)
    delta = -ADAM_LR * (m_hat / (jnp.sqrt(v_hat) + ADAM_EPS) + ADAM_WD * w)
    return delta, m, v


def _adam_big(t, recv, w, m, v, name):
    _, R, C = t.shape
    tr = _pick(R, (256, 128))
    chip = 2 * lax.axis_index("x") + lax.axis_index("y")

    def body(q_ref, t_ref, r_ref, w_ref, m_ref, v_ref, g_ref, d_ref, nm_ref, nv_ref):
        g = t_ref[0] + r_ref[0] + r_ref[1] + r_ref[2]
        g_ref[...] = g
        d_ref[...], nm_ref[...], nv_ref[...] = _adamw(w_ref[...], g, m_ref[...], v_ref[...])

    blk = pl.BlockSpec((tr, C), lambda i, q_ref: (i, 0))
    o = jax.ShapeDtypeStruct((R, C), F32)
    return pl.pallas_call(
        body,
        grid_spec=pltpu.PrefetchScalarGridSpec(
            num_scalar_prefetch=1,
            grid=(R // tr,),
            in_specs=[pl.BlockSpec((1, tr, C), lambda i, q_ref: (q_ref[0], i, 0)),
                      pl.BlockSpec((3, tr, C), lambda i, q_ref: (0, i, 0)), blk, blk, blk],
            out_specs=(blk, blk, blk, blk),
        ),
        out_shape=(o, o, o, o),
        compiler_params=_params(("parallel",)),
        name=name,
    )(jnp.reshape(chip, (1,)).astype(jnp.int32), t, recv, w, m, v)


def _small_allreduce_adam(gbuf, wbuf, mbuf, vbuf):
    R = gbuf.shape[0]

    def body(g_ref, w_ref, m_ref, v_ref, gs_ref, d_ref, nm_ref, nv_ref, slots, send_sems, recv_sems):
        x, y, c = lax.axis_index("x"), lax.axis_index("y"), lax.axis_index("c")
        me = 4 * x + 2 * y + c
        copies = []
        for k in range(1, N_DEV):
            fx, fy, fc = (k >> 2) & 1, (k >> 1) & 1, k & 1
            px = x + fx - 2 * x * fx
            py = y + fy - 2 * y * fy
            pc = c + fc - 2 * c * fc
            cp = pltpu.make_async_remote_copy(
                src_ref=g_ref, dst_ref=slots.at[me],
                send_sem=send_sems.at[k - 1], recv_sem=recv_sems.at[k - 1],
                device_id=(px, py, pc), device_id_type=MESH)
            cp.start()
            copies.append((cp, 4 * px + 2 * py + pc))
        slots[me] = g_ref[...]
        for k, (cp, src) in enumerate(copies):
            pltpu.make_async_remote_copy(
                src_ref=g_ref, dst_ref=slots.at[src], send_sem=send_sems.at[k], recv_sem=recv_sems.at[k],
                device_id=(x, y, c), device_id_type=MESH).wait_recv()
        for cp, _ in copies:
            cp.wait_send()
        g = slots[0]
        for s in range(1, N_DEV):
            g = g + slots[s]
        gs_ref[...] = g
        d_ref[...], nm_ref[...], nv_ref[...] = _adamw(w_ref[...], g, m_ref[...], v_ref[...])

    o = jax.ShapeDtypeStruct((R, LANES), F32)
    vm = pl.BlockSpec(memory_space=pltpu.VMEM)
    return pl.pallas_call(
        body,
        in_specs=[vm, vm, vm, vm],
        out_specs=(vm, vm, vm, vm),
        out_shape=(o, o, o, o),
        scratch_shapes=[pltpu.VMEM((N_DEV, R, LANES), F32), pltpu.SemaphoreType.DMA((N_DEV - 1,)),
                        pltpu.SemaphoreType.DMA((N_DEV - 1,))],
        compiler_params=pltpu.CompilerParams(vmem_limit_bytes=VMEM_LIMIT, has_side_effects=True),
        name="small_allreduce_adam",
    )(gbuf, wbuf, mbuf, vbuf)


def _pack(items):
    rows, spans, r0 = [], [], 0
    for a in items:
        n = a.size
        nr = -(-n // LANES)
        rows.append(jnp.pad(a.reshape(-1).astype(F32), (0, nr * LANES - n)).reshape(nr, LANES))
        spans.append((r0, nr, a.shape))
        r0 += nr
    pad = -r0 % SUBLANES
    if pad:
        rows.append(jnp.zeros((pad, LANES), F32))
    return jnp.concatenate(rows, axis=0), spans


def _unpack(buf, spans):
    return [buf[r0:r0 + nr].reshape(-1)[:math.prod(shape)].reshape(shape) for r0, nr, shape in spans]


def kernel(x, norm_mix_pre, norm_mix_post, norm_mlp_pre, norm_mlp_post, w_in, sinks, lam_re, lam_im, log_dt, b_re, b_im, c_re, c_im, d_skip, w_glu, w_branch, w_out, w_up, w_down, loss_target, m_norm_mix_pre, m_norm_mix_post, m_norm_mlp_pre, m_norm_mlp_post, m_w_in, m_sinks, m_lam_re, m_lam_im, m_log_dt, m_b_re, m_b_im, m_c_re, m_c_im, m_d_skip, m_w_glu, m_w_branch, m_w_out, m_w_up, m_w_down, v_norm_mix_pre, v_norm_mix_post, v_norm_mlp_pre, v_norm_mlp_post, v_w_in, v_sinks, v_lam_re, v_lam_im, v_log_dt, v_b_re, v_b_im, v_c_re, v_c_im, v_d_skip, v_w_glu, v_w_branch, v_w_out, v_w_up, v_w_down):
    _, L, D = x.shape
    xs = x[0]
    tgt = loss_target[0]
    ssm_w = D // 2
    n_groups = ssm_w // SSM_GC
    ngb = n_groups // GROUPS_PER_BLOCK
    n_state = n_groups * SSM_P
    d_ff = w_up.shape[2] * N_DEV
    o_k, o_v, o_u = Q_W, Q_W + KV_W, Q_W + 2 * KV_W
    o_ga = o_u + ssm_w
    o_gs = o_ga + D

    big = {"w_in": w_in[0], "w_glu": w_glu[0], "w_branch": w_branch[0], "w_out": w_out[0],
           "w_up": w_up[0], "w_down": w_down[0]}
    col_sharded = ("w_in", "w_glu", "w_up")
    names = list(big)
    gathered = _all_gather([big[k].astype(BF16) for k in names], "all_gather_weights")
    full = {}
    for k, g in zip(names, gathered):
        _, r, c = g.shape
        full[k] = g.transpose(1, 0, 2).reshape(r, N_DEV * c) if k in col_sharded else g.reshape(N_DEV * r, c)
    wb_a, wb_s = full["w_branch"][:Q_W], full["w_branch"][Q_W:]

    col = lambda a: a.reshape(n_state, 1)
    lr_c, li_c = col(lam_re[0]), col(lam_im[0])
    ldt_c = jnp.repeat(log_dt[0], SSM_P).reshape(n_state, 1)
    b_re_c, b_im_c = b_re[0].reshape(n_state, SSM_GC), b_im[0].reshape(n_state, SSM_GC)
    a_re, a_im, bb_re, bb_im = _disc_fwd(lr_c, li_c, ldt_c, b_re_c, b_im_c)
    a_re3 = a_re.reshape(n_state // LANES, 1, LANES)
    a_im3 = a_im.reshape(n_state // LANES, 1, LANES)
    br_m = _block_diag_in(bb_re, ngb).astype(BF16)
    bi_m = _block_diag_in(bb_im, ngb).astype(BF16)
    cr_m = _block_diag_out(c_re[0], ngb).astype(BF16)
    ci_m = _block_diag_out(c_im[0], ngb).astype(BF16)
    d_row = d_skip[0].reshape(1, ssm_w)
    cos_t, sin_t = _rope_tables(L)

    h = _rms_pre(xs, norm_mix_pre)
    z = _mm(h, full["w_in"], mode="nn", name="mm_z")
    o_attn = _attn_fwd(z, cos_t, sin_t, sinks)
    y_pre, xs_r, xs_i = _ssm_fwd(z, o_u, br_m, bi_m, cr_m, ci_m, a_re3, a_im3, d_row)
    gy = _ew(lambda y: (_gelu(y),), [(y_pre, 0)], (BF16,), rows=L, ncols=ssm_w, name="gelu")
    zg = _mm(gy, full["w_glu"], mode="nn", name="mm_zg")
    o_ssm = _ew(lambda a, b: (a * _sigmoid(b),), [(zg, 0), (zg, ssm_w)], (BF16,), rows=L, ncols=ssm_w, name="glu")
    y_attn = _mm(o_attn, wb_a, mode="nn", name="mm_y_attn")
    y_ssm = _mm(o_ssm, wb_s, mode="nn", name="mm_y_ssm")
    mix = _ew(lambda ga, gs, ya, ys: (_sigmoid(ga) * ya + _sigmoid(gs) * ys,),
              [(z, o_ga), (z, o_gs), (y_attn, 0), (y_ssm, 0)], (BF16,), rows=L, ncols=D, name="mix")
    mixed = _mm(mix, full["w_out"], mode="nn", name="mm_mixed")
    x1, h2 = _post_pre(xs, mixed, norm_mix_post, norm_mlp_pre)

    def relu_sq(acc):
        a = jnp.maximum(acc, 0.0)
        return a, a * a

    act, act2 = _mm(h2, full["w_up"], mode="nn", name="mm_up", out_dtypes=(BF16, BF16), epi=relu_sq)
    dn = _mm(act2, full["w_down"], mode="nn", name="mm_down")
    dx2, d_dn, dg_mlp_post, loss_part = _loss_bwd(x1, dn, norm_mlp_post, tgt)

    d_pre = _mm(d_dn, full["w_down"], mode="nt", name="mm_d_act", out_dtypes=(BF16,),
                epi=lambda acc, a: (acc * (2.0 * a.astype(F32)),), extras=(act,))
    gw_down = _mm(act2, d_dn, mode="tn", name="mm_gw_down")
    dh2 = _mm(d_pre, full["w_up"], mode="nt", name="mm_dh2")
    gw_up = _mm(h2, d_pre, mode="tn", name="mm_gw_up")
    dx1, d_mixed, dg_mlp_pre, dg_mix_post = _norm_bwd_pair(x1, dh2, dx2, mixed, norm_mlp_pre, norm_mix_post)
    d_mix = _mm(d_mixed, full["w_out"], mode="nt", name="mm_d_mix")
    gw_out = _mm(mix, d_mixed, mode="tn", name="mm_gw_out")

    def gate_bwd(dm, ga, gs, ya, ys):
        sa, ss = _sigmoid(ga), _sigmoid(gs)
        return dm * sa, dm * ss, dm * ya * sa * (1.0 - sa), dm * ys * ss * (1.0 - ss)

    d_ya, d_ys, d_zga, d_zgs = _ew(gate_bwd, [(d_mix, 0), (z, o_ga), (z, o_gs), (y_attn, 0), (y_ssm, 0)],
                                   (BF16, BF16, BF16, BF16), rows=L, ncols=D, name="gate_bwd")
    d_o_attn = _mm(d_ya, wb_a, mode="nt", name="mm_d_o_attn")
    gwb_a = _mm(o_attn, d_ya, mode="tn", name="mm_gwb_a")
    d_o_ssm = _mm(d_ys, wb_s, mode="nt", name="mm_d_o_ssm")
    gwb_s = _mm(o_ssm, d_ys, mode="tn", name="mm_gwb_s")

    def glu_bwd(do, a, b):
        s = _sigmoid(b)
        return do * s, do * a * s * (1.0 - s)

    d_zg_a, d_zg_b = _ew(glu_bwd, [(d_o_ssm, 0), (zg, 0), (zg, ssm_w)], (BF16, BF16), rows=L, ncols=ssm_w, name="glu_bwd")
    d_zg = jnp.concatenate([d_zg_a, d_zg_b], axis=1)
    dy_pre = _mm(d_zg, full["w_glu"], mode="nt", name="mm_d_gy",
                 epi=lambda acc, y: (acc * _gelu_grad(y),), extras=(y_pre,))
    gw_glu = _mm(gy, d_zg, mode="tn", name="mm_gw_glu")
    (du, g_dskip, g_ar3, g_ai3, g_br_m, g_bi_m, g_cr_m, g_ci_m) = _ssm_bwd(
        dy_pre, z, o_u, xs_r, xs_i, br_m, bi_m, cr_m, ci_m, a_re3, a_im3, d_row)
    g_lr, g_li, g_ldt, g_b_re, g_b_im = _disc_bwd(
        lr_c, li_c, ldt_c, b_re_c, b_im_c, g_ar3.reshape(n_state, 1), g_ai3.reshape(n_state, 1),
        _diag_in(g_br_m, ngb), _diag_in(g_bi_m, ngb))
    dq, dkc, dkp, dvc, dvp, dsink_rows = _attn_bwd(z, d_o_attn, cos_t, sin_t, sinks)
    dk, dv = _kv_combine(dkc, dkp, dvc, dvp)
    dz = jnp.concatenate([dq, dk, dv, du, d_zga, d_zgs], axis=1)
    dh = _mm(dz, full["w_in"], mode="nt", name="mm_dh")
    gw_in = _mm(h, dz, mode="tn", name="mm_gw_in")
    grad_x, dg_mix_pre = _final_bwd(xs, dh, dx1, norm_mix_pre)

    gfull = {"w_in": gw_in, "w_glu": gw_glu, "w_branch": jnp.concatenate([gwb_a, gwb_s], axis=0),
             "w_out": gw_out, "w_up": gw_up, "w_down": gw_down}
    parts = []
    for k in names:
        r, c = big[k].shape
        g = gfull[k]
        parts.append(g.reshape(r, N_DEV, c).transpose(1, 0, 2) if k in col_sharded else g.reshape(N_DEV, r, c))
    from_sibling = _exchange_sibling(parts, "rs_sibling")
    chip_sums = [_sibling_add(p, r, "rs_add_" + k) for k, p, r in zip(names, parts, from_sibling)]
    from_chips = _exchange_chips(chip_sums, "rs_chips")
    moments = {"w_in": (m_w_in, v_w_in), "w_glu": (m_w_glu, v_w_glu), "w_branch": (m_w_branch, v_w_branch),
               "w_out": (m_w_out, v_w_out), "w_up": (m_w_up, v_w_up), "w_down": (m_w_down, v_w_down)}
    big_out = {}
    for k, t, r in zip(names, chip_sums, from_chips):
        mm_, vv_ = moments[k]
        big_out[k] = [o[None] for o in _adam_big(t, r, big[k], mm_[0], vv_[0], "adam_" + k)]

    dsink = jnp.stack([dsink_rows[:, 0], dsink_rows[:, HEAD_DIM]], axis=1).reshape(1, N_Q_HEADS)
    small_names = ["norm_mix_pre", "norm_mix_post", "norm_mlp_pre", "norm_mlp_post", "sinks", "lam_re", "lam_im",
                   "log_dt", "b_re", "b_im", "c_re", "c_im", "d_skip"]
    small_w = [norm_mix_pre, norm_mix_post, norm_mlp_pre, norm_mlp_post, sinks, lam_re, lam_im, log_dt,
               b_re, b_im, c_re, c_im, d_skip]
    small_m = [m_norm_mix_pre, m_norm_mix_post, m_norm_mlp_pre, m_norm_mlp_post, m_sinks, m_lam_re, m_lam_im,
               m_log_dt, m_b_re, m_b_im, m_c_re, m_c_im, m_d_skip]
    small_v = [v_norm_mix_pre, v_norm_mix_post, v_norm_mlp_pre, v_norm_mlp_post, v_sinks, v_lam_re, v_lam_im,
               v_log_dt, v_b_re, v_b_im, v_c_re, v_c_im, v_d_skip]
    small_g = [dg_mix_pre, dg_mix_post, dg_mlp_pre, dg_mlp_post, dsink,
               g_lr.reshape(lam_re.shape), g_li.reshape(lam_im.shape), g_ldt.reshape(log_dt.shape),
               g_b_re.reshape(b_re.shape), g_b_im.reshape(b_im.shape),
               _diag_out(g_cr_m, ngb).reshape(c_re.shape), _diag_out(g_ci_m, ngb).reshape(c_im.shape),
               g_dskip.reshape(d_skip.shape)]
    zero1 = jnp.zeros((1, 1), F32)
    gbuf, spans = _pack(small_g + [loss_part])
    wbuf, _ = _pack(small_w + [zero1])
    mbuf, _ = _pack(small_m + [zero1])
    vbuf, _ = _pack(small_v + [zero1])
    gs, ds, nms, nvs = [_unpack(b, spans) for b in _small_allreduce_adam(gbuf, wbuf, mbuf, vbuf)]
    loss = gs[-1].reshape(())

    order = ["norm_mix_pre", "norm_mix_post", "norm_mlp_pre", "norm_mlp_post", "w_in", "sinks", "lam_re", "lam_im",
             "log_dt", "b_re", "b_im", "c_re", "c_im", "d_skip", "w_glu", "w_branch", "w_out", "w_up", "w_down"]
    outs = [loss, grad_x[None]]
    for idx, src in enumerate((gs, ds, nms, nvs)):
        for k in order:
            outs.append(big_out[k][idx] if k in big_out else src[small_names.index(k)])
    return tuple(outs)
```

```python
import functools
import math

import jax
import jax.numpy as jnp
from jax import lax
from jax.experimental import pallas as pl
from jax.experimental.pallas import tpu as pltpu

F32 = jnp.float32
BF16 = jnp.bfloat16
MESH = pl.DeviceIdType.MESH

LANES = 128
SUBLANES = 8
VMEM_LIMIT = 56 * 1024 * 1024

HEAD_DIM = 64
N_Q_HEADS = 16
N_KV_HEADS = 2
Q_W = N_Q_HEADS * HEAD_DIM
KV_W = N_KV_HEADS * HEAD_DIM
BLOCK = 128
ROT_DIM = HEAD_DIM // 4
ROPE_THETA = 500000.0
SSM_GC = 16
SSM_P = 64
GROUPS_PER_BLOCK = 8
NW = GROUPS_PER_BLOCK * SSM_P // LANES
EPS = 1e-6
N_DEV = 8

ADAM_LR = 0.001
ADAM_B1 = 0.9
ADAM_B2 = 0.999
ADAM_EPS = 1e-08
ADAM_WD = 0.01
ADAM_STEP = 10


def _params(sem=None):
    return pltpu.CompilerParams(dimension_semantics=sem, vmem_limit_bytes=VMEM_LIMIT)


def _pick(dim, prefs):
    for p in prefs:
        if dim % p == 0:
            return p
    return dim


def _sigmoid(x):
    return 1.0 / (1.0 + jnp.exp(-x))


_GELU_C = math.sqrt(2.0 / math.pi)


def _gelu(x):
    return 0.5 * x * (1.0 + jnp.tanh(_GELU_C * (x + 0.044715 * x * x * x)))


def _gelu_grad(x):
    t = jnp.tanh(_GELU_C * (x + 0.044715 * x * x * x))
    return 0.5 * (1.0 + t) + 0.5 * x * (1.0 - t * t) * _GELU_C * (1.0 + 3.0 * 0.044715 * x * x)


_DIMS = {"nn": (((1,), (0,)), ((), ())), "nt": (((1,), (1,)), ((), ())), "tn": (((0,), (0,)), ((), ()))}


def _mm(a, b, *, mode, name, out_dtypes=(F32,), epi=None, extras=()):
    if mode == "nn":
        (M, K), (K2, N) = a.shape, b.shape
    elif mode == "nt":
        (M, K), (N, K2) = a.shape, b.shape
    else:
        (K, M), (K2, N) = a.shape, b.shape
    assert K == K2, (a.shape, b.shape, mode)
    tm = _pick(M, (1024, 512, 256, 128))
    tn = _pick(N, (1024, 1280, 640, 512, 384, 256, 128))
    tk = K if K <= 2048 else _pick(K, (2048, 1280, 1024, 640, 512, 256, 128))
    nk = K // tk
    n_ex = len(extras)
    n_out = len(out_dtypes)

    def body(*refs):
        a_ref, b_ref = refs[0], refs[1]
        ex_refs = refs[2:2 + n_ex]
        o_refs = refs[2 + n_ex:2 + n_ex + n_out]

        def finish(r):
            outs = (r,) if epi is None else epi(r, *[e[...] for e in ex_refs])
            for o_ref, o in zip(o_refs, outs):
                o_ref[...] = o.astype(o_ref.dtype)

        part = lax.dot_general(a_ref[...].astype(BF16), b_ref[...].astype(BF16), _DIMS[mode],
                               preferred_element_type=F32)
        if nk == 1:
            finish(part)
            return
        acc = refs[-1]
        k = pl.program_id(2)

        @pl.when(k == 0)
        def _():
            acc[...] = part

        @pl.when((k > 0) & (k < nk - 1))
        def _():
            acc[...] += part

        @pl.when(k == nk - 1)
        def _():
            finish(acc[...] + part)

    if mode == "nn":
        a_spec = pl.BlockSpec((tm, tk), lambda i, j, k: (i, k))
        b_spec = pl.BlockSpec((tk, tn), lambda i, j, k: (k, j))
    elif mode == "nt":
        a_spec = pl.BlockSpec((tm, tk), lambda i, j, k: (i, k))
        b_spec = pl.BlockSpec((tn, tk), lambda i, j, k: (j, k))
    else:
        a_spec = pl.BlockSpec((tk, tm), lambda i, j, k: (k, i))
        b_spec = pl.BlockSpec((tk, tn), lambda i, j, k: (k, j))
    o_spec = pl.BlockSpec((tm, tn), lambda i, j, k: (i, j))
    res = pl.pallas_call(
        body,
        grid=(M // tm, N // tn, nk),
        in_specs=[a_spec, b_spec] + [o_spec] * n_ex,
        out_specs=tuple([o_spec] * n_out),
        out_shape=tuple(jax.ShapeDtypeStruct((M, N), d) for d in out_dtypes),
        scratch_shapes=[pltpu.VMEM((tm, tn), F32)] if nk > 1 else [],
        compiler_params=_params(("parallel", "parallel", "arbitrary")),
        name=name,
    )(a, b, *extras)
    return res[0] if n_out == 1 else res


def _ew(fn, ins, out_dtypes, *, rows, ncols, name):
    g = ncols
    for _, off in ins:
        g = math.gcd(g, off)
    tc = _pick(g, (512, 256, 128))
    tr = _pick(rows, (1024, 512, 256, 128))
    n_in = len(ins)

    def body(*refs):
        outs = fn(*[r[...] for r in refs[:n_in]])
        for o_ref, o in zip(refs[n_in:], outs):
            o_ref[...] = o.astype(o_ref.dtype)

    def in_spec(off):
        ob = off // tc
        return pl.BlockSpec((tr, tc), lambda i, j: (i, j + ob))

    o_spec = pl.BlockSpec((tr, tc), lambda i, j: (i, j))
    res = pl.pallas_call(
        body,
        grid=(rows // tr, ncols // tc),
        in_specs=[in_spec(off) for _, off in ins],
        out_specs=tuple([o_spec] * len(out_dtypes)),
        out_shape=tuple(jax.ShapeDtypeStruct((rows, ncols), d) for d in out_dtypes),
        compiler_params=_params(("parallel", "parallel")),
        name=name,
    )(*[arr for arr, _ in ins])
    return res[0] if len(out_dtypes) == 1 else res


def _rstd(x):
    return lax.rsqrt(jnp.mean(x * x, axis=-1, keepdims=True) + EPS)


def _norm_bwd(x, r, g, dy):
    t = dy * g
    dx = r * t - x * (r * r * r) * jnp.mean(t * x, axis=-1, keepdims=True)
    return dx, dy * x * r


def _row_call(body, ins, row_ins, outs, acc_outs, *, rows, width, name):
    tr = _pick(rows, (256, 128))
    t_spec = pl.BlockSpec((tr, width), lambda i: (i, 0))
    r_spec = pl.BlockSpec((1, width), lambda i: (0, 0))
    return pl.pallas_call(
        body,
        grid=(rows // tr,),
        in_specs=[t_spec] * len(ins) + [r_spec] * len(row_ins),
        out_specs=tuple([t_spec] * len(outs) + [pl.BlockSpec(s, lambda i: (0, 0)) for s in acc_outs]),
        out_shape=tuple([jax.ShapeDtypeStruct((rows, width), d) for d in outs]
                        + [jax.ShapeDtypeStruct(s, F32) for s in acc_outs]),
        compiler_params=_params(("arbitrary",)),
        name=name,
    )(*ins, *row_ins)


def _rms_pre(x, g):
    L, D = x.shape

    def body(x_ref, g_ref, h_ref):
        xv = x_ref[...]
        h_ref[...] = (xv * _rstd(xv) * g_ref[...]).astype(BF16)

    return _row_call(body, [x], [g], [BF16], [], rows=L, width=D, name="rms_pre")[0]


def _post_pre(x, mixed, g_post, g_pre):
    L, D = x.shape

    def body(x_ref, m_ref, gp_ref, gq_ref, x1_ref, h2_ref):
        mv = m_ref[...]
        x1 = x_ref[...] + mv * _rstd(mv) * gp_ref[...]
        x1_ref[...] = x1
        h2_ref[...] = (x1 * _rstd(x1) * gq_ref[...]).astype(BF16)

    return _row_call(body, [x, mixed], [g_post, g_pre], [F32, BF16], [], rows=L, width=D, name="post_pre")


def _loss_bwd(x1, dn, g_post, target):
    L, D = x1.shape

    def body(x1_ref, dn_ref, t_ref, g_ref, dx2_ref, ddn_ref, dg_ref, loss_ref):
        @pl.when(pl.program_id(0) == 0)
        def _():
            dg_ref[...] = jnp.zeros_like(dg_ref)
            loss_ref[...] = jnp.zeros_like(loss_ref)

        dnv = dn_ref[...]
        g = g_ref[...]
        r = _rstd(dnv)
        err = x1_ref[...] + dnv * r * g - t_ref[...]
        loss_ref[...] += 0.5 * jnp.sum(jnp.mean(err * err, axis=-1, keepdims=True), axis=0, keepdims=True)
        dx2 = err * (1.0 / D)
        dx2_ref[...] = dx2
        ddn, dgr = _norm_bwd(dnv, r, g, dx2)
        ddn_ref[...] = ddn.astype(BF16)
        dg_ref[...] += jnp.sum(dgr, axis=0, keepdims=True)

    return _row_call(body, [x1, dn, target], [g_post], [F32, BF16], [(1, D), (1, 1)],
                     rows=L, width=D, name="loss_bwd")


def _norm_bwd_pair(x1, dh2, dx2, mixed, g_pre, g_post):
    L, D = x1.shape

    def body(x1_ref, dh_ref, dx2_ref, m_ref, gq_ref, gp_ref, dx1_ref, dm_ref, dgq_ref, dgp_ref):
        @pl.when(pl.program_id(0) == 0)
        def _():
            dgq_ref[...] = jnp.zeros_like(dgq_ref)
            dgp_ref[...] = jnp.zeros_like(dgp_ref)

        x1v = x1_ref[...]
        d1, dgq = _norm_bwd(x1v, _rstd(x1v), gq_ref[...], dh_ref[...])
        dx1 = dx2_ref[...] + d1
        dx1_ref[...] = dx1
        mv = m_ref[...]
        dm, dgp = _norm_bwd(mv, _rstd(mv), gp_ref[...], dx1)
        dm_ref[...] = dm.astype(BF16)
        dgq_ref[...] += jnp.sum(dgq, axis=0, keepdims=True)
        dgp_ref[...] += jnp.sum(dgp, axis=0, keepdims=True)

    return _row_call(body, [x1, dh2, dx2, mixed], [g_pre, g_post], [F32, BF16], [(1, D), (1, D)],
                     rows=L, width=D, name="norm_bwd_pair")


def _final_bwd(x, dh, dx1, g_pre):
    L, D = x.shape

    def body(x_ref, dh_ref, dx1_ref, g_ref, gx_ref, dg_ref):
        @pl.when(pl.program_id(0) == 0)
        def _():
            dg_ref[...] = jnp.zeros_like(dg_ref)

        xv = x_ref[...]
        d0, dg = _norm_bwd(xv, _rstd(xv), g_ref[...], dh_ref[...])
        gx_ref[...] = dx1_ref[...] + d0
        dg_ref[...] += jnp.sum(dg, axis=0, keepdims=True)

    return _row_call(body, [x, dh, dx1], [g_pre], [F32], [(1, D)], rows=L, width=D, name="final_bwd")


def _rope_tables(L):
    half = ROT_DIM // 2
    inv = ROPE_THETA ** (-jnp.arange(half, dtype=F32) * 2.0 / ROT_DIM)
    ang = jnp.arange(L, dtype=F32)[:, None] * inv[None, :]
    d = jnp.arange(LANES) % HEAD_DIM
    a = ang[:, d % half]
    cos_t = jnp.where(d[None, :] < ROT_DIM, jnp.cos(a), 1.0)
    sin_t = jnp.where(d[None, :] < half, -jnp.sin(a), jnp.where(d[None, :] < ROT_DIM, jnp.sin(a), 0.0))
    return cos_t.astype(F32), sin_t.astype(F32)


def _lane_lo(shape):
    return lax.broadcasted_iota(jnp.int32, shape, 1) < HEAD_DIM


def _rope(x, cos_t, sin_t):
    d = lax.broadcasted_iota(jnp.int32, x.shape, 1) % HEAD_DIM
    partner = jnp.where(d < ROT_DIM // 2, pltpu.roll(x, LANES - ROT_DIM // 2, 1), pltpu.roll(x, ROT_DIM // 2, 1))
    return x * cos_t + partner * sin_t


def _dup(kv, g):
    sw = pltpu.roll(kv, HEAD_DIM, 1)
    lo = _lane_lo(kv.shape)
    return jnp.where(lo, kv, sw) if g == 0 else jnp.where(lo, sw, kv)


def _attn_mask(n):
    qi = lax.broadcasted_iota(jnp.int32, (BLOCK, 2 * BLOCK), 0)
    kj = lax.broadcasted_iota(jnp.int32, (BLOCK, 2 * BLOCK), 1)
    rel = qi + BLOCK - kj
    return (rel >= 0) & (rel < BLOCK) & ((kj >= BLOCK) | (n > 0))


def _softmax_sink(s, mask, sink):
    s = jnp.where(mask, s, -1e30)
    m = jnp.maximum(jnp.max(s, axis=-1, keepdims=True), sink)
    e = jnp.where(mask, jnp.exp(s - m), 0.0)
    es = jnp.exp(sink - m)
    inv = 1.0 / (jnp.sum(e, axis=-1, keepdims=True) + es)
    return e * inv, es * inv


_NT = (((1,), (1,)), ((), ()))
_TN = (((0,), (0,)), ((), ()))


def _dot(a, b):
    return jnp.dot(a.astype(BF16), b.astype(BF16), preferred_element_type=F32)


def _dot_nt(a, b):
    return lax.dot_general(a.astype(BF16), b.astype(BF16), _NT, preferred_element_type=F32)


def _dot_tn(a, b):
    return lax.dot_general(a.astype(BF16), b.astype(BF16), _TN, preferred_element_type=F32)


def _attn_specs(nb):
    kcol, vcol = Q_W // LANES, Q_W // LANES + 1
    prev = lambda n: jnp.maximum(n - 1, 0)
    return [
        pl.BlockSpec((BLOCK, Q_W), lambda n: (n, 0)),
        pl.BlockSpec((BLOCK, LANES), lambda n: (n, kcol)),
        pl.BlockSpec((BLOCK, LANES), lambda n: (prev(n), kcol)),
        pl.BlockSpec((BLOCK, LANES), lambda n: (n, vcol)),
        pl.BlockSpec((BLOCK, LANES), lambda n: (prev(n), vcol)),
        pl.BlockSpec((BLOCK, LANES), lambda n: (n, 0)),
        pl.BlockSpec((BLOCK, LANES), lambda n: (prev(n), 0)),
        pl.BlockSpec((BLOCK, LANES), lambda n: (n, 0)),
        pl.BlockSpec((BLOCK, LANES), lambda n: (prev(n), 0)),
        pl.BlockSpec(memory_space=pltpu.SMEM),
    ]


def _attn_prep(refs):
    q_ref, kc_ref, kp_ref, vc_ref, vp_ref, cc_ref, cp_ref, sc_ref, sp_ref = refs
    cos_c, sin_c, cos_p, sin_p = cc_ref[...], sc_ref[...], cp_ref[...], sp_ref[...]
    k2 = jnp.concatenate([_rope(kp_ref[...], cos_p, sin_p), _rope(kc_ref[...], cos_c, sin_c)], axis=0)
    v2 = jnp.concatenate([vp_ref[...], vc_ref[...]], axis=0)
    lo2 = _lane_lo(v2.shape)
    kd = [_dup(k2, g).astype(BF16) for g in range(N_KV_HEADS)]
    vd = [_dup(v2, g) for g in range(N_KV_HEADS)]
    va = [jnp.where(lo2, v, 0.0).astype(BF16) for v in vd]
    vb = [jnp.where(lo2, 0.0, v).astype(BF16) for v in vd]
    return cos_c, sin_c, cos_p, sin_p, kd, va, vb


_SCALE = 1.0 / math.sqrt(HEAD_DIM)
_TILES = Q_W // LANES
_TILES_PER_KV = _TILES // N_KV_HEADS


def _attn_fwd(z, cos_t, sin_t, sinks):
    L = z.shape[0]
    nb = L // BLOCK

    def body(q_ref, kc_ref, kp_ref, vc_ref, vp_ref, cc_ref, cp_ref, sc_ref, sp_ref, sink_ref, o_ref):
        n = pl.program_id(0)
        cos_c, sin_c, _, _, kd, va, vb = _attn_prep((q_ref, kc_ref, kp_ref, vc_ref, vp_ref, cc_ref, cp_ref, sc_ref, sp_ref))
        mask = _attn_mask(n)
        lo = _lane_lo((BLOCK, LANES))
        for t in range(_TILES):
            g = t // _TILES_PER_KV
            qt = _rope(q_ref[:, t * LANES:(t + 1) * LANES], cos_c, sin_c) * _SCALE
            pa, _ = _softmax_sink(_dot_nt(jnp.where(lo, qt, 0.0), kd[g]), mask, sink_ref[0, 2 * t])
            pb, _ = _softmax_sink(_dot_nt(jnp.where(lo, 0.0, qt), kd[g]), mask, sink_ref[0, 2 * t + 1])
            o_ref[:, t * LANES:(t + 1) * LANES] = (_dot(pa, va[g]) + _dot(pb, vb[g])).astype(BF16)

    return pl.pallas_call(
        body,
        grid=(nb,),
        in_specs=_attn_specs(nb),
        out_specs=pl.BlockSpec((BLOCK, Q_W), lambda n: (n, 0)),
        out_shape=jax.ShapeDtypeStruct((L, Q_W), BF16),
        compiler_params=_params(("parallel",)),
        name="attn_fwd",
    )(z, z, z, z, z, cos_t, cos_t, sin_t, sin_t, sinks)


def _attn_bwd(z, d_o, cos_t, sin_t, sinks):
    L = z.shape[0]
    nb = L // BLOCK

    def body(q_ref, kc_ref, kp_ref, vc_ref, vp_ref, cc_ref, cp_ref, sc_ref, sp_ref, sink_ref, do_ref,
             dq_ref, dkc_ref, dkp_ref, dvc_ref, dvp_ref, ds_ref):
        n = pl.program_id(0)

        @pl.when(n == 0)
        def _():
            ds_ref[...] = jnp.zeros_like(ds_ref)

        cos_c, sin_c, cos_p, sin_p, kd, va, vb = _attn_prep(
            (q_ref, kc_ref, kp_ref, vc_ref, vp_ref, cc_ref, cp_ref, sc_ref, sp_ref))
        mask = _attn_mask(n)
        lo = _lane_lo((BLOCK, LANES))
        lo2 = _lane_lo((2 * BLOCK, LANES))
        acc_k = [jnp.zeros((2 * BLOCK, LANES), F32) for _ in range(N_KV_HEADS)]
        acc_v = [jnp.zeros((2 * BLOCK, LANES), F32) for _ in range(N_KV_HEADS)]
        for t in range(_TILES):
            g = t // _TILES_PER_KV
            sl = slice(t * LANES, (t + 1) * LANES)
            qt = _rope(q_ref[:, sl], cos_c, sin_c) * _SCALE
            qa, qb = jnp.where(lo, qt, 0.0), jnp.where(lo, 0.0, qt)
            pa, psa = _softmax_sink(_dot_nt(qa, kd[g]), mask, sink_ref[0, 2 * t])
            pb, psb = _softmax_sink(_dot_nt(qb, kd[g]), mask, sink_ref[0, 2 * t + 1])
            dot_ = do_ref[:, sl]
            ot = _dot(pa, va[g]) + _dot(pb, vb[g])
            prod = dot_ * ot
            da = jnp.sum(jnp.where(lo, prod, 0.0), axis=-1, keepdims=True)
            db = jnp.sum(jnp.where(lo, 0.0, prod), axis=-1, keepdims=True)
            dsa = pa * (_dot_nt(dot_, va[g]) - da)
            dsb = pb * (_dot_nt(dot_, vb[g]) - db)
            dqt = jnp.where(lo, _dot(dsa, kd[g]), _dot(dsb, kd[g])) * _SCALE
            dq_ref[:, sl] = _rope(dqt, cos_c, -sin_c).astype(BF16)
            acc_k[g] = acc_k[g] + _dot_tn(dsa, qa) + _dot_tn(dsb, qb)
            acc_v[g] = acc_v[g] + _dot_tn(pa, jnp.where(lo, dot_, 0.0)) + _dot_tn(pb, jnp.where(lo, 0.0, dot_))
            row = jnp.where(_lane_lo((1, LANES)), -jnp.sum(psa * da, axis=0, keepdims=True),
                            -jnp.sum(psb * db, axis=0, keepdims=True))
            ds_ref[t:t + 1, :] += row
        fk = [a + pltpu.roll(a, HEAD_DIM, 1) for a in acc_k]
        fv = [a + pltpu.roll(a, HEAD_DIM, 1) for a in acc_v]
        dk2 = jnp.where(lo2, fk[0], fk[1])
        dv2 = jnp.where(lo2, fv[0], fv[1])
        dkp_ref[...] = _rope(dk2[:BLOCK], cos_p, -sin_p)
        dkc_ref[...] = _rope(dk2[BLOCK:], cos_c, -sin_c)
        dvp_ref[...] = dv2[:BLOCK]
        dvc_ref[...] = dv2[BLOCK:]

    blk = pl.BlockSpec((BLOCK, LANES), lambda n: (n, 0))
    kv = jax.ShapeDtypeStruct((L, LANES), F32)
    return pl.pallas_call(
        body,
        grid=(nb,),
        in_specs=_attn_specs(nb) + [pl.BlockSpec((BLOCK, Q_W), lambda n: (n, 0))],
        out_specs=(pl.BlockSpec((BLOCK, Q_W), lambda n: (n, 0)), blk, blk, blk, blk,
                   pl.BlockSpec((_TILES, LANES), lambda n: (0, 0))),
        out_shape=(jax.ShapeDtypeStruct((L, Q_W), BF16), kv, kv, kv, kv,
                   jax.ShapeDtypeStruct((_TILES, LANES), F32)),
        compiler_params=_params(("arbitrary",)),
        name="attn_bwd",
    )(z, z, z, z, z, cos_t, cos_t, sin_t, sin_t, sinks, d_o)


def _kv_combine(dkc, dkp, dvc, dvp):
    L = dkc.shape[0]
    nb = L // BLOCK

    def body(kc_ref, kp_ref, vc_ref, vp_ref, dk_ref, dv_ref):
        live = jnp.where(pl.program_id(0) + 1 < nb, 1.0, 0.0)
        dk_ref[...] = (kc_ref[...] + live * kp_ref[...]).astype(BF16)
        dv_ref[...] = (vc_ref[...] + live * vp_ref[...]).astype(BF16)

    cur = pl.BlockSpec((BLOCK, LANES), lambda n: (n, 0))
    nxt = pl.BlockSpec((BLOCK, LANES), lambda n: (jnp.minimum(n + 1, nb - 1), 0))
    o = jax.ShapeDtypeStruct((L, LANES), BF16)
    return pl.pallas_call(body, grid=(nb,), in_specs=[cur, nxt, cur, nxt], out_specs=(cur, cur),
                          out_shape=(o, o), compiler_params=_params(("parallel",)), name="kv_combine")(dkc, dkp, dvc, dvp)


def _discretise(lr, li, ldt, br, bi):
    dt = jnp.exp(ldt)
    mag = jnp.exp(lr * dt)
    a_re, a_im = mag * jnp.cos(li * dt), mag * jnp.sin(li * dt)
    den = lr * lr + li * li
    nr, ni = a_re - 1.0, a_im
    coef_re = (nr * lr + ni * li) / den
    coef_im = (ni * lr - nr * li) / den
    return a_re, a_im, coef_re * br - coef_im * bi, coef_re * bi + coef_im * br


def _disc_specs(n):
    tr = _pick(n, (512,))
    cs = pl.BlockSpec((tr, 1), lambda i: (i, 0))
    ms = pl.BlockSpec((tr, SSM_GC), lambda i: (i, 0))
    return tr, cs, ms


def _disc_fwd(lr, li, ldt, br, bi):
    n = lr.shape[0]
    tr, cs, ms = _disc_specs(n)

    def body(lr_ref, li_ref, dt_ref, br_ref, bi_ref, o1, o2, o3, o4):
        r = _discretise(lr_ref[...], li_ref[...], dt_ref[...], br_ref[...], bi_ref[...])
        o1[...], o2[...], o3[...], o4[...] = r

    col = jax.ShapeDtypeStruct((n, 1), F32)
    mat = jax.ShapeDtypeStruct((n, SSM_GC), F32)
    return pl.pallas_call(body, grid=(n // tr,), in_specs=[cs, cs, cs, ms, ms], out_specs=(cs, cs, ms, ms),
                          out_shape=(col, col, mat, mat), compiler_params=_params(("parallel",)), name="disc_fwd")(
        lr, li, ldt, br, bi)


def _disc_bwd(lr, li, ldt, br, bi, gar, gai, gbr, gbi):
    n = lr.shape[0]
    tr, cs, ms = _disc_specs(n)

    def body(lr_ref, li_ref, dt_ref, br_ref, bi_ref, gar_ref, gai_ref, gbr_ref, gbi_ref, o_lr, o_li, o_dt, o_br, o_bi):
        _, vjp = jax.vjp(_discretise, lr_ref[...], li_ref[...], dt_ref[...], br_ref[...], bi_ref[...])
        g = vjp((gar_ref[...], gai_ref[...], gbr_ref[...], gbi_ref[...]))
        o_lr[...] = g[0]
        o_li[...] = g[1]
        o_dt[...] = jnp.sum(g[2].reshape(tr // SSM_P, SSM_P, 1), axis=1)
        o_br[...] = g[3]
        o_bi[...] = g[4]

    col = jax.ShapeDtypeStruct((n, 1), F32)
    mat = jax.ShapeDtypeStruct((n, SSM_GC), F32)
    return pl.pallas_call(
        body, grid=(n // tr,), in_specs=[cs, cs, cs, ms, ms, cs, cs, ms, ms],
        out_specs=(cs, cs, pl.BlockSpec((tr // SSM_P, 1), lambda i: (i, 0)), ms, ms),
        out_shape=(col, col, jax.ShapeDtypeStruct((n // SSM_P, 1), F32), mat, mat),
        compiler_params=_params(("parallel",)), name="disc_bwd")(lr, li, ldt, br, bi, gar, gai, gbr, gbi)


def _cpow(ar, ai, nsq):
    for _ in range(nsq):
        ar, ai = ar * ar - ai * ai, 2.0 * ar * ai
    return ar, ai


def _ssm_dims(L):
    tc = min(1024, L)
    seg = tc // SUBLANES
    assert seg & (seg - 1) == 0
    return tc, seg, L // tc, seg.bit_length() - 1


def _tile_rows(i):
    return pl.ds(pl.multiple_of(i * SUBLANES, SUBLANES), SUBLANES)


def _rows_to_segments(src_ref, dst_ref, seg):
    def body(i, _):
        dst_ref[_tile_rows(i), :] = src_ref[pl.ds(i, SUBLANES, stride=seg), :]
        return 0
    lax.fori_loop(0, seg, body, 0, unroll=8)


def _segments_to_rows(src_ref, dst_ref, seg):
    def body(i, _):
        dst_ref[pl.ds(i, SUBLANES, stride=seg), :] = src_ref[_tile_rows(i), :]
        return 0
    lax.fori_loop(0, seg, body, 0, unroll=8)


def _ssm_fwd(z, u_off, br_m, bi_m, cr_m, ci_m, a_re3, a_im3, d_row):
    L = z.shape[0]
    ngb = br_m.shape[0]
    tc, seg, nc, nsq = _ssm_dims(L)
    ucol = u_off // LANES

    def body(u_ref, br_ref, bi_ref, cr_ref, ci_ref, ar_ref, ai_ref, d_ref, y_ref, xr_ref, xi_ref,
             bur, bui, car_r, car_i, ini_r, ini_i, up, ys):
        @pl.when(pl.program_id(1) == 0)
        def _():
            car_r[...] = jnp.zeros_like(car_r)
            car_i[...] = jnp.zeros_like(car_i)

        _rows_to_segments(u_ref, up, seg)
        u = up[...]
        pr = _dot(u, br_ref[0])
        pi = _dot(u, bi_ref[0])
        for w in range(NW):
            bur[w] = pr[:, w * LANES:(w + 1) * LANES]
            bui[w] = pi[:, w * LANES:(w + 1) * LANES]
        ar = [jnp.broadcast_to(ar_ref[w], (SUBLANES, LANES)) for w in range(NW)]
        ai = [jnp.broadcast_to(ai_ref[w], (SUBLANES, LANES)) for w in range(NW)]

        def step(i, carry, store):
            xr, xi = carry
            rows = _tile_rows(i)
            nr, ni = [], []
            for w in range(NW):
                r = ar[w] * xr[w] - ai[w] * xi[w] + bur[w, rows, :]
                m = ar[w] * xi[w] + ai[w] * xr[w] + bui[w, rows, :]
                if store:
                    xr_ref[w, rows, :] = r
                    xi_ref[w, rows, :] = m
                nr.append(r)
                ni.append(m)
            return tuple(nr), tuple(ni)

        zero = tuple(jnp.zeros((SUBLANES, LANES), F32) for _ in range(NW))
        er, ei = lax.fori_loop(0, seg, functools.partial(step, store=False), (zero, zero), unroll=2)
        for w in range(NW):
            pr_, pi_ = _cpow(ar[w][0:1], ai[w][0:1], nsq)
            sr, si = car_r[w, 0:1, :], car_i[w, 0:1, :]
            for j in range(SUBLANES):
                ini_r[w, j:j + 1, :] = sr
                ini_i[w, j:j + 1, :] = si
                sr, si = (pr_ * sr - pi_ * si + er[w][j:j + 1], pr_ * si + pi_ * sr + ei[w][j:j + 1])
            car_r[w, 0:1, :] = sr
            car_i[w, 0:1, :] = si
        init = (tuple(ini_r[w] for w in range(NW)), tuple(ini_i[w] for w in range(NW)))
        lax.fori_loop(0, seg, functools.partial(step, store=True), init, unroll=2)
        acc = d_ref[...] * u
        for w in range(NW):
            sl = slice(w * LANES, (w + 1) * LANES)
            acc = acc + _dot(xr_ref[w], cr_ref[0, sl, :]) - _dot(xi_ref[w], ci_ref[0, sl, :])
        ys[...] = acc
        _segments_to_rows(ys, y_ref, seg)

    mat_b = pl.BlockSpec((1, LANES, NW * LANES), lambda b, k: (b, 0, 0))
    mat_c = pl.BlockSpec((1, NW * LANES, LANES), lambda b, k: (b, 0, 0))
    a_spec = pl.BlockSpec((NW, 1, LANES), lambda b, k: (b, 0, 0))
    x_spec = pl.BlockSpec((NW, tc, LANES), lambda b, k: (b, k, 0))
    xs = jax.ShapeDtypeStruct((ngb * NW, L, LANES), F32)
    st = pltpu.VMEM((NW, SUBLANES, LANES), F32)
    return pl.pallas_call(
        body,
        grid=(ngb, nc),
        in_specs=[pl.BlockSpec((tc, LANES), lambda b, k: (k, b + ucol)), mat_b, mat_b, mat_c, mat_c, a_spec, a_spec,
                  pl.BlockSpec((1, LANES), lambda b, k: (0, b))],
        out_specs=(pl.BlockSpec((tc, LANES), lambda b, k: (k, b)), x_spec, x_spec),
        out_shape=(jax.ShapeDtypeStruct((L, ngb * LANES), F32), xs, xs),
        scratch_shapes=[pltpu.VMEM((NW, tc, LANES), F32), pltpu.VMEM((NW, tc, LANES), F32), st, st, st, st,
                        pltpu.VMEM((tc, LANES), F32), pltpu.VMEM((tc, LANES), F32)],
        compiler_params=_params(("arbitrary", "arbitrary")),
        name="ssm_fwd",
    )(z, br_m, bi_m, cr_m, ci_m, a_re3, a_im3, d_row)


def _ssm_bwd(dy, z, u_off, xs_r, xs_i, br_m, bi_m, cr_m, ci_m, a_re3, a_im3, d_row):
    L = z.shape[0]
    ngb = br_m.shape[0]
    tc, seg, nc, nsq = _ssm_dims(L)
    ucol = u_off // LANES

    def body(dy_ref, u_ref, xr_ref, xi_ref, br_ref, bi_ref, cr_ref, ci_ref, ar_ref, ai_ref, d_ref,
             du_ref, gd_ref, gar_ref, gai_ref, gbr_ref, gbi_ref, gcr_ref, gci_ref,
             gr_s, gi_s, car_r, car_i, ini_r, ini_i, acc_r, acc_i, dyp, up, dus):
        k = pl.program_id(1)

        @pl.when(k == 0)
        def _():
            for ref in (car_r, car_i, acc_r, acc_i, gd_ref, gbr_ref, gbi_ref, gcr_ref, gci_ref):
                ref[...] = jnp.zeros_like(ref)

        _rows_to_segments(dy_ref, dyp, seg)
        _rows_to_segments(u_ref, up, seg)
        dy_v = dyp[...]
        u = up[...]
        g_re = _dot_nt(dy_v, cr_ref[0])
        g_im = -_dot_nt(dy_v, ci_ref[0])
        for w in range(NW):
            gr_s[w] = g_re[:, w * LANES:(w + 1) * LANES]
            gi_s[w] = g_im[:, w * LANES:(w + 1) * LANES]
        ar = [jnp.broadcast_to(ar_ref[w], (SUBLANES, LANES)) for w in range(NW)]
        ai = [jnp.broadcast_to(ai_ref[w], (SUBLANES, LANES)) for w in range(NW)]

        def step1(ii, carry):
            xr, xi = carry
            rows = _tile_rows(seg - 1 - ii)
            nr = tuple(ar[w] * xr[w] + ai[w] * xi[w] + gr_s[w, rows, :] for w in range(NW))
            ni = tuple(ar[w] * xi[w] - ai[w] * xr[w] + gi_s[w, rows, :] for w in range(NW))
            return nr, ni

        zero = tuple(jnp.zeros((SUBLANES, LANES), F32) for _ in range(NW))
        er, ei = lax.fori_loop(0, seg, step1, (zero, zero), unroll=2)
        for w in range(NW):
            pr_, pi_ = _cpow(ar[w][0:1], ai[w][0:1], nsq)
            sr, si = car_r[w, 0:1, :], car_i[w, 0:1, :]
            for j in reversed(range(SUBLANES)):
                ini_r[w, j:j + 1, :] = sr
                ini_i[w, j:j + 1, :] = si
                sr, si = (pr_ * sr + pi_ * si + er[w][j:j + 1], pr_ * si - pi_ * sr + ei[w][j:j + 1])
            car_r[w, 0:1, :] = sr
            car_i[w, 0:1, :] = si

        def step2(ii, carry):
            gxr, gxi, acr, aci = carry
            rows = _tile_rows(seg - 1 - ii)
            nr, ni, nar, nai = [], [], [], []
            for w in range(NW):
                xr_t, xi_t = xr_ref[w, rows, :], xi_ref[w, rows, :]
                nar.append(acr[w] + gxr[w] * xr_t + gxi[w] * xi_t)
                nai.append(aci[w] + gxi[w] * xr_t - gxr[w] * xi_t)
                r = ar[w] * gxr[w] + ai[w] * gxi[w] + gr_s[w, rows, :]
                m = ar[w] * gxi[w] - ai[w] * gxr[w] + gi_s[w, rows, :]
                gr_s[w, rows, :] = r
                gi_s[w, rows, :] = m
                nr.append(r)
                ni.append(m)
            return tuple(nr), tuple(ni), tuple(nar), tuple(nai)

        init = (tuple(ini_r[w] for w in range(NW)), tuple(ini_i[w] for w in range(NW)),
                tuple(acc_r[w] for w in range(NW)), tuple(acc_i[w] for w in range(NW)))
        _, _, acr, aci = lax.fori_loop(0, seg, step2, init, unroll=2)
        du = d_ref[...] * dy_v
        for w in range(NW):
            sl = slice(w * LANES, (w + 1) * LANES)
            acc_r[w] = acr[w]
            acc_i[w] = aci[w]
            gxr_w, gxi_w = gr_s[w], gi_s[w]
            du = du + _dot_nt(gxr_w, br_ref[0, :, sl]) + _dot_nt(gxi_w, bi_ref[0, :, sl])
            gbr_ref[0, :, sl] += _dot_tn(u, gxr_w)
            gbi_ref[0, :, sl] += _dot_tn(u, gxi_w)
            gcr_ref[0, sl, :] += _dot_tn(xr_ref[w], dy_v)
            gci_ref[0, sl, :] += _dot_tn(-xi_ref[w], dy_v)
        dus[...] = du
        _segments_to_rows(dus, du_ref, seg)
        gd_ref[...] += jnp.sum(dy_v * u, axis=0, keepdims=True)

        @pl.when(k == nc - 1)
        def _():
            for w in range(NW):
                gar_ref[w] = jnp.sum(acc_r[w], axis=0, keepdims=True)
                gai_ref[w] = jnp.sum(acc_i[w], axis=0, keepdims=True)

    rk = lambda k: nc - 1 - k
    mat_b = pl.BlockSpec((1, LANES, NW * LANES), lambda b, k: (b, 0, 0))
    mat_c = pl.BlockSpec((1, NW * LANES, LANES), lambda b, k: (b, 0, 0))
    a_spec = pl.BlockSpec((NW, 1, LANES), lambda b, k: (b, 0, 0))
    x_spec = pl.BlockSpec((NW, tc, LANES), lambda b, k: (b, rk(k), 0))
    st = pltpu.VMEM((NW, SUBLANES, LANES), F32)
    big = pltpu.VMEM((NW, tc, LANES), F32)
    return pl.pallas_call(
        body,
        grid=(ngb, nc),
        in_specs=[pl.BlockSpec((tc, LANES), lambda b, k: (rk(k), b)),
                  pl.BlockSpec((tc, LANES), lambda b, k: (rk(k), b + ucol)),
                  x_spec, x_spec, mat_b, mat_b, mat_c, mat_c, a_spec, a_spec,
                  pl.BlockSpec((1, LANES), lambda b, k: (0, b))],
        out_specs=(pl.BlockSpec((tc, LANES), lambda b, k: (rk(k), b)),
                   pl.BlockSpec((1, LANES), lambda b, k: (0, b)), a_spec, a_spec, mat_b, mat_b, mat_c, mat_c),
        out_shape=(jax.ShapeDtypeStruct((L, ngb * LANES), F32),
                   jax.ShapeDtypeStruct((1, ngb * LANES), F32),
                   jax.ShapeDtypeStruct((ngb * NW, 1, LANES), F32), jax.ShapeDtypeStruct((ngb * NW, 1, LANES), F32),
                   jax.ShapeDtypeStruct(br_m.shape, F32), jax.ShapeDtypeStruct(br_m.shape, F32),
                   jax.ShapeDtypeStruct(cr_m.shape, F32), jax.ShapeDtypeStruct(cr_m.shape, F32)),
        scratch_shapes=[big, big, st, st, st, st, st, st] + [pltpu.VMEM((tc, LANES), F32)] * 3,
        compiler_params=_params(("arbitrary", "arbitrary")),
        name="ssm_bwd",
    )(dy, z, xs_r, xs_i, br_m, bi_m, cr_m, ci_m, a_re3, a_im3, d_row)


def _block_diag_in(bb, ngb):
    t = bb.reshape(ngb, GROUPS_PER_BLOCK, SSM_P, SSM_GC).transpose(0, 1, 3, 2)
    eye = jnp.eye(GROUPS_PER_BLOCK, dtype=F32)
    m = t[:, :, :, None, :] * eye[None, :, None, :, None]
    return m.reshape(ngb, GROUPS_PER_BLOCK * SSM_GC, GROUPS_PER_BLOCK * SSM_P)


def _block_diag_out(c, ngb):
    t = c.reshape(ngb, GROUPS_PER_BLOCK, SSM_GC, SSM_P).transpose(0, 1, 3, 2)
    eye = jnp.eye(GROUPS_PER_BLOCK, dtype=F32)
    m = t[:, :, :, None, :] * eye[None, :, None, :, None]
    return m.reshape(ngb, GROUPS_PER_BLOCK * SSM_P, GROUPS_PER_BLOCK * SSM_GC)


def _diag_in(m, ngb):
    m5 = m.reshape(ngb, GROUPS_PER_BLOCK, SSM_GC, GROUPS_PER_BLOCK, SSM_P)
    d = jnp.diagonal(m5, axis1=1, axis2=3)
    return d.transpose(0, 3, 2, 1).reshape(ngb * GROUPS_PER_BLOCK * SSM_P, SSM_GC)


def _diag_out(m, ngb):
    m5 = m.reshape(ngb, GROUPS_PER_BLOCK, SSM_P, GROUPS_PER_BLOCK, SSM_GC)
    d = jnp.diagonal(m5, axis1=1, axis2=3)
    return d.transpose(0, 3, 2, 1).reshape(ngb * GROUPS_PER_BLOCK, SSM_GC, SSM_P)


_ANY = pl.BlockSpec(memory_space=pl.ANY)


def _all_gather(shards, name):
    n = len(shards)

    def body(*refs):
        ins, outs = refs[:n], refs[n:2 * n]
        send_sems, recv_sems, local_sems = refs[2 * n:]
        x, y, c = lax.axis_index("x"), lax.axis_index("y"), lax.axis_index("c")
        me, sibling = (x, y, c), (x, y, 1 - c)
        chips = [(1 - x, y), (x, 1 - y), (1 - x, 1 - y)]

        def copy(a, k, block, to, src=None):
            s = 4 * block[0] + 2 * block[1] + block[2]
            return pltpu.make_async_remote_copy(
                src_ref=outs[a].at[s] if src is None else src, dst_ref=outs[a].at[s],
                send_sem=send_sems.at[7 * a + k], recv_sem=recv_sems.at[7 * a + k],
                device_id=to, device_id_type=MESH)

        started = []
        for a in range(n):
            mine = pltpu.make_async_copy(ins[a], outs[a].at[4 * x + 2 * y + c], local_sems.at[a])
            mine.start()
            started.append(mine)
        sends = []
        for a in range(n):
            first = [copy(a, 0, me, sibling, src=ins[a])]
            first += [copy(a, 1 + j, me, (*chip, c), src=ins[a]) for j, chip in enumerate(chips)]
            for cp in first:
                cp.start()
            sends += first
        for a in range(n):
            for j, chip in enumerate(chips):
                copy(a, 1 + j, (*chip, c), me).wait_recv()
                fwd = copy(a, 4 + j, (*chip, c), sibling)
                fwd.start()
                sends.append(fwd)
        for a in range(n):
            copy(a, 0, sibling, me).wait_recv()
            for j, chip in enumerate(chips):
                copy(a, 4 + j, (*chip, 1 - c), me).wait_recv()
        for cp in sends:
            cp.wait_send()
        for mine in started:
            mine.wait()

    return pl.pallas_call(
        body,
        in_specs=[_ANY] * n,
        out_specs=tuple([_ANY] * n),
        out_shape=tuple(jax.ShapeDtypeStruct((N_DEV,) + s.shape, s.dtype) for s in shards),
        scratch_shapes=[pltpu.SemaphoreType.DMA((7 * n,)), pltpu.SemaphoreType.DMA((7 * n,)),
                        pltpu.SemaphoreType.DMA((n,))],
        compiler_params=pltpu.CompilerParams(has_side_effects=True),
        name=name,
    )(*shards)


def _exchange_sibling(parts, name):
    n = len(parts)

    def body(*refs):
        ins, outs = refs[:n], refs[n:2 * n]
        send_sems, recv_sems = refs[2 * n:]
        x, y, c = lax.axis_index("x"), lax.axis_index("y"), lax.axis_index("c")
        copies = []
        for a in range(n):
            for q in range(4):
                cp = pltpu.make_async_remote_copy(
                    src_ref=ins[a].at[2 * q + (1 - c)], dst_ref=outs[a].at[q],
                    send_sem=send_sems.at[4 * a + q], recv_sem=recv_sems.at[4 * a + q],
                    device_id=(x, y, 1 - c), device_id_type=MESH)
                cp.start()
                copies.append(cp)
        for cp in copies:
            cp.wait()

    return pl.pallas_call(
        body,
        in_specs=[_ANY] * n,
        out_specs=tuple([_ANY] * n),
        out_shape=tuple(jax.ShapeDtypeStruct((4,) + p.shape[1:], p.dtype) for p in parts),
        scratch_shapes=[pltpu.SemaphoreType.DMA((4 * n,)), pltpu.SemaphoreType.DMA((4 * n,))],
        compiler_params=pltpu.CompilerParams(has_side_effects=True),
        name=name,
    )(*parts)


def _exchange_chips(parts, name):
    n = len(parts)

    def body(*refs):
        ins, outs = refs[:n], refs[n:2 * n]
        send_sems, recv_sems = refs[2 * n:]
        x, y, c = lax.axis_index("x"), lax.axis_index("y"), lax.axis_index("c")
        chips = [(1 - x, y), (x, 1 - y), (1 - x, 1 - y)]
        copies = []
        for a in range(n):
            for j, (px, py) in enumerate(chips):
                cp = pltpu.make_async_remote_copy(
                    src_ref=ins[a].at[2 * px + py], dst_ref=outs[a].at[j],
                    send_sem=send_sems.at[3 * a + j], recv_sem=recv_sems.at[3 * a + j],
                    device_id=(px, py, c), device_id_type=MESH)
                cp.start()
                copies.append(cp)
        for cp in copies:
            cp.wait()

    return pl.pallas_call(
        body,
        in_specs=[_ANY] * n,
        out_specs=tuple([_ANY] * n),
        out_shape=tuple(jax.ShapeDtypeStruct((3,) + p.shape[1:], p.dtype) for p in parts),
        scratch_shapes=[pltpu.SemaphoreType.DMA((3 * n,)), pltpu.SemaphoreType.DMA((3 * n,))],
        compiler_params=pltpu.CompilerParams(has_side_effects=True),
        name=name,
    )(*parts)


def _sibling_add(part, recv, name):
    _, R, C = part.shape
    tr = _pick(R, (256, 128))
    c = lax.axis_index("c")

    def body(c_ref, p_ref, r_ref, o_ref, o16_ref):
        t = p_ref[...] + r_ref[...]
        o_ref[...] = t
        o16_ref[...] = t.astype(BF16)

    blk = pl.BlockSpec((1, tr, C), lambda q, i, c_ref: (q, i, 0))
    return pl.pallas_call(
        body,
        grid_spec=pltpu.PrefetchScalarGridSpec(
            num_scalar_prefetch=1,
            grid=(4, R // tr),
            in_specs=[pl.BlockSpec((1, tr, C), lambda q, i, c_ref: (2 * q + c_ref[0], i, 0)), blk],
            out_specs=(blk, blk),
        ),
        out_shape=(jax.ShapeDtypeStruct((4, R, C), F32), jax.ShapeDtypeStruct((4, R, C), BF16)),
        compiler_params=_params(("parallel", "parallel")),
        name=name,
    )(jnp.reshape(c, (1,)).astype(jnp.int32), part, recv)


def _adamw(w, g, m, v):
    m = ADAM_B1 * m + (1.0 - ADAM_B1) * g
    v = ADAM_B2 * v + (1.0 - ADAM_B2) * (g * g)
    m_hat = m / (1.0 - ADAM_B1 ** ADAM_STEP)
    v_hat = v / (1.0 - ADAM_B2 ** ADAM_STEP)
    delta = -ADAM_LR * (m_hat / (jnp.sqrt(v_hat) + ADAM_EPS) + ADAM_WD * w)
    return delta, m, v


def _adam_big(t, recv, w, m, v, name):
    _, R, C = t.shape
    tr = _pick(R, (256, 128))
    chip = 2 * lax.axis_index("x") + lax.axis_index("y")

    def body(q_ref, t_ref, r_ref, w_ref, m_ref, v_ref, g_ref, d_ref, nm_ref, nv_ref):
        g = t_ref[0] + r_ref[0].astype(F32) + r_ref[1].astype(F32) + r_ref[2].astype(F32)
        g_ref[...] = g
        d_ref[...], nm_ref[...], nv_ref[...] = _adamw(w_ref[...], g, m_ref[...], v_ref[...])

    blk = pl.BlockSpec((tr, C), lambda i, q_ref: (i, 0))
    o = jax.ShapeDtypeStruct((R, C), F32)
    return pl.pallas_call(
        body,
        grid_spec=pltpu.PrefetchScalarGridSpec(
            num_scalar_prefetch=1,
            grid=(R // tr,),
            in_specs=[pl.BlockSpec((1, tr, C), lambda i, q_ref: (q_ref[0], i, 0)),
                      pl.BlockSpec((3, tr, C), lambda i, q_ref: (0, i, 0)), blk, blk, blk],
            out_specs=(blk, blk, blk, blk),
        ),
        out_shape=(o, o, o, o),
        compiler_params=_params(("parallel",)),
        name=name,
    )(jnp.reshape(chip, (1,)).astype(jnp.int32), t, recv, w, m, v)


def _small_allreduce_adam(gbuf, wbuf, mbuf, vbuf):
    R = gbuf.shape[0]

    def body(g_ref, w_ref, m_ref, v_ref, gs_ref, d_ref, nm_ref, nv_ref, slots, send_sems, recv_sems):
        x, y, c = lax.axis_index("x"), lax.axis_index("y"), lax.axis_index("c")
        me = 4 * x + 2 * y + c
        copies = []
        for k in range(1, N_DEV):
            fx, fy, fc = (k >> 2) & 1, (k >> 1) & 1, k & 1
            px = x + fx - 2 * x * fx
            py = y + fy - 2 * y * fy
            pc = c + fc - 2 * c * fc
            cp = pltpu.make_async_remote_copy(
                src_ref=g_ref, dst_ref=slots.at[me],
                send_sem=send_sems.at[k - 1], recv_sem=recv_sems.at[k - 1],
                device_id=(px, py, pc), device_id_type=MESH)
            cp.start()
            copies.append((cp, 4 * px + 2 * py + pc))
        slots[me] = g_ref[...]
        for k, (cp, src) in enumerate(copies):
            pltpu.make_async_remote_copy(
                src_ref=g_ref, dst_ref=slots.at[src], send_sem=send_sems.at[k], recv_sem=recv_sems.at[k],
                device_id=(x, y, c), device_id_type=MESH).wait_recv()
        for cp, _ in copies:
            cp.wait_send()
        g = slots[0]
        for s in range(1, N_DEV):
            g = g + slots[s]
        gs_ref[...] = g
        d_ref[...], nm_ref[...], nv_ref[...] = _adamw(w_ref[...], g, m_ref[...], v_ref[...])

    o = jax.ShapeDtypeStruct((R, LANES), F32)
    vm = pl.BlockSpec(memory_space=pltpu.VMEM)
    return pl.pallas_call(
        body,
        in_specs=[vm, vm, vm, vm],
        out_specs=(vm, vm, vm, vm),
        out_shape=(o, o, o, o),
        scratch_shapes=[pltpu.VMEM((N_DEV, R, LANES), F32), pltpu.SemaphoreType.DMA((N_DEV - 1,)),
                        pltpu.SemaphoreType.DMA((N_DEV - 1,))],
        compiler_params=pltpu.CompilerParams(vmem_limit_bytes=VMEM_LIMIT, has_side_effects=True),
        name="small_allreduce_adam",
    )(gbuf, wbuf, mbuf, vbuf)


def _pack(items):
    rows, spans, r0 = [], [], 0
    for a in items:
        n = a.size
        nr = -(-n // LANES)
        rows.append(jnp.pad(a.reshape(-1).astype(F32), (0, nr * LANES - n)).reshape(nr, LANES))
        spans.append((r0, nr, a.shape))
        r0 += nr
    pad = -r0 % SUBLANES
    if pad:
        rows.append(jnp.zeros((pad, LANES), F32))
    return jnp.concatenate(rows, axis=0), spans


def _unpack(buf, spans):
    return [buf[r0:r0 + nr].reshape(-1)[:math.prod(shape)].reshape(shape) for r0, nr, shape in spans]


def kernel(x, norm_mix_pre, norm_mix_post, norm_mlp_pre, norm_mlp_post, w_in, sinks, lam_re, lam_im, log_dt, b_re, b_im, c_re, c_im, d_skip, w_glu, w_branch, w_out, w_up, w_down, loss_target, m_norm_mix_pre, m_norm_mix_post, m_norm_mlp_pre, m_norm_mlp_post, m_w_in, m_sinks, m_lam_re, m_lam_im, m_log_dt, m_b_re, m_b_im, m_c_re, m_c_im, m_d_skip, m_w_glu, m_w_branch, m_w_out, m_w_up, m_w_down, v_norm_mix_pre, v_norm_mix_post, v_norm_mlp_pre, v_norm_mlp_post, v_w_in, v_sinks, v_lam_re, v_lam_im, v_log_dt, v_b_re, v_b_im, v_c_re, v_c_im, v_d_skip, v_w_glu, v_w_branch, v_w_out, v_w_up, v_w_down):
    _, L, D = x.shape
    xs = x[0]
    tgt = loss_target[0]
    ssm_w = D // 2
    n_groups = ssm_w // SSM_GC
    ngb = n_groups // GROUPS_PER_BLOCK
    n_state = n_groups * SSM_P
    d_ff = w_up.shape[2] * N_DEV
    o_k, o_v, o_u = Q_W, Q_W + KV_W, Q_W + 2 * KV_W
    o_ga = o_u + ssm_w
    o_gs = o_ga + D

    big = {"w_in": w_in[0], "w_glu": w_glu[0], "w_branch": w_branch[0], "w_out": w_out[0],
           "w_up": w_up[0], "w_down": w_down[0]}
    col_sharded = ("w_in", "w_glu", "w_up")
    names = list(big)
    gathered = _all_gather([big[k].astype(BF16) for k in names], "all_gather_weights")
    full = {}
    for k, g in zip(names, gathered):
        _, r, c = g.shape
        full[k] = g.transpose(1, 0, 2).reshape(r, N_DEV * c) if k in col_sharded else g.reshape(N_DEV * r, c)
    wb_a, wb_s = full["w_branch"][:Q_W], full["w_branch"][Q_W:]

    col = lambda a: a.reshape(n_state, 1)
    lr_c, li_c = col(lam_re[0]), col(lam_im[0])
    ldt_c = jnp.repeat(log_dt[0], SSM_P).reshape(n_state, 1)
    b_re_c, b_im_c = b_re[0].reshape(n_state, SSM_GC), b_im[0].reshape(n_state, SSM_GC)
    a_re, a_im, bb_re, bb_im = _disc_fwd(lr_c, li_c, ldt_c, b_re_c, b_im_c)
    a_re3 = a_re.reshape(n_state // LANES, 1, LANES)
    a_im3 = a_im.reshape(n_state // LANES, 1, LANES)
    br_m = _block_diag_in(bb_re, ngb).astype(BF16)
    bi_m = _block_diag_in(bb_im, ngb).astype(BF16)
    cr_m = _block_diag_out(c_re[0], ngb).astype(BF16)
    ci_m = _block_diag_out(c_im[0], ngb).astype(BF16)
    d_row = d_skip[0].reshape(1, ssm_w)
    cos_t, sin_t = _rope_tables(L)

    h = _rms_pre(xs, norm_mix_pre)
    z = _mm(h, full["w_in"], mode="nn", name="mm_z")
    o_attn = _attn_fwd(z, cos_t, sin_t, sinks)
    y_pre, xs_r, xs_i = _ssm_fwd(z, o_u, br_m, bi_m, cr_m, ci_m, a_re3, a_im3, d_row)
    gy = _ew(lambda y: (_gelu(y),), [(y_pre, 0)], (BF16,), rows=L, ncols=ssm_w, name="gelu")
    zg = _mm(gy, full["w_glu"], mode="nn", name="mm_zg")
    o_ssm = _ew(lambda a, b: (a * _sigmoid(b),), [(zg, 0), (zg, ssm_w)], (BF16,), rows=L, ncols=ssm_w, name="glu")
    y_attn = _mm(o_attn, wb_a, mode="nn", name="mm_y_attn")
    y_ssm = _mm(o_ssm, wb_s, mode="nn", name="mm_y_ssm")
    mix = _ew(lambda ga, gs, ya, ys: (_sigmoid(ga) * ya + _sigmoid(gs) * ys,),
              [(z, o_ga), (z, o_gs), (y_attn, 0), (y_ssm, 0)], (BF16,), rows=L, ncols=D, name="mix")
    mixed = _mm(mix, full["w_out"], mode="nn", name="mm_mixed")
    x1, h2 = _post_pre(xs, mixed, norm_mix_post, norm_mlp_pre)

    def relu_sq(acc):
        a = jnp.maximum(acc, 0.0)
        return a, a * a

    act, act2 = _mm(h2, full["w_up"], mode="nn", name="mm_up", out_dtypes=(BF16, BF16), epi=relu_sq)
    dn = _mm(act2, full["w_down"], mode="nn", name="mm_down")
    dx2, d_dn, dg_mlp_post, loss_part = _loss_bwd(x1, dn, norm_mlp_post, tgt)

    d_pre = _mm(d_dn, full["w_down"], mode="nt", name="mm_d_act", out_dtypes=(BF16,),
                epi=lambda acc, a: (acc * (2.0 * a.astype(F32)),), extras=(act,))
    gw_down = _mm(act2, d_dn, mode="tn", name="mm_gw_down")
    dh2 = _mm(d_pre, full["w_up"], mode="nt", name="mm_dh2")
    gw_up = _mm(h2, d_pre, mode="tn", name="mm_gw_up")
    dx1, d_mixed, dg_mlp_pre, dg_mix_post = _norm_bwd_pair(x1, dh2, dx2, mixed, norm_mlp_pre, norm_mix_post)
    d_mix = _mm(d_mixed, full["w_out"], mode="nt", name="mm_d_mix")
    gw_out = _mm(mix, d_mixed, mode="tn", name="mm_gw_out")

    def gate_bwd(dm, ga, gs, ya, ys):
        sa, ss = _sigmoid(ga), _sigmoid(gs)
        return dm * sa, dm * ss, dm * ya * sa * (1.0 - sa), dm * ys * ss * (1.0 - ss)

    d_ya, d_ys, d_zga, d_zgs = _ew(gate_bwd, [(d_mix, 0), (z, o_ga), (z, o_gs), (y_attn, 0), (y_ssm, 0)],
                                   (BF16, BF16, BF16, BF16), rows=L, ncols=D, name="gate_bwd")
    d_o_attn = _mm(d_ya, wb_a, mode="nt", name="mm_d_o_attn")
    gwb_a = _mm(o_attn, d_ya, mode="tn", name="mm_gwb_a")
    d_o_ssm = _mm(d_ys, wb_s, mode="nt", name="mm_d_o_ssm")
    gwb_s = _mm(o_ssm, d_ys, mode="tn", name="mm_gwb_s")

    def glu_bwd(do, a, b):
        s = _sigmoid(b)
        return do * s, do * a * s * (1.0 - s)

    d_zg_a, d_zg_b = _ew(glu_bwd, [(d_o_ssm, 0), (zg, 0), (zg, ssm_w)], (BF16, BF16), rows=L, ncols=ssm_w, name="glu_bwd")
    d_zg = jnp.concatenate([d_zg_a, d_zg_b], axis=1)
    dy_pre = _mm(d_zg, full["w_glu"], mode="nt", name="mm_d_gy",
                 epi=lambda acc, y: (acc * _gelu_grad(y),), extras=(y_pre,))
    gw_glu = _mm(gy, d_zg, mode="tn", name="mm_gw_glu")
    (du, g_dskip, g_ar3, g_ai3, g_br_m, g_bi_m, g_cr_m, g_ci_m) = _ssm_bwd(
        dy_pre, z, o_u, xs_r, xs_i, br_m, bi_m, cr_m, ci_m, a_re3, a_im3, d_row)
    g_lr, g_li, g_ldt, g_b_re, g_b_im = _disc_bwd(
        lr_c, li_c, ldt_c, b_re_c, b_im_c, g_ar3.reshape(n_state, 1), g_ai3.reshape(n_state, 1),
        _diag_in(g_br_m, ngb), _diag_in(g_bi_m, ngb))
    dq, dkc, dkp, dvc, dvp, dsink_rows = _attn_bwd(z, d_o_attn, cos_t, sin_t, sinks)
    dk, dv = _kv_combine(dkc, dkp, dvc, dvp)
    dz = jnp.concatenate([dq, dk, dv, du.astype(BF16), d_zga, d_zgs], axis=1)
    dh = _mm(dz, full["w_in"], mode="nt", name="mm_dh")
    gw_in = _mm(h, dz, mode="tn", name="mm_gw_in")
    grad_x, dg_mix_pre = _final_bwd(xs, dh, dx1, norm_mix_pre)

    gfull = {"w_in": gw_in, "w_glu": gw_glu, "w_branch": jnp.concatenate([gwb_a, gwb_s], axis=0),
             "w_out": gw_out, "w_up": gw_up, "w_down": gw_down}
    parts = []
    for k in names:
        r, c = big[k].shape
        g = gfull[k]
        parts.append(g.reshape(r, N_DEV, c).transpose(1, 0, 2) if k in col_sharded else g.reshape(N_DEV, r, c))
    from_sibling = _exchange_sibling(parts, "rs_sibling")
    chip_sums = [_sibling_add(p, r, "rs_add_" + k) for k, p, r in zip(names, parts, from_sibling)]
    from_chips = _exchange_chips([t16 for _, t16 in chip_sums], "rs_chips")
    chip_sums = [t32 for t32, _ in chip_sums]
    moments = {"w_in": (m_w_in, v_w_in), "w_glu": (m_w_glu, v_w_glu), "w_branch": (m_w_branch, v_w_branch),
               "w_out": (m_w_out, v_w_out), "w_up": (m_w_up, v_w_up), "w_down": (m_w_down, v_w_down)}
    big_out = {}
    for k, t, r in zip(names, chip_sums, from_chips):
        mm_, vv_ = moments[k]
        big_out[k] = [o[None] for o in _adam_big(t, r, big[k], mm_[0], vv_[0], "adam_" + k)]

    dsink = jnp.stack([dsink_rows[:, 0], dsink_rows[:, HEAD_DIM]], axis=1).reshape(1, N_Q_HEADS)
    small_names = ["norm_mix_pre", "norm_mix_post", "norm_mlp_pre", "norm_mlp_post", "sinks", "lam_re", "lam_im",
                   "log_dt", "b_re", "b_im", "c_re", "c_im", "d_skip"]
    small_w = [norm_mix_pre, norm_mix_post, norm_mlp_pre, norm_mlp_post, sinks, lam_re, lam_im, log_dt,
               b_re, b_im, c_re, c_im, d_skip]
    small_m = [m_norm_mix_pre, m_norm_mix_post, m_norm_mlp_pre, m_norm_mlp_post, m_sinks, m_lam_re, m_lam_im,
               m_log_dt, m_b_re, m_b_im, m_c_re, m_c_im, m_d_skip]
    small_v = [v_norm_mix_pre, v_norm_mix_post, v_norm_mlp_pre, v_norm_mlp_post, v_sinks, v_lam_re, v_lam_im,
               v_log_dt, v_b_re, v_b_im, v_c_re, v_c_im, v_d_skip]
    small_g = [dg_mix_pre, dg_mix_post, dg_mlp_pre, dg_mlp_post, dsink,
               g_lr.reshape(lam_re.shape), g_li.reshape(lam_im.shape), g_ldt.reshape(log_dt.shape),
               g_b_re.reshape(b_re.shape), g_b_im.reshape(b_im.shape),
               _diag_out(g_cr_m, ngb).reshape(c_re.shape), _diag_out(g_ci_m, ngb).reshape(c_im.shape),
               g_dskip.reshape(d_skip.shape)]
    zero1 = jnp.zeros((1, 1), F32)
    gbuf, spans = _pack(small_g + [loss_part])
    wbuf, _ = _pack(small_w + [zero1])
    mbuf, _ = _pack(small_m + [zero1])
    vbuf, _ = _pack(small_v + [zero1])
    gs, ds, nms, nvs = [_unpack(b, spans) for b in _small_allreduce_adam(gbuf, wbuf, mbuf, vbuf)]
    loss = gs[-1].reshape(())

    order = ["norm_mix_pre", "norm_mix_post", "norm_mlp_pre", "norm_mlp_post", "w_in", "sinks", "lam_re", "lam_im",
             "log_dt", "b_re", "b_im", "c_re", "c_im", "d_skip", "w_glu", "w_branch", "w_out", "w_up", "w_down"]
    outs = [loss, grad_x[None]]
    for idx, src in enumerate((gs, ds, nms, nvs)):
        for k in order:
            outs.append(big_out[k][idx] if k in big_out else src[small_names.index(k)])
    return tuple(outs)
```

```python
import functools
import math

import jax
import jax.numpy as jnp
from jax import lax
from jax.experimental import pallas as pl
from jax.experimental.pallas import tpu as pltpu

F32 = jnp.float32
BF16 = jnp.bfloat16
MESH = pl.DeviceIdType.MESH

LANES = 128
SUBLANES = 8
VMEM_LIMIT = 56 * 1024 * 1024

HEAD_DIM = 64
N_Q_HEADS = 16
N_KV_HEADS = 2
Q_W = N_Q_HEADS * HEAD_DIM
KV_W = N_KV_HEADS * HEAD_DIM
BLOCK = 128
ROT_DIM = HEAD_DIM // 4
ROPE_THETA = 500000.0
SSM_GC = 16
SSM_P = 64
GROUPS_PER_BLOCK = 8
NW = GROUPS_PER_BLOCK * SSM_P // LANES
EPS = 1e-6
N_DEV = 8

ADAM_LR = 0.001
ADAM_B1 = 0.9
ADAM_B2 = 0.999
ADAM_EPS = 1e-08
ADAM_WD = 0.01
ADAM_STEP = 10


def _params(sem=None):
    return pltpu.CompilerParams(dimension_semantics=sem, vmem_limit_bytes=VMEM_LIMIT)


def _host_params(sem, comm):
    if comm:
        return pltpu.CompilerParams(dimension_semantics=("arbitrary",) * len(sem), vmem_limit_bytes=VMEM_LIMIT,
                                    has_side_effects=True)
    return _params(sem)


def _pick(dim, prefs):
    for p in prefs:
        if dim % p == 0:
            return p
    return dim


def _sigmoid(x):
    return 1.0 / (1.0 + jnp.exp(-x))


_GELU_C = math.sqrt(2.0 / math.pi)


def _gelu(x):
    return 0.5 * x * (1.0 + jnp.tanh(_GELU_C * (x + 0.044715 * x * x * x)))


def _gelu_grad(x):
    t = jnp.tanh(_GELU_C * (x + 0.044715 * x * x * x))
    return 0.5 * (1.0 + t) + 0.5 * x * (1.0 - t * t) * _GELU_C * (1.0 + 3.0 * 0.044715 * x * x)


_DIMS = {"nn": (((1,), (0,)), ((), ())), "nt": (((1,), (1,)), ((), ())), "tn": (((0,), (0,)), ((), ()))}


def _mm(a, b, *, mode, name, out_dtypes=(F32,), epi=None, extras=(), comm=None):
    if mode == "nn":
        (M, K), (K2, N) = a.shape, b.shape
    elif mode == "nt":
        (M, K), (N, K2) = a.shape, b.shape
    else:
        (K, M), (K2, N) = a.shape, b.shape
    assert K == K2, (a.shape, b.shape, mode)
    tm = _pick(M, (1024, 512, 256, 128))
    tn = _pick(N, (1024, 1280, 640, 512, 384, 256, 128))
    tk = K if K <= 2048 else _pick(K, (2048, 1280, 1024, 640, 512, 256, 128))
    nk = K // tk
    n_ex = len(extras)
    n_out = len(out_dtypes)
    gi, gj = M // tm, N // tn
    steps = gi * gj * nk

    def body(*refs):
        ins, c_ins, o_refs, c_outs, scratch, sems = _split_refs(refs, 2 + n_ex, n_out, comm)
        a_ref, b_ref = ins[0], ins[1]
        ex_refs = ins[2:]
        if comm:
            s = (pl.program_id(0) * gj + pl.program_id(1)) * nk + pl.program_id(2)
            comm.run(c_ins, c_outs, sems, s == 0, s == steps // 2, s == steps - 1)

        def finish(r):
            outs = (r,) if epi is None else epi(r, *[e[...] for e in ex_refs])
            for o_ref, o in zip(o_refs, outs):
                o_ref[...] = o.astype(o_ref.dtype)

        part = lax.dot_general(a_ref[...].astype(BF16), b_ref[...].astype(BF16), _DIMS[mode],
                               preferred_element_type=F32)
        if nk == 1:
            finish(part)
            return
        acc = scratch[0]
        k = pl.program_id(2)

        @pl.when(k == 0)
        def _():
            acc[...] = part

        @pl.when((k > 0) & (k < nk - 1))
        def _():
            acc[...] += part

        @pl.when(k == nk - 1)
        def _():
            finish(acc[...] + part)

    if mode == "nn":
        a_spec = pl.BlockSpec((tm, tk), lambda i, j, k: (i, k))
        b_spec = pl.BlockSpec((tk, tn), lambda i, j, k: (k, j))
    elif mode == "nt":
        a_spec = pl.BlockSpec((tm, tk), lambda i, j, k: (i, k))
        b_spec = pl.BlockSpec((tn, tk), lambda i, j, k: (j, k))
    else:
        a_spec = pl.BlockSpec((tk, tm), lambda i, j, k: (k, i))
        b_spec = pl.BlockSpec((tk, tn), lambda i, j, k: (k, j))
    o_spec = pl.BlockSpec((tm, tn), lambda i, j, k: (i, j))
    c_ins = comm.ins if comm else []
    c_shapes = comm.out_shapes if comm else []
    res = pl.pallas_call(
        body,
        grid=(gi, gj, nk),
        in_specs=[a_spec, b_spec] + [o_spec] * n_ex + [_ANY] * len(c_ins),
        out_specs=tuple([o_spec] * n_out + [_ANY] * len(c_shapes)),
        out_shape=tuple([jax.ShapeDtypeStruct((M, N), d) for d in out_dtypes] + c_shapes),
        scratch_shapes=([pltpu.VMEM((tm, tn), F32)] if nk > 1 else []) + (comm.scratch() if comm else []),
        compiler_params=_host_params(("parallel", "parallel", "arbitrary"), comm),
        name=name,
    )(a, b, *extras, *c_ins)
    if comm:
        return (res[0] if n_out == 1 else res[:n_out]), list(res[n_out:])
    return res[0] if n_out == 1 else res


def _ew(fn, ins, out_dtypes, *, rows, ncols, name):
    g = ncols
    for _, off in ins:
        g = math.gcd(g, off)
    tc = _pick(g, (512, 256, 128))
    tr = _pick(rows, (1024, 512, 256, 128))
    n_in = len(ins)

    def body(*refs):
        outs = fn(*[r[...] for r in refs[:n_in]])
        for o_ref, o in zip(refs[n_in:], outs):
            o_ref[...] = o.astype(o_ref.dtype)

    def in_spec(off):
        ob = off // tc
        return pl.BlockSpec((tr, tc), lambda i, j: (i, j + ob))

    o_spec = pl.BlockSpec((tr, tc), lambda i, j: (i, j))
    res = pl.pallas_call(
        body,
        grid=(rows // tr, ncols // tc),
        in_specs=[in_spec(off) for _, off in ins],
        out_specs=tuple([o_spec] * len(out_dtypes)),
        out_shape=tuple(jax.ShapeDtypeStruct((rows, ncols), d) for d in out_dtypes),
        compiler_params=_params(("parallel", "parallel")),
        name=name,
    )(*[arr for arr, _ in ins])
    return res[0] if len(out_dtypes) == 1 else res


def _rstd(x):
    return lax.rsqrt(jnp.mean(x * x, axis=-1, keepdims=True) + EPS)


def _norm_bwd(x, r, g, dy):
    t = dy * g
    dx = r * t - x * (r * r * r) * jnp.mean(t * x, axis=-1, keepdims=True)
    return dx, dy * x * r


def _row_call(body, ins, row_ins, outs, acc_outs, *, rows, width, name):
    tr = _pick(rows, (256, 128))
    t_spec = pl.BlockSpec((tr, width), lambda i: (i, 0))
    r_spec = pl.BlockSpec((1, width), lambda i: (0, 0))
    return pl.pallas_call(
        body,
        grid=(rows // tr,),
        in_specs=[t_spec] * len(ins) + [r_spec] * len(row_ins),
        out_specs=tuple([t_spec] * len(outs) + [pl.BlockSpec(s, lambda i: (0, 0)) for s in acc_outs]),
        out_shape=tuple([jax.ShapeDtypeStruct((rows, width), d) for d in outs]
                        + [jax.ShapeDtypeStruct(s, F32) for s in acc_outs]),
        compiler_params=_params(("arbitrary",)),
        name=name,
    )(*ins, *row_ins)


def _rms_pre(x, g):
    L, D = x.shape

    def body(x_ref, g_ref, h_ref):
        xv = x_ref[...]
        h_ref[...] = (xv * _rstd(xv) * g_ref[...]).astype(BF16)

    return _row_call(body, [x], [g], [BF16], [], rows=L, width=D, name="rms_pre")[0]


def _post_pre(x, mixed, g_post, g_pre):
    L, D = x.shape

    def body(x_ref, m_ref, gp_ref, gq_ref, x1_ref, h2_ref):
        mv = m_ref[...]
        x1 = x_ref[...] + mv * _rstd(mv) * gp_ref[...]
        x1_ref[...] = x1
        h2_ref[...] = (x1 * _rstd(x1) * gq_ref[...]).astype(BF16)

    return _row_call(body, [x, mixed], [g_post, g_pre], [F32, BF16], [], rows=L, width=D, name="post_pre")


def _loss_bwd(x1, dn, g_post, target):
    L, D = x1.shape

    def body(x1_ref, dn_ref, t_ref, g_ref, dx2_ref, ddn_ref, dg_ref, loss_ref):
        @pl.when(pl.program_id(0) == 0)
        def _():
            dg_ref[...] = jnp.zeros_like(dg_ref)
            loss_ref[...] = jnp.zeros_like(loss_ref)

        dnv = dn_ref[...]
        g = g_ref[...]
        r = _rstd(dnv)
        err = x1_ref[...] + dnv * r * g - t_ref[...]
        loss_ref[...] += 0.5 * jnp.sum(jnp.mean(err * err, axis=-1, keepdims=True), axis=0, keepdims=True)
        dx2 = err * (1.0 / D)
        dx2_ref[...] = dx2
        ddn, dgr = _norm_bwd(dnv, r, g, dx2)
        ddn_ref[...] = ddn.astype(BF16)
        dg_ref[...] += jnp.sum(dgr, axis=0, keepdims=True)

    return _row_call(body, [x1, dn, target], [g_post], [F32, BF16], [(1, D), (1, 1)],
                     rows=L, width=D, name="loss_bwd")


def _norm_bwd_pair(x1, dh2, dx2, mixed, g_pre, g_post):
    L, D = x1.shape

    def body(x1_ref, dh_ref, dx2_ref, m_ref, gq_ref, gp_ref, dx1_ref, dm_ref, dgq_ref, dgp_ref):
        @pl.when(pl.program_id(0) == 0)
        def _():
            dgq_ref[...] = jnp.zeros_like(dgq_ref)
            dgp_ref[...] = jnp.zeros_like(dgp_ref)

        x1v = x1_ref[...]
        d1, dgq = _norm_bwd(x1v, _rstd(x1v), gq_ref[...], dh_ref[...])
        dx1 = dx2_ref[...] + d1
        dx1_ref[...] = dx1
        mv = m_ref[...]
        dm, dgp = _norm_bwd(mv, _rstd(mv), gp_ref[...], dx1)
        dm_ref[...] = dm.astype(BF16)
        dgq_ref[...] += jnp.sum(dgq, axis=0, keepdims=True)
        dgp_ref[...] += jnp.sum(dgp, axis=0, keepdims=True)

    return _row_call(body, [x1, dh2, dx2, mixed], [g_pre, g_post], [F32, BF16], [(1, D), (1, D)],
                     rows=L, width=D, name="norm_bwd_pair")


def _final_bwd(x, dh, dx1, g_pre):
    L, D = x.shape

    def body(x_ref, dh_ref, dx1_ref, g_ref, gx_ref, dg_ref):
        @pl.when(pl.program_id(0) == 0)
        def _():
            dg_ref[...] = jnp.zeros_like(dg_ref)

        xv = x_ref[...]
        d0, dg = _norm_bwd(xv, _rstd(xv), g_ref[...], dh_ref[...])
        gx_ref[...] = dx1_ref[...] + d0
        dg_ref[...] += jnp.sum(dg, axis=0, keepdims=True)

    return _row_call(body, [x, dh, dx1], [g_pre], [F32], [(1, D)], rows=L, width=D, name="final_bwd")


def _rope_tables(L):
    half = ROT_DIM // 2
    inv = ROPE_THETA ** (-jnp.arange(half, dtype=F32) * 2.0 / ROT_DIM)
    ang = jnp.arange(L, dtype=F32)[:, None] * inv[None, :]
    d = jnp.arange(LANES) % HEAD_DIM
    a = ang[:, d % half]
    cos_t = jnp.where(d[None, :] < ROT_DIM, jnp.cos(a), 1.0)
    sin_t = jnp.where(d[None, :] < half, -jnp.sin(a), jnp.where(d[None, :] < ROT_DIM, jnp.sin(a), 0.0))
    return cos_t.astype(F32), sin_t.astype(F32)


def _lane_lo(shape):
    return lax.broadcasted_iota(jnp.int32, shape, 1) < HEAD_DIM


def _rope(x, cos_t, sin_t):
    d = lax.broadcasted_iota(jnp.int32, x.shape, 1) % HEAD_DIM
    partner = jnp.where(d < ROT_DIM // 2, pltpu.roll(x, LANES - ROT_DIM // 2, 1), pltpu.roll(x, ROT_DIM // 2, 1))
    return x * cos_t + partner * sin_t


def _dup(kv, g):
    sw = pltpu.roll(kv, HEAD_DIM, 1)
    lo = _lane_lo(kv.shape)
    return jnp.where(lo, kv, sw) if g == 0 else jnp.where(lo, sw, kv)


def _attn_mask(n):
    qi = lax.broadcasted_iota(jnp.int32, (BLOCK, 2 * BLOCK), 0)
    kj = lax.broadcasted_iota(jnp.int32, (BLOCK, 2 * BLOCK), 1)
    rel = qi + BLOCK - kj
    return (rel >= 0) & (rel < BLOCK) & ((kj >= BLOCK) | (n > 0))


def _softmax_sink(s, mask, sink):
    s = jnp.where(mask, s, -1e30)
    m = jnp.maximum(jnp.max(s, axis=-1, keepdims=True), sink)
    e = jnp.where(mask, jnp.exp(s - m), 0.0)
    es = jnp.exp(sink - m)
    inv = 1.0 / (jnp.sum(e, axis=-1, keepdims=True) + es)
    return e * inv, es * inv


_NT = (((1,), (1,)), ((), ()))
_TN = (((0,), (0,)), ((), ()))


def _dot(a, b):
    return jnp.dot(a.astype(BF16), b.astype(BF16), preferred_element_type=F32)


def _dot_nt(a, b):
    return lax.dot_general(a.astype(BF16), b.astype(BF16), _NT, preferred_element_type=F32)


def _dot_tn(a, b):
    return lax.dot_general(a.astype(BF16), b.astype(BF16), _TN, preferred_element_type=F32)


def _attn_specs(nb):
    kcol, vcol = Q_W // LANES, Q_W // LANES + 1
    prev = lambda n: jnp.maximum(n - 1, 0)
    return [
        pl.BlockSpec((BLOCK, Q_W), lambda n: (n, 0)),
        pl.BlockSpec((BLOCK, LANES), lambda n: (n, kcol)),
        pl.BlockSpec((BLOCK, LANES), lambda n: (prev(n), kcol)),
        pl.BlockSpec((BLOCK, LANES), lambda n: (n, vcol)),
        pl.BlockSpec((BLOCK, LANES), lambda n: (prev(n), vcol)),
        pl.BlockSpec((BLOCK, LANES), lambda n: (n, 0)),
        pl.BlockSpec((BLOCK, LANES), lambda n: (prev(n), 0)),
        pl.BlockSpec((BLOCK, LANES), lambda n: (n, 0)),
        pl.BlockSpec((BLOCK, LANES), lambda n: (prev(n), 0)),
        pl.BlockSpec(memory_space=pltpu.SMEM),
    ]


def _attn_prep(refs):
    q_ref, kc_ref, kp_ref, vc_ref, vp_ref, cc_ref, cp_ref, sc_ref, sp_ref = refs
    cos_c, sin_c, cos_p, sin_p = cc_ref[...], sc_ref[...], cp_ref[...], sp_ref[...]
    k2 = jnp.concatenate([_rope(kp_ref[...], cos_p, sin_p), _rope(kc_ref[...], cos_c, sin_c)], axis=0)
    v2 = jnp.concatenate([vp_ref[...], vc_ref[...]], axis=0)
    lo2 = _lane_lo(v2.shape)
    kd = [_dup(k2, g).astype(BF16) for g in range(N_KV_HEADS)]
    vd = [_dup(v2, g) for g in range(N_KV_HEADS)]
    va = [jnp.where(lo2, v, 0.0).astype(BF16) for v in vd]
    vb = [jnp.where(lo2, 0.0, v).astype(BF16) for v in vd]
    return cos_c, sin_c, cos_p, sin_p, kd, va, vb


_SCALE = 1.0 / math.sqrt(HEAD_DIM)
_TILES = Q_W // LANES
_TILES_PER_KV = _TILES // N_KV_HEADS


def _attn_fwd(z, cos_t, sin_t, sinks, comm=None):
    L = z.shape[0]
    nb = L // BLOCK

    def body(*refs):
        ins, c_ins, outs, c_outs, _, sems = _split_refs(refs, 10, 1, comm)
        q_ref, kc_ref, kp_ref, vc_ref, vp_ref, cc_ref, cp_ref, sc_ref, sp_ref, sink_ref = ins
        o_ref = outs[0]
        n = pl.program_id(0)
        if comm:
            comm.run(c_ins, c_outs, sems, n == 0, n == nb // 2, n == nb - 1)
        cos_c, sin_c, _, _, kd, va, vb = _attn_prep((q_ref, kc_ref, kp_ref, vc_ref, vp_ref, cc_ref, cp_ref, sc_ref, sp_ref))
        mask = _attn_mask(n)
        lo = _lane_lo((BLOCK, LANES))
        for t in range(_TILES):
            g = t // _TILES_PER_KV
            qt = _rope(q_ref[:, t * LANES:(t + 1) * LANES], cos_c, sin_c) * _SCALE
            pa, _ = _softmax_sink(_dot_nt(jnp.where(lo, qt, 0.0), kd[g]), mask, sink_ref[0, 2 * t])
            pb, _ = _softmax_sink(_dot_nt(jnp.where(lo, 0.0, qt), kd[g]), mask, sink_ref[0, 2 * t + 1])
            o_ref[:, t * LANES:(t + 1) * LANES] = (_dot(pa, va[g]) + _dot(pb, vb[g])).astype(BF16)

    c_ins = comm.ins if comm else []
    c_shapes = comm.out_shapes if comm else []
    res = pl.pallas_call(
        body,
        grid=(nb,),
        in_specs=_attn_specs(nb) + [_ANY] * len(c_ins),
        out_specs=tuple([pl.BlockSpec((BLOCK, Q_W), lambda n: (n, 0))] + [_ANY] * len(c_shapes)),
        out_shape=tuple([jax.ShapeDtypeStruct((L, Q_W), BF16)] + c_shapes),
        scratch_shapes=comm.scratch() if comm else [],
        compiler_params=_host_params(("parallel",), comm),
        name="attn_fwd",
    )(z, z, z, z, z, cos_t, cos_t, sin_t, sin_t, sinks, *c_ins)
    return (res[0], list(res[1:])) if comm else res[0]


def _attn_bwd(z, d_o, cos_t, sin_t, sinks, comm=None):
    L = z.shape[0]
    nb = L // BLOCK

    def body(*refs):
        ins, c_ins, outs, c_outs, _, sems = _split_refs(refs, 11, 6, comm)
        q_ref, kc_ref, kp_ref, vc_ref, vp_ref, cc_ref, cp_ref, sc_ref, sp_ref, sink_ref, do_ref = ins
        dq_ref, dkc_ref, dkp_ref, dvc_ref, dvp_ref, ds_ref = outs
        n = pl.program_id(0)
        if comm:
            comm.run(c_ins, c_outs, sems, n == 0, n == nb // 2, n == nb - 1)

        @pl.when(n == 0)
        def _():
            ds_ref[...] = jnp.zeros_like(ds_ref)

        cos_c, sin_c, cos_p, sin_p, kd, va, vb = _attn_prep(
            (q_ref, kc_ref, kp_ref, vc_ref, vp_ref, cc_ref, cp_ref, sc_ref, sp_ref))
        mask = _attn_mask(n)
        lo = _lane_lo((BLOCK, LANES))
        lo2 = _lane_lo((2 * BLOCK, LANES))
        acc_k = [jnp.zeros((2 * BLOCK, LANES), F32) for _ in range(N_KV_HEADS)]
        acc_v = [jnp.zeros((2 * BLOCK, LANES), F32) for _ in range(N_KV_HEADS)]
        for t in range(_TILES):
            g = t // _TILES_PER_KV
            sl = slice(t * LANES, (t + 1) * LANES)
            qt = _rope(q_ref[:, sl], cos_c, sin_c) * _SCALE
            qa, qb = jnp.where(lo, qt, 0.0), jnp.where(lo, 0.0, qt)
            pa, psa = _softmax_sink(_dot_nt(qa, kd[g]), mask, sink_ref[0, 2 * t])
            pb, psb = _softmax_sink(_dot_nt(qb, kd[g]), mask, sink_ref[0, 2 * t + 1])
            dot_ = do_ref[:, sl]
            ot = _dot(pa, va[g]) + _dot(pb, vb[g])
            prod = dot_ * ot
            da = jnp.sum(jnp.where(lo, prod, 0.0), axis=-1, keepdims=True)
            db = jnp.sum(jnp.where(lo, 0.0, prod), axis=-1, keepdims=True)
            dsa = pa * (_dot_nt(dot_, va[g]) - da)
            dsb = pb * (_dot_nt(dot_, vb[g]) - db)
            dqt = jnp.where(lo, _dot(dsa, kd[g]), _dot(dsb, kd[g])) * _SCALE
            dq_ref[:, sl] = _rope(dqt, cos_c, -sin_c).astype(BF16)
            acc_k[g] = acc_k[g] + _dot_tn(dsa, qa) + _dot_tn(dsb, qb)
            acc_v[g] = acc_v[g] + _dot_tn(pa, jnp.where(lo, dot_, 0.0)) + _dot_tn(pb, jnp.where(lo, 0.0, dot_))
            row = jnp.where(_lane_lo((1, LANES)), -jnp.sum(psa * da, axis=0, keepdims=True),
                            -jnp.sum(psb * db, axis=0, keepdims=True))
            ds_ref[t:t + 1, :] += row
        fk = [a + pltpu.roll(a, HEAD_DIM, 1) for a in acc_k]
        fv = [a + pltpu.roll(a, HEAD_DIM, 1) for a in acc_v]
        dk2 = jnp.where(lo2, fk[0], fk[1])
        dv2 = jnp.where(lo2, fv[0], fv[1])
        dkp_ref[...] = _rope(dk2[:BLOCK], cos_p, -sin_p)
        dkc_ref[...] = _rope(dk2[BLOCK:], cos_c, -sin_c)
        dvp_ref[...] = dv2[:BLOCK]
        dvc_ref[...] = dv2[BLOCK:]

    blk = pl.BlockSpec((BLOCK, LANES), lambda n: (n, 0))
    kv = jax.ShapeDtypeStruct((L, LANES), F32)
    c_ins = comm.ins if comm else []
    c_shapes = comm.out_shapes if comm else []
    res = pl.pallas_call(
        body,
        grid=(nb,),
        in_specs=_attn_specs(nb) + [pl.BlockSpec((BLOCK, Q_W), lambda n: (n, 0))] + [_ANY] * len(c_ins),
        out_specs=tuple([pl.BlockSpec((BLOCK, Q_W), lambda n: (n, 0)), blk, blk, blk, blk,
                         pl.BlockSpec((_TILES, LANES), lambda n: (0, 0))] + [_ANY] * len(c_shapes)),
        out_shape=tuple([jax.ShapeDtypeStruct((L, Q_W), BF16), kv, kv, kv, kv,
                         jax.ShapeDtypeStruct((_TILES, LANES), F32)] + c_shapes),
        scratch_shapes=comm.scratch() if comm else [],
        compiler_params=_host_params(("arbitrary",), comm),
        name="attn_bwd",
    )(z, z, z, z, z, cos_t, cos_t, sin_t, sin_t, sinks, d_o, *c_ins)
    return (res[:6], list(res[6:])) if comm else res


def _kv_combine(dkc, dkp, dvc, dvp):
    L = dkc.shape[0]
    nb = L // BLOCK

    def body(kc_ref, kp_ref, vc_ref, vp_ref, dk_ref, dv_ref):
        live = jnp.where(pl.program_id(0) + 1 < nb, 1.0, 0.0)
        dk_ref[...] = (kc_ref[...] + live * kp_ref[...]).astype(BF16)
        dv_ref[...] = (vc_ref[...] + live * vp_ref[...]).astype(BF16)

    cur = pl.BlockSpec((BLOCK, LANES), lambda n: (n, 0))
    nxt = pl.BlockSpec((BLOCK, LANES), lambda n: (jnp.minimum(n + 1, nb - 1), 0))
    o = jax.ShapeDtypeStruct((L, LANES), BF16)
    return pl.pallas_call(body, grid=(nb,), in_specs=[cur, nxt, cur, nxt], out_specs=(cur, cur),
                          out_shape=(o, o), compiler_params=_params(("parallel",)), name="kv_combine")(dkc, dkp, dvc, dvp)


def _discretise(lr, li, ldt, br, bi):
    dt = jnp.exp(ldt)
    mag = jnp.exp(lr * dt)
    a_re, a_im = mag * jnp.cos(li * dt), mag * jnp.sin(li * dt)
    den = lr * lr + li * li
    nr, ni = a_re - 1.0, a_im
    coef_re = (nr * lr + ni * li) / den
    coef_im = (ni * lr - nr * li) / den
    return a_re, a_im, coef_re * br - coef_im * bi, coef_re * bi + coef_im * br


def _disc_specs(n):
    tr = _pick(n, (512,))
    cs = pl.BlockSpec((tr, 1), lambda i: (i, 0))
    ms = pl.BlockSpec((tr, SSM_GC), lambda i: (i, 0))
    return tr, cs, ms


def _disc_fwd(lr, li, ldt, br, bi):
    n = lr.shape[0]
    tr, cs, ms = _disc_specs(n)

    def body(lr_ref, li_ref, dt_ref, br_ref, bi_ref, o1, o2, o3, o4):
        r = _discretise(lr_ref[...], li_ref[...], dt_ref[...], br_ref[...], bi_ref[...])
        o1[...], o2[...], o3[...], o4[...] = r

    col = jax.ShapeDtypeStruct((n, 1), F32)
    mat = jax.ShapeDtypeStruct((n, SSM_GC), F32)
    return pl.pallas_call(body, grid=(n // tr,), in_specs=[cs, cs, cs, ms, ms], out_specs=(cs, cs, ms, ms),
                          out_shape=(col, col, mat, mat), compiler_params=_params(("parallel",)), name="disc_fwd")(
        lr, li, ldt, br, bi)


def _disc_bwd(lr, li, ldt, br, bi, gar, gai, gbr, gbi):
    n = lr.shape[0]
    tr, cs, ms = _disc_specs(n)

    def body(lr_ref, li_ref, dt_ref, br_ref, bi_ref, gar_ref, gai_ref, gbr_ref, gbi_ref, o_lr, o_li, o_dt, o_br, o_bi):
        _, vjp = jax.vjp(_discretise, lr_ref[...], li_ref[...], dt_ref[...], br_ref[...], bi_ref[...])
        g = vjp((gar_ref[...], gai_ref[...], gbr_ref[...], gbi_ref[...]))
        o_lr[...] = g[0]
        o_li[...] = g[1]
        o_dt[...] = jnp.sum(g[2].reshape(tr // SSM_P, SSM_P, 1), axis=1)
        o_br[...] = g[3]
        o_bi[...] = g[4]

    col = jax.ShapeDtypeStruct((n, 1), F32)
    mat = jax.ShapeDtypeStruct((n, SSM_GC), F32)
    return pl.pallas_call(
        body, grid=(n // tr,), in_specs=[cs, cs, cs, ms, ms, cs, cs, ms, ms],
        out_specs=(cs, cs, pl.BlockSpec((tr // SSM_P, 1), lambda i: (i, 0)), ms, ms),
        out_shape=(col, col, jax.ShapeDtypeStruct((n // SSM_P, 1), F32), mat, mat),
        compiler_params=_params(("parallel",)), name="disc_bwd")(lr, li, ldt, br, bi, gar, gai, gbr, gbi)


def _cpow(ar, ai, nsq):
    for _ in range(nsq):
        ar, ai = ar * ar - ai * ai, 2.0 * ar * ai
    return ar, ai


def _ssm_dims(L):
    tc = min(1024, L)
    seg = tc // SUBLANES
    assert seg & (seg - 1) == 0
    return tc, seg, L // tc, seg.bit_length() - 1


def _tile_rows(i):
    return pl.ds(pl.multiple_of(i * SUBLANES, SUBLANES), SUBLANES)


def _rows_to_segments(src_ref, dst_ref, seg):
    def body(i, _):
        dst_ref[_tile_rows(i), :] = src_ref[pl.ds(i, SUBLANES, stride=seg), :]
        return 0
    lax.fori_loop(0, seg, body, 0, unroll=8)


def _segments_to_rows(src_ref, dst_ref, seg):
    def body(i, _):
        dst_ref[pl.ds(i, SUBLANES, stride=seg), :] = src_ref[_tile_rows(i), :]
        return 0
    lax.fori_loop(0, seg, body, 0, unroll=8)


def _ssm_fwd(z, u_off, br_m, bi_m, cr_m, ci_m, a_re3, a_im3, d_row, comm=None):
    L = z.shape[0]
    ngb = br_m.shape[0]
    tc, seg, nc, nsq = _ssm_dims(L)
    ucol = u_off // LANES

    def body(*refs):
        ins, c_ins, outs, c_outs, scratch, sems = _split_refs(refs, 8, 3, comm)
        u_ref, br_ref, bi_ref, cr_ref, ci_ref, ar_ref, ai_ref, d_ref = ins
        y_ref, xr_ref, xi_ref = outs
        bur, bui, car_r, car_i, ini_r, ini_i, up, ys = scratch
        if comm:
            s = pl.program_id(0) * nc + pl.program_id(1)
            comm.run(c_ins, c_outs, sems, s == 0, s == (ngb * nc) // 2, s == ngb * nc - 1)

        @pl.when(pl.program_id(1) == 0)
        def _():
            car_r[...] = jnp.zeros_like(car_r)
            car_i[...] = jnp.zeros_like(car_i)

        _rows_to_segments(u_ref, up, seg)
        u = up[...]
        pr = _dot(u, br_ref[0])
        pi = _dot(u, bi_ref[0])
        for w in range(NW):
            bur[w] = pr[:, w * LANES:(w + 1) * LANES]
            bui[w] = pi[:, w * LANES:(w + 1) * LANES]
        ar = [jnp.broadcast_to(ar_ref[w], (SUBLANES, LANES)) for w in range(NW)]
        ai = [jnp.broadcast_to(ai_ref[w], (SUBLANES, LANES)) for w in range(NW)]

        def step(i, carry, store):
            xr, xi = carry
            rows = _tile_rows(i)
            nr, ni = [], []
            for w in range(NW):
                r = ar[w] * xr[w] - ai[w] * xi[w] + bur[w, rows, :]
                m = ar[w] * xi[w] + ai[w] * xr[w] + bui[w, rows, :]
                if store:
                    xr_ref[w, rows, :] = r
                    xi_ref[w, rows, :] = m
                nr.append(r)
                ni.append(m)
            return tuple(nr), tuple(ni)

        zero = tuple(jnp.zeros((SUBLANES, LANES), F32) for _ in range(NW))
        er, ei = lax.fori_loop(0, seg, functools.partial(step, store=False), (zero, zero), unroll=2)
        for w in range(NW):
            pr_, pi_ = _cpow(ar[w][0:1], ai[w][0:1], nsq)
            sr, si = car_r[w, 0:1, :], car_i[w, 0:1, :]
            for j in range(SUBLANES):
                ini_r[w, j:j + 1, :] = sr
                ini_i[w, j:j + 1, :] = si
                sr, si = (pr_ * sr - pi_ * si + er[w][j:j + 1], pr_ * si + pi_ * sr + ei[w][j:j + 1])
            car_r[w, 0:1, :] = sr
            car_i[w, 0:1, :] = si
        init = (tuple(ini_r[w] for w in range(NW)), tuple(ini_i[w] for w in range(NW)))
        lax.fori_loop(0, seg, functools.partial(step, store=True), init, unroll=2)
        acc = d_ref[...] * u
        for w in range(NW):
            sl = slice(w * LANES, (w + 1) * LANES)
            acc = acc + _dot(xr_ref[w], cr_ref[0, sl, :]) - _dot(xi_ref[w], ci_ref[0, sl, :])
        ys[...] = acc
        _segments_to_rows(ys, y_ref, seg)

    mat_b = pl.BlockSpec((1, LANES, NW * LANES), lambda b, k: (b, 0, 0))
    mat_c = pl.BlockSpec((1, NW * LANES, LANES), lambda b, k: (b, 0, 0))
    a_spec = pl.BlockSpec((NW, 1, LANES), lambda b, k: (b, 0, 0))
    x_spec = pl.BlockSpec((NW, tc, LANES), lambda b, k: (b, k, 0))
    xs = jax.ShapeDtypeStruct((ngb * NW, L, LANES), F32)
    st = pltpu.VMEM((NW, SUBLANES, LANES), F32)
    c_ins = comm.ins if comm else []
    c_shapes = comm.out_shapes if comm else []
    res = pl.pallas_call(
        body,
        grid=(ngb, nc),
        in_specs=[pl.BlockSpec((tc, LANES), lambda b, k: (k, b + ucol)), mat_b, mat_b, mat_c, mat_c, a_spec, a_spec,
                  pl.BlockSpec((1, LANES), lambda b, k: (0, b))] + [_ANY] * len(c_ins),
        out_specs=tuple([pl.BlockSpec((tc, LANES), lambda b, k: (k, b)), x_spec, x_spec] + [_ANY] * len(c_shapes)),
        out_shape=tuple([jax.ShapeDtypeStruct((L, ngb * LANES), F32), xs, xs] + c_shapes),
        scratch_shapes=[pltpu.VMEM((NW, tc, LANES), F32), pltpu.VMEM((NW, tc, LANES), F32), st, st, st, st,
                        pltpu.VMEM((tc, LANES), F32), pltpu.VMEM((tc, LANES), F32)]
        + (comm.scratch() if comm else []),
        compiler_params=_host_params(("arbitrary", "arbitrary"), comm),
        name="ssm_fwd",
    )(z, br_m, bi_m, cr_m, ci_m, a_re3, a_im3, d_row, *c_ins)
    return (res[:3], list(res[3:])) if comm else res


def _ssm_bwd(dy, z, u_off, xs_r, xs_i, br_m, bi_m, cr_m, ci_m, a_re3, a_im3, d_row):
    L = z.shape[0]
    ngb = br_m.shape[0]
    tc, seg, nc, nsq = _ssm_dims(L)
    ucol = u_off // LANES

    def body(dy_ref, u_ref, xr_ref, xi_ref, br_ref, bi_ref, cr_ref, ci_ref, ar_ref, ai_ref, d_ref,
             du_ref, gd_ref, gar_ref, gai_ref, gbr_ref, gbi_ref, gcr_ref, gci_ref,
             gr_s, gi_s, car_r, car_i, ini_r, ini_i, acc_r, acc_i, dyp, up, dus):
        k = pl.program_id(1)

        @pl.when(k == 0)
        def _():
            for ref in (car_r, car_i, acc_r, acc_i, gd_ref, gbr_ref, gbi_ref, gcr_ref, gci_ref):
                ref[...] = jnp.zeros_like(ref)

        _rows_to_segments(dy_ref, dyp, seg)
        _rows_to_segments(u_ref, up, seg)
        dy_v = dyp[...]
        u = up[...]
        g_re = _dot_nt(dy_v, cr_ref[0])
        g_im = -_dot_nt(dy_v, ci_ref[0])
        for w in range(NW):
            gr_s[w] = g_re[:, w * LANES:(w + 1) * LANES]
            gi_s[w] = g_im[:, w * LANES:(w + 1) * LANES]
        ar = [jnp.broadcast_to(ar_ref[w], (SUBLANES, LANES)) for w in range(NW)]
        ai = [jnp.broadcast_to(ai_ref[w], (SUBLANES, LANES)) for w in range(NW)]

        def step1(ii, carry):
            xr, xi = carry
            rows = _tile_rows(seg - 1 - ii)
            nr = tuple(ar[w] * xr[w] + ai[w] * xi[w] + gr_s[w, rows, :] for w in range(NW))
            ni = tuple(ar[w] * xi[w] - ai[w] * xr[w] + gi_s[w, rows, :] for w in range(NW))
            return nr, ni

        zero = tuple(jnp.zeros((SUBLANES, LANES), F32) for _ in range(NW))
        er, ei = lax.fori_loop(0, seg, step1, (zero, zero), unroll=2)
        for w in range(NW):
            pr_, pi_ = _cpow(ar[w][0:1], ai[w][0:1], nsq)
            sr, si = car_r[w, 0:1, :], car_i[w, 0:1, :]
            for j in reversed(range(SUBLANES)):
                ini_r[w, j:j + 1, :] = sr
                ini_i[w, j:j + 1, :] = si
                sr, si = (pr_ * sr + pi_ * si + er[w][j:j + 1], pr_ * si - pi_ * sr + ei[w][j:j + 1])
            car_r[w, 0:1, :] = sr
            car_i[w, 0:1, :] = si

        def step2(ii, carry):
            gxr, gxi, acr, aci = carry
            rows = _tile_rows(seg - 1 - ii)
            nr, ni, nar, nai = [], [], [], []
            for w in range(NW):
                xr_t, xi_t = xr_ref[w, rows, :], xi_ref[w, rows, :]
                nar.append(acr[w] + gxr[w] * xr_t + gxi[w] * xi_t)
                nai.append(aci[w] + gxi[w] * xr_t - gxr[w] * xi_t)
                r = ar[w] * gxr[w] + ai[w] * gxi[w] + gr_s[w, rows, :]
                m = ar[w] * gxi[w] - ai[w] * gxr[w] + gi_s[w, rows, :]
                gr_s[w, rows, :] = r
                gi_s[w, rows, :] = m
                nr.append(r)
                ni.append(m)
            return tuple(nr), tuple(ni), tuple(nar), tuple(nai)

        init = (tuple(ini_r[w] for w in range(NW)), tuple(ini_i[w] for w in range(NW)),
                tuple(acc_r[w] for w in range(NW)), tuple(acc_i[w] for w in range(NW)))
        _, _, acr, aci = lax.fori_loop(0, seg, step2, init, unroll=2)
        du = d_ref[...] * dy_v
        for w in range(NW):
            sl = slice(w * LANES, (w + 1) * LANES)
            acc_r[w] = acr[w]
            acc_i[w] = aci[w]
            gxr_w, gxi_w = gr_s[w], gi_s[w]
            du = du + _dot_nt(gxr_w, br_ref[0, :, sl]) + _dot_nt(gxi_w, bi_ref[0, :, sl])
            gbr_ref[0, :, sl] += _dot_tn(u, gxr_w)
            gbi_ref[0, :, sl] += _dot_tn(u, gxi_w)
            gcr_ref[0, sl, :] += _dot_tn(xr_ref[w], dy_v)
            gci_ref[0, sl, :] += _dot_tn(-xi_ref[w], dy_v)
        dus[...] = du
        _segments_to_rows(dus, du_ref, seg)
        gd_ref[...] += jnp.sum(dy_v * u, axis=0, keepdims=True)

        @pl.when(k == nc - 1)
        def _():
            for w in range(NW):
                gar_ref[w] = jnp.sum(acc_r[w], axis=0, keepdims=True)
                gai_ref[w] = jnp.sum(acc_i[w], axis=0, keepdims=True)

    rk = lambda k: nc - 1 - k
    mat_b = pl.BlockSpec((1, LANES, NW * LANES), lambda b, k: (b, 0, 0))
    mat_c = pl.BlockSpec((1, NW * LANES, LANES), lambda b, k: (b, 0, 0))
    a_spec = pl.BlockSpec((NW, 1, LANES), lambda b, k: (b, 0, 0))
    x_spec = pl.BlockSpec((NW, tc, LANES), lambda b, k: (b, rk(k), 0))
    st = pltpu.VMEM((NW, SUBLANES, LANES), F32)
    big = pltpu.VMEM((NW, tc, LANES), F32)
    return pl.pallas_call(
        body,
        grid=(ngb, nc),
        in_specs=[pl.BlockSpec((tc, LANES), lambda b, k: (rk(k), b)),
                  pl.BlockSpec((tc, LANES), lambda b, k: (rk(k), b + ucol)),
                  x_spec, x_spec, mat_b, mat_b, mat_c, mat_c, a_spec, a_spec,
                  pl.BlockSpec((1, LANES), lambda b, k: (0, b))],
        out_specs=(pl.BlockSpec((tc, LANES), lambda b, k: (rk(k), b)),
                   pl.BlockSpec((1, LANES), lambda b, k: (0, b)), a_spec, a_spec, mat_b, mat_b, mat_c, mat_c),
        out_shape=(jax.ShapeDtypeStruct((L, ngb * LANES), F32),
                   jax.ShapeDtypeStruct((1, ngb * LANES), F32),
                   jax.ShapeDtypeStruct((ngb * NW, 1, LANES), F32), jax.ShapeDtypeStruct((ngb * NW, 1, LANES), F32),
                   jax.ShapeDtypeStruct(br_m.shape, F32), jax.ShapeDtypeStruct(br_m.shape, F32),
                   jax.ShapeDtypeStruct(cr_m.shape, F32), jax.ShapeDtypeStruct(cr_m.shape, F32)),
        scratch_shapes=[big, big, st, st, st, st, st, st] + [pltpu.VMEM((tc, LANES), F32)] * 3,
        compiler_params=_params(("arbitrary", "arbitrary")),
        name="ssm_bwd",
    )(dy, z, xs_r, xs_i, br_m, bi_m, cr_m, ci_m, a_re3, a_im3, d_row)


def _block_diag_in(bb, ngb):
    t = bb.reshape(ngb, GROUPS_PER_BLOCK, SSM_P, SSM_GC).transpose(0, 1, 3, 2)
    eye = jnp.eye(GROUPS_PER_BLOCK, dtype=F32)
    m = t[:, :, :, None, :] * eye[None, :, None, :, None]
    return m.reshape(ngb, GROUPS_PER_BLOCK * SSM_GC, GROUPS_PER_BLOCK * SSM_P)


def _block_diag_out(c, ngb):
    t = c.reshape(ngb, GROUPS_PER_BLOCK, SSM_GC, SSM_P).transpose(0, 1, 3, 2)
    eye = jnp.eye(GROUPS_PER_BLOCK, dtype=F32)
    m = t[:, :, :, None, :] * eye[None, :, None, :, None]
    return m.reshape(ngb, GROUPS_PER_BLOCK * SSM_P, GROUPS_PER_BLOCK * SSM_GC)


def _diag_in(m, ngb):
    m5 = m.reshape(ngb, GROUPS_PER_BLOCK, SSM_GC, GROUPS_PER_BLOCK, SSM_P)
    d = jnp.diagonal(m5, axis1=1, axis2=3)
    return d.transpose(0, 3, 2, 1).reshape(ngb * GROUPS_PER_BLOCK * SSM_P, SSM_GC)


def _diag_out(m, ngb):
    m5 = m.reshape(ngb, GROUPS_PER_BLOCK, SSM_P, GROUPS_PER_BLOCK, SSM_GC)
    d = jnp.diagonal(m5, axis1=1, axis2=3)
    return d.transpose(0, 3, 2, 1).reshape(ngb * GROUPS_PER_BLOCK, SSM_GC, SSM_P)


_ANY = pl.BlockSpec(memory_space=pl.ANY)


class _Comm:
    def __init__(self, ins, out_shapes, n_sem, start, mid, finish):
        self.ins, self.out_shapes, self.n_sem = list(ins), list(out_shapes), n_sem
        self.start, self.mid, self.finish = start, mid, finish

    def scratch(self):
        return [pltpu.SemaphoreType.DMA((self.n_sem,)), pltpu.SemaphoreType.DMA((self.n_sem,)),
                pltpu.SemaphoreType.DMA((len(self.ins),))]

    def run(self, in_refs, out_refs, sems, first, mid, last):
        send, recv, local = sems

        @pl.when(first)
        def _():
            self.start(in_refs, out_refs, send, recv, local)

        @pl.when(mid)
        def _():
            self.mid(in_refs, out_refs, send, recv, local)

        @pl.when(last)
        def _():
            self.finish(in_refs, out_refs, send, recv, local)


def _split_refs(refs, n_in, n_out, comm):
    ci = len(comm.ins) if comm else 0
    co = len(comm.out_shapes) if comm else 0
    ins = refs[:n_in]
    c_ins = refs[n_in:n_in + ci]
    outs = refs[n_in + ci:n_in + ci + n_out]
    c_outs = refs[n_in + ci + n_out:n_in + ci + n_out + co]
    rest = refs[n_in + ci + n_out + co:]
    if comm:
        return ins, c_ins, outs, c_outs, rest[:-3], rest[-3:]
    return ins, c_ins, outs, c_outs, rest, ()


def _run_comm(comm, name):
    ni, no = len(comm.ins), len(comm.out_shapes)

    def body(*refs):
        args = (refs[:ni], refs[ni:ni + no]) + tuple(refs[ni + no:])
        comm.start(*args)
        comm.mid(*args)
        comm.finish(*args)

    return pl.pallas_call(
        body,
        in_specs=[_ANY] * ni,
        out_specs=tuple([_ANY] * no),
        out_shape=tuple(comm.out_shapes),
        scratch_shapes=comm.scratch(),
        compiler_params=pltpu.CompilerParams(has_side_effects=True),
        name=name,
    )(*comm.ins)


def _ag_comm(shards):
    n = len(shards)

    def env(ins, outs, send_sems, recv_sems):
        x, y, c = lax.axis_index("x"), lax.axis_index("y"), lax.axis_index("c")
        me, sibling = (x, y, c), (x, y, 1 - c)
        chips = [(1 - x, y), (x, 1 - y), (1 - x, 1 - y)]

        def copy(a, k, block, to, src=None):
            s = 4 * block[0] + 2 * block[1] + block[2]
            return pltpu.make_async_remote_copy(
                src_ref=outs[a].at[s] if src is None else src, dst_ref=outs[a].at[s],
                send_sem=send_sems.at[7 * a + k], recv_sem=recv_sems.at[7 * a + k],
                device_id=to, device_id_type=MESH)

        return c, me, sibling, chips, copy

    def own(ins, outs, local_sems, a):
        x, y, c = lax.axis_index("x"), lax.axis_index("y"), lax.axis_index("c")
        return pltpu.make_async_copy(ins[a], outs[a].at[4 * x + 2 * y + c], local_sems.at[a])

    def first_sends(ins, copy, me, sibling, chips, c, a):
        return [copy(a, 0, me, sibling, src=ins[a])] + [
            copy(a, 1 + j, me, (*chip, c), src=ins[a]) for j, chip in enumerate(chips)]

    def start(ins, outs, send_sems, recv_sems, local_sems):
        c, me, sibling, chips, copy = env(ins, outs, send_sems, recv_sems)
        for a in range(n):
            own(ins, outs, local_sems, a).start()
        for a in range(n):
            for cp in first_sends(ins, copy, me, sibling, chips, c, a):
                cp.start()

    def mid(ins, outs, send_sems, recv_sems, local_sems):
        c, me, sibling, chips, copy = env(ins, outs, send_sems, recv_sems)
        for a in range(n):
            for j, chip in enumerate(chips):
                copy(a, 1 + j, (*chip, c), me).wait_recv()
                copy(a, 4 + j, (*chip, c), sibling).start()

    def finish(ins, outs, send_sems, recv_sems, local_sems):
        c, me, sibling, chips, copy = env(ins, outs, send_sems, recv_sems)
        for a in range(n):
            copy(a, 0, sibling, me).wait_recv()
            for j, chip in enumerate(chips):
                copy(a, 4 + j, (*chip, 1 - c), me).wait_recv()
        for a in range(n):
            for cp in first_sends(ins, copy, me, sibling, chips, c, a):
                cp.wait_send()
            for j, chip in enumerate(chips):
                copy(a, 4 + j, (*chip, c), sibling).wait_send()
            own(ins, outs, local_sems, a).wait()

    return _Comm(shards, [jax.ShapeDtypeStruct((N_DEV,) + s.shape, s.dtype) for s in shards], 7 * n,
                 start, mid, finish)


def _sibling_comm(parts):
    n = len(parts)

    def copies(ins, outs, send_sems, recv_sems):
        x, y, c = lax.axis_index("x"), lax.axis_index("y"), lax.axis_index("c")
        return [pltpu.make_async_remote_copy(
            src_ref=ins[a].at[2 * q + (1 - c)], dst_ref=outs[a].at[q],
            send_sem=send_sems.at[4 * a + q], recv_sem=recv_sems.at[4 * a + q],
            device_id=(x, y, 1 - c), device_id_type=MESH) for a in range(n) for q in range(4)]

    def start(ins, outs, send_sems, recv_sems, local_sems):
        for cp in copies(ins, outs, send_sems, recv_sems):
            cp.start()

    def mid(ins, outs, send_sems, recv_sems, local_sems):
        pass

    def finish(ins, outs, send_sems, recv_sems, local_sems):
        for cp in copies(ins, outs, send_sems, recv_sems):
            cp.wait()

    return _Comm(parts, [jax.ShapeDtypeStruct((4,) + p.shape[1:], p.dtype) for p in parts], 4 * n,
                 start, mid, finish)


def _chips_comm(parts):
    n = len(parts)

    def copies(ins, outs, send_sems, recv_sems):
        x, y, c = lax.axis_index("x"), lax.axis_index("y"), lax.axis_index("c")
        chips = [(1 - x, y), (x, 1 - y), (1 - x, 1 - y)]
        return [pltpu.make_async_remote_copy(
            src_ref=ins[a].at[2 * px + py], dst_ref=outs[a].at[j],
            send_sem=send_sems.at[3 * a + j], recv_sem=recv_sems.at[3 * a + j],
            device_id=(px, py, c), device_id_type=MESH) for a in range(n) for j, (px, py) in enumerate(chips)]

    def start(ins, outs, send_sems, recv_sems, local_sems):
        for cp in copies(ins, outs, send_sems, recv_sems):
            cp.start()

    def mid(ins, outs, send_sems, recv_sems, local_sems):
        pass

    def finish(ins, outs, send_sems, recv_sems, local_sems):
        for cp in copies(ins, outs, send_sems, recv_sems):
            cp.wait()

    return _Comm(parts, [jax.ShapeDtypeStruct((3,) + p.shape[1:], p.dtype) for p in parts], 3 * n,
                 start, mid, finish)


def _sibling_add(part, recv, name):
    _, R, C = part.shape
    tr = _pick(R, (256, 128))
    c = lax.axis_index("c")

    def body(c_ref, p_ref, r_ref, o_ref, o16_ref):
        t = p_ref[...] + r_ref[...]
        o_ref[...] = t
        o16_ref[...] = t.astype(BF16)

    blk = pl.BlockSpec((1, tr, C), lambda q, i, c_ref: (q, i, 0))
    return pl.pallas_call(
        body,
        grid_spec=pltpu.PrefetchScalarGridSpec(
            num_scalar_prefetch=1,
            grid=(4, R // tr),
            in_specs=[pl.BlockSpec((1, tr, C), lambda q, i, c_ref: (2 * q + c_ref[0], i, 0)), blk],
            out_specs=(blk, blk),
        ),
        out_shape=(jax.ShapeDtypeStruct((4, R, C), F32), jax.ShapeDtypeStruct((4, R, C), BF16)),
        compiler_params=_params(("parallel", "parallel")),
        name=name,
    )(jnp.reshape(c, (1,)).astype(jnp.int32), part, recv)


def _adamw(w, g, m, v):
    m = ADAM_B1 * m + (1.0 - ADAM_B1) * g
    v = ADAM_B2 * v + (1.0 - ADAM_B2) * (g * g)
    m_hat = m / (1.0 - ADAM_B1 ** ADAM_STEP)
    v_hat = v / (1.0 - ADAM_B2 ** ADAM_STEP)
    delta = -ADAM_LR * (m_hat / (jnp.sqrt(v_hat) + ADAM_EPS) + ADAM_WD * w)
    return delta, m, v


def _adam_big(t, recv, w, m, v, name):
    _, R, C = t.shape
    tr = _pick(R, (256, 128))
    chip = 2 * lax.axis_index("x") + lax.axis_index("y")

    def body(q_ref, t_ref, r_ref, w_ref, m_ref, v_ref, g_ref, d_ref, nm_ref, nv_ref):
        g = t_ref[0] + r_ref[0].astype(F32) + r_ref[1].astype(F32) + r_ref[2].astype(F32)
        g_ref[...] = g
        d_ref[...], nm_ref[...], nv_ref[...] = _adamw(w_ref[...], g, m_ref[...], v_ref[...])

    blk = pl.BlockSpec((tr, C), lambda i, q_ref: (i, 0))
    o = jax.ShapeDtypeStruct((R, C), F32)
    return pl.pallas_call(
        body,
        grid_spec=pltpu.PrefetchScalarGridSpec(
            num_scalar_prefetch=1,
            grid=(R // tr,),
            in_specs=[pl.BlockSpec((1, tr, C), lambda i, q_ref: (q_ref[0], i, 0)),
                      pl.BlockSpec((3, tr, C), lambda i, q_ref: (0, i, 0)), blk, blk, blk],
            out_specs=(blk, blk, blk, blk),
        ),
        out_shape=(o, o, o, o),
        compiler_params=_params(("parallel",)),
        name=name,
    )(jnp.reshape(chip, (1,)).astype(jnp.int32), t, recv, w, m, v)


def _small_allreduce_adam(gbuf, wbuf, mbuf, vbuf):
    R = gbuf.shape[0]

    def body(g_ref, w_ref, m_ref, v_ref, gs_ref, d_ref, nm_ref, nv_ref, slots, send_sems, recv_sems):
        x, y, c = lax.axis_index("x"), lax.axis_index("y"), lax.axis_index("c")
        me = 4 * x + 2 * y + c
        copies = []
        for k in range(1, N_DEV):
            fx, fy, fc = (k >> 2) & 1, (k >> 1) & 1, k & 1
            px = x + fx - 2 * x * fx
            py = y + fy - 2 * y * fy
            pc = c + fc - 2 * c * fc
            cp = pltpu.make_async_remote_copy(
                src_ref=g_ref, dst_ref=slots.at[me],
                send_sem=send_sems.at[k - 1], recv_sem=recv_sems.at[k - 1],
                device_id=(px, py, pc), device_id_type=MESH)
            cp.start()
            copies.append((cp, 4 * px + 2 * py + pc))
        slots[me] = g_ref[...]
        for k, (cp, src) in enumerate(copies):
            pltpu.make_async_remote_copy(
                src_ref=g_ref, dst_ref=slots.at[src], send_sem=send_sems.at[k], recv_sem=recv_sems.at[k],
                device_id=(x, y, c), device_id_type=MESH).wait_recv()
        for cp, _ in copies:
            cp.wait_send()
        g = slots[0]
        for s in range(1, N_DEV):
            g = g + slots[s]
        gs_ref[...] = g
        d_ref[...], nm_ref[...], nv_ref[...] = _adamw(w_ref[...], g, m_ref[...], v_ref[...])

    o = jax.ShapeDtypeStruct((R, LANES), F32)
    vm = pl.BlockSpec(memory_space=pltpu.VMEM)
    return pl.pallas_call(
        body,
        in_specs=[vm, vm, vm, vm],
        out_specs=(vm, vm, vm, vm),
        out_shape=(o, o, o, o),
        scratch_shapes=[pltpu.VMEM((N_DEV, R, LANES), F32), pltpu.SemaphoreType.DMA((N_DEV - 1,)),
                        pltpu.SemaphoreType.DMA((N_DEV - 1,))],
        compiler_params=pltpu.CompilerParams(vmem_limit_bytes=VMEM_LIMIT, has_side_effects=True),
        name="small_allreduce_adam",
    )(gbuf, wbuf, mbuf, vbuf)


def _pack(items):
    rows, spans, r0 = [], [], 0
    for a in items:
        n = a.size
        nr = -(-n // LANES)
        rows.append(jnp.pad(a.reshape(-1).astype(F32), (0, nr * LANES - n)).reshape(nr, LANES))
        spans.append((r0, nr, a.shape))
        r0 += nr
    pad = -r0 % SUBLANES
    if pad:
        rows.append(jnp.zeros((pad, LANES), F32))
    return jnp.concatenate(rows, axis=0), spans


def _unpack(buf, spans):
    return [buf[r0:r0 + nr].reshape(-1)[:math.prod(shape)].reshape(shape) for r0, nr, shape in spans]


def kernel(x, norm_mix_pre, norm_mix_post, norm_mlp_pre, norm_mlp_post, w_in, sinks, lam_re, lam_im, log_dt, b_re, b_im, c_re, c_im, d_skip, w_glu, w_branch, w_out, w_up, w_down, loss_target, m_norm_mix_pre, m_norm_mix_post, m_norm_mlp_pre, m_norm_mlp_post, m_w_in, m_sinks, m_lam_re, m_lam_im, m_log_dt, m_b_re, m_b_im, m_c_re, m_c_im, m_d_skip, m_w_glu, m_w_branch, m_w_out, m_w_up, m_w_down, v_norm_mix_pre, v_norm_mix_post, v_norm_mlp_pre, v_norm_mlp_post, v_w_in, v_sinks, v_lam_re, v_lam_im, v_log_dt, v_b_re, v_b_im, v_c_re, v_c_im, v_d_skip, v_w_glu, v_w_branch, v_w_out, v_w_up, v_w_down):
    _, L, D = x.shape
    xs = x[0]
    tgt = loss_target[0]
    ssm_w = D // 2
    n_groups = ssm_w // SSM_GC
    ngb = n_groups // GROUPS_PER_BLOCK
    n_state = n_groups * SSM_P
    d_ff = w_up.shape[2] * N_DEV
    o_k, o_v, o_u = Q_W, Q_W + KV_W, Q_W + 2 * KV_W
    o_ga = o_u + ssm_w
    o_gs = o_ga + D

    big = {"w_in": w_in[0], "w_glu": w_glu[0], "w_branch": w_branch[0], "w_out": w_out[0],
           "w_up": w_up[0], "w_down": w_down[0]}
    col_sharded = ("w_in", "w_glu", "w_up")
    names = list(big)
    shard16 = {k: big[k].astype(BF16) for k in names}
    full = {}

    def gathered(keys, arrays):
        for k, g in zip(keys, arrays):
            _, r, c = g.shape
            full[k] = g.transpose(1, 0, 2).reshape(r, N_DEV * c) if k in col_sharded else g.reshape(N_DEV * r, c)

    def by_owner(k, g):
        r, c = big[k].shape
        return g.reshape(r, N_DEV, c).transpose(1, 0, 2) if k in col_sharded else g.reshape(N_DEV, r, c)

    gathered(["w_in"], _run_comm(_ag_comm([shard16["w_in"]]), "ag_w_in"))

    col = lambda a: a.reshape(n_state, 1)
    lr_c, li_c = col(lam_re[0]), col(lam_im[0])
    ldt_c = jnp.repeat(log_dt[0], SSM_P).reshape(n_state, 1)
    b_re_c, b_im_c = b_re[0].reshape(n_state, SSM_GC), b_im[0].reshape(n_state, SSM_GC)
    a_re, a_im, bb_re, bb_im = _disc_fwd(lr_c, li_c, ldt_c, b_re_c, b_im_c)
    a_re3 = a_re.reshape(n_state // LANES, 1, LANES)
    a_im3 = a_im.reshape(n_state // LANES, 1, LANES)
    br_m = _block_diag_in(bb_re, ngb).astype(BF16)
    bi_m = _block_diag_in(bb_im, ngb).astype(BF16)
    cr_m = _block_diag_out(c_re[0], ngb).astype(BF16)
    ci_m = _block_diag_out(c_im[0], ngb).astype(BF16)
    d_row = d_skip[0].reshape(1, ssm_w)
    cos_t, sin_t = _rope_tables(L)

    h = _rms_pre(xs, norm_mix_pre)
    z, g3 = _mm(h, full["w_in"], mode="nn", name="mm_z",
                comm=_ag_comm([shard16[k] for k in ("w_glu", "w_branch", "w_out")]))
    gathered(["w_glu", "w_branch", "w_out"], g3)
    wb_a, wb_s = full["w_branch"][:Q_W], full["w_branch"][Q_W:]
    o_attn, g1 = _attn_fwd(z, cos_t, sin_t, sinks, comm=_ag_comm([shard16["w_up"]]))
    gathered(["w_up"], g1)
    (y_pre, xs_r, xs_i), g1 = _ssm_fwd(z, o_u, br_m, bi_m, cr_m, ci_m, a_re3, a_im3, d_row,
                                       comm=_ag_comm([shard16["w_down"]]))
    gathered(["w_down"], g1)
    gy = _ew(lambda y: (_gelu(y),), [(y_pre, 0)], (BF16,), rows=L, ncols=ssm_w, name="gelu")
    zg = _mm(gy, full["w_glu"], mode="nn", name="mm_zg")
    o_ssm = _ew(lambda a, b: (a * _sigmoid(b),), [(zg, 0), (zg, ssm_w)], (BF16,), rows=L, ncols=ssm_w, name="glu")
    y_attn = _mm(o_attn, wb_a, mode="nn", name="mm_y_attn")
    y_ssm = _mm(o_ssm, wb_s, mode="nn", name="mm_y_ssm")
    mix = _ew(lambda ga, gs, ya, ys: (_sigmoid(ga) * ya + _sigmoid(gs) * ys,),
              [(z, o_ga), (z, o_gs), (y_attn, 0), (y_ssm, 0)], (BF16,), rows=L, ncols=D, name="mix")
    mixed = _mm(mix, full["w_out"], mode="nn", name="mm_mixed")
    x1, h2 = _post_pre(xs, mixed, norm_mix_post, norm_mlp_pre)

    def relu_sq(acc):
        a = jnp.maximum(acc, 0.0)
        return a, a * a

    act, act2 = _mm(h2, full["w_up"], mode="nn", name="mm_up", out_dtypes=(BF16, BF16), epi=relu_sq)
    dn = _mm(act2, full["w_down"], mode="nn", name="mm_down")
    dx2, d_dn, dg_mlp_post, loss_part = _loss_bwd(x1, dn, norm_mlp_post, tgt)

    d_pre = _mm(d_dn, full["w_down"], mode="nt", name="mm_d_act", out_dtypes=(BF16,),
                epi=lambda acc, a: (acc * (2.0 * a.astype(F32)),), extras=(act,))
    gw_down = _mm(act2, d_dn, mode="tn", name="mm_gw_down")
    p_down = by_owner("w_down", gw_down)
    dh2, (sib_down,) = _mm(d_pre, full["w_up"], mode="nt", name="mm_dh2", comm=_sibling_comm([p_down]))
    t_down, t16_down = _sibling_add(p_down, sib_down, "rs_add_w_down")
    gw_up, (chips_down,) = _mm(h2, d_pre, mode="tn", name="mm_gw_up", comm=_chips_comm([t16_down]))
    p_up = by_owner("w_up", gw_up)
    dx1, d_mixed, dg_mlp_pre, dg_mix_post = _norm_bwd_pair(x1, dh2, dx2, mixed, norm_mlp_pre, norm_mix_post)
    d_mix, (sib_up,) = _mm(d_mixed, full["w_out"], mode="nt", name="mm_d_mix", comm=_sibling_comm([p_up]))
    t_up, t16_up = _sibling_add(p_up, sib_up, "rs_add_w_up")
    gw_out = _mm(mix, d_mixed, mode="tn", name="mm_gw_out")

    def gate_bwd(dm, ga, gs, ya, ys):
        sa, ss = _sigmoid(ga), _sigmoid(gs)
        return dm * sa, dm * ss, dm * ya * sa * (1.0 - sa), dm * ys * ss * (1.0 - ss)

    d_ya, d_ys, d_zga, d_zgs = _ew(gate_bwd, [(d_mix, 0), (z, o_ga), (z, o_gs), (y_attn, 0), (y_ssm, 0)],
                                   (BF16, BF16, BF16, BF16), rows=L, ncols=D, name="gate_bwd")
    d_o_attn = _mm(d_ya, wb_a, mode="nt", name="mm_d_o_attn")
    gwb_a = _mm(o_attn, d_ya, mode="tn", name="mm_gwb_a")
    d_o_ssm = _mm(d_ys, wb_s, mode="nt", name="mm_d_o_ssm")
    gwb_s = _mm(o_ssm, d_ys, mode="tn", name="mm_gwb_s")

    def glu_bwd(do, a, b):
        s = _sigmoid(b)
        return do * s, do * a * s * (1.0 - s)

    d_zg_a, d_zg_b = _ew(glu_bwd, [(d_o_ssm, 0), (zg, 0), (zg, ssm_w)], (BF16, BF16), rows=L, ncols=ssm_w, name="glu_bwd")
    d_zg = jnp.concatenate([d_zg_a, d_zg_b], axis=1)
    dy_pre = _mm(d_zg, full["w_glu"], mode="nt", name="mm_d_gy",
                 epi=lambda acc, y: (acc * _gelu_grad(y),), extras=(y_pre,))
    gw_glu = _mm(gy, d_zg, mode="tn", name="mm_gw_glu")
    (du, g_dskip, g_ar3, g_ai3, g_br_m, g_bi_m, g_cr_m, g_ci_m) = _ssm_bwd(
        dy_pre, z, o_u, xs_r, xs_i, br_m, bi_m, cr_m, ci_m, a_re3, a_im3, d_row)
    g_lr, g_li, g_ldt, g_b_re, g_b_im = _disc_bwd(
        lr_c, li_c, ldt_c, b_re_c, b_im_c, g_ar3.reshape(n_state, 1), g_ai3.reshape(n_state, 1),
        _diag_in(g_br_m, ngb), _diag_in(g_bi_m, ngb))
    (dq, dkc, dkp, dvc, dvp, dsink_rows), (chips_up,) = _attn_bwd(z, d_o_attn, cos_t, sin_t, sinks,
                                                                    comm=_chips_comm([t16_up]))
    dk, dv = _kv_combine(dkc, dkp, dvc, dvp)
    dz = jnp.concatenate([dq, dk, dv, du.astype(BF16), d_zga, d_zgs], axis=1)
    mids = ["w_glu", "w_branch", "w_out"]
    p_mid = [by_owner("w_glu", gw_glu), by_owner("w_branch", jnp.concatenate([gwb_a, gwb_s], axis=0)),
             by_owner("w_out", gw_out)]
    sib_mid = _run_comm(_sibling_comm(p_mid), "rs_sibling_mid")
    t_mid = [_sibling_add(p, r, "rs_add_" + k) for k, p, r in zip(mids, p_mid, sib_mid)]
    dh, chips_mid = _mm(dz, full["w_in"], mode="nt", name="mm_dh", comm=_chips_comm([t16 for _, t16 in t_mid]))
    gw_in = _mm(h, dz, mode="tn", name="mm_gw_in")
    grad_x, dg_mix_pre = _final_bwd(xs, dh, dx1, norm_mix_pre)
    p_in = by_owner("w_in", gw_in)
    (sib_in,) = _run_comm(_sibling_comm([p_in]), "rs_sibling_w_in")
    t_in, t16_in = _sibling_add(p_in, sib_in, "rs_add_w_in")
    (chips_in,) = _run_comm(_chips_comm([t16_in]), "rs_chips_w_in")

    reduced = {"w_in": (t_in, chips_in), "w_up": (t_up, chips_up), "w_down": (t_down, chips_down)}
    for k, (t32, _), r in zip(mids, t_mid, chips_mid):
        reduced[k] = (t32, r)
    chip_sums = [reduced[k][0] for k in names]
    from_chips = [reduced[k][1] for k in names]
    moments = {"w_in": (m_w_in, v_w_in), "w_glu": (m_w_glu, v_w_glu), "w_branch": (m_w_branch, v_w_branch),
               "w_out": (m_w_out, v_w_out), "w_up": (m_w_up, v_w_up), "w_down": (m_w_down, v_w_down)}
    big_out = {}
    for k, t, r in zip(names, chip_sums, from_chips):
        mm_, vv_ = moments[k]
        big_out[k] = [o[None] for o in _adam_big(t, r, big[k], mm_[0], vv_[0], "adam_" + k)]

    dsink = jnp.stack([dsink_rows[:, 0], dsink_rows[:, HEAD_DIM]], axis=1).reshape(1, N_Q_HEADS)
    small_names = ["norm_mix_pre", "norm_mix_post", "norm_mlp_pre", "norm_mlp_post", "sinks", "lam_re", "lam_im",
                   "log_dt", "b_re", "b_im", "c_re", "c_im", "d_skip"]
    small_w = [norm_mix_pre, norm_mix_post, norm_mlp_pre, norm_mlp_post, sinks, lam_re, lam_im, log_dt,
               b_re, b_im, c_re, c_im, d_skip]
    small_m = [m_norm_mix_pre, m_norm_mix_post, m_norm_mlp_pre, m_norm_mlp_post, m_sinks, m_lam_re, m_lam_im,
               m_log_dt, m_b_re, m_b_im, m_c_re, m_c_im, m_d_skip]
    small_v = [v_norm_mix_pre, v_norm_mix_post, v_norm_mlp_pre, v_norm_mlp_post, v_sinks, v_lam_re, v_lam_im,
               v_log_dt, v_b_re, v_b_im, v_c_re, v_c_im, v_d_skip]
    small_g = [dg_mix_pre, dg_mix_post, dg_mlp_pre, dg_mlp_post, dsink,
               g_lr.reshape(lam_re.shape), g_li.reshape(lam_im.shape), g_ldt.reshape(log_dt.shape),
               g_b_re.reshape(b_re.shape), g_b_im.reshape(b_im.shape),
               _diag_out(g_cr_m, ngb).reshape(c_re.shape), _diag_out(g_ci_m, ngb).reshape(c_im.shape),
               g_dskip.reshape(d_skip.shape)]
    zero1 = jnp.zeros((1, 1), F32)
    gbuf, spans = _pack(small_g + [loss_part])
    wbuf, _ = _pack(small_w + [zero1])
    mbuf, _ = _pack(small_m + [zero1])
    vbuf, _ = _pack(small_v + [zero1])
    gs, ds, nms, nvs = [_unpack(b, spans) for b in _small_allreduce_adam(gbuf, wbuf, mbuf, vbuf)]
    loss = gs[-1].reshape(())

    order = ["norm_mix_pre", "norm_mix_post", "norm_mlp_pre", "norm_mlp_post", "w_in", "sinks", "lam_re", "lam_im",
             "log_dt", "b_re", "b_im", "c_re", "c_im", "d_skip", "w_glu", "w_branch", "w_out", "w_up", "w_down"]
    outs = [loss, grad_x[None]]
    for idx, src in enumerate((gs, ds, nms, nvs)):
        for k in order:
            outs.append(big_out[k][idx] if k in big_out else src[small_names.index(k)])
    return tuple(outs)
```

```python
import functools
import math

import jax
import jax.numpy as jnp
from jax import lax
from jax.experimental import pallas as pl
from jax.experimental.pallas import tpu as pltpu

F32 = jnp.float32
BF16 = jnp.bfloat16
MESH = pl.DeviceIdType.MESH

LANES = 128
SUBLANES = 8
VMEM_LIMIT = 56 * 1024 * 1024

HEAD_DIM = 64
N_Q_HEADS = 16
N_KV_HEADS = 2
Q_W = N_Q_HEADS * HEAD_DIM
KV_W = N_KV_HEADS * HEAD_DIM
BLOCK = 128
ROT_DIM = HEAD_DIM // 4
ROPE_THETA = 500000.0
SSM_GC = 16
SSM_P = 64
GROUPS_PER_BLOCK = 8
NW = GROUPS_PER_BLOCK * SSM_P // LANES
EPS = 1e-6
N_DEV = 8

ADAM_LR = 0.001
ADAM_B1 = 0.9
ADAM_B2 = 0.999
ADAM_EPS = 1e-08
ADAM_WD = 0.01
ADAM_STEP = 10


def _params(sem=None):
    return pltpu.CompilerParams(dimension_semantics=sem, vmem_limit_bytes=VMEM_LIMIT)


def _host_params(sem, comm):
    if comm:
        return pltpu.CompilerParams(dimension_semantics=("arbitrary",) * len(sem), vmem_limit_bytes=VMEM_LIMIT,
                                    has_side_effects=True)
    return _params(sem)


def _pick(dim, prefs):
    for p in prefs:
        if dim % p == 0:
            return p
    return dim


def _sigmoid(x):
    return 1.0 / (1.0 + jnp.exp(-x))


_GELU_C = math.sqrt(2.0 / math.pi)


def _gelu(x):
    return 0.5 * x * (1.0 + jnp.tanh(_GELU_C * (x + 0.044715 * x * x * x)))


def _gelu_grad(x):
    t = jnp.tanh(_GELU_C * (x + 0.044715 * x * x * x))
    return 0.5 * (1.0 + t) + 0.5 * x * (1.0 - t * t) * _GELU_C * (1.0 + 3.0 * 0.044715 * x * x)


_DIMS = {"nn": (((1,), (0,)), ((), ())), "nt": (((1,), (1,)), ((), ())), "tn": (((0,), (0,)), ((), ()))}


def _mm(a, b, *, mode, name, out_dtypes=(F32,), epi=None, extras=(), comm=None, a_win=None, b_win=None):
    ar, ac = a.shape[0], (a_win[1] if a_win else a.shape[1])
    br, bc = b.shape[0], (b_win[1] if b_win else b.shape[1])
    if mode == "nn":
        (M, K), (K2, N) = (ar, ac), (br, bc)
    elif mode == "nt":
        (M, K), (N, K2) = (ar, ac), (br, bc)
    else:
        (K, M), (K2, N) = (ar, ac), (br, bc)
    assert K == K2, (a.shape, b.shape, mode)
    tm = _pick(M, (1024, 1280, 640, 512, 256, 128))
    tn = _pick(N, (1024, 1280, 640, 512, 384, 256, 128))
    tk = K if K <= 2048 else _pick(K, (2048, 1280, 1024, 640, 512, 256, 128))
    nk = K // tk
    a_col_tile = tm if mode == "tn" else tk
    b_col_tile = tk if mode == "nt" else tn
    ao = a_win[0] // a_col_tile if a_win else 0
    bo = b_win[0] // b_col_tile if b_win else 0
    assert (not a_win or a_win[0] % a_col_tile == 0) and (not b_win or b_win[0] % b_col_tile == 0)
    n_ex = len(extras)
    n_out = len(out_dtypes)
    gi, gj = M // tm, N // tn
    steps = gi * gj * nk

    def body(*refs):
        ins, c_ins, o_refs, c_outs, scratch, sems = _split_refs(refs, 2 + n_ex, n_out, comm)
        a_ref, b_ref = ins[0], ins[1]
        ex_refs = ins[2:]
        if comm:
            s = (pl.program_id(0) * gj + pl.program_id(1)) * nk + pl.program_id(2)
            comm.run(c_ins, c_outs, sems, s == 0, s == steps // 2, s == steps - 1)

        def finish(r):
            outs = (r,) if epi is None else epi(r, *[e[...] for e in ex_refs])
            for o_ref, o in zip(o_refs, outs):
                o_ref[...] = o.astype(o_ref.dtype)

        part = lax.dot_general(a_ref[...].astype(BF16), b_ref[...].astype(BF16), _DIMS[mode],
                               preferred_element_type=F32)
        if nk == 1:
            finish(part)
            return
        acc = scratch[0]
        k = pl.program_id(2)

        @pl.when(k == 0)
        def _():
            acc[...] = part

        @pl.when((k > 0) & (k < nk - 1))
        def _():
            acc[...] += part

        @pl.when(k == nk - 1)
        def _():
            finish(acc[...] + part)

    if mode == "nn":
        a_spec = pl.BlockSpec((tm, tk), lambda i, j, k: (i, k + ao))
        b_spec = pl.BlockSpec((tk, tn), lambda i, j, k: (k, j + bo))
    elif mode == "nt":
        a_spec = pl.BlockSpec((tm, tk), lambda i, j, k: (i, k + ao))
        b_spec = pl.BlockSpec((tn, tk), lambda i, j, k: (j, k + bo))
    else:
        a_spec = pl.BlockSpec((tk, tm), lambda i, j, k: (k, i + ao))
        b_spec = pl.BlockSpec((tk, tn), lambda i, j, k: (k, j + bo))
    o_spec = pl.BlockSpec((tm, tn), lambda i, j, k: (i, j))
    c_ins = comm.ins if comm else []
    c_shapes = comm.out_shapes if comm else []
    res = pl.pallas_call(
        body,
        grid=(gi, gj, nk),
        in_specs=[a_spec, b_spec] + [o_spec] * n_ex + [_ANY] * len(c_ins),
        out_specs=tuple([o_spec] * n_out + [_ANY] * len(c_shapes)),
        out_shape=tuple([jax.ShapeDtypeStruct((M, N), d) for d in out_dtypes] + c_shapes),
        scratch_shapes=([pltpu.VMEM((tm, tn), F32)] if nk > 1 else []) + (comm.scratch() if comm else []),
        compiler_params=_host_params(("parallel", "parallel", "arbitrary"), comm),
        name=name,
    )(a, b, *extras, *c_ins)
    if comm:
        return (res[0] if n_out == 1 else res[:n_out]), list(res[n_out:])
    return res[0] if n_out == 1 else res


def _ew(fn, ins, out_dtypes, *, rows, ncols, name):
    g = ncols
    for _, off in ins:
        g = math.gcd(g, off)
    tc = _pick(g, (512, 256, 128))
    tr = _pick(rows, (1024, 512, 256, 128))
    n_in = len(ins)

    def body(*refs):
        outs = fn(*[r[...] for r in refs[:n_in]])
        for o_ref, o in zip(refs[n_in:], outs):
            o_ref[...] = o.astype(o_ref.dtype)

    def in_spec(off):
        ob = off // tc
        return pl.BlockSpec((tr, tc), lambda i, j: (i, j + ob))

    o_spec = pl.BlockSpec((tr, tc), lambda i, j: (i, j))
    res = pl.pallas_call(
        body,
        grid=(rows // tr, ncols // tc),
        in_specs=[in_spec(off) for _, off in ins],
        out_specs=tuple([o_spec] * len(out_dtypes)),
        out_shape=tuple(jax.ShapeDtypeStruct((rows, ncols), d) for d in out_dtypes),
        compiler_params=_params(("parallel", "parallel")),
        name=name,
    )(*[arr for arr, _ in ins])
    return res[0] if len(out_dtypes) == 1 else res


def _rstd(x):
    return lax.rsqrt(jnp.mean(x * x, axis=-1, keepdims=True) + EPS)


def _norm_bwd(x, r, g, dy):
    t = dy * g
    dx = r * t - x * (r * r * r) * jnp.mean(t * x, axis=-1, keepdims=True)
    return dx, dy * x * r


def _row_call(body, ins, row_ins, outs, acc_outs, *, rows, width, name, comm=None):
    tr = _pick(rows, (256, 128))
    steps = rows // tr
    t_spec = pl.BlockSpec((tr, width), lambda i: (i, 0))
    r_spec = pl.BlockSpec((1, width), lambda i: (0, 0))
    n_in, n_out = len(ins) + len(row_ins), len(outs) + len(acc_outs)

    def hosted(*refs):
        h_ins, c_ins, h_outs, c_outs, _, sems = _split_refs(refs, n_in, n_out, comm)
        i = pl.program_id(0)
        comm.run(c_ins, c_outs, sems, i == 0, i == steps // 2, i == steps - 1)
        body(*h_ins, *h_outs)

    c_ins = comm.ins if comm else []
    c_shapes = comm.out_shapes if comm else []
    res = pl.pallas_call(
        hosted if comm else body,
        grid=(steps,),
        in_specs=[t_spec] * len(ins) + [r_spec] * len(row_ins) + [_ANY] * len(c_ins),
        out_specs=tuple([t_spec] * len(outs) + [pl.BlockSpec(s, lambda i: (0, 0)) for s in acc_outs]
                        + [_ANY] * len(c_shapes)),
        out_shape=tuple([jax.ShapeDtypeStruct((rows, width), d) for d in outs]
                        + [jax.ShapeDtypeStruct(s, F32) for s in acc_outs] + c_shapes),
        scratch_shapes=comm.scratch() if comm else [],
        compiler_params=_host_params(("arbitrary",), comm),
        name=name,
    )(*ins, *row_ins, *c_ins)
    return (res[:n_out], list(res[n_out:])) if comm else res


def _rms_pre(x, g, comm):
    L, D = x.shape

    def body(x_ref, g_ref, h_ref):
        xv = x_ref[...]
        h_ref[...] = (xv * _rstd(xv) * g_ref[...]).astype(BF16)

    (h,), c_outs = _row_call(body, [x], [g], [BF16], [], rows=L, width=D, name="rms_pre", comm=comm)
    return h, c_outs


def _post_pre(x, mixed, g_post, g_pre):
    L, D = x.shape

    def body(x_ref, m_ref, gp_ref, gq_ref, x1_ref, h2_ref):
        mv = m_ref[...]
        x1 = x_ref[...] + mv * _rstd(mv) * gp_ref[...]
        x1_ref[...] = x1
        h2_ref[...] = (x1 * _rstd(x1) * gq_ref[...]).astype(BF16)

    return _row_call(body, [x, mixed], [g_post, g_pre], [F32, BF16], [], rows=L, width=D, name="post_pre")


def _loss_bwd(x1, dn, g_post, target):
    L, D = x1.shape

    def body(x1_ref, dn_ref, t_ref, g_ref, dx2_ref, ddn_ref, dg_ref, loss_ref):
        @pl.when(pl.program_id(0) == 0)
        def _():
            dg_ref[...] = jnp.zeros_like(dg_ref)
            loss_ref[...] = jnp.zeros_like(loss_ref)

        dnv = dn_ref[...]
        g = g_ref[...]
        r = _rstd(dnv)
        err = x1_ref[...] + dnv * r * g - t_ref[...]
        loss_ref[...] += 0.5 * jnp.sum(jnp.mean(err * err, axis=-1, keepdims=True), axis=0, keepdims=True)
        dx2 = err * (1.0 / D)
        dx2_ref[...] = dx2
        ddn, dgr = _norm_bwd(dnv, r, g, dx2)
        ddn_ref[...] = ddn.astype(BF16)
        dg_ref[...] += jnp.sum(dgr, axis=0, keepdims=True)

    return _row_call(body, [x1, dn, target], [g_post], [F32, BF16], [(1, D), (1, 1)],
                     rows=L, width=D, name="loss_bwd")


def _norm_bwd_pair(x1, dh2, dx2, mixed, g_pre, g_post):
    L, D = x1.shape

    def body(x1_ref, dh_ref, dx2_ref, m_ref, gq_ref, gp_ref, dx1_ref, dm_ref, dgq_ref, dgp_ref):
        @pl.when(pl.program_id(0) == 0)
        def _():
            dgq_ref[...] = jnp.zeros_like(dgq_ref)
            dgp_ref[...] = jnp.zeros_like(dgp_ref)

        x1v = x1_ref[...]
        d1, dgq = _norm_bwd(x1v, _rstd(x1v), gq_ref[...], dh_ref[...])
        dx1 = dx2_ref[...] + d1
        dx1_ref[...] = dx1
        mv = m_ref[...]
        dm, dgp = _norm_bwd(mv, _rstd(mv), gp_ref[...], dx1)
        dm_ref[...] = dm.astype(BF16)
        dgq_ref[...] += jnp.sum(dgq, axis=0, keepdims=True)
        dgp_ref[...] += jnp.sum(dgp, axis=0, keepdims=True)

    return _row_call(body, [x1, dh2, dx2, mixed], [g_pre, g_post], [F32, BF16], [(1, D), (1, D)],
                     rows=L, width=D, name="norm_bwd_pair")


def _final_bwd(x, dh, dx1, g_pre, comm):
    L, D = x.shape

    def body(x_ref, dh_ref, dx1_ref, g_ref, gx_ref, dg_ref):
        @pl.when(pl.program_id(0) == 0)
        def _():
            dg_ref[...] = jnp.zeros_like(dg_ref)

        xv = x_ref[...]
        d0, dg = _norm_bwd(xv, _rstd(xv), g_ref[...], dh_ref[...])
        gx_ref[...] = dx1_ref[...] + d0
        dg_ref[...] += jnp.sum(dg, axis=0, keepdims=True)

    return _row_call(body, [x, dh, dx1], [g_pre], [F32], [(1, D)], rows=L, width=D, name="final_bwd", comm=comm)


def _rope_tables(L):
    half = ROT_DIM // 2
    inv = ROPE_THETA ** (-jnp.arange(half, dtype=F32) * 2.0 / ROT_DIM)
    ang = jnp.arange(L, dtype=F32)[:, None] * inv[None, :]
    d = jnp.arange(LANES) % HEAD_DIM
    a = ang[:, d % half]
    cos_t = jnp.where(d[None, :] < ROT_DIM, jnp.cos(a), 1.0)
    sin_t = jnp.where(d[None, :] < half, -jnp.sin(a), jnp.where(d[None, :] < ROT_DIM, jnp.sin(a), 0.0))
    return cos_t.astype(F32), sin_t.astype(F32)


def _lane_lo(shape):
    return lax.broadcasted_iota(jnp.int32, shape, 1) < HEAD_DIM


def _rope(x, cos_t, sin_t):
    d = lax.broadcasted_iota(jnp.int32, x.shape, 1) % HEAD_DIM
    partner = jnp.where(d < ROT_DIM // 2, pltpu.roll(x, LANES - ROT_DIM // 2, 1), pltpu.roll(x, ROT_DIM // 2, 1))
    return x * cos_t + partner * sin_t


def _dup(kv, g):
    sw = pltpu.roll(kv, HEAD_DIM, 1)
    lo = _lane_lo(kv.shape)
    return jnp.where(lo, kv, sw) if g == 0 else jnp.where(lo, sw, kv)


def _attn_mask(n):
    qi = lax.broadcasted_iota(jnp.int32, (BLOCK, 2 * BLOCK), 0)
    kj = lax.broadcasted_iota(jnp.int32, (BLOCK, 2 * BLOCK), 1)
    rel = qi + BLOCK - kj
    return (rel >= 0) & (rel < BLOCK) & ((kj >= BLOCK) | (n > 0))


def _softmax_sink(s, mask, sink):
    s = jnp.where(mask, s, -1e30)
    m = jnp.maximum(jnp.max(s, axis=-1, keepdims=True), sink)
    e = jnp.where(mask, jnp.exp(s - m), 0.0)
    es = jnp.exp(sink - m)
    inv = 1.0 / (jnp.sum(e, axis=-1, keepdims=True) + es)
    return e * inv, es * inv


_NT = (((1,), (1,)), ((), ()))
_TN = (((0,), (0,)), ((), ()))


def _dot(a, b):
    return jnp.dot(a.astype(BF16), b.astype(BF16), preferred_element_type=F32)


def _dot_nt(a, b):
    return lax.dot_general(a.astype(BF16), b.astype(BF16), _NT, preferred_element_type=F32)


def _dot_tn(a, b):
    return lax.dot_general(a.astype(BF16), b.astype(BF16), _TN, preferred_element_type=F32)


def _attn_specs(nb):
    kcol, vcol = Q_W // LANES, Q_W // LANES + 1
    prev = lambda n: jnp.maximum(n - 1, 0)
    return [
        pl.BlockSpec((BLOCK, Q_W), lambda n: (n, 0)),
        pl.BlockSpec((BLOCK, LANES), lambda n: (n, kcol)),
        pl.BlockSpec((BLOCK, LANES), lambda n: (prev(n), kcol)),
        pl.BlockSpec((BLOCK, LANES), lambda n: (n, vcol)),
        pl.BlockSpec((BLOCK, LANES), lambda n: (prev(n), vcol)),
        pl.BlockSpec((BLOCK, LANES), lambda n: (n, 0)),
        pl.BlockSpec((BLOCK, LANES), lambda n: (prev(n), 0)),
        pl.BlockSpec((BLOCK, LANES), lambda n: (n, 0)),
        pl.BlockSpec((BLOCK, LANES), lambda n: (prev(n), 0)),
        pl.BlockSpec(memory_space=pltpu.SMEM),
    ]


def _attn_prep(refs):
    q_ref, kc_ref, kp_ref, vc_ref, vp_ref, cc_ref, cp_ref, sc_ref, sp_ref = refs
    cos_c, sin_c, cos_p, sin_p = cc_ref[...], sc_ref[...], cp_ref[...], sp_ref[...]
    k2 = jnp.concatenate([_rope(kp_ref[...], cos_p, sin_p), _rope(kc_ref[...], cos_c, sin_c)], axis=0)
    v2 = jnp.concatenate([vp_ref[...], vc_ref[...]], axis=0)
    lo2 = _lane_lo(v2.shape)
    kd = [_dup(k2, g).astype(BF16) for g in range(N_KV_HEADS)]
    vd = [_dup(v2, g) for g in range(N_KV_HEADS)]
    va = [jnp.where(lo2, v, 0.0).astype(BF16) for v in vd]
    vb = [jnp.where(lo2, 0.0, v).astype(BF16) for v in vd]
    return cos_c, sin_c, cos_p, sin_p, kd, va, vb


_SCALE = 1.0 / math.sqrt(HEAD_DIM)
_TILES = Q_W // LANES
_TILES_PER_KV = _TILES // N_KV_HEADS


def _attn_fwd(z, cos_t, sin_t, sinks, comm=None):
    L = z.shape[0]
    nb = L // BLOCK

    def body(*refs):
        ins, c_ins, outs, c_outs, _, sems = _split_refs(refs, 10, 1, comm)
        q_ref, kc_ref, kp_ref, vc_ref, vp_ref, cc_ref, cp_ref, sc_ref, sp_ref, sink_ref = ins
        o_ref = outs[0]
        n = pl.program_id(0)
        if comm:
            comm.run(c_ins, c_outs, sems, n == 0, n == nb // 2, n == nb - 1)
        cos_c, sin_c, _, _, kd, va, vb = _attn_prep((q_ref, kc_ref, kp_ref, vc_ref, vp_ref, cc_ref, cp_ref, sc_ref, sp_ref))
        mask = _attn_mask(n)
        lo = _lane_lo((BLOCK, LANES))
        for t in range(_TILES):
            g = t // _TILES_PER_KV
            qt = _rope(q_ref[:, t * LANES:(t + 1) * LANES], cos_c, sin_c) * _SCALE
            pa, _ = _softmax_sink(_dot_nt(jnp.where(lo, qt, 0.0), kd[g]), mask, sink_ref[0, 2 * t])
            pb, _ = _softmax_sink(_dot_nt(jnp.where(lo, 0.0, qt), kd[g]), mask, sink_ref[0, 2 * t + 1])
            o_ref[:, t * LANES:(t + 1) * LANES] = (_dot(pa, va[g]) + _dot(pb, vb[g])).astype(BF16)

    c_ins = comm.ins if comm else []
    c_shapes = comm.out_shapes if comm else []
    res = pl.pallas_call(
        body,
        grid=(nb,),
        in_specs=_attn_specs(nb) + [_ANY] * len(c_ins),
        out_specs=tuple([pl.BlockSpec((BLOCK, Q_W), lambda n: (n, 0))] + [_ANY] * len(c_shapes)),
        out_shape=tuple([jax.ShapeDtypeStruct((L, Q_W), BF16)] + c_shapes),
        scratch_shapes=comm.scratch() if comm else [],
        compiler_params=_host_params(("parallel",), comm),
        name="attn_fwd",
    )(z, z, z, z, z, cos_t, cos_t, sin_t, sin_t, sinks, *c_ins)
    return (res[0], list(res[1:])) if comm else res[0]


def _attn_bwd(z, d_o, cos_t, sin_t, sinks, dz, comm=None):
    L = z.shape[0]
    nb = L // BLOCK

    def body(*refs):
        ins, c_ins, outs, c_outs, _, sems = _split_refs(refs, 12, 6, comm)
        q_ref, kc_ref, kp_ref, vc_ref, vp_ref, cc_ref, cp_ref, sc_ref, sp_ref, sink_ref, do_ref, _ = ins
        dq_ref, dkc_ref, dkp_ref, dvc_ref, dvp_ref, ds_ref = outs
        n = pl.program_id(0)
        if comm:
            comm.run(c_ins, c_outs, sems, n == 0, n == nb // 2, n == nb - 1)

        @pl.when(n == 0)
        def _():
            ds_ref[...] = jnp.zeros_like(ds_ref)

        cos_c, sin_c, cos_p, sin_p, kd, va, vb = _attn_prep(
            (q_ref, kc_ref, kp_ref, vc_ref, vp_ref, cc_ref, cp_ref, sc_ref, sp_ref))
        mask = _attn_mask(n)
        lo = _lane_lo((BLOCK, LANES))
        lo2 = _lane_lo((2 * BLOCK, LANES))
        acc_k = [jnp.zeros((2 * BLOCK, LANES), F32) for _ in range(N_KV_HEADS)]
        acc_v = [jnp.zeros((2 * BLOCK, LANES), F32) for _ in range(N_KV_HEADS)]
        for t in range(_TILES):
            g = t // _TILES_PER_KV
            sl = slice(t * LANES, (t + 1) * LANES)
            qt = _rope(q_ref[:, sl], cos_c, sin_c) * _SCALE
            qa, qb = jnp.where(lo, qt, 0.0), jnp.where(lo, 0.0, qt)
            pa, psa = _softmax_sink(_dot_nt(qa, kd[g]), mask, sink_ref[0, 2 * t])
            pb, psb = _softmax_sink(_dot_nt(qb, kd[g]), mask, sink_ref[0, 2 * t + 1])
            dot_ = do_ref[:, sl]
            ot = _dot(pa, va[g]) + _dot(pb, vb[g])
            prod = dot_ * ot
            da = jnp.sum(jnp.where(lo, prod, 0.0), axis=-1, keepdims=True)
            db = jnp.sum(jnp.where(lo, 0.0, prod), axis=-1, keepdims=True)
            dsa = pa * (_dot_nt(dot_, va[g]) - da)
            dsb = pb * (_dot_nt(dot_, vb[g]) - db)
            dqt = jnp.where(lo, _dot(dsa, kd[g]), _dot(dsb, kd[g])) * _SCALE
            dq_ref[:, sl] = _rope(dqt, cos_c, -sin_c).astype(BF16)
            acc_k[g] = acc_k[g] + _dot_tn(dsa, qa) + _dot_tn(dsb, qb)
            acc_v[g] = acc_v[g] + _dot_tn(pa, jnp.where(lo, dot_, 0.0)) + _dot_tn(pb, jnp.where(lo, 0.0, dot_))
            row = jnp.where(_lane_lo((1, LANES)), -jnp.sum(psa * da, axis=0, keepdims=True),
                            -jnp.sum(psb * db, axis=0, keepdims=True))
            ds_ref[t:t + 1, :] += row
        fk = [a + pltpu.roll(a, HEAD_DIM, 1) for a in acc_k]
        fv = [a + pltpu.roll(a, HEAD_DIM, 1) for a in acc_v]
        dk2 = jnp.where(lo2, fk[0], fk[1])
        dv2 = jnp.where(lo2, fv[0], fv[1])
        dkp_ref[...] = _rope(dk2[:BLOCK], cos_p, -sin_p)
        dkc_ref[...] = _rope(dk2[BLOCK:], cos_c, -sin_c)
        dvp_ref[...] = dv2[:BLOCK]
        dvc_ref[...] = dv2[BLOCK:]

    blk = pl.BlockSpec((BLOCK, LANES), lambda n: (n, 0))
    kv = jax.ShapeDtypeStruct((L, LANES), F32)
    c_ins = comm.ins if comm else []
    c_shapes = comm.out_shapes if comm else []
    res = pl.pallas_call(
        body,
        grid=(nb,),
        in_specs=_attn_specs(nb) + [pl.BlockSpec((BLOCK, Q_W), lambda n: (n, 0)), _ANY] + [_ANY] * len(c_ins),
        out_specs=tuple([pl.BlockSpec((BLOCK, Q_W), lambda n: (n, 0)), blk, blk, blk, blk,
                         pl.BlockSpec((_TILES, LANES), lambda n: (0, 0))] + [_ANY] * len(c_shapes)),
        out_shape=tuple([jax.ShapeDtypeStruct(dz.shape, dz.dtype), kv, kv, kv, kv,
                         jax.ShapeDtypeStruct((_TILES, LANES), F32)] + c_shapes),
        scratch_shapes=comm.scratch() if comm else [],
        input_output_aliases={11: 0},
        compiler_params=_host_params(("arbitrary",), comm),
        name="attn_bwd",
    )(z, z, z, z, z, cos_t, cos_t, sin_t, sin_t, sinks, d_o, dz, *c_ins)
    return (res[:6], list(res[6:])) if comm else res


def _kv_combine(dkc, dkp, dvc, dvp, dz):
    L = dkc.shape[0]
    nb = L // BLOCK

    def body(kc_ref, kp_ref, vc_ref, vp_ref, dz_in, o_ref):
        live = jnp.where(pl.program_id(0) + 1 < nb, 1.0, 0.0)
        o_ref[:, :LANES] = (kc_ref[...] + live * kp_ref[...]).astype(BF16)
        o_ref[:, LANES:] = (vc_ref[...] + live * vp_ref[...]).astype(BF16)

    cur = pl.BlockSpec((BLOCK, LANES), lambda n: (n, 0))
    nxt = pl.BlockSpec((BLOCK, LANES), lambda n: (jnp.minimum(n + 1, nb - 1), 0))
    kv_block = Q_W // (2 * KV_W)
    return pl.pallas_call(body, grid=(nb,), in_specs=[cur, nxt, cur, nxt, _ANY],
                          out_specs=pl.BlockSpec((BLOCK, 2 * KV_W), lambda n: (n, kv_block)),
                          out_shape=jax.ShapeDtypeStruct(dz.shape, dz.dtype), input_output_aliases={4: 0},
                          compiler_params=_params(("parallel",)), name="kv_combine")(dkc, dkp, dvc, dvp, dz)


def _gate_bwd(d_mix, z, o_ga, y_attn, y_ssm):
    L, D = d_mix.shape
    tc = _pick(math.gcd(D, o_ga), (512, 256, 128))
    tr = _pick(L, (1024, 512, 256, 128))
    nd, gb = D // tc, o_ga // tc

    def body(dm_ref, g_ref, ya_ref, ys_ref, dy_ref, dz_ref):
        dm = dm_ref[...]
        s = _sigmoid(g_ref[...])
        y = jnp.where(pl.program_id(1) < nd, ya_ref[...], ys_ref[...])
        dy_ref[...] = (dm * s).astype(BF16)
        dz_ref[...] = (dm * y * s * (1.0 - s)).astype(BF16)

    blk = lambda f: pl.BlockSpec((tr, tc), f)
    return pl.pallas_call(
        body,
        grid=(L // tr, 2 * nd),
        in_specs=[blk(lambda i, j: (i, j % nd)), blk(lambda i, j: (i, j + gb)),
                  blk(lambda i, j: (i, jnp.minimum(j, nd - 1))), blk(lambda i, j: (i, jnp.maximum(j - nd, 0)))],
        out_specs=(blk(lambda i, j: (i, j)), blk(lambda i, j: (i, j + gb))),
        out_shape=(jax.ShapeDtypeStruct((L, 2 * D), BF16), jax.ShapeDtypeStruct(z.shape, BF16)),
        compiler_params=_params(("parallel", "arbitrary")),
        name="gate_bwd",
    )(d_mix, z, y_attn, y_ssm)


def _discretise(lr, li, ldt, br, bi):
    dt = jnp.exp(ldt)
    mag = jnp.exp(lr * dt)
    a_re, a_im = mag * jnp.cos(li * dt), mag * jnp.sin(li * dt)
    den = lr * lr + li * li
    nr, ni = a_re - 1.0, a_im
    coef_re = (nr * lr + ni * li) / den
    coef_im = (ni * lr - nr * li) / den
    return a_re, a_im, coef_re * br - coef_im * bi, coef_re * bi + coef_im * br


def _disc_specs(n):
    tr = _pick(n, (512,))
    cs = pl.BlockSpec((tr, 1), lambda i: (i, 0))
    ms = pl.BlockSpec((tr, SSM_GC), lambda i: (i, 0))
    return tr, cs, ms


def _disc_fwd(lr, li, ldt, br, bi):
    n = lr.shape[0]
    tr, cs, ms = _disc_specs(n)

    def body(lr_ref, li_ref, dt_ref, br_ref, bi_ref, o1, o2, o3, o4):
        r = _discretise(lr_ref[...], li_ref[...], dt_ref[...], br_ref[...], bi_ref[...])
        o1[...], o2[...], o3[...], o4[...] = r

    col = jax.ShapeDtypeStruct((n, 1), F32)
    mat = jax.ShapeDtypeStruct((n, SSM_GC), F32)
    return pl.pallas_call(body, grid=(n // tr,), in_specs=[cs, cs, cs, ms, ms], out_specs=(cs, cs, ms, ms),
                          out_shape=(col, col, mat, mat), compiler_params=_params(("parallel",)), name="disc_fwd")(
        lr, li, ldt, br, bi)


def _disc_bwd(lr, li, ldt, br, bi, gar, gai, gbr, gbi):
    n = lr.shape[0]
    tr, cs, ms = _disc_specs(n)

    def body(lr_ref, li_ref, dt_ref, br_ref, bi_ref, gar_ref, gai_ref, gbr_ref, gbi_ref, o_lr, o_li, o_dt, o_br, o_bi):
        _, vjp = jax.vjp(_discretise, lr_ref[...], li_ref[...], dt_ref[...], br_ref[...], bi_ref[...])
        g = vjp((gar_ref[...], gai_ref[...], gbr_ref[...], gbi_ref[...]))
        o_lr[...] = g[0]
        o_li[...] = g[1]
        o_dt[...] = jnp.sum(g[2].reshape(tr // SSM_P, SSM_P, 1), axis=1)
        o_br[...] = g[3]
        o_bi[...] = g[4]

    col = jax.ShapeDtypeStruct((n, 1), F32)
    mat = jax.ShapeDtypeStruct((n, SSM_GC), F32)
    return pl.pallas_call(
        body, grid=(n // tr,), in_specs=[cs, cs, cs, ms, ms, cs, cs, ms, ms],
        out_specs=(cs, cs, pl.BlockSpec((tr // SSM_P, 1), lambda i: (i, 0)), ms, ms),
        out_shape=(col, col, jax.ShapeDtypeStruct((n // SSM_P, 1), F32), mat, mat),
        compiler_params=_params(("parallel",)), name="disc_bwd")(lr, li, ldt, br, bi, gar, gai, gbr, gbi)


def _cpow(ar, ai, nsq):
    for _ in range(nsq):
        ar, ai = ar * ar - ai * ai, 2.0 * ar * ai
    return ar, ai


def _ssm_dims(L):
    tc = min(1024, L)
    seg = tc // SUBLANES
    assert seg & (seg - 1) == 0
    return tc, seg, L // tc, seg.bit_length() - 1


def _tile_rows(i):
    return pl.ds(pl.multiple_of(i * SUBLANES, SUBLANES), SUBLANES)


def _rows_to_segments(src_ref, dst_ref, seg):
    def body(i, _):
        dst_ref[_tile_rows(i), :] = src_ref[pl.ds(i, SUBLANES, stride=seg), :]
        return 0
    lax.fori_loop(0, seg, body, 0, unroll=8)


def _segments_to_rows(src_ref, dst_ref, seg):
    def body(i, _):
        dst_ref[pl.ds(i, SUBLANES, stride=seg), :] = src_ref[_tile_rows(i), :]
        return 0
    lax.fori_loop(0, seg, body, 0, unroll=8)


def _ssm_fwd(z, u_off, br_m, bi_m, cr_m, ci_m, a_re3, a_im3, d_row, comm=None):
    L = z.shape[0]
    ngb = br_m.shape[0]
    tc, seg, nc, nsq = _ssm_dims(L)
    ucol = u_off // LANES

    def body(*refs):
        ins, c_ins, outs, c_outs, scratch, sems = _split_refs(refs, 8, 3, comm)
        u_ref, br_ref, bi_ref, cr_ref, ci_ref, ar_ref, ai_ref, d_ref = ins
        y_ref, xr_ref, xi_ref = outs
        bur, bui, car_r, car_i, ini_r, ini_i, up, ys = scratch
        if comm:
            s = pl.program_id(0) * nc + pl.program_id(1)
            comm.run(c_ins, c_outs, sems, s == 0, s == (ngb * nc) // 2, s == ngb * nc - 1)

        @pl.when(pl.program_id(1) == 0)
        def _():
            car_r[...] = jnp.zeros_like(car_r)
            car_i[...] = jnp.zeros_like(car_i)

        _rows_to_segments(u_ref, up, seg)
        u = up[...]
        pr = _dot(u, br_ref[0])
        pi = _dot(u, bi_ref[0])
        for w in range(NW):
            bur[w] = pr[:, w * LANES:(w + 1) * LANES]
            bui[w] = pi[:, w * LANES:(w + 1) * LANES]
        ar = [jnp.broadcast_to(ar_ref[w], (SUBLANES, LANES)) for w in range(NW)]
        ai = [jnp.broadcast_to(ai_ref[w], (SUBLANES, LANES)) for w in range(NW)]

        def step(i, carry, store):
            xr, xi = carry
            rows = _tile_rows(i)
            nr, ni = [], []
            for w in range(NW):
                r = ar[w] * xr[w] - ai[w] * xi[w] + bur[w, rows, :]
                m = ar[w] * xi[w] + ai[w] * xr[w] + bui[w, rows, :]
                if store:
                    xr_ref[w, rows, :] = r
                    xi_ref[w, rows, :] = m
                nr.append(r)
                ni.append(m)
            return tuple(nr), tuple(ni)

        zero = tuple(jnp.zeros((SUBLANES, LANES), F32) for _ in range(NW))
        er, ei = lax.fori_loop(0, seg, functools.partial(step, store=False), (zero, zero), unroll=2)
        for w in range(NW):
            pr_, pi_ = _cpow(ar[w][0:1], ai[w][0:1], nsq)
            sr, si = car_r[w, 0:1, :], car_i[w, 0:1, :]
            for j in range(SUBLANES):
                ini_r[w, j:j + 1, :] = sr
                ini_i[w, j:j + 1, :] = si
                sr, si = (pr_ * sr - pi_ * si + er[w][j:j + 1], pr_ * si + pi_ * sr + ei[w][j:j + 1])
            car_r[w, 0:1, :] = sr
            car_i[w, 0:1, :] = si
        init = (tuple(ini_r[w] for w in range(NW)), tuple(ini_i[w] for w in range(NW)))
        lax.fori_loop(0, seg, functools.partial(step, store=True), init, unroll=2)
        acc = d_ref[...] * u
        for w in range(NW):
            sl = slice(w * LANES, (w + 1) * LANES)
            acc = acc + _dot(xr_ref[w], cr_ref[0, sl, :]) - _dot(xi_ref[w], ci_ref[0, sl, :])
        ys[...] = acc
        _segments_to_rows(ys, y_ref, seg)

    mat_b = pl.BlockSpec((1, LANES, NW * LANES), lambda b, k: (b, 0, 0))
    mat_c = pl.BlockSpec((1, NW * LANES, LANES), lambda b, k: (b, 0, 0))
    a_spec = pl.BlockSpec((NW, 1, LANES), lambda b, k: (b, 0, 0))
    x_spec = pl.BlockSpec((NW, tc, LANES), lambda b, k: (b, k, 0))
    xs = jax.ShapeDtypeStruct((ngb * NW, L, LANES), F32)
    st = pltpu.VMEM((NW, SUBLANES, LANES), F32)
    c_ins = comm.ins if comm else []
    c_shapes = comm.out_shapes if comm else []
    res = pl.pallas_call(
        body,
        grid=(ngb, nc),
        in_specs=[pl.BlockSpec((tc, LANES), lambda b, k: (k, b + ucol)), mat_b, mat_b, mat_c, mat_c, a_spec, a_spec,
                  pl.BlockSpec((1, LANES), lambda b, k: (0, b))] + [_ANY] * len(c_ins),
        out_specs=tuple([pl.BlockSpec((tc, LANES), lambda b, k: (k, b)), x_spec, x_spec] + [_ANY] * len(c_shapes)),
        out_shape=tuple([jax.ShapeDtypeStruct((L, ngb * LANES), F32), xs, xs] + c_shapes),
        scratch_shapes=[pltpu.VMEM((NW, tc, LANES), F32), pltpu.VMEM((NW, tc, LANES), F32), st, st, st, st,
                        pltpu.VMEM((tc, LANES), F32), pltpu.VMEM((tc, LANES), F32)]
        + (comm.scratch() if comm else []),
        compiler_params=_host_params(("arbitrary", "arbitrary"), comm),
        name="ssm_fwd",
    )(z, br_m, bi_m, cr_m, ci_m, a_re3, a_im3, d_row, *c_ins)
    return (res[:3], list(res[3:])) if comm else res


def _ssm_bwd(dy, z, u_off, xs_r, xs_i, br_m, bi_m, cr_m, ci_m, a_re3, a_im3, d_row, dz):
    L = z.shape[0]
    ngb = br_m.shape[0]
    tc, seg, nc, nsq = _ssm_dims(L)
    ucol = u_off // LANES

    def body(dy_ref, u_ref, xr_ref, xi_ref, br_ref, bi_ref, cr_ref, ci_ref, ar_ref, ai_ref, d_ref, dz_in,
             du_ref, gd_ref, gar_ref, gai_ref, gbr_ref, gbi_ref, gcr_ref, gci_ref,
             gr_s, gi_s, car_r, car_i, ini_r, ini_i, acc_r, acc_i, dyp, up, dus, dun):
        k = pl.program_id(1)

        @pl.when(k == 0)
        def _():
            for ref in (car_r, car_i, acc_r, acc_i, gd_ref, gbr_ref, gbi_ref, gcr_ref, gci_ref):
                ref[...] = jnp.zeros_like(ref)

        _rows_to_segments(dy_ref, dyp, seg)
        _rows_to_segments(u_ref, up, seg)
        dy_v = dyp[...]
        u = up[...]
        g_re = _dot_nt(dy_v, cr_ref[0])
        g_im = -_dot_nt(dy_v, ci_ref[0])
        for w in range(NW):
            gr_s[w] = g_re[:, w * LANES:(w + 1) * LANES]
            gi_s[w] = g_im[:, w * LANES:(w + 1) * LANES]
        ar = [jnp.broadcast_to(ar_ref[w], (SUBLANES, LANES)) for w in range(NW)]
        ai = [jnp.broadcast_to(ai_ref[w], (SUBLANES, LANES)) for w in range(NW)]

        def step1(ii, carry):
            xr, xi = carry
            rows = _tile_rows(seg - 1 - ii)
            nr = tuple(ar[w] * xr[w] + ai[w] * xi[w] + gr_s[w, rows, :] for w in range(NW))
            ni = tuple(ar[w] * xi[w] - ai[w] * xr[w] + gi_s[w, rows, :] for w in range(NW))
            return nr, ni

        zero = tuple(jnp.zeros((SUBLANES, LANES), F32) for _ in range(NW))
        er, ei = lax.fori_loop(0, seg, step1, (zero, zero), unroll=2)
        for w in range(NW):
            pr_, pi_ = _cpow(ar[w][0:1], ai[w][0:1], nsq)
            sr, si = car_r[w, 0:1, :], car_i[w, 0:1, :]
            for j in reversed(range(SUBLANES)):
                ini_r[w, j:j + 1, :] = sr
                ini_i[w, j:j + 1, :] = si
                sr, si = (pr_ * sr + pi_ * si + er[w][j:j + 1], pr_ * si - pi_ * sr + ei[w][j:j + 1])
            car_r[w, 0:1, :] = sr
            car_i[w, 0:1, :] = si

        def step2(ii, carry):
            gxr, gxi, acr, aci = carry
            rows = _tile_rows(seg - 1 - ii)
            nr, ni, nar, nai = [], [], [], []
            for w in range(NW):
                xr_t, xi_t = xr_ref[w, rows, :], xi_ref[w, rows, :]
                nar.append(acr[w] + gxr[w] * xr_t + gxi[w] * xi_t)
                nai.append(aci[w] + gxi[w] * xr_t - gxr[w] * xi_t)
                r = ar[w] * gxr[w] + ai[w] * gxi[w] + gr_s[w, rows, :]
                m = ar[w] * gxi[w] - ai[w] * gxr[w] + gi_s[w, rows, :]
                gr_s[w, rows, :] = r
                gi_s[w, rows, :] = m
                nr.append(r)
                ni.append(m)
            return tuple(nr), tuple(ni), tuple(nar), tuple(nai)

        init = (tuple(ini_r[w] for w in range(NW)), tuple(ini_i[w] for w in range(NW)),
                tuple(acc_r[w] for w in range(NW)), tuple(acc_i[w] for w in range(NW)))
        _, _, acr, aci = lax.fori_loop(0, seg, step2, init, unroll=2)
        du = d_ref[...] * dy_v
        for w in range(NW):
            sl = slice(w * LANES, (w + 1) * LANES)
            acc_r[w] = acr[w]
            acc_i[w] = aci[w]
            gxr_w, gxi_w = gr_s[w], gi_s[w]
            du = du + _dot_nt(gxr_w, br_ref[0, :, sl]) + _dot_nt(gxi_w, bi_ref[0, :, sl])
            gbr_ref[0, :, sl] += _dot_tn(u, gxr_w)
            gbi_ref[0, :, sl] += _dot_tn(u, gxi_w)
            gcr_ref[0, sl, :] += _dot_tn(xr_ref[w], dy_v)
            gci_ref[0, sl, :] += _dot_tn(-xi_ref[w], dy_v)
        dus[...] = du
        _segments_to_rows(dus, dun, seg)
        du_ref[...] = dun[...].astype(BF16)
        gd_ref[...] += jnp.sum(dy_v * u, axis=0, keepdims=True)

        @pl.when(k == nc - 1)
        def _():
            for w in range(NW):
                gar_ref[w] = jnp.sum(acc_r[w], axis=0, keepdims=True)
                gai_ref[w] = jnp.sum(acc_i[w], axis=0, keepdims=True)

    rk = lambda k: nc - 1 - k
    mat_b = pl.BlockSpec((1, LANES, NW * LANES), lambda b, k: (b, 0, 0))
    mat_c = pl.BlockSpec((1, NW * LANES, LANES), lambda b, k: (b, 0, 0))
    a_spec = pl.BlockSpec((NW, 1, LANES), lambda b, k: (b, 0, 0))
    x_spec = pl.BlockSpec((NW, tc, LANES), lambda b, k: (b, rk(k), 0))
    st = pltpu.VMEM((NW, SUBLANES, LANES), F32)
    big = pltpu.VMEM((NW, tc, LANES), F32)
    return pl.pallas_call(
        body,
        grid=(ngb, nc),
        in_specs=[pl.BlockSpec((tc, LANES), lambda b, k: (rk(k), b)),
                  pl.BlockSpec((tc, LANES), lambda b, k: (rk(k), b + ucol)),
                  x_spec, x_spec, mat_b, mat_b, mat_c, mat_c, a_spec, a_spec,
                  pl.BlockSpec((1, LANES), lambda b, k: (0, b)), _ANY],
        out_specs=(pl.BlockSpec((tc, LANES), lambda b, k: (rk(k), b + ucol)),
                   pl.BlockSpec((1, LANES), lambda b, k: (0, b)), a_spec, a_spec, mat_b, mat_b, mat_c, mat_c),
        out_shape=(jax.ShapeDtypeStruct(dz.shape, dz.dtype),
                   jax.ShapeDtypeStruct((1, ngb * LANES), F32),
                   jax.ShapeDtypeStruct((ngb * NW, 1, LANES), F32), jax.ShapeDtypeStruct((ngb * NW, 1, LANES), F32),
                   jax.ShapeDtypeStruct(br_m.shape, F32), jax.ShapeDtypeStruct(br_m.shape, F32),
                   jax.ShapeDtypeStruct(cr_m.shape, F32), jax.ShapeDtypeStruct(cr_m.shape, F32)),
        scratch_shapes=[big, big, st, st, st, st, st, st] + [pltpu.VMEM((tc, LANES), F32)] * 4,
        input_output_aliases={11: 0},
        compiler_params=_params(("arbitrary", "arbitrary")),
        name="ssm_bwd",
    )(dy, z, xs_r, xs_i, br_m, bi_m, cr_m, ci_m, a_re3, a_im3, d_row, dz)


def _block_diag_in(bb, ngb):
    t = bb.reshape(ngb, GROUPS_PER_BLOCK, SSM_P, SSM_GC).transpose(0, 1, 3, 2)
    eye = jnp.eye(GROUPS_PER_BLOCK, dtype=F32)
    m = t[:, :, :, None, :] * eye[None, :, None, :, None]
    return m.reshape(ngb, GROUPS_PER_BLOCK * SSM_GC, GROUPS_PER_BLOCK * SSM_P)


def _block_diag_out(c, ngb):
    t = c.reshape(ngb, GROUPS_PER_BLOCK, SSM_GC, SSM_P).transpose(0, 1, 3, 2)
    eye = jnp.eye(GROUPS_PER_BLOCK, dtype=F32)
    m = t[:, :, :, None, :] * eye[None, :, None, :, None]
    return m.reshape(ngb, GROUPS_PER_BLOCK * SSM_P, GROUPS_PER_BLOCK * SSM_GC)


def _diag_in(m, ngb):
    m5 = m.reshape(ngb, GROUPS_PER_BLOCK, SSM_GC, GROUPS_PER_BLOCK, SSM_P)
    d = jnp.diagonal(m5, axis1=1, axis2=3)
    return d.transpose(0, 3, 2, 1).reshape(ngb * GROUPS_PER_BLOCK * SSM_P, SSM_GC)


def _diag_out(m, ngb):
    m5 = m.reshape(ngb, GROUPS_PER_BLOCK, SSM_P, GROUPS_PER_BLOCK, SSM_GC)
    d = jnp.diagonal(m5, axis1=1, axis2=3)
    return d.transpose(0, 3, 2, 1).reshape(ngb * GROUPS_PER_BLOCK, SSM_GC, SSM_P)


_ANY = pl.BlockSpec(memory_space=pl.ANY)


class _Comm:
    def __init__(self, ins, out_shapes, n_sem, start, mid, finish):
        self.ins, self.out_shapes, self.n_sem = list(ins), list(out_shapes), n_sem
        self.start, self.mid, self.finish = start, mid, finish

    def scratch(self):
        return [pltpu.SemaphoreType.DMA((self.n_sem,)), pltpu.SemaphoreType.DMA((self.n_sem,)),
                pltpu.SemaphoreType.DMA((len(self.ins),))]

    def run(self, in_refs, out_refs, sems, first, mid, last):
        send, recv, local = sems

        @pl.when(first)
        def _():
            self.start(in_refs, out_refs, send, recv, local)

        @pl.when(mid)
        def _():
            self.mid(in_refs, out_refs, send, recv, local)

        @pl.when(last)
        def _():
            self.finish(in_refs, out_refs, send, recv, local)


def _split_refs(refs, n_in, n_out, comm):
    ci = len(comm.ins) if comm else 0
    co = len(comm.out_shapes) if comm else 0
    ins = refs[:n_in]
    c_ins = refs[n_in:n_in + ci]
    outs = refs[n_in + ci:n_in + ci + n_out]
    c_outs = refs[n_in + ci + n_out:n_in + ci + n_out + co]
    rest = refs[n_in + ci + n_out + co:]
    if comm:
        return ins, c_ins, outs, c_outs, rest[:-3], rest[-3:]
    return ins, c_ins, outs, c_outs, rest, ()


def _run_comm(comm, name):
    ni, no = len(comm.ins), len(comm.out_shapes)

    def body(*refs):
        args = (refs[:ni], refs[ni:ni + no]) + tuple(refs[ni + no:])
        comm.start(*args)
        comm.mid(*args)
        comm.finish(*args)

    return pl.pallas_call(
        body,
        in_specs=[_ANY] * ni,
        out_specs=tuple([_ANY] * no),
        out_shape=tuple(comm.out_shapes),
        scratch_shapes=comm.scratch(),
        compiler_params=pltpu.CompilerParams(has_side_effects=True),
        name=name,
    )(*comm.ins)


def _ag_comm(shards):
    n = len(shards)

    def env(ins, outs, send_sems, recv_sems):
        x, y, c = lax.axis_index("x"), lax.axis_index("y"), lax.axis_index("c")
        me, sibling = (x, y, c), (x, y, 1 - c)
        chips = [(1 - x, y), (x, 1 - y), (1 - x, 1 - y)]

        def copy(a, k, block, to, src=None):
            s = 4 * block[0] + 2 * block[1] + block[2]
            return pltpu.make_async_remote_copy(
                src_ref=outs[a].at[s] if src is None else src, dst_ref=outs[a].at[s],
                send_sem=send_sems.at[7 * a + k], recv_sem=recv_sems.at[7 * a + k],
                device_id=to, device_id_type=MESH)

        return c, me, sibling, chips, copy

    def own(ins, outs, local_sems, a):
        x, y, c = lax.axis_index("x"), lax.axis_index("y"), lax.axis_index("c")
        return pltpu.make_async_copy(ins[a], outs[a].at[4 * x + 2 * y + c], local_sems.at[a])

    def first_sends(ins, copy, me, sibling, chips, c, a):
        return [copy(a, 0, me, sibling, src=ins[a])] + [
            copy(a, 1 + j, me, (*chip, c), src=ins[a]) for j, chip in enumerate(chips)]

    def start(ins, outs, send_sems, recv_sems, local_sems):
        c, me, sibling, chips, copy = env(ins, outs, send_sems, recv_sems)
        for a in range(n):
            own(ins, outs, local_sems, a).start()
        for a in range(n):
            for cp in first_sends(ins, copy, me, sibling, chips, c, a):
                cp.start()

    def mid(ins, outs, send_sems, recv_sems, local_sems):
        c, me, sibling, chips, copy = env(ins, outs, send_sems, recv_sems)
        for a in range(n):
            for j, chip in enumerate(chips):
                copy(a, 1 + j, (*chip, c), me).wait_recv()
                copy(a, 4 + j, (*chip, c), sibling).start()

    def finish(ins, outs, send_sems, recv_sems, local_sems):
        c, me, sibling, chips, copy = env(ins, outs, send_sems, recv_sems)
        for a in range(n):
            copy(a, 0, sibling, me).wait_recv()
            for j, chip in enumerate(chips):
                copy(a, 4 + j, (*chip, 1 - c), me).wait_recv()
        for a in range(n):
            for cp in first_sends(ins, copy, me, sibling, chips, c, a):
                cp.wait_send()
            for j, chip in enumerate(chips):
                copy(a, 4 + j, (*chip, c), sibling).wait_send()
            own(ins, outs, local_sems, a).wait()

    return _Comm(shards, [jax.ShapeDtypeStruct((N_DEV,) + s.shape, s.dtype) for s in shards], 7 * n,
                 start, mid, finish)


def _sibling_comm(parts):
    n = len(parts)

    def copies(ins, outs, send_sems, recv_sems):
        x, y, c = lax.axis_index("x"), lax.axis_index("y"), lax.axis_index("c")
        return [pltpu.make_async_remote_copy(
            src_ref=ins[a].at[2 * q + (1 - c)], dst_ref=outs[a].at[q],
            send_sem=send_sems.at[4 * a + q], recv_sem=recv_sems.at[4 * a + q],
            device_id=(x, y, 1 - c), device_id_type=MESH) for a in range(n) for q in range(4)]

    def start(ins, outs, send_sems, recv_sems, local_sems):
        for cp in copies(ins, outs, send_sems, recv_sems):
            cp.start()

    def mid(ins, outs, send_sems, recv_sems, local_sems):
        pass

    def finish(ins, outs, send_sems, recv_sems, local_sems):
        for cp in copies(ins, outs, send_sems, recv_sems):
            cp.wait()

    return _Comm(parts, [jax.ShapeDtypeStruct((4,) + p.shape[1:], p.dtype) for p in parts], 4 * n,
                 start, mid, finish)


def _chips_comm(parts):
    n = len(parts)

    def copies(ins, outs, send_sems, recv_sems):
        x, y, c = lax.axis_index("x"), lax.axis_index("y"), lax.axis_index("c")
        chips = [(1 - x, y), (x, 1 - y), (1 - x, 1 - y)]
        return [pltpu.make_async_remote_copy(
            src_ref=ins[a].at[2 * px + py], dst_ref=outs[a].at[j],
            send_sem=send_sems.at[3 * a + j], recv_sem=recv_sems.at[3 * a + j],
            device_id=(px, py, c), device_id_type=MESH) for a in range(n) for j, (px, py) in enumerate(chips)]

    def start(ins, outs, send_sems, recv_sems, local_sems):
        for cp in copies(ins, outs, send_sems, recv_sems):
            cp.start()

    def mid(ins, outs, send_sems, recv_sems, local_sems):
        pass

    def finish(ins, outs, send_sems, recv_sems, local_sems):
        for cp in copies(ins, outs, send_sems, recv_sems):
            cp.wait()

    return _Comm(parts, [jax.ShapeDtypeStruct((3,) + p.shape[1:], p.dtype) for p in parts], 3 * n,
                 start, mid, finish)


def _sibling_add(part, recv, name):
    _, R, C = part.shape
    tr = _pick(R, (256, 128, 80))
    c = lax.axis_index("c")

    def body(c_ref, p_ref, r_ref, o_ref, o16_ref):
        t = p_ref[...] + r_ref[...]
        o_ref[...] = t
        o16_ref[...] = t.astype(BF16)

    blk = pl.BlockSpec((1, tr, C), lambda q, i, c_ref: (q, i, 0))
    return pl.pallas_call(
        body,
        grid_spec=pltpu.PrefetchScalarGridSpec(
            num_scalar_prefetch=1,
            grid=(4, R // tr),
            in_specs=[pl.BlockSpec((1, tr, C), lambda q, i, c_ref: (2 * q + c_ref[0], i, 0)), blk],
            out_specs=(blk, blk),
        ),
        out_shape=(jax.ShapeDtypeStruct((4, R, C), F32), jax.ShapeDtypeStruct((4, R, C), BF16)),
        compiler_params=_params(("parallel", "parallel")),
        name=name,
    )(jnp.reshape(c, (1,)).astype(jnp.int32), part, recv)


def _adamw(w, g, m, v):
    m = ADAM_B1 * m + (1.0 - ADAM_B1) * g
    v = ADAM_B2 * v + (1.0 - ADAM_B2) * (g * g)
    m_hat = m / (1.0 - ADAM_B1 ** ADAM_STEP)
    v_hat = v / (1.0 - ADAM_B2 ** ADAM_STEP)
    delta = -ADAM_LR * (m_hat / (jnp.sqrt(v_hat) + ADAM_EPS) + ADAM_WD * w)
    return delta, m, v


def _adam_big(t, recv, w, m, v, name):
    _, R, C = t.shape
    tr = _pick(R, (256, 128))
    chip = 2 * lax.axis_index("x") + lax.axis_index("y")

    def body(q_ref, t_ref, r_ref, w_ref, m_ref, v_ref, g_ref, d_ref, nm_ref, nv_ref):
        g = t_ref[0] + r_ref[0].astype(F32) + r_ref[1].astype(F32) + r_ref[2].astype(F32)
        g_ref[...] = g
        d_ref[...], nm_ref[...], nv_ref[...] = _adamw(w_ref[...], g, m_ref[...], v_ref[...])

    blk = pl.BlockSpec((tr, C), lambda i, q_ref: (i, 0))
    o = jax.ShapeDtypeStruct((R, C), F32)
    return pl.pallas_call(
        body,
        grid_spec=pltpu.PrefetchScalarGridSpec(
            num_scalar_prefetch=1,
            grid=(R // tr,),
            in_specs=[pl.BlockSpec((1, tr, C), lambda i, q_ref: (q_ref[0], i, 0)),
                      pl.BlockSpec((3, tr, C), lambda i, q_ref: (0, i, 0)), blk, blk, blk],
            out_specs=(blk, blk, blk, blk),
        ),
        out_shape=(o, o, o, o),
        compiler_params=_params(("parallel",)),
        name=name,
    )(jnp.reshape(chip, (1,)).astype(jnp.int32), t, recv, w, m, v)


def _reduce_big(t, recv, name):
    _, R, C = t.shape
    tr = _pick(R, (256, 128, 80))
    chip = 2 * lax.axis_index("x") + lax.axis_index("y")

    def body(q_ref, t_ref, r_ref, g_ref):
        g_ref[...] = t_ref[0] + r_ref[0].astype(F32) + r_ref[1].astype(F32) + r_ref[2].astype(F32)

    return pl.pallas_call(
        body,
        grid_spec=pltpu.PrefetchScalarGridSpec(
            num_scalar_prefetch=1,
            grid=(R // tr,),
            in_specs=[pl.BlockSpec((1, tr, C), lambda i, q_ref: (q_ref[0], i, 0)),
                      pl.BlockSpec((3, tr, C), lambda i, q_ref: (0, i, 0))],
            out_specs=pl.BlockSpec((tr, C), lambda i, q_ref: (i, 0)),
        ),
        out_shape=jax.ShapeDtypeStruct((R, C), F32),
        compiler_params=_params(("parallel",)),
        name=name,
    )(jnp.reshape(chip, (1,)).astype(jnp.int32), t, recv)


def _adam_only(g, w, m, v, name):
    R, C = g.shape
    tr = _pick(R, (256, 128))

    def body(g_ref, w_ref, m_ref, v_ref, d_ref, nm_ref, nv_ref):
        d_ref[...], nm_ref[...], nv_ref[...] = _adamw(w_ref[...], g_ref[...], m_ref[...], v_ref[...])

    blk = pl.BlockSpec((tr, C), lambda i: (i, 0))
    o = jax.ShapeDtypeStruct((R, C), F32)
    return pl.pallas_call(body, grid=(R // tr,), in_specs=[blk] * 4, out_specs=(blk, blk, blk),
                          out_shape=(o, o, o), compiler_params=_params(("parallel",)), name=name)(g, w, m, v)


def _small_allreduce_adam(gbuf, wbuf, mbuf, vbuf):
    R = gbuf.shape[0]

    def body(g_ref, w_ref, m_ref, v_ref, gs_ref, d_ref, nm_ref, nv_ref, slots, send_sems, recv_sems):
        x, y, c = lax.axis_index("x"), lax.axis_index("y"), lax.axis_index("c")
        me = 4 * x + 2 * y + c
        copies = []
        for k in range(1, N_DEV):
            fx, fy, fc = (k >> 2) & 1, (k >> 1) & 1, k & 1
            px = x + fx - 2 * x * fx
            py = y + fy - 2 * y * fy
            pc = c + fc - 2 * c * fc
            cp = pltpu.make_async_remote_copy(
                src_ref=g_ref, dst_ref=slots.at[me],
                send_sem=send_sems.at[k - 1], recv_sem=recv_sems.at[k - 1],
                device_id=(px, py, pc), device_id_type=MESH)
            cp.start()
            copies.append((cp, 4 * px + 2 * py + pc))
        slots[me] = g_ref[...]
        for k, (cp, src) in enumerate(copies):
            pltpu.make_async_remote_copy(
                src_ref=g_ref, dst_ref=slots.at[src], send_sem=send_sems.at[k], recv_sem=recv_sems.at[k],
                device_id=(x, y, c), device_id_type=MESH).wait_recv()
        for cp, _ in copies:
            cp.wait_send()
        g = slots[0]
        for s in range(1, N_DEV):
            g = g + slots[s]
        gs_ref[...] = g
        d_ref[...], nm_ref[...], nv_ref[...] = _adamw(w_ref[...], g, m_ref[...], v_ref[...])

    o = jax.ShapeDtypeStruct((R, LANES), F32)
    vm = pl.BlockSpec(memory_space=pltpu.VMEM)
    return pl.pallas_call(
        body,
        in_specs=[vm, vm, vm, vm],
        out_specs=(vm, vm, vm, vm),
        out_shape=(o, o, o, o),
        scratch_shapes=[pltpu.VMEM((N_DEV, R, LANES), F32), pltpu.SemaphoreType.DMA((N_DEV - 1,)),
                        pltpu.SemaphoreType.DMA((N_DEV - 1,))],
        compiler_params=pltpu.CompilerParams(vmem_limit_bytes=VMEM_LIMIT, has_side_effects=True),
        name="small_allreduce_adam",
    )(gbuf, wbuf, mbuf, vbuf)


def _pack(items):
    rows, spans, r0 = [], [], 0
    for a in items:
        n = a.size
        nr = -(-n // LANES)
        rows.append(jnp.pad(a.reshape(-1).astype(F32), (0, nr * LANES - n)).reshape(nr, LANES))
        spans.append((r0, nr, a.shape))
        r0 += nr
    pad = -r0 % SUBLANES
    if pad:
        rows.append(jnp.zeros((pad, LANES), F32))
    return jnp.concatenate(rows, axis=0), spans


def _unpack(buf, spans):
    return [buf[r0:r0 + nr].reshape(-1)[:math.prod(shape)].reshape(shape) for r0, nr, shape in spans]


def kernel(x, norm_mix_pre, norm_mix_post, norm_mlp_pre, norm_mlp_post, w_in, sinks, lam_re, lam_im, log_dt, b_re, b_im, c_re, c_im, d_skip, w_glu, w_branch, w_out, w_up, w_down, loss_target, m_norm_mix_pre, m_norm_mix_post, m_norm_mlp_pre, m_norm_mlp_post, m_w_in, m_sinks, m_lam_re, m_lam_im, m_log_dt, m_b_re, m_b_im, m_c_re, m_c_im, m_d_skip, m_w_glu, m_w_branch, m_w_out, m_w_up, m_w_down, v_norm_mix_pre, v_norm_mix_post, v_norm_mlp_pre, v_norm_mlp_post, v_w_in, v_sinks, v_lam_re, v_lam_im, v_log_dt, v_b_re, v_b_im, v_c_re, v_c_im, v_d_skip, v_w_glu, v_w_branch, v_w_out, v_w_up, v_w_down):
    _, L, D = x.shape
    xs = x[0]
    tgt = loss_target[0]
    ssm_w = D // 2
    n_groups = ssm_w // SSM_GC
    ngb = n_groups // GROUPS_PER_BLOCK
    n_state = n_groups * SSM_P
    d_ff = w_up.shape[2] * N_DEV
    o_k, o_v, o_u = Q_W, Q_W + KV_W, Q_W + 2 * KV_W
    o_ga = o_u + ssm_w
    o_gs = o_ga + D

    big = {"w_in": w_in[0], "w_glu": w_glu[0], "w_branch": w_branch[0], "w_out": w_out[0],
           "w_up": w_up[0], "w_down": w_down[0]}
    col_sharded = ("w_in", "w_glu", "w_up")
    names = list(big)
    shard16 = {k: (big[k].T if k in col_sharded else big[k]).astype(BF16) for k in names}
    full = {}

    def gathered(keys, arrays):
        for k, g in zip(keys, arrays):
            _, r, c = g.shape
            full[k] = g.reshape(N_DEV * r, c)

    def by_owner(g):
        return g.reshape(N_DEV, g.shape[0] // N_DEV, g.shape[1])

    col = lambda a: a.reshape(n_state, 1)
    lr_c, li_c = col(lam_re[0]), col(lam_im[0])
    ldt_c = jnp.repeat(log_dt[0], SSM_P).reshape(n_state, 1)
    b_re_c, b_im_c = b_re[0].reshape(n_state, SSM_GC), b_im[0].reshape(n_state, SSM_GC)
    a_re, a_im, bb_re, bb_im = _disc_fwd(lr_c, li_c, ldt_c, b_re_c, b_im_c)
    a_re3 = a_re.reshape(n_state // LANES, 1, LANES)
    a_im3 = a_im.reshape(n_state // LANES, 1, LANES)
    br_m = _block_diag_in(bb_re, ngb).astype(BF16)
    bi_m = _block_diag_in(bb_im, ngb).astype(BF16)
    cr_m = _block_diag_out(c_re[0], ngb).astype(BF16)
    ci_m = _block_diag_out(c_im[0], ngb).astype(BF16)
    d_row = d_skip[0].reshape(1, ssm_w)
    cos_t, sin_t = _rope_tables(L)

    h, g1 = _rms_pre(xs, norm_mix_pre, _ag_comm([shard16["w_in"]]))
    gathered(["w_in"], g1)
    z, g3 = _mm(h, full["w_in"], mode="nt", name="mm_z",
                comm=_ag_comm([shard16[k] for k in ("w_glu", "w_branch", "w_out")]))
    gathered(["w_glu", "w_branch", "w_out"], g3)
    wb_a, wb_s = full["w_branch"][:Q_W], full["w_branch"][Q_W:]
    o_attn, g1 = _attn_fwd(z, cos_t, sin_t, sinks, comm=_ag_comm([shard16["w_up"]]))
    gathered(["w_up"], g1)
    (y_pre, xs_r, xs_i), g1 = _ssm_fwd(z, o_u, br_m, bi_m, cr_m, ci_m, a_re3, a_im3, d_row,
                                       comm=_ag_comm([shard16["w_down"]]))
    gathered(["w_down"], g1)
    gy = _ew(lambda y: (_gelu(y),), [(y_pre, 0)], (BF16,), rows=L, ncols=ssm_w, name="gelu")
    zg = _mm(gy, full["w_glu"], mode="nt", name="mm_zg")
    o_ssm = _ew(lambda a, b: (a * _sigmoid(b),), [(zg, 0), (zg, ssm_w)], (BF16,), rows=L, ncols=ssm_w, name="glu")
    y_attn = _mm(o_attn, wb_a, mode="nn", name="mm_y_attn")
    y_ssm = _mm(o_ssm, wb_s, mode="nn", name="mm_y_ssm")
    mix = _ew(lambda ga, gs, ya, ys: (_sigmoid(ga) * ya + _sigmoid(gs) * ys,),
              [(z, o_ga), (z, o_gs), (y_attn, 0), (y_ssm, 0)], (BF16,), rows=L, ncols=D, name="mix")
    mixed = _mm(mix, full["w_out"], mode="nn", name="mm_mixed")
    x1, h2 = _post_pre(xs, mixed, norm_mix_post, norm_mlp_pre)

    def relu_sq(acc):
        a = jnp.maximum(acc, 0.0)
        return a, a * a

    act, act2 = _mm(h2, full["w_up"], mode="nt", name="mm_up", out_dtypes=(BF16, BF16), epi=relu_sq)
    dn = _mm(act2, full["w_down"], mode="nn", name="mm_down")
    dx2, d_dn, dg_mlp_post, loss_part = _loss_bwd(x1, dn, norm_mlp_post, tgt)

    d_pre = _mm(d_dn, full["w_down"], mode="nt", name="mm_d_act", out_dtypes=(BF16,),
                epi=lambda acc, a: (acc * (2.0 * a.astype(F32)),), extras=(act,))
    gw_down = _mm(act2, d_dn, mode="tn", name="mm_gw_down")
    p_down = by_owner(gw_down)
    dh2, (sib_down,) = _mm(d_pre, full["w_up"], mode="nn", name="mm_dh2", comm=_sibling_comm([p_down]))
    t_down, t16_down = _sibling_add(p_down, sib_down, "rs_add_w_down")
    gw_up, (chips_down,) = _mm(d_pre, h2, mode="tn", name="mm_gw_up", comm=_chips_comm([t16_down]))
    p_up = by_owner(gw_up)
    dx1, d_mixed, dg_mlp_pre, dg_mix_post = _norm_bwd_pair(x1, dh2, dx2, mixed, norm_mlp_pre, norm_mix_post)
    d_mix, (sib_up,) = _mm(d_mixed, full["w_out"], mode="nt", name="mm_d_mix", comm=_sibling_comm([p_up]))
    t_up, t16_up = _sibling_add(p_up, sib_up, "rs_add_w_up")
    gw_out = _mm(mix, d_mixed, mode="tn", name="mm_gw_out")
    d_y2, dz = _gate_bwd(d_mix, z, o_ga, y_attn, y_ssm)
    d_o_attn = _mm(d_y2, wb_a, mode="nt", name="mm_d_o_attn", a_win=(0, D))
    gwb_a = _mm(o_attn, d_y2, mode="tn", name="mm_gwb_a", b_win=(0, D))
    d_o_ssm = _mm(d_y2, wb_s, mode="nt", name="mm_d_o_ssm", a_win=(D, D))
    gwb_s = _mm(o_ssm, d_y2, mode="tn", name="mm_gwb_s", b_win=(D, D))

    def glu_bwd(do, a, b):
        s = _sigmoid(b)
        return do * s, do * a * s * (1.0 - s)

    d_zg_a, d_zg_b = _ew(glu_bwd, [(d_o_ssm, 0), (zg, 0), (zg, ssm_w)], (BF16, BF16), rows=L, ncols=ssm_w, name="glu_bwd")
    d_zg = jnp.concatenate([d_zg_a, d_zg_b], axis=1)
    dy_pre = _mm(d_zg, full["w_glu"], mode="nn", name="mm_d_gy",
                 epi=lambda acc, y: (acc * _gelu_grad(y),), extras=(y_pre,))
    gw_glu = _mm(d_zg, gy, mode="tn", name="mm_gw_glu")
    mids = ["w_glu", "w_branch", "w_out"]
    p_mid = [by_owner(gw_glu), by_owner(jnp.concatenate([gwb_a, gwb_s], axis=0)), by_owner(gw_out)]
    sib_mid = _run_comm(_sibling_comm(p_mid), "rs_sibling_mid")
    t_mid = [_sibling_add(p, r, "rs_add_" + k) for k, p, r in zip(mids, p_mid, sib_mid)]
    (dz, g_dskip, g_ar3, g_ai3, g_br_m, g_bi_m, g_cr_m, g_ci_m) = _ssm_bwd(
        dy_pre, z, o_u, xs_r, xs_i, br_m, bi_m, cr_m, ci_m, a_re3, a_im3, d_row, dz)
    g_lr, g_li, g_ldt, g_b_re, g_b_im = _disc_bwd(
        lr_c, li_c, ldt_c, b_re_c, b_im_c, g_ar3.reshape(n_state, 1), g_ai3.reshape(n_state, 1),
        _diag_in(g_br_m, ngb), _diag_in(g_bi_m, ngb))
    (dz, dkc, dkp, dvc, dvp, dsink_rows), chips_a = _attn_bwd(
        z, d_o_attn, cos_t, sin_t, sinks, dz, comm=_chips_comm([t16_up] + [t16 for _, t16 in t_mid]))
    chips_up, chips_mid = chips_a[0], chips_a[1:]
    dz = _kv_combine(dkc, dkp, dvc, dvp, dz)
    gw_in = _mm(dz, h, mode="tn", name="mm_gw_in")
    p_in = by_owner(gw_in)
    dh, (sib_in,) = _mm(dz, full["w_in"], mode="nn", name="mm_dh", comm=_sibling_comm([p_in]))
    t_in, t16_in = _sibling_add(p_in, sib_in, "rs_add_w_in")
    (grad_x, dg_mix_pre), (chips_in,) = _final_bwd(xs, dh, dx1, norm_mix_pre, _chips_comm([t16_in]))

    reduced = {"w_in": (t_in, chips_in), "w_up": (t_up, chips_up), "w_down": (t_down, chips_down)}
    for k, (t32, _), r in zip(mids, t_mid, chips_mid):
        reduced[k] = (t32, r)
    moments = {"w_in": (m_w_in, v_w_in), "w_glu": (m_w_glu, v_w_glu), "w_branch": (m_w_branch, v_w_branch),
               "w_out": (m_w_out, v_w_out), "w_up": (m_w_up, v_w_up), "w_down": (m_w_down, v_w_down)}
    big_out = {}
    for k in names:
        t, r = reduced[k]
        mm_, vv_ = moments[k]
        if k in col_sharded:
            g = _reduce_big(t, r, "reduce_" + k).T
            big_out[k] = [o[None] for o in (g,) + tuple(_adam_only(g, big[k], mm_[0], vv_[0], "adam_" + k))]
        else:
            big_out[k] = [o[None] for o in _adam_big(t, r, big[k], mm_[0], vv_[0], "adam_" + k)]

    dsink = jnp.stack([dsink_rows[:, 0], dsink_rows[:, HEAD_DIM]], axis=1).reshape(1, N_Q_HEADS)
    small_names = ["norm_mix_pre", "norm_mix_post", "norm_mlp_pre", "norm_mlp_post", "sinks", "lam_re", "lam_im",
                   "log_dt", "b_re", "b_im", "c_re", "c_im", "d_skip"]
    small_w = [norm_mix_pre, norm_mix_post, norm_mlp_pre, norm_mlp_post, sinks, lam_re, lam_im, log_dt,
               b_re, b_im, c_re, c_im, d_skip]
    small_m = [m_norm_mix_pre, m_norm_mix_post, m_norm_mlp_pre, m_norm_mlp_post, m_sinks, m_lam_re, m_lam_im,
               m_log_dt, m_b_re, m_b_im, m_c_re, m_c_im, m_d_skip]
    small_v = [v_norm_mix_pre, v_norm_mix_post, v_norm_mlp_pre, v_norm_mlp_post, v_sinks, v_lam_re, v_lam_im,
               v_log_dt, v_b_re, v_b_im, v_c_re, v_c_im, v_d_skip]
    small_g = [dg_mix_pre, dg_mix_post, dg_mlp_pre, dg_mlp_post, dsink,
               g_lr.reshape(lam_re.shape), g_li.reshape(lam_im.shape), g_ldt.reshape(log_dt.shape),
               g_b_re.reshape(b_re.shape), g_b_im.reshape(b_im.shape),
               _diag_out(g_cr_m, ngb).reshape(c_re.shape), _diag_out(g_ci_m, ngb).reshape(c_im.shape),
               g_dskip.reshape(d_skip.shape)]
    zero1 = jnp.zeros((1, 1), F32)
    gbuf, spans = _pack(small_g + [loss_part])
    wbuf, _ = _pack(small_w + [zero1])
    mbuf, _ = _pack(small_m + [zero1])
    vbuf, _ = _pack(small_v + [zero1])
    gs, ds, nms, nvs = [_unpack(b, spans) for b in _small_allreduce_adam(gbuf, wbuf, mbuf, vbuf)]
    loss = gs[-1].reshape(())

    order = ["norm_mix_pre", "norm_mix_post", "norm_mlp_pre", "norm_mlp_post", "w_in", "sinks", "lam_re", "lam_im",
             "log_dt", "b_re", "b_im", "c_re", "c_im", "d_skip", "w_glu", "w_branch", "w_out", "w_up", "w_down"]
    outs = [loss, grad_x[None]]
    for idx, src in enumerate((gs, ds, nms, nvs)):
        for k in order:
            outs.append(big_out[k][idx] if k in big_out else src[small_names.index(k)])
    return tuple(outs)
```

```python
import functools
import math

import jax
import jax.numpy as jnp
from jax import lax
from jax.experimental import pallas as pl
from jax.experimental.pallas import tpu as pltpu

F32 = jnp.float32
BF16 = jnp.bfloat16
MESH = pl.DeviceIdType.MESH

LANES = 128
SUBLANES = 8
VMEM_LIMIT = 56 * 1024 * 1024

HEAD_DIM = 64
N_Q_HEADS = 16
N_KV_HEADS = 2
Q_W = N_Q_HEADS * HEAD_DIM
KV_W = N_KV_HEADS * HEAD_DIM
BLOCK = 128
ROT_DIM = HEAD_DIM // 4
ROPE_THETA = 500000.0
SSM_GC = 16
SSM_P = 64
GROUPS_PER_BLOCK = 8
NW = GROUPS_PER_BLOCK * SSM_P // LANES
EPS = 1e-6
N_DEV = 8

ADAM_LR = 0.001
ADAM_B1 = 0.9
ADAM_B2 = 0.999
ADAM_EPS = 1e-08
ADAM_WD = 0.01
ADAM_STEP = 10


def _params(sem=None):
    return pltpu.CompilerParams(dimension_semantics=sem, vmem_limit_bytes=VMEM_LIMIT)


def _host_params(sem, comm):
    if comm:
        return pltpu.CompilerParams(dimension_semantics=("arbitrary",) * len(sem), vmem_limit_bytes=VMEM_LIMIT,
                                    has_side_effects=True)
    return _params(sem)


def _pick(dim, prefs):
    for p in prefs:
        if dim % p == 0:
            return p
    return dim


def _sigmoid(x):
    return 1.0 / (1.0 + jnp.exp(-x))


_GELU_C = math.sqrt(2.0 / math.pi)


def _gelu(x):
    return 0.5 * x * (1.0 + jnp.tanh(_GELU_C * (x + 0.044715 * x * x * x)))


def _gelu_grad(x):
    t = jnp.tanh(_GELU_C * (x + 0.044715 * x * x * x))
    return 0.5 * (1.0 + t) + 0.5 * x * (1.0 - t * t) * _GELU_C * (1.0 + 3.0 * 0.044715 * x * x)


_DIMS = {"nn": (((1,), (0,)), ((), ())), "nt": (((1,), (1,)), ((), ())), "tn": (((0,), (0,)), ((), ()))}


def _mm(a, b, *, mode, name, out_dtypes=(F32,), epi=None, extras=(), comm=None, a_win=None, b_win=None):
    ar, ac = a.shape[0], (a_win[1] if a_win else a.shape[1])
    br, bc = b.shape[0], (b_win[1] if b_win else b.shape[1])
    if mode == "nn":
        (M, K), (K2, N) = (ar, ac), (br, bc)
    elif mode == "nt":
        (M, K), (N, K2) = (ar, ac), (br, bc)
    else:
        (K, M), (K2, N) = (ar, ac), (br, bc)
    assert K == K2, (a.shape, b.shape, mode)
    tm = _pick(M, (1024, 1280, 640, 512, 256, 128))
    tn = _pick(N, (1024, 1280, 640, 512, 384, 256, 128))
    tk = K if K <= 2048 else _pick(K, (2048, 1280, 1024, 640, 512, 256, 128))
    nk = K // tk
    a_col_tile = tm if mode == "tn" else tk
    b_col_tile = tk if mode == "nt" else tn
    ao = a_win[0] // a_col_tile if a_win else 0
    bo = b_win[0] // b_col_tile if b_win else 0
    assert (not a_win or a_win[0] % a_col_tile == 0) and (not b_win or b_win[0] % b_col_tile == 0)
    n_ex = len(extras)
    n_out = len(out_dtypes)
    gi, gj = M // tm, N // tn
    steps = gi * gj * nk

    def body(*refs):
        ins, c_ins, o_refs, c_outs, scratch, sems = _split_refs(refs, 2 + n_ex, n_out, comm)
        a_ref, b_ref = ins[0], ins[1]
        ex_refs = ins[2:]
        if comm:
            s = (pl.program_id(0) * gj + pl.program_id(1)) * nk + pl.program_id(2)
            comm.run(c_ins, c_outs, sems, s == 0, s == steps // 2, s == steps - 1)

        def finish(r):
            outs = (r,) if epi is None else epi(r, *[e[...] for e in ex_refs])
            for o_ref, o in zip(o_refs, outs):
                o_ref[...] = o.astype(o_ref.dtype)

        part = lax.dot_general(a_ref[...].astype(BF16), b_ref[...].astype(BF16), _DIMS[mode],
                               preferred_element_type=F32)
        if nk == 1:
            finish(part)
            return
        acc = scratch[0]
        k = pl.program_id(2)

        @pl.when(k == 0)
        def _():
            acc[...] = part

        @pl.when((k > 0) & (k < nk - 1))
        def _():
            acc[...] += part

        @pl.when(k == nk - 1)
        def _():
            finish(acc[...] + part)

    if mode == "nn":
        a_spec = pl.BlockSpec((tm, tk), lambda i, j, k: (i, k + ao))
        b_spec = pl.BlockSpec((tk, tn), lambda i, j, k: (k, j + bo))
    elif mode == "nt":
        a_spec = pl.BlockSpec((tm, tk), lambda i, j, k: (i, k + ao))
        b_spec = pl.BlockSpec((tn, tk), lambda i, j, k: (j, k + bo))
    else:
        a_spec = pl.BlockSpec((tk, tm), lambda i, j, k: (k, i + ao))
        b_spec = pl.BlockSpec((tk, tn), lambda i, j, k: (k, j + bo))
    o_spec = pl.BlockSpec((tm, tn), lambda i, j, k: (i, j))
    c_ins = comm.ins if comm else []
    c_shapes = comm.out_shapes if comm else []
    res = pl.pallas_call(
        body,
        grid=(gi, gj, nk),
        in_specs=[a_spec, b_spec] + [o_spec] * n_ex + [_ANY] * len(c_ins),
        out_specs=tuple([o_spec] * n_out + [_ANY] * len(c_shapes)),
        out_shape=tuple([jax.ShapeDtypeStruct((M, N), d) for d in out_dtypes] + c_shapes),
        scratch_shapes=([pltpu.VMEM((tm, tn), F32)] if nk > 1 else []) + (comm.scratch() if comm else []),
        compiler_params=_host_params(("parallel", "parallel", "arbitrary"), comm),
        name=name,
    )(a, b, *extras, *c_ins)
    if comm:
        return (res[0] if n_out == 1 else res[:n_out]), list(res[n_out:])
    return res[0] if n_out == 1 else res


def _ew(fn, ins, out_dtypes, *, rows, ncols, name):
    g = ncols
    for _, off in ins:
        g = math.gcd(g, off)
    tc = _pick(g, (512, 256, 128))
    tr = _pick(rows, (1024, 512, 256, 128))
    n_in = len(ins)

    def body(*refs):
        outs = fn(*[r[...] for r in refs[:n_in]])
        for o_ref, o in zip(refs[n_in:], outs):
            o_ref[...] = o.astype(o_ref.dtype)

    def in_spec(off):
        ob = off // tc
        return pl.BlockSpec((tr, tc), lambda i, j: (i, j + ob))

    o_spec = pl.BlockSpec((tr, tc), lambda i, j: (i, j))
    res = pl.pallas_call(
        body,
        grid=(rows // tr, ncols // tc),
        in_specs=[in_spec(off) for _, off in ins],
        out_specs=tuple([o_spec] * len(out_dtypes)),
        out_shape=tuple(jax.ShapeDtypeStruct((rows, ncols), d) for d in out_dtypes),
        compiler_params=_params(("parallel", "parallel")),
        name=name,
    )(*[arr for arr, _ in ins])
    return res[0] if len(out_dtypes) == 1 else res


def _rstd(x):
    return lax.rsqrt(jnp.mean(x * x, axis=-1, keepdims=True) + EPS)


def _norm_bwd(x, r, g, dy):
    t = dy * g
    dx = r * t - x * (r * r * r) * jnp.mean(t * x, axis=-1, keepdims=True)
    return dx, dy * x * r


def _row_call(body, ins, row_ins, outs, acc_outs, *, rows, width, name, comm=None):
    tr = _pick(rows, (256, 128))
    steps = rows // tr
    t_spec = pl.BlockSpec((tr, width), lambda i: (i, 0))
    r_spec = pl.BlockSpec((1, width), lambda i: (0, 0))
    n_in, n_out = len(ins) + len(row_ins), len(outs) + len(acc_outs)

    def hosted(*refs):
        h_ins, c_ins, h_outs, c_outs, _, sems = _split_refs(refs, n_in, n_out, comm)
        i = pl.program_id(0)
        comm.run(c_ins, c_outs, sems, i == 0, i == steps // 2, i == steps - 1)
        body(*h_ins, *h_outs)

    c_ins = comm.ins if comm else []
    c_shapes = comm.out_shapes if comm else []
    res = pl.pallas_call(
        hosted if comm else body,
        grid=(steps,),
        in_specs=[t_spec] * len(ins) + [r_spec] * len(row_ins) + [_ANY] * len(c_ins),
        out_specs=tuple([t_spec] * len(outs) + [pl.BlockSpec(s, lambda i: (0, 0)) for s in acc_outs]
                        + [_ANY] * len(c_shapes)),
        out_shape=tuple([jax.ShapeDtypeStruct((rows, width), d) for d in outs]
                        + [jax.ShapeDtypeStruct(s, F32) for s in acc_outs] + c_shapes),
        scratch_shapes=comm.scratch() if comm else [],
        compiler_params=_host_params(("arbitrary",), comm),
        name=name,
    )(*ins, *row_ins, *c_ins)
    return (res[:n_out], list(res[n_out:])) if comm else res


def _rms_pre(x, g, comm):
    L, D = x.shape

    def body(x_ref, g_ref, h_ref):
        xv = x_ref[...]
        h_ref[...] = (xv * _rstd(xv) * g_ref[...]).astype(BF16)

    (h,), c_outs = _row_call(body, [x], [g], [BF16], [], rows=L, width=D, name="rms_pre", comm=comm)
    return h, c_outs


def _post_pre(x, mixed, g_post, g_pre):
    L, D = x.shape

    def body(x_ref, m_ref, gp_ref, gq_ref, x1_ref, h2_ref):
        mv = m_ref[...]
        x1 = x_ref[...] + mv * _rstd(mv) * gp_ref[...]
        x1_ref[...] = x1
        h2_ref[...] = (x1 * _rstd(x1) * gq_ref[...]).astype(BF16)

    return _row_call(body, [x, mixed], [g_post, g_pre], [F32, BF16], [], rows=L, width=D, name="post_pre")


def _loss_bwd(x1, dn, g_post, target):
    L, D = x1.shape

    def body(x1_ref, dn_ref, t_ref, g_ref, dx2_ref, ddn_ref, dg_ref, loss_ref):
        @pl.when(pl.program_id(0) == 0)
        def _():
            dg_ref[...] = jnp.zeros_like(dg_ref)
            loss_ref[...] = jnp.zeros_like(loss_ref)

        dnv = dn_ref[...]
        g = g_ref[...]
        r = _rstd(dnv)
        err = x1_ref[...] + dnv * r * g - t_ref[...]
        loss_ref[...] += 0.5 * jnp.sum(jnp.mean(err * err, axis=-1, keepdims=True), axis=0, keepdims=True)
        dx2 = err * (1.0 / D)
        dx2_ref[...] = dx2
        ddn, dgr = _norm_bwd(dnv, r, g, dx2)
        ddn_ref[...] = ddn.astype(BF16)
        dg_ref[...] += jnp.sum(dgr, axis=0, keepdims=True)

    return _row_call(body, [x1, dn, target], [g_post], [F32, BF16], [(1, D), (1, 1)],
                     rows=L, width=D, name="loss_bwd")


def _norm_bwd_pair(x1, dh2, dx2, mixed, g_pre, g_post):
    L, D = x1.shape

    def body(x1_ref, dh_ref, dx2_ref, m_ref, gq_ref, gp_ref, dx1_ref, dm_ref, dgq_ref, dgp_ref):
        @pl.when(pl.program_id(0) == 0)
        def _():
            dgq_ref[...] = jnp.zeros_like(dgq_ref)
            dgp_ref[...] = jnp.zeros_like(dgp_ref)

        x1v = x1_ref[...]
        d1, dgq = _norm_bwd(x1v, _rstd(x1v), gq_ref[...], dh_ref[...])
        dx1 = dx2_ref[...] + d1
        dx1_ref[...] = dx1
        mv = m_ref[...]
        dm, dgp = _norm_bwd(mv, _rstd(mv), gp_ref[...], dx1)
        dm_ref[...] = dm.astype(BF16)
        dgq_ref[...] += jnp.sum(dgq, axis=0, keepdims=True)
        dgp_ref[...] += jnp.sum(dgp, axis=0, keepdims=True)

    return _row_call(body, [x1, dh2, dx2, mixed], [g_pre, g_post], [F32, BF16], [(1, D), (1, D)],
                     rows=L, width=D, name="norm_bwd_pair")


def _final_bwd(x, dh, dx1, g_pre, comm):
    L, D = x.shape

    def body(x_ref, dh_ref, dx1_ref, g_ref, gx_ref, dg_ref):
        @pl.when(pl.program_id(0) == 0)
        def _():
            dg_ref[...] = jnp.zeros_like(dg_ref)

        xv = x_ref[...]
        d0, dg = _norm_bwd(xv, _rstd(xv), g_ref[...], dh_ref[...])
        gx_ref[...] = dx1_ref[...] + d0
        dg_ref[...] += jnp.sum(dg, axis=0, keepdims=True)

    return _row_call(body, [x, dh, dx1], [g_pre], [F32], [(1, D)], rows=L, width=D, name="final_bwd", comm=comm)


def _rope_tables(L):
    half = ROT_DIM // 2
    inv = ROPE_THETA ** (-jnp.arange(half, dtype=F32) * 2.0 / ROT_DIM)
    ang = jnp.arange(L, dtype=F32)[:, None] * inv[None, :]
    d = jnp.arange(LANES) % HEAD_DIM
    a = ang[:, d % half]
    cos_t = jnp.where(d[None, :] < ROT_DIM, jnp.cos(a), 1.0)
    sin_t = jnp.where(d[None, :] < half, -jnp.sin(a), jnp.where(d[None, :] < ROT_DIM, jnp.sin(a), 0.0))
    return cos_t.astype(F32), sin_t.astype(F32)


def _lane_lo(shape):
    return lax.broadcasted_iota(jnp.int32, shape, 1) < HEAD_DIM


def _rope(x, cos_t, sin_t):
    d = lax.broadcasted_iota(jnp.int32, x.shape, 1) % HEAD_DIM
    partner = jnp.where(d < ROT_DIM // 2, pltpu.roll(x, LANES - ROT_DIM // 2, 1), pltpu.roll(x, ROT_DIM // 2, 1))
    return x * cos_t + partner * sin_t


def _dup(kv, g):
    sw = pltpu.roll(kv, HEAD_DIM, 1)
    lo = _lane_lo(kv.shape)
    return jnp.where(lo, kv, sw) if g == 0 else jnp.where(lo, sw, kv)


def _attn_mask(n):
    qi = lax.broadcasted_iota(jnp.int32, (BLOCK, 2 * BLOCK), 0)
    kj = lax.broadcasted_iota(jnp.int32, (BLOCK, 2 * BLOCK), 1)
    rel = qi + BLOCK - kj
    return (rel >= 0) & (rel < BLOCK) & ((kj >= BLOCK) | (n > 0))


def _softmax_sink(s, mask, sink):
    s = jnp.where(mask, s, -1e30)
    m = jnp.maximum(jnp.max(s, axis=-1, keepdims=True), sink)
    e = jnp.where(mask, jnp.exp(s - m), 0.0)
    es = jnp.exp(sink - m)
    inv = 1.0 / (jnp.sum(e, axis=-1, keepdims=True) + es)
    return e * inv, es * inv


_NT = (((1,), (1,)), ((), ()))
_TN = (((0,), (0,)), ((), ()))


def _dot(a, b):
    return jnp.dot(a.astype(BF16), b.astype(BF16), preferred_element_type=F32)


def _dot_nt(a, b):
    return lax.dot_general(a.astype(BF16), b.astype(BF16), _NT, preferred_element_type=F32)


def _dot_tn(a, b):
    return lax.dot_general(a.astype(BF16), b.astype(BF16), _TN, preferred_element_type=F32)


def _attn_specs(nb):
    kcol, vcol = Q_W // LANES, Q_W // LANES + 1
    prev = lambda n: jnp.maximum(n - 1, 0)
    return [
        pl.BlockSpec((BLOCK, Q_W), lambda n: (n, 0)),
        pl.BlockSpec((BLOCK, LANES), lambda n: (n, kcol)),
        pl.BlockSpec((BLOCK, LANES), lambda n: (prev(n), kcol)),
        pl.BlockSpec((BLOCK, LANES), lambda n: (n, vcol)),
        pl.BlockSpec((BLOCK, LANES), lambda n: (prev(n), vcol)),
        pl.BlockSpec((BLOCK, LANES), lambda n: (n, 0)),
        pl.BlockSpec((BLOCK, LANES), lambda n: (prev(n), 0)),
        pl.BlockSpec((BLOCK, LANES), lambda n: (n, 0)),
        pl.BlockSpec((BLOCK, LANES), lambda n: (prev(n), 0)),
        pl.BlockSpec(memory_space=pltpu.SMEM),
    ]


def _attn_prep(refs):
    q_ref, kc_ref, kp_ref, vc_ref, vp_ref, cc_ref, cp_ref, sc_ref, sp_ref = refs
    cos_c, sin_c, cos_p, sin_p = cc_ref[...], sc_ref[...], cp_ref[...], sp_ref[...]
    k2 = jnp.concatenate([_rope(kp_ref[...], cos_p, sin_p), _rope(kc_ref[...], cos_c, sin_c)], axis=0)
    v2 = jnp.concatenate([vp_ref[...], vc_ref[...]], axis=0)
    lo2 = _lane_lo(v2.shape)
    kd = [_dup(k2, g).astype(BF16) for g in range(N_KV_HEADS)]
    vd = [_dup(v2, g) for g in range(N_KV_HEADS)]
    va = [jnp.where(lo2, v, 0.0).astype(BF16) for v in vd]
    vb = [jnp.where(lo2, 0.0, v).astype(BF16) for v in vd]
    return cos_c, sin_c, cos_p, sin_p, kd, va, vb


_SCALE = 1.0 / math.sqrt(HEAD_DIM)
_TILES = Q_W // LANES
_TILES_PER_KV = _TILES // N_KV_HEADS


def _attn_fwd(z, cos_t, sin_t, sinks, comm=None):
    L = z.shape[0]
    nb = L // BLOCK

    def body(*refs):
        ins, c_ins, outs, c_outs, _, sems = _split_refs(refs, 10, 1, comm)
        q_ref, kc_ref, kp_ref, vc_ref, vp_ref, cc_ref, cp_ref, sc_ref, sp_ref, sink_ref = ins
        o_ref = outs[0]
        n = pl.program_id(0)
        if comm:
            comm.run(c_ins, c_outs, sems, n == 0, n == nb // 2, n == nb - 1)
        cos_c, sin_c, _, _, kd, va, vb = _attn_prep((q_ref, kc_ref, kp_ref, vc_ref, vp_ref, cc_ref, cp_ref, sc_ref, sp_ref))
        mask = _attn_mask(n)
        lo = _lane_lo((BLOCK, LANES))
        for t in range(_TILES):
            g = t // _TILES_PER_KV
            qt = _rope(q_ref[:, t * LANES:(t + 1) * LANES], cos_c, sin_c) * _SCALE
            pa, _ = _softmax_sink(_dot_nt(jnp.where(lo, qt, 0.0), kd[g]), mask, sink_ref[0, 2 * t])
            pb, _ = _softmax_sink(_dot_nt(jnp.where(lo, 0.0, qt), kd[g]), mask, sink_ref[0, 2 * t + 1])
            o_ref[:, t * LANES:(t + 1) * LANES] = (_dot(pa, va[g]) + _dot(pb, vb[g])).astype(BF16)

    c_ins = comm.ins if comm else []
    c_shapes = comm.out_shapes if comm else []
    res = pl.pallas_call(
        body,
        grid=(nb,),
        in_specs=_attn_specs(nb) + [_ANY] * len(c_ins),
        out_specs=tuple([pl.BlockSpec((BLOCK, Q_W), lambda n: (n, 0))] + [_ANY] * len(c_shapes)),
        out_shape=tuple([jax.ShapeDtypeStruct((L, Q_W), BF16)] + c_shapes),
        scratch_shapes=comm.scratch() if comm else [],
        compiler_params=_host_params(("parallel",), comm),
        name="attn_fwd",
    )(z, z, z, z, z, cos_t, cos_t, sin_t, sin_t, sinks, *c_ins)
    return (res[0], list(res[1:])) if comm else res[0]


def _attn_bwd(z, d_o, cos_t, sin_t, sinks, dz, comm=None):
    L = z.shape[0]
    nb = L // BLOCK

    def body(*refs):
        ins, c_ins, outs, c_outs, _, sems = _split_refs(refs, 12, 6, comm)
        q_ref, kc_ref, kp_ref, vc_ref, vp_ref, cc_ref, cp_ref, sc_ref, sp_ref, sink_ref, do_ref, _ = ins
        dq_ref, dkc_ref, dkp_ref, dvc_ref, dvp_ref, ds_ref = outs
        n = pl.program_id(0)
        if comm:
            comm.run(c_ins, c_outs, sems, n == 0, n == nb // 2, n == nb - 1)

        @pl.when(n == 0)
        def _():
            ds_ref[...] = jnp.zeros_like(ds_ref)

        cos_c, sin_c, cos_p, sin_p, kd, va, vb = _attn_prep(
            (q_ref, kc_ref, kp_ref, vc_ref, vp_ref, cc_ref, cp_ref, sc_ref, sp_ref))
        mask = _attn_mask(n)
        lo = _lane_lo((BLOCK, LANES))
        lo2 = _lane_lo((2 * BLOCK, LANES))
        acc_k = [jnp.zeros((2 * BLOCK, LANES), F32) for _ in range(N_KV_HEADS)]
        acc_v = [jnp.zeros((2 * BLOCK, LANES), F32) for _ in range(N_KV_HEADS)]
        for t in range(_TILES):
            g = t // _TILES_PER_KV
            sl = slice(t * LANES, (t + 1) * LANES)
            qt = _rope(q_ref[:, sl], cos_c, sin_c) * _SCALE
            qa, qb = jnp.where(lo, qt, 0.0), jnp.where(lo, 0.0, qt)
            pa, psa = _softmax_sink(_dot_nt(qa, kd[g]), mask, sink_ref[0, 2 * t])
            pb, psb = _softmax_sink(_dot_nt(qb, kd[g]), mask, sink_ref[0, 2 * t + 1])
            dot_ = do_ref[:, sl]
            ot = _dot(pa, va[g]) + _dot(pb, vb[g])
            prod = dot_ * ot
            da = jnp.sum(jnp.where(lo, prod, 0.0), axis=-1, keepdims=True)
            db = jnp.sum(jnp.where(lo, 0.0, prod), axis=-1, keepdims=True)
            dsa = pa * (_dot_nt(dot_, va[g]) - da)
            dsb = pb * (_dot_nt(dot_, vb[g]) - db)
            dqt = jnp.where(lo, _dot(dsa, kd[g]), _dot(dsb, kd[g])) * _SCALE
            dq_ref[:, sl] = _rope(dqt, cos_c, -sin_c).astype(BF16)
            acc_k[g] = acc_k[g] + _dot_tn(dsa, qa) + _dot_tn(dsb, qb)
            acc_v[g] = acc_v[g] + _dot_tn(pa, jnp.where(lo, dot_, 0.0)) + _dot_tn(pb, jnp.where(lo, 0.0, dot_))
            row = jnp.where(_lane_lo((1, LANES)), -jnp.sum(psa * da, axis=0, keepdims=True),
                            -jnp.sum(psb * db, axis=0, keepdims=True))
            ds_ref[t:t + 1, :] += row
        fk = [a + pltpu.roll(a, HEAD_DIM, 1) for a in acc_k]
        fv = [a + pltpu.roll(a, HEAD_DIM, 1) for a in acc_v]
        dk2 = jnp.where(lo2, fk[0], fk[1])
        dv2 = jnp.where(lo2, fv[0], fv[1])
        dkp_ref[...] = _rope(dk2[:BLOCK], cos_p, -sin_p)
        dkc_ref[...] = _rope(dk2[BLOCK:], cos_c, -sin_c)
        dvp_ref[...] = dv2[:BLOCK]
        dvc_ref[...] = dv2[BLOCK:]

    blk = pl.BlockSpec((BLOCK, LANES), lambda n: (n, 0))
    kv = jax.ShapeDtypeStruct((L, LANES), F32)
    c_ins = comm.ins if comm else []
    c_shapes = comm.out_shapes if comm else []
    res = pl.pallas_call(
        body,
        grid=(nb,),
        in_specs=_attn_specs(nb) + [pl.BlockSpec((BLOCK, Q_W), lambda n: (n, 0)), _ANY] + [_ANY] * len(c_ins),
        out_specs=tuple([pl.BlockSpec((BLOCK, Q_W), lambda n: (n, 0)), blk, blk, blk, blk,
                         pl.BlockSpec((_TILES, LANES), lambda n: (0, 0))] + [_ANY] * len(c_shapes)),
        out_shape=tuple([jax.ShapeDtypeStruct(dz.shape, dz.dtype), kv, kv, kv, kv,
                         jax.ShapeDtypeStruct((_TILES, LANES), F32)] + c_shapes),
        scratch_shapes=comm.scratch() if comm else [],
        input_output_aliases={11: 0},
        compiler_params=_host_params(("arbitrary",), comm),
        name="attn_bwd",
    )(z, z, z, z, z, cos_t, cos_t, sin_t, sin_t, sinks, d_o, dz, *c_ins)
    return (res[:6], list(res[6:])) if comm else res


def _kv_combine(dkc, dkp, dvc, dvp, dz):
    L = dkc.shape[0]
    nb = L // BLOCK

    def body(kc_ref, kp_ref, vc_ref, vp_ref, dz_in, o_ref):
        live = jnp.where(pl.program_id(0) + 1 < nb, 1.0, 0.0)
        o_ref[:, :LANES] = (kc_ref[...] + live * kp_ref[...]).astype(BF16)
        o_ref[:, LANES:] = (vc_ref[...] + live * vp_ref[...]).astype(BF16)

    cur = pl.BlockSpec((BLOCK, LANES), lambda n: (n, 0))
    nxt = pl.BlockSpec((BLOCK, LANES), lambda n: (jnp.minimum(n + 1, nb - 1), 0))
    kv_block = Q_W // (2 * KV_W)
    return pl.pallas_call(body, grid=(nb,), in_specs=[cur, nxt, cur, nxt, _ANY],
                          out_specs=pl.BlockSpec((BLOCK, 2 * KV_W), lambda n: (n, kv_block)),
                          out_shape=jax.ShapeDtypeStruct(dz.shape, dz.dtype), input_output_aliases={4: 0},
                          compiler_params=_params(("parallel",)), name="kv_combine")(dkc, dkp, dvc, dvp, dz)


def _gate_bwd(d_mix, z, o_ga, y_attn, y_ssm):
    L, D = d_mix.shape
    tc = _pick(math.gcd(D, o_ga), (512, 256, 128))
    tr = _pick(L, (1024, 512, 256, 128))
    nd, gb = D // tc, o_ga // tc

    def body(dm_ref, g_ref, ya_ref, ys_ref, dy_ref, dz_ref):
        dm = dm_ref[...]
        s = _sigmoid(g_ref[...])
        y = jnp.where(pl.program_id(1) < nd, ya_ref[...], ys_ref[...])
        dy_ref[...] = (dm * s).astype(BF16)
        dz_ref[...] = (dm * y * s * (1.0 - s)).astype(BF16)

    blk = lambda f: pl.BlockSpec((tr, tc), f)
    return pl.pallas_call(
        body,
        grid=(L // tr, 2 * nd),
        in_specs=[blk(lambda i, j: (i, j % nd)), blk(lambda i, j: (i, j + gb)),
                  blk(lambda i, j: (i, jnp.minimum(j, nd - 1))), blk(lambda i, j: (i, jnp.maximum(j - nd, 0)))],
        out_specs=(blk(lambda i, j: (i, j)), blk(lambda i, j: (i, j + gb))),
        out_shape=(jax.ShapeDtypeStruct((L, 2 * D), BF16), jax.ShapeDtypeStruct(z.shape, BF16)),
        compiler_params=_params(("parallel", "arbitrary")),
        name="gate_bwd",
    )(d_mix, z, y_attn, y_ssm)


def _discretise(lr, li, ldt, br, bi):
    dt = jnp.exp(ldt)
    mag = jnp.exp(lr * dt)
    a_re, a_im = mag * jnp.cos(li * dt), mag * jnp.sin(li * dt)
    den = lr * lr + li * li
    nr, ni = a_re - 1.0, a_im
    coef_re = (nr * lr + ni * li) / den
    coef_im = (ni * lr - nr * li) / den
    return a_re, a_im, coef_re * br - coef_im * bi, coef_re * bi + coef_im * br


def _disc_specs(n):
    tr = _pick(n, (512,))
    cs = pl.BlockSpec((tr, 1), lambda i: (i, 0))
    ms = pl.BlockSpec((tr, SSM_GC), lambda i: (i, 0))
    return tr, cs, ms


def _disc_fwd(lr, li, ldt, br, bi):
    n = lr.shape[0]
    tr, cs, ms = _disc_specs(n)

    def body(lr_ref, li_ref, dt_ref, br_ref, bi_ref, o1, o2, o3, o4):
        r = _discretise(lr_ref[...], li_ref[...], dt_ref[...], br_ref[...], bi_ref[...])
        o1[...], o2[...], o3[...], o4[...] = r

    col = jax.ShapeDtypeStruct((n, 1), F32)
    mat = jax.ShapeDtypeStruct((n, SSM_GC), F32)
    return pl.pallas_call(body, grid=(n // tr,), in_specs=[cs, cs, cs, ms, ms], out_specs=(cs, cs, ms, ms),
                          out_shape=(col, col, mat, mat), compiler_params=_params(("parallel",)), name="disc_fwd")(
        lr, li, ldt, br, bi)


def _disc_bwd(lr, li, ldt, br, bi, gar, gai, gbr, gbi):
    n = lr.shape[0]
    tr, cs, ms = _disc_specs(n)

    def body(lr_ref, li_ref, dt_ref, br_ref, bi_ref, gar_ref, gai_ref, gbr_ref, gbi_ref, o_lr, o_li, o_dt, o_br, o_bi):
        _, vjp = jax.vjp(_discretise, lr_ref[...], li_ref[...], dt_ref[...], br_ref[...], bi_ref[...])
        g = vjp((gar_ref[...], gai_ref[...], gbr_ref[...], gbi_ref[...]))
        o_lr[...] = g[0]
        o_li[...] = g[1]
        o_dt[...] = jnp.sum(g[2].reshape(tr // SSM_P, SSM_P, 1), axis=1)
        o_br[...] = g[3]
        o_bi[...] = g[4]

    col = jax.ShapeDtypeStruct((n, 1), F32)
    mat = jax.ShapeDtypeStruct((n, SSM_GC), F32)
    return pl.pallas_call(
        body, grid=(n // tr,), in_specs=[cs, cs, cs, ms, ms, cs, cs, ms, ms],
        out_specs=(cs, cs, pl.BlockSpec((tr // SSM_P, 1), lambda i: (i, 0)), ms, ms),
        out_shape=(col, col, jax.ShapeDtypeStruct((n // SSM_P, 1), F32), mat, mat),
        compiler_params=_params(("parallel",)), name="disc_bwd")(lr, li, ldt, br, bi, gar, gai, gbr, gbi)


def _cpow(ar, ai, nsq):
    for _ in range(nsq):
        ar, ai = ar * ar - ai * ai, 2.0 * ar * ai
    return ar, ai


def _ssm_dims(L):
    tc = min(1024, L)
    seg = tc // SUBLANES
    assert seg & (seg - 1) == 0
    return tc, seg, L // tc, seg.bit_length() - 1


def _tile_rows(i):
    return pl.ds(pl.multiple_of(i * SUBLANES, SUBLANES), SUBLANES)


def _rows_to_segments(src_ref, dst_ref, seg):
    def body(i, _):
        dst_ref[_tile_rows(i), :] = src_ref[pl.ds(i, SUBLANES, stride=seg), :]
        return 0
    lax.fori_loop(0, seg, body, 0, unroll=8)


def _segments_to_rows(src_ref, dst_ref, seg):
    def body(i, _):
        dst_ref[pl.ds(i, SUBLANES, stride=seg), :] = src_ref[_tile_rows(i), :]
        return 0
    lax.fori_loop(0, seg, body, 0, unroll=8)


def _ssm_fwd(z, u_off, br_m, bi_m, cr_m, ci_m, a_re3, a_im3, d_row, comm=None):
    L = z.shape[0]
    ngb = br_m.shape[0]
    tc, seg, nc, nsq = _ssm_dims(L)
    ucol = u_off // LANES

    def body(*refs):
        ins, c_ins, outs, c_outs, scratch, sems = _split_refs(refs, 8, 3, comm)
        u_ref, br_ref, bi_ref, cr_ref, ci_ref, ar_ref, ai_ref, d_ref = ins
        y_ref, xr_ref, xi_ref = outs
        bur, bui, car_r, car_i, ini_r, ini_i, up, ys = scratch
        if comm:
            s = pl.program_id(0) * nc + pl.program_id(1)
            comm.run(c_ins, c_outs, sems, s == 0, s == (ngb * nc) // 2, s == ngb * nc - 1)

        @pl.when(pl.program_id(1) == 0)
        def _():
            car_r[...] = jnp.zeros_like(car_r)
            car_i[...] = jnp.zeros_like(car_i)

        _rows_to_segments(u_ref, up, seg)
        u = up[...]
        pr = _dot(u, br_ref[0])
        pi = _dot(u, bi_ref[0])
        for w in range(NW):
            bur[w] = pr[:, w * LANES:(w + 1) * LANES]
            bui[w] = pi[:, w * LANES:(w + 1) * LANES]
        ar = [jnp.broadcast_to(ar_ref[w], (SUBLANES, LANES)) for w in range(NW)]
        ai = [jnp.broadcast_to(ai_ref[w], (SUBLANES, LANES)) for w in range(NW)]

        def step(i, carry, store):
            xr, xi = carry
            rows = _tile_rows(i)
            nr, ni = [], []
            for w in range(NW):
                r = ar[w] * xr[w] - ai[w] * xi[w] + bur[w, rows, :]
                m = ar[w] * xi[w] + ai[w] * xr[w] + bui[w, rows, :]
                if store:
                    xr_ref[w, rows, :] = r
                    xi_ref[w, rows, :] = m
                nr.append(r)
                ni.append(m)
            return tuple(nr), tuple(ni)

        zero = tuple(jnp.zeros((SUBLANES, LANES), F32) for _ in range(NW))
        er, ei = lax.fori_loop(0, seg, functools.partial(step, store=False), (zero, zero), unroll=2)
        for w in range(NW):
            pr_, pi_ = _cpow(ar[w][0:1], ai[w][0:1], nsq)
            sr, si = car_r[w, 0:1, :], car_i[w, 0:1, :]
            for j in range(SUBLANES):
                ini_r[w, j:j + 1, :] = sr
                ini_i[w, j:j + 1, :] = si
                sr, si = (pr_ * sr - pi_ * si + er[w][j:j + 1], pr_ * si + pi_ * sr + ei[w][j:j + 1])
            car_r[w, 0:1, :] = sr
            car_i[w, 0:1, :] = si
        init = (tuple(ini_r[w] for w in range(NW)), tuple(ini_i[w] for w in range(NW)))
        lax.fori_loop(0, seg, functools.partial(step, store=True), init, unroll=2)
        acc = d_ref[...] * u
        for w in range(NW):
            sl = slice(w * LANES, (w + 1) * LANES)
            acc = acc + _dot(xr_ref[w], cr_ref[0, sl, :]) - _dot(xi_ref[w], ci_ref[0, sl, :])
        ys[...] = acc
        _segments_to_rows(ys, y_ref, seg)

    mat_b = pl.BlockSpec((1, LANES, NW * LANES), lambda b, k: (b, 0, 0))
    mat_c = pl.BlockSpec((1, NW * LANES, LANES), lambda b, k: (b, 0, 0))
    a_spec = pl.BlockSpec((NW, 1, LANES), lambda b, k: (b, 0, 0))
    x_spec = pl.BlockSpec((NW, tc, LANES), lambda b, k: (b, k, 0))
    xs = jax.ShapeDtypeStruct((ngb * NW, L, LANES), F32)
    st = pltpu.VMEM((NW, SUBLANES, LANES), F32)
    c_ins = comm.ins if comm else []
    c_shapes = comm.out_shapes if comm else []
    res = pl.pallas_call(
        body,
        grid=(ngb, nc),
        in_specs=[pl.BlockSpec((tc, LANES), lambda b, k: (k, b + ucol)), mat_b, mat_b, mat_c, mat_c, a_spec, a_spec,
                  pl.BlockSpec((1, LANES), lambda b, k: (0, b))] + [_ANY] * len(c_ins),
        out_specs=tuple([pl.BlockSpec((tc, LANES), lambda b, k: (k, b)), x_spec, x_spec] + [_ANY] * len(c_shapes)),
        out_shape=tuple([jax.ShapeDtypeStruct((L, ngb * LANES), F32), xs, xs] + c_shapes),
        scratch_shapes=[pltpu.VMEM((NW, tc, LANES), F32), pltpu.VMEM((NW, tc, LANES), F32), st, st, st, st,
                        pltpu.VMEM((tc, LANES), F32), pltpu.VMEM((tc, LANES), F32)]
        + (comm.scratch() if comm else []),
        compiler_params=_host_params(("arbitrary", "arbitrary"), comm),
        name="ssm_fwd",
    )(z, br_m, bi_m, cr_m, ci_m, a_re3, a_im3, d_row, *c_ins)
    return (res[:3], list(res[3:])) if comm else res


def _ssm_bwd(dy, z, u_off, xs_r, xs_i, br_m, bi_m, cr_m, ci_m, a_re3, a_im3, d_row, dz):
    L = z.shape[0]
    ngb = br_m.shape[0]
    tc, seg, nc, nsq = _ssm_dims(L)
    ucol = u_off // LANES

    def body(dy_ref, u_ref, xr_ref, xi_ref, br_ref, bi_ref, cr_ref, ci_ref, ar_ref, ai_ref, d_ref, dz_in,
             du_ref, gd_ref, gar_ref, gai_ref, gbr_ref, gbi_ref, gcr_ref, gci_ref,
             gr_s, gi_s, car_r, car_i, ini_r, ini_i, acc_r, acc_i, dyp, up, dus, dun):
        k = pl.program_id(1)

        @pl.when(k == 0)
        def _():
            for ref in (car_r, car_i, acc_r, acc_i, gd_ref, gbr_ref, gbi_ref, gcr_ref, gci_ref):
                ref[...] = jnp.zeros_like(ref)

        _rows_to_segments(dy_ref, dyp, seg)
        _rows_to_segments(u_ref, up, seg)
        dy_v = dyp[...]
        u = up[...]
        g_re = _dot_nt(dy_v, cr_ref[0])
        g_im = -_dot_nt(dy_v, ci_ref[0])
        for w in range(NW):
            gr_s[w] = g_re[:, w * LANES:(w + 1) * LANES]
            gi_s[w] = g_im[:, w * LANES:(w + 1) * LANES]
        ar = [jnp.broadcast_to(ar_ref[w], (SUBLANES, LANES)) for w in range(NW)]
        ai = [jnp.broadcast_to(ai_ref[w], (SUBLANES, LANES)) for w in range(NW)]

        def step1(ii, carry):
            xr, xi = carry
            rows = _tile_rows(seg - 1 - ii)
            nr = tuple(ar[w] * xr[w] + ai[w] * xi[w] + gr_s[w, rows, :] for w in range(NW))
            ni = tuple(ar[w] * xi[w] - ai[w] * xr[w] + gi_s[w, rows, :] for w in range(NW))
            return nr, ni

        zero = tuple(jnp.zeros((SUBLANES, LANES), F32) for _ in range(NW))
        er, ei = lax.fori_loop(0, seg, step1, (zero, zero), unroll=2)
        for w in range(NW):
            pr_, pi_ = _cpow(ar[w][0:1], ai[w][0:1], nsq)
            sr, si = car_r[w, 0:1, :], car_i[w, 0:1, :]
            for j in reversed(range(SUBLANES)):
                ini_r[w, j:j + 1, :] = sr
                ini_i[w, j:j + 1, :] = si
                sr, si = (pr_ * sr + pi_ * si + er[w][j:j + 1], pr_ * si - pi_ * sr + ei[w][j:j + 1])
            car_r[w, 0:1, :] = sr
            car_i[w, 0:1, :] = si

        def step2(ii, carry):
            gxr, gxi, acr, aci = carry
            rows = _tile_rows(seg - 1 - ii)
            nr, ni, nar, nai = [], [], [], []
            for w in range(NW):
                xr_t, xi_t = xr_ref[w, rows, :], xi_ref[w, rows, :]
                nar.append(acr[w] + gxr[w] * xr_t + gxi[w] * xi_t)
                nai.append(aci[w] + gxi[w] * xr_t - gxr[w] * xi_t)
                r = ar[w] * gxr[w] + ai[w] * gxi[w] + gr_s[w, rows, :]
                m = ar[w] * gxi[w] - ai[w] * gxr[w] + gi_s[w, rows, :]
                gr_s[w, rows, :] = r
                gi_s[w, rows, :] = m
                nr.append(r)
                ni.append(m)
            return tuple(nr), tuple(ni), tuple(nar), tuple(nai)

        init = (tuple(ini_r[w] for w in range(NW)), tuple(ini_i[w] for w in range(NW)),
                tuple(acc_r[w] for w in range(NW)), tuple(acc_i[w] for w in range(NW)))
        _, _, acr, aci = lax.fori_loop(0, seg, step2, init, unroll=2)
        du = d_ref[...] * dy_v
        for w in range(NW):
            sl = slice(w * LANES, (w + 1) * LANES)
            acc_r[w] = acr[w]
            acc_i[w] = aci[w]
            gxr_w, gxi_w = gr_s[w], gi_s[w]
            du = du + _dot_nt(gxr_w, br_ref[0, :, sl]) + _dot_nt(gxi_w, bi_ref[0, :, sl])
            gbr_ref[0, :, sl] += _dot_tn(u, gxr_w)
            gbi_ref[0, :, sl] += _dot_tn(u, gxi_w)
            gcr_ref[0, sl, :] += _dot_tn(xr_ref[w], dy_v)
            gci_ref[0, sl, :] += _dot_tn(-xi_ref[w], dy_v)
        dus[...] = du
        _segments_to_rows(dus, dun, seg)
        du_ref[...] = dun[...].astype(BF16)
        gd_ref[...] += jnp.sum(dy_v * u, axis=0, keepdims=True)

        @pl.when(k == nc - 1)
        def _():
            for w in range(NW):
                gar_ref[w] = jnp.sum(acc_r[w], axis=0, keepdims=True)
                gai_ref[w] = jnp.sum(acc_i[w], axis=0, keepdims=True)

    rk = lambda k: nc - 1 - k
    mat_b = pl.BlockSpec((1, LANES, NW * LANES), lambda b, k: (b, 0, 0))
    mat_c = pl.BlockSpec((1, NW * LANES, LANES), lambda b, k: (b, 0, 0))
    a_spec = pl.BlockSpec((NW, 1, LANES), lambda b, k: (b, 0, 0))
    x_spec = pl.BlockSpec((NW, tc, LANES), lambda b, k: (b, rk(k), 0))
    st = pltpu.VMEM((NW, SUBLANES, LANES), F32)
    big = pltpu.VMEM((NW, tc, LANES), F32)
    return pl.pallas_call(
        body,
        grid=(ngb, nc),
        in_specs=[pl.BlockSpec((tc, LANES), lambda b, k: (rk(k), b)),
                  pl.BlockSpec((tc, LANES), lambda b, k: (rk(k), b + ucol)),
                  x_spec, x_spec, mat_b, mat_b, mat_c, mat_c, a_spec, a_spec,
                  pl.BlockSpec((1, LANES), lambda b, k: (0, b)), _ANY],
        out_specs=(pl.BlockSpec((tc, LANES), lambda b, k: (rk(k), b + ucol)),
                   pl.BlockSpec((1, LANES), lambda b, k: (0, b)), a_spec, a_spec, mat_b, mat_b, mat_c, mat_c),
        out_shape=(jax.ShapeDtypeStruct(dz.shape, dz.dtype),
                   jax.ShapeDtypeStruct((1, ngb * LANES), F32),
                   jax.ShapeDtypeStruct((ngb * NW, 1, LANES), F32), jax.ShapeDtypeStruct((ngb * NW, 1, LANES), F32),
                   jax.ShapeDtypeStruct(br_m.shape, F32), jax.ShapeDtypeStruct(br_m.shape, F32),
                   jax.ShapeDtypeStruct(cr_m.shape, F32), jax.ShapeDtypeStruct(cr_m.shape, F32)),
        scratch_shapes=[big, big, st, st, st, st, st, st] + [pltpu.VMEM((tc, LANES), F32)] * 4,
        input_output_aliases={11: 0},
        compiler_params=_params(("arbitrary", "arbitrary")),
        name="ssm_bwd",
    )(dy, z, xs_r, xs_i, br_m, bi_m, cr_m, ci_m, a_re3, a_im3, d_row, dz)


def _block_diag_in(bb, ngb):
    t = bb.reshape(ngb, GROUPS_PER_BLOCK, SSM_P, SSM_GC).transpose(0, 1, 3, 2)
    eye = jnp.eye(GROUPS_PER_BLOCK, dtype=F32)
    m = t[:, :, :, None, :] * eye[None, :, None, :, None]
    return m.reshape(ngb, GROUPS_PER_BLOCK * SSM_GC, GROUPS_PER_BLOCK * SSM_P)


def _block_diag_out(c, ngb):
    t = c.reshape(ngb, GROUPS_PER_BLOCK, SSM_GC, SSM_P).transpose(0, 1, 3, 2)
    eye = jnp.eye(GROUPS_PER_BLOCK, dtype=F32)
    m = t[:, :, :, None, :] * eye[None, :, None, :, None]
    return m.reshape(ngb, GROUPS_PER_BLOCK * SSM_P, GROUPS_PER_BLOCK * SSM_GC)


def _diag_in(m, ngb):
    m5 = m.reshape(ngb, GROUPS_PER_BLOCK, SSM_GC, GROUPS_PER_BLOCK, SSM_P)
    d = jnp.diagonal(m5, axis1=1, axis2=3)
    return d.transpose(0, 3, 2, 1).reshape(ngb * GROUPS_PER_BLOCK * SSM_P, SSM_GC)


def _diag_out(m, ngb):
    m5 = m.reshape(ngb, GROUPS_PER_BLOCK, SSM_P, GROUPS_PER_BLOCK, SSM_GC)
    d = jnp.diagonal(m5, axis1=1, axis2=3)
    return d.transpose(0, 3, 2, 1).reshape(ngb * GROUPS_PER_BLOCK, SSM_GC, SSM_P)


_ANY = pl.BlockSpec(memory_space=pl.ANY)


class _Comm:
    def __init__(self, ins, out_shapes, n_sem, start, mid, finish):
        self.ins, self.out_shapes, self.n_sem = list(ins), list(out_shapes), n_sem
        self.start, self.mid, self.finish = start, mid, finish

    def scratch(self):
        return [pltpu.SemaphoreType.DMA((self.n_sem,)), pltpu.SemaphoreType.DMA((self.n_sem,)),
                pltpu.SemaphoreType.DMA((len(self.ins),))]

    def run(self, in_refs, out_refs, sems, first, mid, last):
        send, recv, local = sems

        @pl.when(first)
        def _():
            self.start(in_refs, out_refs, send, recv, local)

        @pl.when(mid)
        def _():
            self.mid(in_refs, out_refs, send, recv, local)

        @pl.when(last)
        def _():
            self.finish(in_refs, out_refs, send, recv, local)


def _split_refs(refs, n_in, n_out, comm):
    ci = len(comm.ins) if comm else 0
    co = len(comm.out_shapes) if comm else 0
    ins = refs[:n_in]
    c_ins = refs[n_in:n_in + ci]
    outs = refs[n_in + ci:n_in + ci + n_out]
    c_outs = refs[n_in + ci + n_out:n_in + ci + n_out + co]
    rest = refs[n_in + ci + n_out + co:]
    if comm:
        return ins, c_ins, outs, c_outs, rest[:-3], rest[-3:]
    return ins, c_ins, outs, c_outs, rest, ()


def _run_comm(comm, name):
    ni, no = len(comm.ins), len(comm.out_shapes)

    def body(*refs):
        args = (refs[:ni], refs[ni:ni + no]) + tuple(refs[ni + no:])
        comm.start(*args)
        comm.mid(*args)
        comm.finish(*args)

    return pl.pallas_call(
        body,
        in_specs=[_ANY] * ni,
        out_specs=tuple([_ANY] * no),
        out_shape=tuple(comm.out_shapes),
        scratch_shapes=comm.scratch(),
        compiler_params=pltpu.CompilerParams(has_side_effects=True),
        name=name,
    )(*comm.ins)


def _ag_comm(shards):
    n = len(shards)

    def env(ins, outs, send_sems, recv_sems):
        x, y, c = lax.axis_index("x"), lax.axis_index("y"), lax.axis_index("c")
        me, sibling = (x, y, c), (x, y, 1 - c)
        chips = [(1 - x, y), (x, 1 - y), (1 - x, 1 - y)]

        def copy(a, k, block, to, src=None):
            s = 4 * block[0] + 2 * block[1] + block[2]
            return pltpu.make_async_remote_copy(
                src_ref=outs[a].at[s] if src is None else src, dst_ref=outs[a].at[s],
                send_sem=send_sems.at[7 * a + k], recv_sem=recv_sems.at[7 * a + k],
                device_id=to, device_id_type=MESH)

        return c, me, sibling, chips, copy

    def own(ins, outs, local_sems, a):
        x, y, c = lax.axis_index("x"), lax.axis_index("y"), lax.axis_index("c")
        return pltpu.make_async_copy(ins[a], outs[a].at[4 * x + 2 * y + c], local_sems.at[a])

    def first_sends(ins, copy, me, sibling, chips, c, a):
        return [copy(a, 0, me, sibling, src=ins[a])] + [
            copy(a, 1 + j, me, (*chip, c), src=ins[a]) for j, chip in enumerate(chips)]

    def start(ins, outs, send_sems, recv_sems, local_sems):
        c, me, sibling, chips, copy = env(ins, outs, send_sems, recv_sems)
        for a in range(n):
            own(ins, outs, local_sems, a).start()
        for a in range(n):
            for cp in first_sends(ins, copy, me, sibling, chips, c, a):
                cp.start()

    def mid(ins, outs, send_sems, recv_sems, local_sems):
        c, me, sibling, chips, copy = env(ins, outs, send_sems, recv_sems)
        for a in range(n):
            for j, chip in enumerate(chips):
                copy(a, 1 + j, (*chip, c), me).wait_recv()
                copy(a, 4 + j, (*chip, c), sibling).start()

    def finish(ins, outs, send_sems, recv_sems, local_sems):
        c, me, sibling, chips, copy = env(ins, outs, send_sems, recv_sems)
        for a in range(n):
            copy(a, 0, sibling, me).wait_recv()
            for j, chip in enumerate(chips):
                copy(a, 4 + j, (*chip, 1 - c), me).wait_recv()
        for a in range(n):
            for cp in first_sends(ins, copy, me, sibling, chips, c, a):
                cp.wait_send()
            for j, chip in enumerate(chips):
                copy(a, 4 + j, (*chip, c), sibling).wait_send()
            own(ins, outs, local_sems, a).wait()

    return _Comm(shards, [jax.ShapeDtypeStruct((N_DEV,) + s.shape, s.dtype) for s in shards], 7 * n,
                 start, mid, finish)


def _sibling_comm(parts):
    n = len(parts)

    def copies(ins, outs, send_sems, recv_sems):
        x, y, c = lax.axis_index("x"), lax.axis_index("y"), lax.axis_index("c")
        return [pltpu.make_async_remote_copy(
            src_ref=ins[a].at[2 * q + (1 - c)], dst_ref=outs[a].at[q],
            send_sem=send_sems.at[4 * a + q], recv_sem=recv_sems.at[4 * a + q],
            device_id=(x, y, 1 - c), device_id_type=MESH) for a in range(n) for q in range(4)]

    def start(ins, outs, send_sems, recv_sems, local_sems):
        for cp in copies(ins, outs, send_sems, recv_sems):
            cp.start()

    def mid(ins, outs, send_sems, recv_sems, local_sems):
        pass

    def finish(ins, outs, send_sems, recv_sems, local_sems):
        for cp in copies(ins, outs, send_sems, recv_sems):
            cp.wait()

    return _Comm(parts, [jax.ShapeDtypeStruct((4,) + p.shape[1:], p.dtype) for p in parts], 4 * n,
                 start, mid, finish)


def _chips_comm(parts):
    n = len(parts)

    def copies(ins, outs, send_sems, recv_sems):
        x, y, c = lax.axis_index("x"), lax.axis_index("y"), lax.axis_index("c")
        chips = [(1 - x, y), (x, 1 - y), (1 - x, 1 - y)]
        return [pltpu.make_async_remote_copy(
            src_ref=ins[a].at[2 * px + py], dst_ref=outs[a].at[j],
            send_sem=send_sems.at[3 * a + j], recv_sem=recv_sems.at[3 * a + j],
            device_id=(px, py, c), device_id_type=MESH) for a in range(n) for j, (px, py) in enumerate(chips)]

    def start(ins, outs, send_sems, recv_sems, local_sems):
        for cp in copies(ins, outs, send_sems, recv_sems):
            cp.start()

    def mid(ins, outs, send_sems, recv_sems, local_sems):
        pass

    def finish(ins, outs, send_sems, recv_sems, local_sems):
        for cp in copies(ins, outs, send_sems, recv_sems):
            cp.wait()

    return _Comm(parts, [jax.ShapeDtypeStruct((3,) + p.shape[1:], p.dtype) for p in parts], 3 * n,
                 start, mid, finish)


def _sibling_add(part, recv, name):
    _, R, C = part.shape
    tr = _pick(R, (256, 128, 80))
    c = lax.axis_index("c")

    def body(c_ref, p_ref, r_ref, o_ref, o16_ref):
        t = p_ref[...] + r_ref[...]
        o_ref[...] = t
        o16_ref[...] = t.astype(BF16)

    blk = pl.BlockSpec((1, tr, C), lambda q, i, c_ref: (q, i, 0))
    return pl.pallas_call(
        body,
        grid_spec=pltpu.PrefetchScalarGridSpec(
            num_scalar_prefetch=1,
            grid=(4, R // tr),
            in_specs=[pl.BlockSpec((1, tr, C), lambda q, i, c_ref: (2 * q + c_ref[0], i, 0)), blk],
            out_specs=(blk, blk),
        ),
        out_shape=(jax.ShapeDtypeStruct((4, R, C), F32), jax.ShapeDtypeStruct((4, R, C), BF16)),
        compiler_params=_params(("parallel", "parallel")),
        name=name,
    )(jnp.reshape(c, (1,)).astype(jnp.int32), part, recv)


def _adamw(w, g, m, v):
    m = ADAM_B1 * m + (1.0 - ADAM_B1) * g
    v = ADAM_B2 * v + (1.0 - ADAM_B2) * (g * g)
    m_hat = m / (1.0 - ADAM_B1 ** ADAM_STEP)
    v_hat = v / (1.0 - ADAM_B2 ** ADAM_STEP)
    delta = -ADAM_LR * (m_hat / (jnp.sqrt(v_hat) + ADAM_EPS) + ADAM_WD * w)
    return delta, m, v


def _adam_big(t, recv, w, m, v, name):
    _, R, C = t.shape
    tr = _pick(R, (256, 128))
    chip = 2 * lax.axis_index("x") + lax.axis_index("y")

    def body(q_ref, t_ref, r_ref, w_ref, m_ref, v_ref, g_ref, d_ref, nm_ref, nv_ref):
        g = t_ref[0] + r_ref[0].astype(F32) + r_ref[1].astype(F32) + r_ref[2].astype(F32)
        g_ref[...] = g
        d_ref[...], nm_ref[...], nv_ref[...] = _adamw(w_ref[...], g, m_ref[...], v_ref[...])

    blk = pl.BlockSpec((tr, C), lambda i, q_ref: (i, 0))
    o = jax.ShapeDtypeStruct((R, C), F32)
    return pl.pallas_call(
        body,
        grid_spec=pltpu.PrefetchScalarGridSpec(
            num_scalar_prefetch=1,
            grid=(R // tr,),
            in_specs=[pl.BlockSpec((1, tr, C), lambda i, q_ref: (q_ref[0], i, 0)),
                      pl.BlockSpec((3, tr, C), lambda i, q_ref: (0, i, 0)), blk, blk, blk],
            out_specs=(blk, blk, blk, blk),
        ),
        out_shape=(o, o, o, o),
        compiler_params=_params(("parallel",)),
        name=name,
    )(jnp.reshape(chip, (1,)).astype(jnp.int32), t, recv, w, m, v)


def _reduce_big(t, recv, name):
    _, R, C = t.shape
    tr = _pick(R, (256, 128, 80))
    chip = 2 * lax.axis_index("x") + lax.axis_index("y")

    def body(q_ref, t_ref, r_ref, g_ref):
        g_ref[...] = t_ref[0] + r_ref[0].astype(F32) + r_ref[1].astype(F32) + r_ref[2].astype(F32)

    return pl.pallas_call(
        body,
        grid_spec=pltpu.PrefetchScalarGridSpec(
            num_scalar_prefetch=1,
            grid=(R // tr,),
            in_specs=[pl.BlockSpec((1, tr, C), lambda i, q_ref: (q_ref[0], i, 0)),
                      pl.BlockSpec((3, tr, C), lambda i, q_ref: (0, i, 0))],
            out_specs=pl.BlockSpec((tr, C), lambda i, q_ref: (i, 0)),
        ),
        out_shape=jax.ShapeDtypeStruct((R, C), F32),
        compiler_params=_params(("parallel",)),
        name=name,
    )(jnp.reshape(chip, (1,)).astype(jnp.int32), t, recv)


def _adam_only(g, w, m, v, name):
    R, C = g.shape
    tr = _pick(R, (256, 128))

    def body(g_ref, w_ref, m_ref, v_ref, d_ref, nm_ref, nv_ref):
        d_ref[...], nm_ref[...], nv_ref[...] = _adamw(w_ref[...], g_ref[...], m_ref[...], v_ref[...])

    blk = pl.BlockSpec((tr, C), lambda i: (i, 0))
    o = jax.ShapeDtypeStruct((R, C), F32)
    return pl.pallas_call(body, grid=(R // tr,), in_specs=[blk] * 4, out_specs=(blk, blk, blk),
                          out_shape=(o, o, o), compiler_params=_params(("parallel",)), name=name)(g, w, m, v)


def _gather8_comm(gbuf):
    def copies(ins, outs, send_sems, recv_sems):
        x, y, c = lax.axis_index("x"), lax.axis_index("y"), lax.axis_index("c")
        me = 4 * x + 2 * y + c
        out = []
        for k in range(1, N_DEV):
            fx, fy, fc = (k >> 2) & 1, (k >> 1) & 1, k & 1
            px, py, pc = x + fx - 2 * x * fx, y + fy - 2 * y * fy, c + fc - 2 * c * fc
            send = pltpu.make_async_remote_copy(
                src_ref=ins[0], dst_ref=outs[0].at[me], send_sem=send_sems.at[k - 1], recv_sem=recv_sems.at[k - 1],
                device_id=(px, py, pc), device_id_type=MESH)
            recv = pltpu.make_async_remote_copy(
                src_ref=ins[0], dst_ref=outs[0].at[4 * px + 2 * py + pc], send_sem=send_sems.at[k - 1],
                recv_sem=recv_sems.at[k - 1], device_id=(px, py, pc), device_id_type=MESH)
            out.append((send, recv))
        return me, out

    def start(ins, outs, send_sems, recv_sems, local_sems):
        me, cps = copies(ins, outs, send_sems, recv_sems)
        pltpu.make_async_copy(ins[0], outs[0].at[me], local_sems.at[0]).start()
        for send, _ in cps:
            send.start()

    def mid(ins, outs, send_sems, recv_sems, local_sems):
        pass

    def finish(ins, outs, send_sems, recv_sems, local_sems):
        me, cps = copies(ins, outs, send_sems, recv_sems)
        for send, recv in cps:
            recv.wait_recv()
            send.wait_send()
        pltpu.make_async_copy(ins[0], outs[0].at[me], local_sems.at[0]).wait()

    return _Comm([gbuf], [jax.ShapeDtypeStruct((N_DEV,) + gbuf.shape, gbuf.dtype)], N_DEV - 1, start, mid, finish)


def _sum8_adam(slots, wbuf, mbuf, vbuf, name):
    R = wbuf.shape[0]

    def body(s_ref, w_ref, m_ref, v_ref, gs_ref, d_ref, nm_ref, nv_ref):
        g = s_ref[0]
        for s in range(1, N_DEV):
            g = g + s_ref[s]
        gs_ref[...] = g
        d_ref[...], nm_ref[...], nv_ref[...] = _adamw(w_ref[...], g, m_ref[...], v_ref[...])

    o = jax.ShapeDtypeStruct((R, LANES), F32)
    return pl.pallas_call(body, out_shape=(o, o, o, o),
                          compiler_params=pltpu.CompilerParams(vmem_limit_bytes=VMEM_LIMIT), name=name)(
        slots, wbuf, mbuf, vbuf)


def _pack(items):
    rows, spans, r0 = [], [], 0
    for a in items:
        n = a.size
        nr = -(-n // LANES)
        rows.append(jnp.pad(a.reshape(-1).astype(F32), (0, nr * LANES - n)).reshape(nr, LANES))
        spans.append((r0, nr, a.shape))
        r0 += nr
    pad = -r0 % SUBLANES
    if pad:
        rows.append(jnp.zeros((pad, LANES), F32))
    return jnp.concatenate(rows, axis=0), spans


def _unpack(buf, spans):
    return [buf[r0:r0 + nr].reshape(-1)[:math.prod(shape)].reshape(shape) for r0, nr, shape in spans]


def kernel(x, norm_mix_pre, norm_mix_post, norm_mlp_pre, norm_mlp_post, w_in, sinks, lam_re, lam_im, log_dt, b_re, b_im, c_re, c_im, d_skip, w_glu, w_branch, w_out, w_up, w_down, loss_target, m_norm_mix_pre, m_norm_mix_post, m_norm_mlp_pre, m_norm_mlp_post, m_w_in, m_sinks, m_lam_re, m_lam_im, m_log_dt, m_b_re, m_b_im, m_c_re, m_c_im, m_d_skip, m_w_glu, m_w_branch, m_w_out, m_w_up, m_w_down, v_norm_mix_pre, v_norm_mix_post, v_norm_mlp_pre, v_norm_mlp_post, v_w_in, v_sinks, v_lam_re, v_lam_im, v_log_dt, v_b_re, v_b_im, v_c_re, v_c_im, v_d_skip, v_w_glu, v_w_branch, v_w_out, v_w_up, v_w_down):
    _, L, D = x.shape
    xs = x[0]
    tgt = loss_target[0]
    ssm_w = D // 2
    n_groups = ssm_w // SSM_GC
    ngb = n_groups // GROUPS_PER_BLOCK
    n_state = n_groups * SSM_P
    d_ff = w_up.shape[2] * N_DEV
    o_k, o_v, o_u = Q_W, Q_W + KV_W, Q_W + 2 * KV_W
    o_ga = o_u + ssm_w
    o_gs = o_ga + D

    big = {"w_in": w_in[0], "w_glu": w_glu[0], "w_branch": w_branch[0], "w_out": w_out[0],
           "w_up": w_up[0], "w_down": w_down[0]}
    col_sharded = ("w_in", "w_glu", "w_up")
    names = list(big)
    shard16 = {k: (big[k].T if k in col_sharded else big[k]).astype(BF16) for k in names}
    full = {}

    def gathered(keys, arrays):
        for k, g in zip(keys, arrays):
            _, r, c = g.shape
            full[k] = g.reshape(N_DEV * r, c)

    def by_owner(g):
        return g.reshape(N_DEV, g.shape[0] // N_DEV, g.shape[1])

    col = lambda a: a.reshape(n_state, 1)
    lr_c, li_c = col(lam_re[0]), col(lam_im[0])
    ldt_c = jnp.repeat(log_dt[0], SSM_P).reshape(n_state, 1)
    b_re_c, b_im_c = b_re[0].reshape(n_state, SSM_GC), b_im[0].reshape(n_state, SSM_GC)
    a_re, a_im, bb_re, bb_im = _disc_fwd(lr_c, li_c, ldt_c, b_re_c, b_im_c)
    a_re3 = a_re.reshape(n_state // LANES, 1, LANES)
    a_im3 = a_im.reshape(n_state // LANES, 1, LANES)
    br_m = _block_diag_in(bb_re, ngb).astype(BF16)
    bi_m = _block_diag_in(bb_im, ngb).astype(BF16)
    cr_m = _block_diag_out(c_re[0], ngb).astype(BF16)
    ci_m = _block_diag_out(c_im[0], ngb).astype(BF16)
    d_row = d_skip[0].reshape(1, ssm_w)
    cos_t, sin_t = _rope_tables(L)

    h, g1 = _rms_pre(xs, norm_mix_pre, _ag_comm([shard16["w_in"]]))
    gathered(["w_in"], g1)
    z, g3 = _mm(h, full["w_in"], mode="nt", name="mm_z",
                comm=_ag_comm([shard16[k] for k in ("w_glu", "w_branch", "w_out")]))
    gathered(["w_glu", "w_branch", "w_out"], g3)
    wb_a, wb_s = full["w_branch"][:Q_W], full["w_branch"][Q_W:]
    o_attn = _attn_fwd(z, cos_t, sin_t, sinks)
    (y_pre, xs_r, xs_i), g1 = _ssm_fwd(z, o_u, br_m, bi_m, cr_m, ci_m, a_re3, a_im3, d_row,
                                       comm=_ag_comm([shard16["w_up"]]))
    gathered(["w_up"], g1)
    gy = _ew(lambda y: (_gelu(y),), [(y_pre, 0)], (BF16,), rows=L, ncols=ssm_w, name="gelu")
    zg = _mm(gy, full["w_glu"], mode="nt", name="mm_zg")
    o_ssm = _ew(lambda a, b: (a * _sigmoid(b),), [(zg, 0), (zg, ssm_w)], (BF16,), rows=L, ncols=ssm_w, name="glu")
    y_attn = _mm(o_attn, wb_a, mode="nn", name="mm_y_attn")
    y_ssm = _mm(o_ssm, wb_s, mode="nn", name="mm_y_ssm")
    mix = _ew(lambda ga, gs, ya, ys: (_sigmoid(ga) * ya + _sigmoid(gs) * ys,),
              [(z, o_ga), (z, o_gs), (y_attn, 0), (y_ssm, 0)], (BF16,), rows=L, ncols=D, name="mix")
    mixed = _mm(mix, full["w_out"], mode="nn", name="mm_mixed")
    x1, h2 = _post_pre(xs, mixed, norm_mix_post, norm_mlp_pre)

    def relu_sq(acc):
        a = jnp.maximum(acc, 0.0)
        return a, a * a

    (act, act2), g1 = _mm(h2, full["w_up"], mode="nt", name="mm_up", out_dtypes=(BF16, BF16), epi=relu_sq,
                          comm=_ag_comm([shard16["w_down"]]))
    gathered(["w_down"], g1)
    dn = _mm(act2, full["w_down"], mode="nn", name="mm_down")
    dx2, d_dn, dg_mlp_post, loss_part = _loss_bwd(x1, dn, norm_mlp_post, tgt)

    d_pre = _mm(d_dn, full["w_down"], mode="nt", name="mm_d_act", out_dtypes=(BF16,),
                epi=lambda acc, a: (acc * (2.0 * a.astype(F32)),), extras=(act,))
    gw_down = _mm(act2, d_dn, mode="tn", name="mm_gw_down")
    p_down = by_owner(gw_down)
    dh2, (sib_down,) = _mm(d_pre, full["w_up"], mode="nn", name="mm_dh2", comm=_sibling_comm([p_down]))
    t_down, t16_down = _sibling_add(p_down, sib_down, "rs_add_w_down")
    gw_up, (chips_down,) = _mm(d_pre, h2, mode="tn", name="mm_gw_up", comm=_chips_comm([t16_down]))
    p_up = by_owner(gw_up)
    dx1, d_mixed, dg_mlp_pre, dg_mix_post = _norm_bwd_pair(x1, dh2, dx2, mixed, norm_mlp_pre, norm_mix_post)
    d_mix, (sib_up,) = _mm(d_mixed, full["w_out"], mode="nt", name="mm_d_mix", comm=_sibling_comm([p_up]))
    t_up, t16_up = _sibling_add(p_up, sib_up, "rs_add_w_up")
    gw_out = _mm(mix, d_mixed, mode="tn", name="mm_gw_out")
    d_y2, dz = _gate_bwd(d_mix, z, o_ga, y_attn, y_ssm)
    d_o_attn = _mm(d_y2, wb_a, mode="nt", name="mm_d_o_attn", a_win=(0, D))
    gwb_a = _mm(o_attn, d_y2, mode="tn", name="mm_gwb_a", b_win=(0, D))
    d_o_ssm = _mm(d_y2, wb_s, mode="nt", name="mm_d_o_ssm", a_win=(D, D))
    gwb_s = _mm(o_ssm, d_y2, mode="tn", name="mm_gwb_s", b_win=(D, D))

    def glu_bwd(do, a, b):
        s = _sigmoid(b)
        return do * s, do * a * s * (1.0 - s)

    d_zg_a, d_zg_b = _ew(glu_bwd, [(d_o_ssm, 0), (zg, 0), (zg, ssm_w)], (BF16, BF16), rows=L, ncols=ssm_w, name="glu_bwd")
    d_zg = jnp.concatenate([d_zg_a, d_zg_b], axis=1)
    dy_pre = _mm(d_zg, full["w_glu"], mode="nn", name="mm_d_gy",
                 epi=lambda acc, y: (acc * _gelu_grad(y),), extras=(y_pre,))
    gw_glu = _mm(d_zg, gy, mode="tn", name="mm_gw_glu")
    mids = ["w_glu", "w_branch", "w_out"]
    p_mid = [by_owner(gw_glu), by_owner(jnp.concatenate([gwb_a, gwb_s], axis=0)), by_owner(gw_out)]
    sib_mid = _run_comm(_sibling_comm(p_mid), "rs_sibling_mid")
    t_mid = [_sibling_add(p, r, "rs_add_" + k) for k, p, r in zip(mids, p_mid, sib_mid)]
    (dz, g_dskip, g_ar3, g_ai3, g_br_m, g_bi_m, g_cr_m, g_ci_m) = _ssm_bwd(
        dy_pre, z, o_u, xs_r, xs_i, br_m, bi_m, cr_m, ci_m, a_re3, a_im3, d_row, dz)
    g_lr, g_li, g_ldt, g_b_re, g_b_im = _disc_bwd(
        lr_c, li_c, ldt_c, b_re_c, b_im_c, g_ar3.reshape(n_state, 1), g_ai3.reshape(n_state, 1),
        _diag_in(g_br_m, ngb), _diag_in(g_bi_m, ngb))
    (dz, dkc, dkp, dvc, dvp, dsink_rows), chips_a = _attn_bwd(
        z, d_o_attn, cos_t, sin_t, sinks, dz, comm=_chips_comm([t16_up] + [t16 for _, t16 in t_mid]))
    chips_up, chips_mid = chips_a[0], chips_a[1:]
    dz = _kv_combine(dkc, dkp, dvc, dvp, dz)
    dsink = jnp.stack([dsink_rows[:, 0], dsink_rows[:, HEAD_DIM]], axis=1).reshape(1, N_Q_HEADS)
    small_names = ["norm_mix_post", "norm_mlp_pre", "norm_mlp_post", "sinks", "lam_re", "lam_im",
                   "log_dt", "b_re", "b_im", "c_re", "c_im", "d_skip"]
    small_g = [dg_mix_post, dg_mlp_pre, dg_mlp_post, dsink,
               g_lr.reshape(lam_re.shape), g_li.reshape(lam_im.shape), g_ldt.reshape(log_dt.shape),
               g_b_re.reshape(b_re.shape), g_b_im.reshape(b_im.shape),
               _diag_out(g_cr_m, ngb).reshape(c_re.shape), _diag_out(g_ci_m, ngb).reshape(c_im.shape),
               g_dskip.reshape(d_skip.shape)]
    gbuf, spans = _pack(small_g + [loss_part])
    gw_in, (small_slots,) = _mm(dz, h, mode="tn", name="mm_gw_in", comm=_gather8_comm(gbuf))
    p_in = by_owner(gw_in)
    (sib_in,) = _run_comm(_sibling_comm([p_in]), "rs_sibling_w_in")
    t_in, t16_in = _sibling_add(p_in, sib_in, "rs_add_w_in")
    dh, (chips_in,) = _mm(dz, full["w_in"], mode="nn", name="mm_dh", comm=_chips_comm([t16_in]))
    grad_x, dg_mix_pre = _final_bwd(xs, dh, dx1, norm_mix_pre, None)

    reduced = {"w_in": (t_in, chips_in), "w_up": (t_up, chips_up), "w_down": (t_down, chips_down)}
    for k, (t32, _), r in zip(mids, t_mid, chips_mid):
        reduced[k] = (t32, r)
    moments = {"w_in": (m_w_in, v_w_in), "w_glu": (m_w_glu, v_w_glu), "w_branch": (m_w_branch, v_w_branch),
               "w_out": (m_w_out, v_w_out), "w_up": (m_w_up, v_w_up), "w_down": (m_w_down, v_w_down)}
    big_out = {}
    for k in names:
        t, r = reduced[k]
        mm_, vv_ = moments[k]
        if k in col_sharded:
            g = _reduce_big(t, r, "reduce_" + k).T
            big_out[k] = [o[None] for o in (g,) + tuple(_adam_only(g, big[k], mm_[0], vv_[0], "adam_" + k))]
        else:
            big_out[k] = [o[None] for o in _adam_big(t, r, big[k], mm_[0], vv_[0], "adam_" + k)]

    small_w = [norm_mix_post, norm_mlp_pre, norm_mlp_post, sinks, lam_re, lam_im, log_dt,
               b_re, b_im, c_re, c_im, d_skip]
    small_m = [m_norm_mix_post, m_norm_mlp_pre, m_norm_mlp_post, m_sinks, m_lam_re, m_lam_im,
               m_log_dt, m_b_re, m_b_im, m_c_re, m_c_im, m_d_skip]
    small_v = [v_norm_mix_post, v_norm_mlp_pre, v_norm_mlp_post, v_sinks, v_lam_re, v_lam_im,
               v_log_dt, v_b_re, v_b_im, v_c_re, v_c_im, v_d_skip]
    zero1 = jnp.zeros((1, 1), F32)
    wbuf, _ = _pack(small_w + [zero1])
    mbuf, _ = _pack(small_m + [zero1])
    vbuf, _ = _pack(small_v + [zero1])
    gs, ds, nms, nvs = [_unpack(b, spans) for b in _sum8_adam(small_slots, wbuf, mbuf, vbuf, "small_adam")]
    loss = gs[-1].reshape(())
    tbuf, tspans = _pack([dg_mix_pre])
    (tail_slots,) = _run_comm(_gather8_comm(tbuf), "gather_tail")
    tail = _sum8_adam(tail_slots, _pack([norm_mix_pre])[0], _pack([m_norm_mix_pre])[0],
                      _pack([v_norm_mix_pre])[0], "small_adam_tail")
    small_names = ["norm_mix_pre"] + small_names
    gs, ds, nms, nvs = [_unpack(t, tspans) + src for t, src in zip(tail, (gs, ds, nms, nvs))]

    order = ["norm_mix_pre", "norm_mix_post", "norm_mlp_pre", "norm_mlp_post", "w_in", "sinks", "lam_re", "lam_im",
             "log_dt", "b_re", "b_im", "c_re", "c_im", "d_skip", "w_glu", "w_branch", "w_out", "w_up", "w_down"]
    outs = [loss, grad_x[None]]
    for idx, src in enumerate((gs, ds, nms, nvs)):
        for k in order:
            outs.append(big_out[k][idx] if k in big_out else src[small_names.index(k)])
    return tuple(outs)
```

```python
import functools
import math

import jax
import jax.numpy as jnp
from jax import lax
from jax.experimental import pallas as pl
from jax.experimental.pallas import tpu as pltpu

F32 = jnp.float32
BF16 = jnp.bfloat16
MESH = pl.DeviceIdType.MESH

LANES = 128
SUBLANES = 8
VMEM_LIMIT = 56 * 1024 * 1024

HEAD_DIM = 64
N_Q_HEADS = 16
N_KV_HEADS = 2
Q_W = N_Q_HEADS * HEAD_DIM
KV_W = N_KV_HEADS * HEAD_DIM
BLOCK = 128
ROT_DIM = HEAD_DIM // 4
ROPE_THETA = 500000.0
SSM_GC = 16
SSM_P = 64
GROUPS_PER_BLOCK = 8
NW = GROUPS_PER_BLOCK * SSM_P // LANES
EPS = 1e-6
N_DEV = 8

ADAM_LR = 0.001
ADAM_B1 = 0.9
ADAM_B2 = 0.999
ADAM_EPS = 1e-08
ADAM_WD = 0.01
ADAM_STEP = 10


def _params(sem=None):
    return pltpu.CompilerParams(dimension_semantics=sem, vmem_limit_bytes=VMEM_LIMIT)


def _host_params(sem, comm):
    if comm:
        return pltpu.CompilerParams(dimension_semantics=("arbitrary",) * len(sem), vmem_limit_bytes=VMEM_LIMIT,
                                    has_side_effects=True)
    return _params(sem)


def _pick(dim, prefs):
    for p in prefs:
        if dim % p == 0:
            return p
    return dim


def _sigmoid(x):
    return 1.0 / (1.0 + jnp.exp(-x))


_GELU_C = math.sqrt(2.0 / math.pi)


def _gelu(x):
    return 0.5 * x * (1.0 + jnp.tanh(_GELU_C * (x + 0.044715 * x * x * x)))


def _gelu_grad(x):
    t = jnp.tanh(_GELU_C * (x + 0.044715 * x * x * x))
    return 0.5 * (1.0 + t) + 0.5 * x * (1.0 - t * t) * _GELU_C * (1.0 + 3.0 * 0.044715 * x * x)


_DIMS = {"nn": (((1,), (0,)), ((), ())), "nt": (((1,), (1,)), ((), ())), "tn": (((0,), (0,)), ((), ()))}


def _mm(a, b, *, mode, name, out_dtypes=(F32,), epi=None, extras=(), comm=None, a_win=None, b_win=None):
    ar, ac = a.shape[0], (a_win[1] if a_win else a.shape[1])
    br, bc = b.shape[0], (b_win[1] if b_win else b.shape[1])
    if mode == "nn":
        (M, K), (K2, N) = (ar, ac), (br, bc)
    elif mode == "nt":
        (M, K), (N, K2) = (ar, ac), (br, bc)
    else:
        (K, M), (K2, N) = (ar, ac), (br, bc)
    assert K == K2, (a.shape, b.shape, mode)
    tm = _pick(M, (1024, 1280, 640, 512, 256, 128))
    tn = _pick(N, (1024, 1280, 640, 512, 384, 256, 128))
    tk = K if K <= 2048 else _pick(K, (2048, 1280, 1024, 640, 512, 256, 128))
    nk = K // tk
    a_col_tile = tm if mode == "tn" else tk
    b_col_tile = tk if mode == "nt" else tn
    ao = a_win[0] // a_col_tile if a_win else 0
    bo = b_win[0] // b_col_tile if b_win else 0
    assert (not a_win or a_win[0] % a_col_tile == 0) and (not b_win or b_win[0] % b_col_tile == 0)
    n_ex = len(extras)
    n_out = len(out_dtypes)
    gi, gj = M // tm, N // tn
    steps = gi * gj * nk

    def body(*refs):
        ins, c_ins, o_refs, c_outs, scratch, sems = _split_refs(refs, 2 + n_ex, n_out, comm)
        a_ref, b_ref = ins[0], ins[1]
        ex_refs = ins[2:]
        if comm:
            s = (pl.program_id(0) * gj + pl.program_id(1)) * nk + pl.program_id(2)
            comm.run(c_ins, c_outs, sems, s == 0, s == steps // 2, s == steps - 1)

        def finish(r):
            outs = (r,) if epi is None else epi(r, *[e[...] for e in ex_refs])
            for o_ref, o in zip(o_refs, outs):
                o_ref[...] = o.astype(o_ref.dtype)

        part = lax.dot_general(a_ref[...].astype(BF16), b_ref[...].astype(BF16), _DIMS[mode],
                               preferred_element_type=F32)
        if nk == 1:
            finish(part)
            return
        acc = scratch[0]
        k = pl.program_id(2)

        @pl.when(k == 0)
        def _():
            acc[...] = part

        @pl.when((k > 0) & (k < nk - 1))
        def _():
            acc[...] += part

        @pl.when(k == nk - 1)
        def _():
            finish(acc[...] + part)

    if mode == "nn":
        a_spec = pl.BlockSpec((tm, tk), lambda i, j, k: (i, k + ao))
        b_spec = pl.BlockSpec((tk, tn), lambda i, j, k: (k, j + bo))
    elif mode == "nt":
        a_spec = pl.BlockSpec((tm, tk), lambda i, j, k: (i, k + ao))
        b_spec = pl.BlockSpec((tn, tk), lambda i, j, k: (j, k + bo))
    else:
        a_spec = pl.BlockSpec((tk, tm), lambda i, j, k: (k, i + ao))
        b_spec = pl.BlockSpec((tk, tn), lambda i, j, k: (k, j + bo))
    o_spec = pl.BlockSpec((tm, tn), lambda i, j, k: (i, j))
    c_ins = comm.ins if comm else []
    c_shapes = comm.out_shapes if comm else []
    res = pl.pallas_call(
        body,
        grid=(gi, gj, nk),
        in_specs=[a_spec, b_spec] + [o_spec] * n_ex + [_ANY] * len(c_ins),
        out_specs=tuple([o_spec] * n_out + [_ANY] * len(c_shapes)),
        out_shape=tuple([jax.ShapeDtypeStruct((M, N), d) for d in out_dtypes] + c_shapes),
        scratch_shapes=([pltpu.VMEM((tm, tn), F32)] if nk > 1 else []) + (comm.scratch() if comm else []),
        compiler_params=_host_params(("parallel", "parallel", "arbitrary"), comm),
        name=name,
    )(a, b, *extras, *c_ins)
    if comm:
        return (res[0] if n_out == 1 else res[:n_out]), list(res[n_out:])
    return res[0] if n_out == 1 else res


def _ew(fn, ins, out_dtypes, *, rows, ncols, name):
    g = ncols
    for _, off in ins:
        g = math.gcd(g, off)
    tc = _pick(g, (512, 256, 128))
    tr = _pick(rows, (1024, 512, 256, 128))
    n_in = len(ins)

    def body(*refs):
        outs = fn(*[r[...] for r in refs[:n_in]])
        for o_ref, o in zip(refs[n_in:], outs):
            o_ref[...] = o.astype(o_ref.dtype)

    def in_spec(off):
        ob = off // tc
        return pl.BlockSpec((tr, tc), lambda i, j: (i, j + ob))

    o_spec = pl.BlockSpec((tr, tc), lambda i, j: (i, j))
    res = pl.pallas_call(
        body,
        grid=(rows // tr, ncols // tc),
        in_specs=[in_spec(off) for _, off in ins],
        out_specs=tuple([o_spec] * len(out_dtypes)),
        out_shape=tuple(jax.ShapeDtypeStruct((rows, ncols), d) for d in out_dtypes),
        compiler_params=_params(("parallel", "parallel")),
        name=name,
    )(*[arr for arr, _ in ins])
    return res[0] if len(out_dtypes) == 1 else res


def _rstd(x):
    return lax.rsqrt(jnp.mean(x * x, axis=-1, keepdims=True) + EPS)


def _norm_bwd(x, r, g, dy):
    t = dy * g
    dx = r * t - x * (r * r * r) * jnp.mean(t * x, axis=-1, keepdims=True)
    return dx, dy * x * r


def _row_call(body, ins, row_ins, outs, acc_outs, *, rows, width, name, comm=None):
    tr = _pick(rows, (256, 128))
    steps = rows // tr
    t_spec = pl.BlockSpec((tr, width), lambda i: (i, 0))
    r_spec = pl.BlockSpec((1, width), lambda i: (0, 0))
    n_in, n_out = len(ins) + len(row_ins), len(outs) + len(acc_outs)

    def hosted(*refs):
        h_ins, c_ins, h_outs, c_outs, _, sems = _split_refs(refs, n_in, n_out, comm)
        i = pl.program_id(0)
        comm.run(c_ins, c_outs, sems, i == 0, i == steps // 2, i == steps - 1)
        body(*h_ins, *h_outs)

    c_ins = comm.ins if comm else []
    c_shapes = comm.out_shapes if comm else []
    res = pl.pallas_call(
        hosted if comm else body,
        grid=(steps,),
        in_specs=[t_spec] * len(ins) + [r_spec] * len(row_ins) + [_ANY] * len(c_ins),
        out_specs=tuple([t_spec] * len(outs) + [pl.BlockSpec(s, lambda i: (0, 0)) for s in acc_outs]
                        + [_ANY] * len(c_shapes)),
        out_shape=tuple([jax.ShapeDtypeStruct((rows, width), d) for d in outs]
                        + [jax.ShapeDtypeStruct(s, F32) for s in acc_outs] + c_shapes),
        scratch_shapes=comm.scratch() if comm else [],
        compiler_params=_host_params(("arbitrary",), comm),
        name=name,
    )(*ins, *row_ins, *c_ins)
    return (res[:n_out], list(res[n_out:])) if comm else res


def _rms_pre(x, g, comm):
    L, D = x.shape

    def body(x_ref, g_ref, h_ref):
        xv = x_ref[...]
        h_ref[...] = (xv * _rstd(xv) * g_ref[...]).astype(BF16)

    (h,), c_outs = _row_call(body, [x], [g], [BF16], [], rows=L, width=D, name="rms_pre", comm=comm)
    return h, c_outs


def _post_pre(x, mixed, g_post, g_pre):
    L, D = x.shape

    def body(x_ref, m_ref, gp_ref, gq_ref, x1_ref, h2_ref):
        mv = m_ref[...]
        x1 = x_ref[...] + mv * _rstd(mv) * gp_ref[...]
        x1_ref[...] = x1
        h2_ref[...] = (x1 * _rstd(x1) * gq_ref[...]).astype(BF16)

    return _row_call(body, [x, mixed], [g_post, g_pre], [F32, BF16], [], rows=L, width=D, name="post_pre")


def _loss_bwd(x1, dn, g_post, target):
    L, D = x1.shape

    def body(x1_ref, dn_ref, t_ref, g_ref, dx2_ref, ddn_ref, dg_ref, loss_ref):
        @pl.when(pl.program_id(0) == 0)
        def _():
            dg_ref[...] = jnp.zeros_like(dg_ref)
            loss_ref[...] = jnp.zeros_like(loss_ref)

        dnv = dn_ref[...]
        g = g_ref[...]
        r = _rstd(dnv)
        err = x1_ref[...] + dnv * r * g - t_ref[...]
        loss_ref[...] += 0.5 * jnp.sum(jnp.mean(err * err, axis=-1, keepdims=True), axis=0, keepdims=True)
        dx2 = err * (1.0 / D)
        dx2_ref[...] = dx2
        ddn, dgr = _norm_bwd(dnv, r, g, dx2)
        ddn_ref[...] = ddn.astype(BF16)
        dg_ref[...] += jnp.sum(dgr, axis=0, keepdims=True)

    return _row_call(body, [x1, dn, target], [g_post], [F32, BF16], [(1, D), (1, 1)],
                     rows=L, width=D, name="loss_bwd")


def _norm_bwd_pair(x1, dh2, dx2, mixed, g_pre, g_post):
    L, D = x1.shape

    def body(x1_ref, dh_ref, dx2_ref, m_ref, gq_ref, gp_ref, dx1_ref, dm_ref, dgq_ref, dgp_ref):
        @pl.when(pl.program_id(0) == 0)
        def _():
            dgq_ref[...] = jnp.zeros_like(dgq_ref)
            dgp_ref[...] = jnp.zeros_like(dgp_ref)

        x1v = x1_ref[...]
        d1, dgq = _norm_bwd(x1v, _rstd(x1v), gq_ref[...], dh_ref[...])
        dx1 = dx2_ref[...] + d1
        dx1_ref[...] = dx1
        mv = m_ref[...]
        dm, dgp = _norm_bwd(mv, _rstd(mv), gp_ref[...], dx1)
        dm_ref[...] = dm.astype(BF16)
        dgq_ref[...] += jnp.sum(dgq, axis=0, keepdims=True)
        dgp_ref[...] += jnp.sum(dgp, axis=0, keepdims=True)

    return _row_call(body, [x1, dh2, dx2, mixed], [g_pre, g_post], [F32, BF16], [(1, D), (1, D)],
                     rows=L, width=D, name="norm_bwd_pair")


def _final_bwd(x, dh, dx1, g_pre, comm):
    L, D = x.shape

    def body(x_ref, dh_ref, dx1_ref, g_ref, gx_ref, dg_ref):
        @pl.when(pl.program_id(0) == 0)
        def _():
            dg_ref[...] = jnp.zeros_like(dg_ref)

        xv = x_ref[...]
        d0, dg = _norm_bwd(xv, _rstd(xv), g_ref[...], dh_ref[...])
        gx_ref[...] = dx1_ref[...] + d0
        dg_ref[...] += jnp.sum(dg, axis=0, keepdims=True)

    return _row_call(body, [x, dh, dx1], [g_pre], [F32], [(1, D)], rows=L, width=D, name="final_bwd", comm=comm)


def _rope_tables(L):
    half = ROT_DIM // 2
    inv = ROPE_THETA ** (-jnp.arange(half, dtype=F32) * 2.0 / ROT_DIM)
    ang = jnp.arange(L, dtype=F32)[:, None] * inv[None, :]
    d = jnp.arange(LANES) % HEAD_DIM
    a = ang[:, d % half]
    cos_t = jnp.where(d[None, :] < ROT_DIM, jnp.cos(a), 1.0)
    sin_t = jnp.where(d[None, :] < half, -jnp.sin(a), jnp.where(d[None, :] < ROT_DIM, jnp.sin(a), 0.0))
    return cos_t.astype(F32), sin_t.astype(F32)


def _lane_lo(shape):
    return lax.broadcasted_iota(jnp.int32, shape, 1) < HEAD_DIM


def _rope(x, cos_t, sin_t):
    d = lax.broadcasted_iota(jnp.int32, x.shape, 1) % HEAD_DIM
    partner = jnp.where(d < ROT_DIM // 2, pltpu.roll(x, LANES - ROT_DIM // 2, 1), pltpu.roll(x, ROT_DIM // 2, 1))
    return x * cos_t + partner * sin_t


def _dup(kv, g):
    sw = pltpu.roll(kv, HEAD_DIM, 1)
    lo = _lane_lo(kv.shape)
    return jnp.where(lo, kv, sw) if g == 0 else jnp.where(lo, sw, kv)


def _attn_mask(n):
    qi = lax.broadcasted_iota(jnp.int32, (BLOCK, 2 * BLOCK), 0)
    kj = lax.broadcasted_iota(jnp.int32, (BLOCK, 2 * BLOCK), 1)
    rel = qi + BLOCK - kj
    return (rel >= 0) & (rel < BLOCK) & ((kj >= BLOCK) | (n > 0))


def _softmax_sink(s, mask, sink):
    s = jnp.where(mask, s, -1e30)
    m = jnp.maximum(jnp.max(s, axis=-1, keepdims=True), sink)
    e = jnp.where(mask, jnp.exp(s - m), 0.0)
    es = jnp.exp(sink - m)
    inv = 1.0 / (jnp.sum(e, axis=-1, keepdims=True) + es)
    return e * inv, es * inv


_NT = (((1,), (1,)), ((), ()))
_TN = (((0,), (0,)), ((), ()))


def _dot(a, b):
    return jnp.dot(a.astype(BF16), b.astype(BF16), preferred_element_type=F32)


def _dot_nt(a, b):
    return lax.dot_general(a.astype(BF16), b.astype(BF16), _NT, preferred_element_type=F32)


def _dot_tn(a, b):
    return lax.dot_general(a.astype(BF16), b.astype(BF16), _TN, preferred_element_type=F32)


def _attn_specs(nb):
    kcol, vcol = Q_W // LANES, Q_W // LANES + 1
    prev = lambda n: jnp.maximum(n - 1, 0)
    return [
        pl.BlockSpec((BLOCK, Q_W), lambda n: (n, 0)),
        pl.BlockSpec((BLOCK, LANES), lambda n: (n, kcol)),
        pl.BlockSpec((BLOCK, LANES), lambda n: (prev(n), kcol)),
        pl.BlockSpec((BLOCK, LANES), lambda n: (n, vcol)),
        pl.BlockSpec((BLOCK, LANES), lambda n: (prev(n), vcol)),
        pl.BlockSpec((BLOCK, LANES), lambda n: (n, 0)),
        pl.BlockSpec((BLOCK, LANES), lambda n: (prev(n), 0)),
        pl.BlockSpec((BLOCK, LANES), lambda n: (n, 0)),
        pl.BlockSpec((BLOCK, LANES), lambda n: (prev(n), 0)),
        pl.BlockSpec(memory_space=pltpu.SMEM),
    ]


def _attn_prep(refs):
    q_ref, kc_ref, kp_ref, vc_ref, vp_ref, cc_ref, cp_ref, sc_ref, sp_ref = refs
    cos_c, sin_c, cos_p, sin_p = cc_ref[...], sc_ref[...], cp_ref[...], sp_ref[...]
    k2 = jnp.concatenate([_rope(kp_ref[...], cos_p, sin_p), _rope(kc_ref[...], cos_c, sin_c)], axis=0)
    v2 = jnp.concatenate([vp_ref[...], vc_ref[...]], axis=0)
    kd = [_dup(k2, g).astype(BF16) for g in range(N_KV_HEADS)]
    vd = [_dup(v2, g).astype(BF16) for g in range(N_KV_HEADS)]
    return cos_c, sin_c, cos_p, sin_p, kd, vd


def _stack_heads(x):
    lo = _lane_lo(x.shape)
    return jnp.concatenate([jnp.where(lo, x, 0.0), jnp.where(lo, 0.0, x)], axis=0)


def _unstack_heads(x2):
    return jnp.where(_lane_lo((BLOCK, LANES)), x2[:BLOCK], x2[BLOCK:])


def _pair_sinks(sink_ref, t):
    first = lax.broadcasted_iota(jnp.int32, (2 * BLOCK, 1), 0) < BLOCK
    return jnp.where(first, sink_ref[0, 2 * t], sink_ref[0, 2 * t + 1])


_SCALE = 1.0 / math.sqrt(HEAD_DIM)
_TILES = Q_W // LANES
_TILES_PER_KV = _TILES // N_KV_HEADS


def _attn_fwd(z, cos_t, sin_t, sinks, comm=None):
    L = z.shape[0]
    nb = L // BLOCK

    def body(*refs):
        ins, c_ins, outs, c_outs, _, sems = _split_refs(refs, 10, 1, comm)
        q_ref, kc_ref, kp_ref, vc_ref, vp_ref, cc_ref, cp_ref, sc_ref, sp_ref, sink_ref = ins
        o_ref = outs[0]
        n = pl.program_id(0)
        if comm:
            comm.run(c_ins, c_outs, sems, n == 0, n == nb // 2, n == nb - 1)
        cos_c, sin_c, _, _, kd, vd = _attn_prep((q_ref, kc_ref, kp_ref, vc_ref, vp_ref, cc_ref, cp_ref, sc_ref, sp_ref))
        mask = _attn_mask(n)
        mask2 = jnp.concatenate([mask, mask], axis=0)
        for t in range(_TILES):
            g = t // _TILES_PER_KV
            q2 = _stack_heads(_rope(q_ref[:, t * LANES:(t + 1) * LANES], cos_c, sin_c) * _SCALE)
            p2, _ = _softmax_sink(_dot_nt(q2, kd[g]), mask2, _pair_sinks(sink_ref, t))
            o_ref[:, t * LANES:(t + 1) * LANES] = _unstack_heads(_dot(p2, vd[g])).astype(BF16)

    c_ins = comm.ins if comm else []
    c_shapes = comm.out_shapes if comm else []
    res = pl.pallas_call(
        body,
        grid=(nb,),
        in_specs=_attn_specs(nb) + [_ANY] * len(c_ins),
        out_specs=tuple([pl.BlockSpec((BLOCK, Q_W), lambda n: (n, 0))] + [_ANY] * len(c_shapes)),
        out_shape=tuple([jax.ShapeDtypeStruct((L, Q_W), BF16)] + c_shapes),
        scratch_shapes=comm.scratch() if comm else [],
        compiler_params=_host_params(("parallel",), comm),
        name="attn_fwd",
    )(z, z, z, z, z, cos_t, cos_t, sin_t, sin_t, sinks, *c_ins)
    return (res[0], list(res[1:])) if comm else res[0]


def _attn_bwd(z, d_o, cos_t, sin_t, sinks, dz, comm=None):
    L = z.shape[0]
    nb = L // BLOCK

    def body(*refs):
        ins, c_ins, outs, c_outs, _, sems = _split_refs(refs, 12, 6, comm)
        q_ref, kc_ref, kp_ref, vc_ref, vp_ref, cc_ref, cp_ref, sc_ref, sp_ref, sink_ref, do_ref, _ = ins
        dq_ref, dkc_ref, dkp_ref, dvc_ref, dvp_ref, ds_ref = outs
        n = pl.program_id(0)
        if comm:
            comm.run(c_ins, c_outs, sems, n == 0, n == nb // 2, n == nb - 1)

        @pl.when(n == 0)
        def _():
            ds_ref[...] = jnp.zeros_like(ds_ref)

        cos_c, sin_c, cos_p, sin_p, kd, vd = _attn_prep(
            (q_ref, kc_ref, kp_ref, vc_ref, vp_ref, cc_ref, cp_ref, sc_ref, sp_ref))
        mask = _attn_mask(n)
        mask2 = jnp.concatenate([mask, mask], axis=0)
        lo2 = _lane_lo((2 * BLOCK, LANES))
        acc_k = [jnp.zeros((2 * BLOCK, LANES), F32) for _ in range(N_KV_HEADS)]
        acc_v = [jnp.zeros((2 * BLOCK, LANES), F32) for _ in range(N_KV_HEADS)]
        for t in range(_TILES):
            g = t // _TILES_PER_KV
            sl = slice(t * LANES, (t + 1) * LANES)
            q2 = _stack_heads(_rope(q_ref[:, sl], cos_c, sin_c) * _SCALE)
            p2, ps2 = _softmax_sink(_dot_nt(q2, kd[g]), mask2, _pair_sinks(sink_ref, t))
            do2 = _stack_heads(do_ref[:, sl])
            d2 = jnp.sum(do2 * _dot(p2, vd[g]), axis=-1, keepdims=True)
            ds2 = p2 * (_dot_nt(do2, vd[g]) - d2)
            dqt = _unstack_heads(_dot(ds2, kd[g])) * _SCALE
            dq_ref[:, sl] = _rope(dqt, cos_c, -sin_c).astype(BF16)
            acc_k[g] = acc_k[g] + _dot_tn(ds2, q2)
            acc_v[g] = acc_v[g] + _dot_tn(p2, do2)
            sd = ps2 * d2
            row = jnp.where(_lane_lo((1, LANES)), -jnp.sum(sd[:BLOCK], axis=0, keepdims=True),
                            -jnp.sum(sd[BLOCK:], axis=0, keepdims=True))
            ds_ref[t:t + 1, :] += row
        fk = [a + pltpu.roll(a, HEAD_DIM, 1) for a in acc_k]
        fv = [a + pltpu.roll(a, HEAD_DIM, 1) for a in acc_v]
        dk2 = jnp.where(lo2, fk[0], fk[1])
        dv2 = jnp.where(lo2, fv[0], fv[1])
        dkp_ref[...] = _rope(dk2[:BLOCK], cos_p, -sin_p)
        dkc_ref[...] = _rope(dk2[BLOCK:], cos_c, -sin_c)
        dvp_ref[...] = dv2[:BLOCK]
        dvc_ref[...] = dv2[BLOCK:]

    blk = pl.BlockSpec((BLOCK, LANES), lambda n: (n, 0))
    kv = jax.ShapeDtypeStruct((L, LANES), F32)
    c_ins = comm.ins if comm else []
    c_shapes = comm.out_shapes if comm else []
    res = pl.pallas_call(
        body,
        grid=(nb,),
        in_specs=_attn_specs(nb) + [pl.BlockSpec((BLOCK, Q_W), lambda n: (n, 0)), _ANY] + [_ANY] * len(c_ins),
        out_specs=tuple([pl.BlockSpec((BLOCK, Q_W), lambda n: (n, 0)), blk, blk, blk, blk,
                         pl.BlockSpec((_TILES, LANES), lambda n: (0, 0))] + [_ANY] * len(c_shapes)),
        out_shape=tuple([jax.ShapeDtypeStruct(dz.shape, dz.dtype), kv, kv, kv, kv,
                         jax.ShapeDtypeStruct((_TILES, LANES), F32)] + c_shapes),
        scratch_shapes=comm.scratch() if comm else [],
        input_output_aliases={11: 0},
        compiler_params=_host_params(("arbitrary",), comm),
        name="attn_bwd",
    )(z, z, z, z, z, cos_t, cos_t, sin_t, sin_t, sinks, d_o, dz, *c_ins)
    return (res[:6], list(res[6:])) if comm else res


def _kv_combine(dkc, dkp, dvc, dvp, dz):
    L = dkc.shape[0]
    nb = L // BLOCK

    def body(kc_ref, kp_ref, vc_ref, vp_ref, dz_in, o_ref):
        live = jnp.where(pl.program_id(0) + 1 < nb, 1.0, 0.0)
        o_ref[:, :LANES] = (kc_ref[...] + live * kp_ref[...]).astype(BF16)
        o_ref[:, LANES:] = (vc_ref[...] + live * vp_ref[...]).astype(BF16)

    cur = pl.BlockSpec((BLOCK, LANES), lambda n: (n, 0))
    nxt = pl.BlockSpec((BLOCK, LANES), lambda n: (jnp.minimum(n + 1, nb - 1), 0))
    kv_block = Q_W // (2 * KV_W)
    return pl.pallas_call(body, grid=(nb,), in_specs=[cur, nxt, cur, nxt, _ANY],
                          out_specs=pl.BlockSpec((BLOCK, 2 * KV_W), lambda n: (n, kv_block)),
                          out_shape=jax.ShapeDtypeStruct(dz.shape, dz.dtype), input_output_aliases={4: 0},
                          compiler_params=_params(("parallel",)), name="kv_combine")(dkc, dkp, dvc, dvp, dz)


def _gate_bwd(d_mix, z, o_ga, y_attn, y_ssm):
    L, D = d_mix.shape
    tc = _pick(math.gcd(D, o_ga), (512, 256, 128))
    tr = _pick(L, (1024, 512, 256, 128))
    nd, gb = D // tc, o_ga // tc

    def body(dm_ref, g_ref, ya_ref, ys_ref, dy_ref, dz_ref):
        dm = dm_ref[...]
        s = _sigmoid(g_ref[...])
        y = jnp.where(pl.program_id(1) < nd, ya_ref[...], ys_ref[...])
        dy_ref[...] = (dm * s).astype(BF16)
        dz_ref[...] = (dm * y * s * (1.0 - s)).astype(BF16)

    blk = lambda f: pl.BlockSpec((tr, tc), f)
    return pl.pallas_call(
        body,
        grid=(L // tr, 2 * nd),
        in_specs=[blk(lambda i, j: (i, j % nd)), blk(lambda i, j: (i, j + gb)),
                  blk(lambda i, j: (i, jnp.minimum(j, nd - 1))), blk(lambda i, j: (i, jnp.maximum(j - nd, 0)))],
        out_specs=(blk(lambda i, j: (i, j)), blk(lambda i, j: (i, j + gb))),
        out_shape=(jax.ShapeDtypeStruct((L, 2 * D), BF16), jax.ShapeDtypeStruct(z.shape, BF16)),
        compiler_params=_params(("parallel", "arbitrary")),
        name="gate_bwd",
    )(d_mix, z, y_attn, y_ssm)


def _discretise(lr, li, ldt, br, bi):
    dt = jnp.exp(ldt)
    mag = jnp.exp(lr * dt)
    a_re, a_im = mag * jnp.cos(li * dt), mag * jnp.sin(li * dt)
    den = lr * lr + li * li
    nr, ni = a_re - 1.0, a_im
    coef_re = (nr * lr + ni * li) / den
    coef_im = (ni * lr - nr * li) / den
    return a_re, a_im, coef_re * br - coef_im * bi, coef_re * bi + coef_im * br


def _disc_specs(n):
    tr = _pick(n, (512,))
    cs = pl.BlockSpec((tr, 1), lambda i: (i, 0))
    ms = pl.BlockSpec((tr, SSM_GC), lambda i: (i, 0))
    return tr, cs, ms


def _disc_fwd(lr, li, ldt, br, bi):
    n = lr.shape[0]
    tr, cs, ms = _disc_specs(n)

    def body(lr_ref, li_ref, dt_ref, br_ref, bi_ref, o1, o2, o3, o4):
        r = _discretise(lr_ref[...], li_ref[...], dt_ref[...], br_ref[...], bi_ref[...])
        o1[...], o2[...], o3[...], o4[...] = r

    col = jax.ShapeDtypeStruct((n, 1), F32)
    mat = jax.ShapeDtypeStruct((n, SSM_GC), F32)
    return pl.pallas_call(body, grid=(n // tr,), in_specs=[cs, cs, cs, ms, ms], out_specs=(cs, cs, ms, ms),
                          out_shape=(col, col, mat, mat), compiler_params=_params(("parallel",)), name="disc_fwd")(
        lr, li, ldt, br, bi)


def _disc_bwd(lr, li, ldt, br, bi, gar, gai, gbr, gbi):
    n = lr.shape[0]
    tr, cs, ms = _disc_specs(n)

    def body(lr_ref, li_ref, dt_ref, br_ref, bi_ref, gar_ref, gai_ref, gbr_ref, gbi_ref, o_lr, o_li, o_dt, o_br, o_bi):
        _, vjp = jax.vjp(_discretise, lr_ref[...], li_ref[...], dt_ref[...], br_ref[...], bi_ref[...])
        g = vjp((gar_ref[...], gai_ref[...], gbr_ref[...], gbi_ref[...]))
        o_lr[...] = g[0]
        o_li[...] = g[1]
        o_dt[...] = jnp.sum(g[2].reshape(tr // SSM_P, SSM_P, 1), axis=1)
        o_br[...] = g[3]
        o_bi[...] = g[4]

    col = jax.ShapeDtypeStruct((n, 1), F32)
    mat = jax.ShapeDtypeStruct((n, SSM_GC), F32)
    return pl.pallas_call(
        body, grid=(n // tr,), in_specs=[cs, cs, cs, ms, ms, cs, cs, ms, ms],
        out_specs=(cs, cs, pl.BlockSpec((tr // SSM_P, 1), lambda i: (i, 0)), ms, ms),
        out_shape=(col, col, jax.ShapeDtypeStruct((n // SSM_P, 1), F32), mat, mat),
        compiler_params=_params(("parallel",)), name="disc_bwd")(lr, li, ldt, br, bi, gar, gai, gbr, gbi)


def _cpow(ar, ai, nsq):
    for _ in range(nsq):
        ar, ai = ar * ar - ai * ai, 2.0 * ar * ai
    return ar, ai


def _ssm_dims(L):
    tc = min(1024, L)
    seg = tc // SUBLANES
    assert seg & (seg - 1) == 0
    return tc, seg, L // tc, seg.bit_length() - 1


def _tile_rows(i):
    return pl.ds(pl.multiple_of(i * SUBLANES, SUBLANES), SUBLANES)


def _rows_to_segments(src_ref, dst_ref, seg):
    def body(i, _):
        dst_ref[_tile_rows(i), :] = src_ref[pl.ds(i, SUBLANES, stride=seg), :]
        return 0
    lax.fori_loop(0, seg, body, 0, unroll=8)


def _segments_to_rows(src_ref, dst_ref, seg):
    def body(i, _):
        dst_ref[pl.ds(i, SUBLANES, stride=seg), :] = src_ref[_tile_rows(i), :]
        return 0
    lax.fori_loop(0, seg, body, 0, unroll=8)


def _ssm_fwd(z, u_off, br_m, bi_m, cr_m, ci_m, a_re3, a_im3, d_row, comm=None):
    L = z.shape[0]
    ngb = br_m.shape[0]
    tc, seg, nc, nsq = _ssm_dims(L)
    ucol = u_off // LANES

    def body(*refs):
        ins, c_ins, outs, c_outs, scratch, sems = _split_refs(refs, 8, 3, comm)
        u_ref, br_ref, bi_ref, cr_ref, ci_ref, ar_ref, ai_ref, d_ref = ins
        y_ref, xr_ref, xi_ref = outs
        bur, bui, car_r, car_i, ini_r, ini_i, up, ys = scratch
        if comm:
            s = pl.program_id(0) * nc + pl.program_id(1)
            comm.run(c_ins, c_outs, sems, s == 0, s == (ngb * nc) // 2, s == ngb * nc - 1)

        @pl.when(pl.program_id(1) == 0)
        def _():
            car_r[...] = jnp.zeros_like(car_r)
            car_i[...] = jnp.zeros_like(car_i)

        _rows_to_segments(u_ref, up, seg)
        u = up[...]
        pr = _dot(u, br_ref[0])
        pi = _dot(u, bi_ref[0])
        for w in range(NW):
            bur[w] = pr[:, w * LANES:(w + 1) * LANES]
            bui[w] = pi[:, w * LANES:(w + 1) * LANES]
        ar = [jnp.broadcast_to(ar_ref[w], (SUBLANES, LANES)) for w in range(NW)]
        ai = [jnp.broadcast_to(ai_ref[w], (SUBLANES, LANES)) for w in range(NW)]

        def step(i, carry, store):
            xr, xi = carry
            rows = _tile_rows(i)
            nr, ni = [], []
            for w in range(NW):
                r = ar[w] * xr[w] - ai[w] * xi[w] + bur[w, rows, :]
                m = ar[w] * xi[w] + ai[w] * xr[w] + bui[w, rows, :]
                if store:
                    xr_ref[w, rows, :] = r
                    xi_ref[w, rows, :] = m
                nr.append(r)
                ni.append(m)
            return tuple(nr), tuple(ni)

        zero = tuple(jnp.zeros((SUBLANES, LANES), F32) for _ in range(NW))
        er, ei = lax.fori_loop(0, seg, functools.partial(step, store=False), (zero, zero), unroll=2)
        for w in range(NW):
            pr_, pi_ = _cpow(ar[w][0:1], ai[w][0:1], nsq)
            sr, si = car_r[w, 0:1, :], car_i[w, 0:1, :]
            for j in range(SUBLANES):
                ini_r[w, j:j + 1, :] = sr
                ini_i[w, j:j + 1, :] = si
                sr, si = (pr_ * sr - pi_ * si + er[w][j:j + 1], pr_ * si + pi_ * sr + ei[w][j:j + 1])
            car_r[w, 0:1, :] = sr
            car_i[w, 0:1, :] = si
        init = (tuple(ini_r[w] for w in range(NW)), tuple(ini_i[w] for w in range(NW)))
        lax.fori_loop(0, seg, functools.partial(step, store=True), init, unroll=2)
        acc = d_ref[...] * u
        for w in range(NW):
            sl = slice(w * LANES, (w + 1) * LANES)
            acc = acc + _dot(xr_ref[w], cr_ref[0, sl, :]) - _dot(xi_ref[w], ci_ref[0, sl, :])
        ys[...] = acc
        _segments_to_rows(ys, y_ref, seg)

    mat_b = pl.BlockSpec((1, LANES, NW * LANES), lambda b, k: (b, 0, 0))
    mat_c = pl.BlockSpec((1, NW * LANES, LANES), lambda b, k: (b, 0, 0))
    a_spec = pl.BlockSpec((NW, 1, LANES), lambda b, k: (b, 0, 0))
    x_spec = pl.BlockSpec((NW, tc, LANES), lambda b, k: (b, k, 0))
    xs = jax.ShapeDtypeStruct((ngb * NW, L, LANES), F32)
    st = pltpu.VMEM((NW, SUBLANES, LANES), F32)
    c_ins = comm.ins if comm else []
    c_shapes = comm.out_shapes if comm else []
    res = pl.pallas_call(
        body,
        grid=(ngb, nc),
        in_specs=[pl.BlockSpec((tc, LANES), lambda b, k: (k, b + ucol)), mat_b, mat_b, mat_c, mat_c, a_spec, a_spec,
                  pl.BlockSpec((1, LANES), lambda b, k: (0, b))] + [_ANY] * len(c_ins),
        out_specs=tuple([pl.BlockSpec((tc, LANES), lambda b, k: (k, b)), x_spec, x_spec] + [_ANY] * len(c_shapes)),
        out_shape=tuple([jax.ShapeDtypeStruct((L, ngb * LANES), F32), xs, xs] + c_shapes),
        scratch_shapes=[pltpu.VMEM((NW, tc, LANES), F32), pltpu.VMEM((NW, tc, LANES), F32), st, st, st, st,
                        pltpu.VMEM((tc, LANES), F32), pltpu.VMEM((tc, LANES), F32)]
        + (comm.scratch() if comm else []),
        compiler_params=_host_params(("arbitrary", "arbitrary"), comm),
        name="ssm_fwd",
    )(z, br_m, bi_m, cr_m, ci_m, a_re3, a_im3, d_row, *c_ins)
    return (res[:3], list(res[3:])) if comm else res


def _ssm_bwd(dy, z, u_off, xs_r, xs_i, br_m, bi_m, cr_m, ci_m, a_re3, a_im3, d_row, dz):
    L = z.shape[0]
    ngb = br_m.shape[0]
    tc, seg, nc, nsq = _ssm_dims(L)
    ucol = u_off // LANES

    def body(dy_ref, u_ref, xr_ref, xi_ref, br_ref, bi_ref, cr_ref, ci_ref, ar_ref, ai_ref, d_ref, dz_in,
             du_ref, gd_ref, gar_ref, gai_ref, gbr_ref, gbi_ref, gcr_ref, gci_ref,
             gr_s, gi_s, car_r, car_i, ini_r, ini_i, acc_r, acc_i, dyp, up, dus, dun):
        k = pl.program_id(1)

        @pl.when(k == 0)
        def _():
            for ref in (car_r, car_i, acc_r, acc_i, gd_ref, gbr_ref, gbi_ref, gcr_ref, gci_ref):
                ref[...] = jnp.zeros_like(ref)

        _rows_to_segments(dy_ref, dyp, seg)
        _rows_to_segments(u_ref, up, seg)
        dy_v = dyp[...]
        u = up[...]
        g_re = _dot_nt(dy_v, cr_ref[0])
        g_im = -_dot_nt(dy_v, ci_ref[0])
        for w in range(NW):
            gr_s[w] = g_re[:, w * LANES:(w + 1) * LANES]
            gi_s[w] = g_im[:, w * LANES:(w + 1) * LANES]
        ar = [jnp.broadcast_to(ar_ref[w], (SUBLANES, LANES)) for w in range(NW)]
        ai = [jnp.broadcast_to(ai_ref[w], (SUBLANES, LANES)) for w in range(NW)]

        def step1(ii, carry):
            xr, xi = carry
            rows = _tile_rows(seg - 1 - ii)
            nr = tuple(ar[w] * xr[w] + ai[w] * xi[w] + gr_s[w, rows, :] for w in range(NW))
            ni = tuple(ar[w] * xi[w] - ai[w] * xr[w] + gi_s[w, rows, :] for w in range(NW))
            return nr, ni

        zero = tuple(jnp.zeros((SUBLANES, LANES), F32) for _ in range(NW))
        er, ei = lax.fori_loop(0, seg, step1, (zero, zero), unroll=2)
        for w in range(NW):
            pr_, pi_ = _cpow(ar[w][0:1], ai[w][0:1], nsq)
            sr, si = car_r[w, 0:1, :], car_i[w, 0:1, :]
            for j in reversed(range(SUBLANES)):
                ini_r[w, j:j + 1, :] = sr
                ini_i[w, j:j + 1, :] = si
                sr, si = (pr_ * sr + pi_ * si + er[w][j:j + 1], pr_ * si - pi_ * sr + ei[w][j:j + 1])
            car_r[w, 0:1, :] = sr
            car_i[w, 0:1, :] = si

        def step2(ii, carry):
            gxr, gxi, acr, aci = carry
            rows = _tile_rows(seg - 1 - ii)
            nr, ni, nar, nai = [], [], [], []
            for w in range(NW):
                xr_t, xi_t = xr_ref[w, rows, :], xi_ref[w, rows, :]
                nar.append(acr[w] + gxr[w] * xr_t + gxi[w] * xi_t)
                nai.append(aci[w] + gxi[w] * xr_t - gxr[w] * xi_t)
                r = ar[w] * gxr[w] + ai[w] * gxi[w] + gr_s[w, rows, :]
                m = ar[w] * gxi[w] - ai[w] * gxr[w] + gi_s[w, rows, :]
                gr_s[w, rows, :] = r
                gi_s[w, rows, :] = m
                nr.append(r)
                ni.append(m)
            return tuple(nr), tuple(ni), tuple(nar), tuple(nai)

        init = (tuple(ini_r[w] for w in range(NW)), tuple(ini_i[w] for w in range(NW)),
                tuple(acc_r[w] for w in range(NW)), tuple(acc_i[w] for w in range(NW)))
        _, _, acr, aci = lax.fori_loop(0, seg, step2, init, unroll=2)
        du = d_ref[...] * dy_v
        for w in range(NW):
            sl = slice(w * LANES, (w + 1) * LANES)
            acc_r[w] = acr[w]
            acc_i[w] = aci[w]
            gxr_w, gxi_w = gr_s[w], gi_s[w]
            du = du + _dot_nt(gxr_w, br_ref[0, :, sl]) + _dot_nt(gxi_w, bi_ref[0, :, sl])
            gbr_ref[0, :, sl] += _dot_tn(u, gxr_w)
            gbi_ref[0, :, sl] += _dot_tn(u, gxi_w)
            gcr_ref[0, sl, :] += _dot_tn(xr_ref[w], dy_v)
            gci_ref[0, sl, :] += _dot_tn(-xi_ref[w], dy_v)
        dus[...] = du
        _segments_to_rows(dus, dun, seg)
        du_ref[...] = dun[...].astype(BF16)
        gd_ref[...] += jnp.sum(dy_v * u, axis=0, keepdims=True)

        @pl.when(k == nc - 1)
        def _():
            for w in range(NW):
                gar_ref[w] = jnp.sum(acc_r[w], axis=0, keepdims=True)
                gai_ref[w] = jnp.sum(acc_i[w], axis=0, keepdims=True)

    rk = lambda k: nc - 1 - k
    mat_b = pl.BlockSpec((1, LANES, NW * LANES), lambda b, k: (b, 0, 0))
    mat_c = pl.BlockSpec((1, NW * LANES, LANES), lambda b, k: (b, 0, 0))
    a_spec = pl.BlockSpec((NW, 1, LANES), lambda b, k: (b, 0, 0))
    x_spec = pl.BlockSpec((NW, tc, LANES), lambda b, k: (b, rk(k), 0))
    st = pltpu.VMEM((NW, SUBLANES, LANES), F32)
    big = pltpu.VMEM((NW, tc, LANES), F32)
    return pl.pallas_call(
        body,
        grid=(ngb, nc),
        in_specs=[pl.BlockSpec((tc, LANES), lambda b, k: (rk(k), b)),
                  pl.BlockSpec((tc, LANES), lambda b, k: (rk(k), b + ucol)),
                  x_spec, x_spec, mat_b, mat_b, mat_c, mat_c, a_spec, a_spec,
                  pl.BlockSpec((1, LANES), lambda b, k: (0, b)), _ANY],
        out_specs=(pl.BlockSpec((tc, LANES), lambda b, k: (rk(k), b + ucol)),
                   pl.BlockSpec((1, LANES), lambda b, k: (0, b)), a_spec, a_spec, mat_b, mat_b, mat_c, mat_c),
        out_shape=(jax.ShapeDtypeStruct(dz.shape, dz.dtype),
                   jax.ShapeDtypeStruct((1, ngb * LANES), F32),
                   jax.ShapeDtypeStruct((ngb * NW, 1, LANES), F32), jax.ShapeDtypeStruct((ngb * NW, 1, LANES), F32),
                   jax.ShapeDtypeStruct(br_m.shape, F32), jax.ShapeDtypeStruct(br_m.shape, F32),
                   jax.ShapeDtypeStruct(cr_m.shape, F32), jax.ShapeDtypeStruct(cr_m.shape, F32)),
        scratch_shapes=[big, big, st, st, st, st, st, st] + [pltpu.VMEM((tc, LANES), F32)] * 4,
        input_output_aliases={11: 0},
        compiler_params=_params(("arbitrary", "arbitrary")),
        name="ssm_bwd",
    )(dy, z, xs_r, xs_i, br_m, bi_m, cr_m, ci_m, a_re3, a_im3, d_row, dz)


def _block_diag_in(bb, ngb):
    t = bb.reshape(ngb, GROUPS_PER_BLOCK, SSM_P, SSM_GC).transpose(0, 1, 3, 2)
    eye = jnp.eye(GROUPS_PER_BLOCK, dtype=F32)
    m = t[:, :, :, None, :] * eye[None, :, None, :, None]
    return m.reshape(ngb, GROUPS_PER_BLOCK * SSM_GC, GROUPS_PER_BLOCK * SSM_P)


def _block_diag_out(c, ngb):
    t = c.reshape(ngb, GROUPS_PER_BLOCK, SSM_GC, SSM_P).transpose(0, 1, 3, 2)
    eye = jnp.eye(GROUPS_PER_BLOCK, dtype=F32)
    m = t[:, :, :, None, :] * eye[None, :, None, :, None]
    return m.reshape(ngb, GROUPS_PER_BLOCK * SSM_P, GROUPS_PER_BLOCK * SSM_GC)


def _diag_in(m, ngb):
    m5 = m.reshape(ngb, GROUPS_PER_BLOCK, SSM_GC, GROUPS_PER_BLOCK, SSM_P)
    d = jnp.diagonal(m5, axis1=1, axis2=3)
    return d.transpose(0, 3, 2, 1).reshape(ngb * GROUPS_PER_BLOCK * SSM_P, SSM_GC)


def _diag_out(m, ngb):
    m5 = m.reshape(ngb, GROUPS_PER_BLOCK, SSM_P, GROUPS_PER_BLOCK, SSM_GC)
    d = jnp.diagonal(m5, axis1=1, axis2=3)
    return d.transpose(0, 3, 2, 1).reshape(ngb * GROUPS_PER_BLOCK, SSM_GC, SSM_P)


_ANY = pl.BlockSpec(memory_space=pl.ANY)


class _Comm:
    def __init__(self, ins, out_shapes, n_sem, start, mid, finish):
        self.ins, self.out_shapes, self.n_sem = list(ins), list(out_shapes), n_sem
        self.start, self.mid, self.finish = start, mid, finish

    def scratch(self):
        return [pltpu.SemaphoreType.DMA((self.n_sem,)), pltpu.SemaphoreType.DMA((self.n_sem,)),
                pltpu.SemaphoreType.DMA((len(self.ins),))]

    def run(self, in_refs, out_refs, sems, first, mid, last):
        send, recv, local = sems

        @pl.when(first)
        def _():
            self.start(in_refs, out_refs, send, recv, local)

        @pl.when(mid)
        def _():
            self.mid(in_refs, out_refs, send, recv, local)

        @pl.when(last)
        def _():
            self.finish(in_refs, out_refs, send, recv, local)


def _split_refs(refs, n_in, n_out, comm):
    ci = len(comm.ins) if comm else 0
    co = len(comm.out_shapes) if comm else 0
    ins = refs[:n_in]
    c_ins = refs[n_in:n_in + ci]
    outs = refs[n_in + ci:n_in + ci + n_out]
    c_outs = refs[n_in + ci + n_out:n_in + ci + n_out + co]
    rest = refs[n_in + ci + n_out + co:]
    if comm:
        return ins, c_ins, outs, c_outs, rest[:-3], rest[-3:]
    return ins, c_ins, outs, c_outs, rest, ()


def _run_comm(comm, name):
    ni, no = len(comm.ins), len(comm.out_shapes)

    def body(*refs):
        args = (refs[:ni], refs[ni:ni + no]) + tuple(refs[ni + no:])
        comm.start(*args)
        comm.mid(*args)
        comm.finish(*args)

    return pl.pallas_call(
        body,
        in_specs=[_ANY] * ni,
        out_specs=tuple([_ANY] * no),
        out_shape=tuple(comm.out_shapes),
        scratch_shapes=comm.scratch(),
        compiler_params=pltpu.CompilerParams(has_side_effects=True),
        name=name,
    )(*comm.ins)


def _ag_comm(shards):
    n = len(shards)

    def env(ins, outs, send_sems, recv_sems):
        x, y, c = lax.axis_index("x"), lax.axis_index("y"), lax.axis_index("c")
        me, sibling = (x, y, c), (x, y, 1 - c)
        chips = [(1 - x, y), (x, 1 - y), (1 - x, 1 - y)]

        def copy(a, k, block, to, src=None):
            s = 4 * block[0] + 2 * block[1] + block[2]
            return pltpu.make_async_remote_copy(
                src_ref=outs[a].at[s] if src is None else src, dst_ref=outs[a].at[s],
                send_sem=send_sems.at[7 * a + k], recv_sem=recv_sems.at[7 * a + k],
                device_id=to, device_id_type=MESH)

        return c, me, sibling, chips, copy

    def own(ins, outs, local_sems, a):
        x, y, c = lax.axis_index("x"), lax.axis_index("y"), lax.axis_index("c")
        return pltpu.make_async_copy(ins[a], outs[a].at[4 * x + 2 * y + c], local_sems.at[a])

    def first_sends(ins, copy, me, sibling, chips, c, a):
        return [copy(a, 0, me, sibling, src=ins[a])] + [
            copy(a, 1 + j, me, (*chip, c), src=ins[a]) for j, chip in enumerate(chips)]

    def start(ins, outs, send_sems, recv_sems, local_sems):
        c, me, sibling, chips, copy = env(ins, outs, send_sems, recv_sems)
        for a in range(n):
            own(ins, outs, local_sems, a).start()
        for a in range(n):
            for cp in first_sends(ins, copy, me, sibling, chips, c, a):
                cp.start()

    def mid(ins, outs, send_sems, recv_sems, local_sems):
        c, me, sibling, chips, copy = env(ins, outs, send_sems, recv_sems)
        for a in range(n):
            for j, chip in enumerate(chips):
                copy(a, 1 + j, (*chip, c), me).wait_recv()
                copy(a, 4 + j, (*chip, c), sibling).start()

    def finish(ins, outs, send_sems, recv_sems, local_sems):
        c, me, sibling, chips, copy = env(ins, outs, send_sems, recv_sems)
        for a in range(n):
            copy(a, 0, sibling, me).wait_recv()
            for j, chip in enumerate(chips):
                copy(a, 4 + j, (*chip, 1 - c), me).wait_recv()
        for a in range(n):
            for cp in first_sends(ins, copy, me, sibling, chips, c, a):
                cp.wait_send()
            for j, chip in enumerate(chips):
                copy(a, 4 + j, (*chip, c), sibling).wait_send()
            own(ins, outs, local_sems, a).wait()

    return _Comm(shards, [jax.ShapeDtypeStruct((N_DEV,) + s.shape, s.dtype) for s in shards], 7 * n,
                 start, mid, finish)


def _sibling_comm(parts):
    n = len(parts)

    def copies(ins, outs, send_sems, recv_sems):
        x, y, c = lax.axis_index("x"), lax.axis_index("y"), lax.axis_index("c")
        return [pltpu.make_async_remote_copy(
            src_ref=ins[a].at[2 * q + (1 - c)], dst_ref=outs[a].at[q],
            send_sem=send_sems.at[4 * a + q], recv_sem=recv_sems.at[4 * a + q],
            device_id=(x, y, 1 - c), device_id_type=MESH) for a in range(n) for q in range(4)]

    def start(ins, outs, send_sems, recv_sems, local_sems):
        for cp in copies(ins, outs, send_sems, recv_sems):
            cp.start()

    def mid(ins, outs, send_sems, recv_sems, local_sems):
        pass

    def finish(ins, outs, send_sems, recv_sems, local_sems):
        for cp in copies(ins, outs, send_sems, recv_sems):
            cp.wait()

    return _Comm(parts, [jax.ShapeDtypeStruct((4,) + p.shape[1:], p.dtype) for p in parts], 4 * n,
                 start, mid, finish)


def _chips_comm(parts):
    n = len(parts)

    def copies(ins, outs, send_sems, recv_sems):
        x, y, c = lax.axis_index("x"), lax.axis_index("y"), lax.axis_index("c")
        chips = [(1 - x, y), (x, 1 - y), (1 - x, 1 - y)]
        return [pltpu.make_async_remote_copy(
            src_ref=ins[a].at[2 * px + py], dst_ref=outs[a].at[j],
            send_sem=send_sems.at[3 * a + j], recv_sem=recv_sems.at[3 * a + j],
            device_id=(px, py, c), device_id_type=MESH) for a in range(n) for j, (px, py) in enumerate(chips)]

    def start(ins, outs, send_sems, recv_sems, local_sems):
        for cp in copies(ins, outs, send_sems, recv_sems):
            cp.start()

    def mid(ins, outs, send_sems, recv_sems, local_sems):
        pass

    def finish(ins, outs, send_sems, recv_sems, local_sems):
        for cp in copies(ins, outs, send_sems, recv_sems):
            cp.wait()

    return _Comm(parts, [jax.ShapeDtypeStruct((3,) + p.shape[1:], p.dtype) for p in parts], 3 * n,
                 start, mid, finish)


def _sibling_add(part, recv, name):
    _, R, C = part.shape
    tr = _pick(R, (256, 128, 80))
    c = lax.axis_index("c")

    def body(c_ref, p_ref, r_ref, o_ref, o16_ref):
        t = p_ref[...] + r_ref[...]
        o_ref[...] = t
        o16_ref[...] = t.astype(BF16)

    blk = pl.BlockSpec((1, tr, C), lambda q, i, c_ref: (q, i, 0))
    return pl.pallas_call(
        body,
        grid_spec=pltpu.PrefetchScalarGridSpec(
            num_scalar_prefetch=1,
            grid=(4, R // tr),
            in_specs=[pl.BlockSpec((1, tr, C), lambda q, i, c_ref: (2 * q + c_ref[0], i, 0)), blk],
            out_specs=(blk, blk),
        ),
        out_shape=(jax.ShapeDtypeStruct((4, R, C), F32), jax.ShapeDtypeStruct((4, R, C), BF16)),
        compiler_params=_params(("parallel", "parallel")),
        name=name,
    )(jnp.reshape(c, (1,)).astype(jnp.int32), part, recv)


def _adamw(w, g, m, v):
    m = ADAM_B1 * m + (1.0 - ADAM_B1) * g
    v = ADAM_B2 * v + (1.0 - ADAM_B2) * (g * g)
    m_hat = m / (1.0 - ADAM_B1 ** ADAM_STEP)
    v_hat = v / (1.0 - ADAM_B2 ** ADAM_STEP)
    delta = -ADAM_LR * (m_hat / (jnp.sqrt(v_hat) + ADAM_EPS) + ADAM_WD * w)
    return delta, m, v


def _adam_big(t, recv, w, m, v, name):
    _, R, C = t.shape
    tr = _pick(R, (256, 128))
    chip = 2 * lax.axis_index("x") + lax.axis_index("y")

    def body(q_ref, t_ref, r_ref, w_ref, m_ref, v_ref, g_ref, d_ref, nm_ref, nv_ref):
        g = t_ref[0] + r_ref[0].astype(F32) + r_ref[1].astype(F32) + r_ref[2].astype(F32)
        g_ref[...] = g
        d_ref[...], nm_ref[...], nv_ref[...] = _adamw(w_ref[...], g, m_ref[...], v_ref[...])

    blk = pl.BlockSpec((tr, C), lambda i, q_ref: (i, 0))
    o = jax.ShapeDtypeStruct((R, C), F32)
    return pl.pallas_call(
        body,
        grid_spec=pltpu.PrefetchScalarGridSpec(
            num_scalar_prefetch=1,
            grid=(R // tr,),
            in_specs=[pl.BlockSpec((1, tr, C), lambda i, q_ref: (q_ref[0], i, 0)),
                      pl.BlockSpec((3, tr, C), lambda i, q_ref: (0, i, 0)), blk, blk, blk],
            out_specs=(blk, blk, blk, blk),
        ),
        out_shape=(o, o, o, o),
        compiler_params=_params(("parallel",)),
        name=name,
    )(jnp.reshape(chip, (1,)).astype(jnp.int32), t, recv, w, m, v)


def _reduce_big(t, recv, name):
    _, R, C = t.shape
    tr = _pick(R, (256, 128, 80))
    chip = 2 * lax.axis_index("x") + lax.axis_index("y")

    def body(q_ref, t_ref, r_ref, g_ref):
        g_ref[...] = t_ref[0] + r_ref[0].astype(F32) + r_ref[1].astype(F32) + r_ref[2].astype(F32)

    return pl.pallas_call(
        body,
        grid_spec=pltpu.PrefetchScalarGridSpec(
            num_scalar_prefetch=1,
            grid=(R // tr,),
            in_specs=[pl.BlockSpec((1, tr, C), lambda i, q_ref: (q_ref[0], i, 0)),
                      pl.BlockSpec((3, tr, C), lambda i, q_ref: (0, i, 0))],
            out_specs=pl.BlockSpec((tr, C), lambda i, q_ref: (i, 0)),
        ),
        out_shape=jax.ShapeDtypeStruct((R, C), F32),
        compiler_params=_params(("parallel",)),
        name=name,
    )(jnp.reshape(chip, (1,)).astype(jnp.int32), t, recv)


def _adam_only(g, w, m, v, name):
    R, C = g.shape
    tr = _pick(R, (256, 128))

    def body(g_ref, w_ref, m_ref, v_ref, d_ref, nm_ref, nv_ref):
        d_ref[...], nm_ref[...], nv_ref[...] = _adamw(w_ref[...], g_ref[...], m_ref[...], v_ref[...])

    blk = pl.BlockSpec((tr, C), lambda i: (i, 0))
    o = jax.ShapeDtypeStruct((R, C), F32)
    return pl.pallas_call(body, grid=(R // tr,), in_specs=[blk] * 4, out_specs=(blk, blk, blk),
                          out_shape=(o, o, o), compiler_params=_params(("parallel",)), name=name)(g, w, m, v)


def _gather8_comm(gbuf):
    def copies(ins, outs, send_sems, recv_sems):
        x, y, c = lax.axis_index("x"), lax.axis_index("y"), lax.axis_index("c")
        me = 4 * x + 2 * y + c
        out = []
        for k in range(1, N_DEV):
            fx, fy, fc = (k >> 2) & 1, (k >> 1) & 1, k & 1
            px, py, pc = x + fx - 2 * x * fx, y + fy - 2 * y * fy, c + fc - 2 * c * fc
            send = pltpu.make_async_remote_copy(
                src_ref=ins[0], dst_ref=outs[0].at[me], send_sem=send_sems.at[k - 1], recv_sem=recv_sems.at[k - 1],
                device_id=(px, py, pc), device_id_type=MESH)
            recv = pltpu.make_async_remote_copy(
                src_ref=ins[0], dst_ref=outs[0].at[4 * px + 2 * py + pc], send_sem=send_sems.at[k - 1],
                recv_sem=recv_sems.at[k - 1], device_id=(px, py, pc), device_id_type=MESH)
            out.append((send, recv))
        return me, out

    def start(ins, outs, send_sems, recv_sems, local_sems):
        me, cps = copies(ins, outs, send_sems, recv_sems)
        pltpu.make_async_copy(ins[0], outs[0].at[me], local_sems.at[0]).start()
        for send, _ in cps:
            send.start()

    def mid(ins, outs, send_sems, recv_sems, local_sems):
        pass

    def finish(ins, outs, send_sems, recv_sems, local_sems):
        me, cps = copies(ins, outs, send_sems, recv_sems)
        for send, recv in cps:
            recv.wait_recv()
            send.wait_send()
        pltpu.make_async_copy(ins[0], outs[0].at[me], local_sems.at[0]).wait()

    return _Comm([gbuf], [jax.ShapeDtypeStruct((N_DEV,) + gbuf.shape, gbuf.dtype)], N_DEV - 1, start, mid, finish)


def _sum8_adam(slots, wbuf, mbuf, vbuf, name):
    R = wbuf.shape[0]

    def body(s_ref, w_ref, m_ref, v_ref, gs_ref, d_ref, nm_ref, nv_ref):
        g = s_ref[0]
        for s in range(1, N_DEV):
            g = g + s_ref[s]
        gs_ref[...] = g
        d_ref[...], nm_ref[...], nv_ref[...] = _adamw(w_ref[...], g, m_ref[...], v_ref[...])

    o = jax.ShapeDtypeStruct((R, LANES), F32)
    return pl.pallas_call(body, out_shape=(o, o, o, o),
                          compiler_params=pltpu.CompilerParams(vmem_limit_bytes=VMEM_LIMIT), name=name)(
        slots, wbuf, mbuf, vbuf)


def _pack(items):
    rows, spans, r0 = [], [], 0
    for a in items:
        n = a.size
        nr = -(-n // LANES)
        rows.append(jnp.pad(a.reshape(-1).astype(F32), (0, nr * LANES - n)).reshape(nr, LANES))
        spans.append((r0, nr, a.shape))
        r0 += nr
    pad = -r0 % SUBLANES
    if pad:
        rows.append(jnp.zeros((pad, LANES), F32))
    return jnp.concatenate(rows, axis=0), spans


def _unpack(buf, spans):
    return [buf[r0:r0 + nr].reshape(-1)[:math.prod(shape)].reshape(shape) for r0, nr, shape in spans]


def kernel(x, norm_mix_pre, norm_mix_post, norm_mlp_pre, norm_mlp_post, w_in, sinks, lam_re, lam_im, log_dt, b_re, b_im, c_re, c_im, d_skip, w_glu, w_branch, w_out, w_up, w_down, loss_target, m_norm_mix_pre, m_norm_mix_post, m_norm_mlp_pre, m_norm_mlp_post, m_w_in, m_sinks, m_lam_re, m_lam_im, m_log_dt, m_b_re, m_b_im, m_c_re, m_c_im, m_d_skip, m_w_glu, m_w_branch, m_w_out, m_w_up, m_w_down, v_norm_mix_pre, v_norm_mix_post, v_norm_mlp_pre, v_norm_mlp_post, v_w_in, v_sinks, v_lam_re, v_lam_im, v_log_dt, v_b_re, v_b_im, v_c_re, v_c_im, v_d_skip, v_w_glu, v_w_branch, v_w_out, v_w_up, v_w_down):
    _, L, D = x.shape
    xs = x[0]
    tgt = loss_target[0]
    ssm_w = D // 2
    n_groups = ssm_w // SSM_GC
    ngb = n_groups // GROUPS_PER_BLOCK
    n_state = n_groups * SSM_P
    d_ff = w_up.shape[2] * N_DEV
    o_k, o_v, o_u = Q_W, Q_W + KV_W, Q_W + 2 * KV_W
    o_ga = o_u + ssm_w
    o_gs = o_ga + D

    big = {"w_in": w_in[0], "w_glu": w_glu[0], "w_branch": w_branch[0], "w_out": w_out[0],
           "w_up": w_up[0], "w_down": w_down[0]}
    col_sharded = ("w_in", "w_glu", "w_up")
    names = list(big)
    shard16 = {k: (big[k].T if k in col_sharded else big[k]).astype(BF16) for k in names}
    full = {}

    def gathered(keys, arrays):
        for k, g in zip(keys, arrays):
            _, r, c = g.shape
            full[k] = g.reshape(N_DEV * r, c)

    def by_owner(g):
        return g.reshape(N_DEV, g.shape[0] // N_DEV, g.shape[1])

    col = lambda a: a.reshape(n_state, 1)
    lr_c, li_c = col(lam_re[0]), col(lam_im[0])
    ldt_c = jnp.repeat(log_dt[0], SSM_P).reshape(n_state, 1)
    b_re_c, b_im_c = b_re[0].reshape(n_state, SSM_GC), b_im[0].reshape(n_state, SSM_GC)
    a_re, a_im, bb_re, bb_im = _disc_fwd(lr_c, li_c, ldt_c, b_re_c, b_im_c)
    a_re3 = a_re.reshape(n_state // LANES, 1, LANES)
    a_im3 = a_im.reshape(n_state // LANES, 1, LANES)
    br_m = _block_diag_in(bb_re, ngb).astype(BF16)
    bi_m = _block_diag_in(bb_im, ngb).astype(BF16)
    cr_m = _block_diag_out(c_re[0], ngb).astype(BF16)
    ci_m = _block_diag_out(c_im[0], ngb).astype(BF16)
    d_row = d_skip[0].reshape(1, ssm_w)
    cos_t, sin_t = _rope_tables(L)

    h, g1 = _rms_pre(xs, norm_mix_pre, _ag_comm([shard16["w_in"]]))
    gathered(["w_in"], g1)
    z, g3 = _mm(h, full["w_in"], mode="nt", name="mm_z",
                comm=_ag_comm([shard16[k] for k in ("w_glu", "w_branch", "w_out")]))
    gathered(["w_glu", "w_branch", "w_out"], g3)
    wb_a, wb_s = full["w_branch"][:Q_W], full["w_branch"][Q_W:]
    o_attn = _attn_fwd(z, cos_t, sin_t, sinks)
    (y_pre, xs_r, xs_i), g1 = _ssm_fwd(z, o_u, br_m, bi_m, cr_m, ci_m, a_re3, a_im3, d_row,
                                       comm=_ag_comm([shard16["w_up"]]))
    gathered(["w_up"], g1)
    gy = _ew(lambda y: (_gelu(y),), [(y_pre, 0)], (BF16,), rows=L, ncols=ssm_w, name="gelu")
    zg = _mm(gy, full["w_glu"], mode="nt", name="mm_zg")
    o_ssm = _ew(lambda a, b: (a * _sigmoid(b),), [(zg, 0), (zg, ssm_w)], (BF16,), rows=L, ncols=ssm_w, name="glu")
    y_attn = _mm(o_attn, wb_a, mode="nn", name="mm_y_attn")
    y_ssm = _mm(o_ssm, wb_s, mode="nn", name="mm_y_ssm")
    mix = _ew(lambda ga, gs, ya, ys: (_sigmoid(ga) * ya + _sigmoid(gs) * ys,),
              [(z, o_ga), (z, o_gs), (y_attn, 0), (y_ssm, 0)], (BF16,), rows=L, ncols=D, name="mix")
    mixed = _mm(mix, full["w_out"], mode="nn", name="mm_mixed")
    x1, h2 = _post_pre(xs, mixed, norm_mix_post, norm_mlp_pre)

    def relu_sq(acc):
        a = jnp.maximum(acc, 0.0)
        return a, a * a

    (act, act2), g1 = _mm(h2, full["w_up"], mode="nt", name="mm_up", out_dtypes=(BF16, BF16), epi=relu_sq,
                          comm=_ag_comm([shard16["w_down"]]))
    gathered(["w_down"], g1)
    dn = _mm(act2, full["w_down"], mode="nn", name="mm_down")
    dx2, d_dn, dg_mlp_post, loss_part = _loss_bwd(x1, dn, norm_mlp_post, tgt)

    d_pre = _mm(d_dn, full["w_down"], mode="nt", name="mm_d_act", out_dtypes=(BF16,),
                epi=lambda acc, a: (acc * (2.0 * a.astype(F32)),), extras=(act,))
    gw_down = _mm(act2, d_dn, mode="tn", name="mm_gw_down")
    p_down = by_owner(gw_down)
    dh2, (sib_down,) = _mm(d_pre, full["w_up"], mode="nn", name="mm_dh2", comm=_sibling_comm([p_down]))
    t_down, t16_down = _sibling_add(p_down, sib_down, "rs_add_w_down")
    gw_up, (chips_down,) = _mm(d_pre, h2, mode="tn", name="mm_gw_up", comm=_chips_comm([t16_down]))
    p_up = by_owner(gw_up)
    dx1, d_mixed, dg_mlp_pre, dg_mix_post = _norm_bwd_pair(x1, dh2, dx2, mixed, norm_mlp_pre, norm_mix_post)
    d_mix, (sib_up,) = _mm(d_mixed, full["w_out"], mode="nt", name="mm_d_mix", comm=_sibling_comm([p_up]))
    t_up, t16_up = _sibling_add(p_up, sib_up, "rs_add_w_up")
    gw_out = _mm(mix, d_mixed, mode="tn", name="mm_gw_out")
    d_y2, dz = _gate_bwd(d_mix, z, o_ga, y_attn, y_ssm)
    d_o_attn = _mm(d_y2, wb_a, mode="nt", name="mm_d_o_attn", a_win=(0, D))
    gwb_a = _mm(o_attn, d_y2, mode="tn", name="mm_gwb_a", b_win=(0, D))
    d_o_ssm = _mm(d_y2, wb_s, mode="nt", name="mm_d_o_ssm", a_win=(D, D))
    gwb_s = _mm(o_ssm, d_y2, mode="tn", name="mm_gwb_s", b_win=(D, D))

    def glu_bwd(do, a, b):
        s = _sigmoid(b)
        return do * s, do * a * s * (1.0 - s)

    d_zg_a, d_zg_b = _ew(glu_bwd, [(d_o_ssm, 0), (zg, 0), (zg, ssm_w)], (BF16, BF16), rows=L, ncols=ssm_w, name="glu_bwd")
    d_zg = jnp.concatenate([d_zg_a, d_zg_b], axis=1)
    dy_pre = _mm(d_zg, full["w_glu"], mode="nn", name="mm_d_gy",
                 epi=lambda acc, y: (acc * _gelu_grad(y),), extras=(y_pre,))
    gw_glu = _mm(d_zg, gy, mode="tn", name="mm_gw_glu")
    mids = ["w_glu", "w_branch", "w_out"]
    p_mid = [by_owner(gw_glu), by_owner(jnp.concatenate([gwb_a, gwb_s], axis=0)), by_owner(gw_out)]
    sib_mid = _run_comm(_sibling_comm(p_mid), "rs_sibling_mid")
    t_mid = [_sibling_add(p, r, "rs_add_" + k) for k, p, r in zip(mids, p_mid, sib_mid)]
    (dz, g_dskip, g_ar3, g_ai3, g_br_m, g_bi_m, g_cr_m, g_ci_m) = _ssm_bwd(
        dy_pre, z, o_u, xs_r, xs_i, br_m, bi_m, cr_m, ci_m, a_re3, a_im3, d_row, dz)
    g_lr, g_li, g_ldt, g_b_re, g_b_im = _disc_bwd(
        lr_c, li_c, ldt_c, b_re_c, b_im_c, g_ar3.reshape(n_state, 1), g_ai3.reshape(n_state, 1),
        _diag_in(g_br_m, ngb), _diag_in(g_bi_m, ngb))
    (dz, dkc, dkp, dvc, dvp, dsink_rows), chips_a = _attn_bwd(
        z, d_o_attn, cos_t, sin_t, sinks, dz, comm=_chips_comm([t16_up] + [t16 for _, t16 in t_mid]))
    chips_up, chips_mid = chips_a[0], chips_a[1:]
    dz = _kv_combine(dkc, dkp, dvc, dvp, dz)
    dsink = jnp.stack([dsink_rows[:, 0], dsink_rows[:, HEAD_DIM]], axis=1).reshape(1, N_Q_HEADS)
    small_names = ["norm_mix_post", "norm_mlp_pre", "norm_mlp_post", "sinks", "lam_re", "lam_im",
                   "log_dt", "b_re", "b_im", "c_re", "c_im", "d_skip"]
    small_g = [dg_mix_post, dg_mlp_pre, dg_mlp_post, dsink,
               g_lr.reshape(lam_re.shape), g_li.reshape(lam_im.shape), g_ldt.reshape(log_dt.shape),
               g_b_re.reshape(b_re.shape), g_b_im.reshape(b_im.shape),
               _diag_out(g_cr_m, ngb).reshape(c_re.shape), _diag_out(g_ci_m, ngb).reshape(c_im.shape),
               g_dskip.reshape(d_skip.shape)]
    gbuf, spans = _pack(small_g + [loss_part])
    gw_in, (small_slots,) = _mm(dz, h, mode="tn", name="mm_gw_in", comm=_gather8_comm(gbuf))
    p_in = by_owner(gw_in)
    (sib_in,) = _run_comm(_sibling_comm([p_in]), "rs_sibling_w_in")
    t_in, t16_in = _sibling_add(p_in, sib_in, "rs_add_w_in")
    dh, (chips_in,) = _mm(dz, full["w_in"], mode="nn", name="mm_dh", comm=_chips_comm([t16_in]))
    grad_x, dg_mix_pre = _final_bwd(xs, dh, dx1, norm_mix_pre, None)

    reduced = {"w_in": (t_in, chips_in), "w_up": (t_up, chips_up), "w_down": (t_down, chips_down)}
    for k, (t32, _), r in zip(mids, t_mid, chips_mid):
        reduced[k] = (t32, r)
    moments = {"w_in": (m_w_in, v_w_in), "w_glu": (m_w_glu, v_w_glu), "w_branch": (m_w_branch, v_w_branch),
               "w_out": (m_w_out, v_w_out), "w_up": (m_w_up, v_w_up), "w_down": (m_w_down, v_w_down)}
    big_out = {}
    for k in names:
        t, r = reduced[k]
        mm_, vv_ = moments[k]
        if k in col_sharded:
            g = _reduce_big(t, r, "reduce_" + k).T
            big_out[k] = [o[None] for o in (g,) + tuple(_adam_only(g, big[k], mm_[0], vv_[0], "adam_" + k))]
        else:
            big_out[k] = [o[None] for o in _adam_big(t, r, big[k], mm_[0], vv_[0], "adam_" + k)]

    small_w = [norm_mix_post, norm_mlp_pre, norm_mlp_post, sinks, lam_re, lam_im, log_dt,
               b_re, b_im, c_re, c_im, d_skip]
    small_m = [m_norm_mix_post, m_norm_mlp_pre, m_norm_mlp_post, m_sinks, m_lam_re, m_lam_im,
               m_log_dt, m_b_re, m_b_im, m_c_re, m_c_im, m_d_skip]
    small_v = [v_norm_mix_post, v_norm_mlp_pre, v_norm_mlp_post, v_sinks, v_lam_re, v_lam_im,
               v_log_dt, v_b_re, v_b_im, v_c_re, v_c_im, v_d_skip]
    zero1 = jnp.zeros((1, 1), F32)
    wbuf, _ = _pack(small_w + [zero1])
    mbuf, _ = _pack(small_m + [zero1])
    vbuf, _ = _pack(small_v + [zero1])
    gs, ds, nms, nvs = [_unpack(b, spans) for b in _sum8_adam(small_slots, wbuf, mbuf, vbuf, "small_adam")]
    loss = gs[-1].reshape(())
    tbuf, tspans = _pack([dg_mix_pre])
    (tail_slots,) = _run_comm(_gather8_comm(tbuf), "gather_tail")
    tail = _sum8_adam(tail_slots, _pack([norm_mix_pre])[0], _pack([m_norm_mix_pre])[0],
                      _pack([v_norm_mix_pre])[0], "small_adam_tail")
    small_names = ["norm_mix_pre"] + small_names
    gs, ds, nms, nvs = [_unpack(t, tspans) + src for t, src in zip(tail, (gs, ds, nms, nvs))]

    order = ["norm_mix_pre", "norm_mix_post", "norm_mlp_pre", "norm_mlp_post", "w_in", "sinks", "lam_re", "lam_im",
             "log_dt", "b_re", "b_im", "c_re", "c_im", "d_skip", "w_glu", "w_branch", "w_out", "w_up", "w_down"]
    outs = [loss, grad_x[None]]
    for idx, src in enumerate((gs, ds, nms, nvs)):
        for k in order:
            outs.append(big_out[k][idx] if k in big_out else src[small_names.index(k)])
    return tuple(outs)
```

```python
import functools
import math

import jax
import jax.numpy as jnp
from jax import lax
from jax.experimental import pallas as pl
from jax.experimental.pallas import tpu as pltpu

F32 = jnp.float32
BF16 = jnp.bfloat16
MESH = pl.DeviceIdType.MESH

LANES = 128
SUBLANES = 8
VMEM_LIMIT = 56 * 1024 * 1024

HEAD_DIM = 64
N_Q_HEADS = 16
N_KV_HEADS = 2
Q_W = N_Q_HEADS * HEAD_DIM
KV_W = N_KV_HEADS * HEAD_DIM
BLOCK = 128
ROT_DIM = HEAD_DIM // 4
ROPE_THETA = 500000.0
SSM_GC = 16
SSM_P = 64
GROUPS_PER_BLOCK = 8
NW = GROUPS_PER_BLOCK * SSM_P // LANES
EPS = 1e-6
N_DEV = 8

ADAM_LR = 0.001
ADAM_B1 = 0.9
ADAM_B2 = 0.999
ADAM_EPS = 1e-08
ADAM_WD = 0.01
ADAM_STEP = 10


def _params(sem=None):
    return pltpu.CompilerParams(dimension_semantics=sem, vmem_limit_bytes=VMEM_LIMIT)


def _host_params(sem, comm):
    if comm:
        return pltpu.CompilerParams(dimension_semantics=("arbitrary",) * len(sem), vmem_limit_bytes=VMEM_LIMIT,
                                    has_side_effects=True)
    return _params(sem)


def _pick(dim, prefs):
    for p in prefs:
        if dim % p == 0:
            return p
    return dim


def _sigmoid(x):
    return 1.0 / (1.0 + jnp.exp(-x))


_GELU_C = math.sqrt(2.0 / math.pi)


def _gelu(x):
    return 0.5 * x * (1.0 + jnp.tanh(_GELU_C * (x + 0.044715 * x * x * x)))


def _gelu_grad(x):
    t = jnp.tanh(_GELU_C * (x + 0.044715 * x * x * x))
    return 0.5 * (1.0 + t) + 0.5 * x * (1.0 - t * t) * _GELU_C * (1.0 + 3.0 * 0.044715 * x * x)


_DIMS = {"nn": (((1,), (0,)), ((), ())), "nt": (((1,), (1,)), ((), ())), "tn": (((0,), (0,)), ((), ()))}


def _mm(a, b, *, mode, name, out_dtypes=(F32,), epi=None, extras=(), comm=None, a_win=None, b_win=None):
    ar, ac = a.shape[0], (a_win[1] if a_win else a.shape[1])
    br, bc = b.shape[0], (b_win[1] if b_win else b.shape[1])
    if mode == "nn":
        (M, K), (K2, N) = (ar, ac), (br, bc)
    elif mode == "nt":
        (M, K), (N, K2) = (ar, ac), (br, bc)
    else:
        (K, M), (K2, N) = (ar, ac), (br, bc)
    assert K == K2, (a.shape, b.shape, mode)
    tm = _pick(M, (1024, 1280, 640, 512, 256, 128))
    tn = _pick(N, (1024, 1280, 640, 512, 384, 256, 128))
    tk = K if K <= 2048 else _pick(K, (2048, 1280, 1024, 640, 512, 256, 128))
    nk = K // tk
    a_col_tile = tm if mode == "tn" else tk
    b_col_tile = tk if mode == "nt" else tn
    ao = a_win[0] // a_col_tile if a_win else 0
    bo = b_win[0] // b_col_tile if b_win else 0
    assert (not a_win or a_win[0] % a_col_tile == 0) and (not b_win or b_win[0] % b_col_tile == 0)
    n_ex = len(extras)
    n_out = len(out_dtypes)
    gi, gj = M // tm, N // tn
    steps = gi * gj * nk

    def body(*refs):
        ins, c_ins, o_refs, c_outs, scratch, sems = _split_refs(refs, 2 + n_ex, n_out, comm)
        a_ref, b_ref = ins[0], ins[1]
        ex_refs = ins[2:]
        if comm:
            s = (pl.program_id(0) * gj + pl.program_id(1)) * nk + pl.program_id(2)
            comm.run(c_ins, c_outs, sems, s == 0, s == steps // 2, s == steps - 1)

        def finish(r):
            outs = (r,) if epi is None else epi(r, *[e[...] for e in ex_refs])
            for o_ref, o in zip(o_refs, outs):
                o_ref[...] = o.astype(o_ref.dtype)

        part = lax.dot_general(a_ref[...].astype(BF16), b_ref[...].astype(BF16), _DIMS[mode],
                               preferred_element_type=F32)
        if nk == 1:
            finish(part)
            return
        acc = scratch[0]
        k = pl.program_id(2)

        @pl.when(k == 0)
        def _():
            acc[...] = part

        @pl.when((k > 0) & (k < nk - 1))
        def _():
            acc[...] += part

        @pl.when(k == nk - 1)
        def _():
            finish(acc[...] + part)

    if mode == "nn":
        a_spec = pl.BlockSpec((tm, tk), lambda i, j, k: (i, k + ao))
        b_spec = pl.BlockSpec((tk, tn), lambda i, j, k: (k, j + bo))
    elif mode == "nt":
        a_spec = pl.BlockSpec((tm, tk), lambda i, j, k: (i, k + ao))
        b_spec = pl.BlockSpec((tn, tk), lambda i, j, k: (j, k + bo))
    else:
        a_spec = pl.BlockSpec((tk, tm), lambda i, j, k: (k, i + ao))
        b_spec = pl.BlockSpec((tk, tn), lambda i, j, k: (k, j + bo))
    o_spec = pl.BlockSpec((tm, tn), lambda i, j, k: (i, j))
    c_ins = comm.ins if comm else []
    c_shapes = comm.out_shapes if comm else []
    res = pl.pallas_call(
        body,
        grid=(gi, gj, nk),
        in_specs=[a_spec, b_spec] + [o_spec] * n_ex + [_ANY] * len(c_ins),
        out_specs=tuple([o_spec] * n_out + [_ANY] * len(c_shapes)),
        out_shape=tuple([jax.ShapeDtypeStruct((M, N), d) for d in out_dtypes] + c_shapes),
        scratch_shapes=([pltpu.VMEM((tm, tn), F32)] if nk > 1 else []) + (comm.scratch() if comm else []),
        compiler_params=_host_params(("parallel", "parallel", "arbitrary"), comm),
        name=name,
    )(a, b, *extras, *c_ins)
    if comm:
        return (res[0] if n_out == 1 else res[:n_out]), list(res[n_out:])
    return res[0] if n_out == 1 else res


def _ew(fn, ins, out_dtypes, *, rows, ncols, name):
    g = ncols
    for _, off in ins:
        g = math.gcd(g, off)
    tc = _pick(g, (512, 256, 128))
    tr = _pick(rows, (1024, 512, 256, 128))
    n_in = len(ins)

    def body(*refs):
        outs = fn(*[r[...] for r in refs[:n_in]])
        for o_ref, o in zip(refs[n_in:], outs):
            o_ref[...] = o.astype(o_ref.dtype)

    def in_spec(off):
        ob = off // tc
        return pl.BlockSpec((tr, tc), lambda i, j: (i, j + ob))

    o_spec = pl.BlockSpec((tr, tc), lambda i, j: (i, j))
    res = pl.pallas_call(
        body,
        grid=(rows // tr, ncols // tc),
        in_specs=[in_spec(off) for _, off in ins],
        out_specs=tuple([o_spec] * len(out_dtypes)),
        out_shape=tuple(jax.ShapeDtypeStruct((rows, ncols), d) for d in out_dtypes),
        compiler_params=_params(("parallel", "parallel")),
        name=name,
    )(*[arr for arr, _ in ins])
    return res[0] if len(out_dtypes) == 1 else res


def _rstd(x):
    return lax.rsqrt(jnp.mean(x * x, axis=-1, keepdims=True) + EPS)


def _norm_bwd(x, r, g, dy):
    t = dy * g
    dx = r * t - x * (r * r * r) * jnp.mean(t * x, axis=-1, keepdims=True)
    return dx, dy * x * r


def _row_call(body, ins, row_ins, outs, acc_outs, *, rows, width, name, comm=None):
    tr = _pick(rows, (256, 128))
    steps = rows // tr
    t_spec = pl.BlockSpec((tr, width), lambda i: (i, 0))
    r_spec = pl.BlockSpec((1, width), lambda i: (0, 0))
    n_in, n_out = len(ins) + len(row_ins), len(outs) + len(acc_outs)

    def hosted(*refs):
        h_ins, c_ins, h_outs, c_outs, _, sems = _split_refs(refs, n_in, n_out, comm)
        i = pl.program_id(0)
        comm.run(c_ins, c_outs, sems, i == 0, i == steps // 2, i == steps - 1)
        body(*h_ins, *h_outs)

    c_ins = comm.ins if comm else []
    c_shapes = comm.out_shapes if comm else []
    res = pl.pallas_call(
        hosted if comm else body,
        grid=(steps,),
        in_specs=[t_spec] * len(ins) + [r_spec] * len(row_ins) + [_ANY] * len(c_ins),
        out_specs=tuple([t_spec] * len(outs) + [pl.BlockSpec(s, lambda i: (0, 0)) for s in acc_outs]
                        + [_ANY] * len(c_shapes)),
        out_shape=tuple([jax.ShapeDtypeStruct((rows, width), d) for d in outs]
                        + [jax.ShapeDtypeStruct(s, F32) for s in acc_outs] + c_shapes),
        scratch_shapes=comm.scratch() if comm else [],
        compiler_params=_host_params(("arbitrary",), comm),
        name=name,
    )(*ins, *row_ins, *c_ins)
    return (res[:n_out], list(res[n_out:])) if comm else res


def _rms_pre(x, g, comm):
    L, D = x.shape

    def body(x_ref, g_ref, h_ref):
        xv = x_ref[...]
        h_ref[...] = (xv * _rstd(xv) * g_ref[...]).astype(BF16)

    (h,), c_outs = _row_call(body, [x], [g], [BF16], [], rows=L, width=D, name="rms_pre", comm=comm)
    return h, c_outs


def _post_pre(x, mixed, g_post, g_pre):
    L, D = x.shape

    def body(x_ref, m_ref, gp_ref, gq_ref, x1_ref, h2_ref):
        mv = m_ref[...]
        x1 = x_ref[...] + mv * _rstd(mv) * gp_ref[...]
        x1_ref[...] = x1
        h2_ref[...] = (x1 * _rstd(x1) * gq_ref[...]).astype(BF16)

    return _row_call(body, [x, mixed], [g_post, g_pre], [F32, BF16], [], rows=L, width=D, name="post_pre")


def _loss_bwd(x1, dn, g_post, target):
    L, D = x1.shape

    def body(x1_ref, dn_ref, t_ref, g_ref, dx2_ref, ddn_ref, dg_ref, loss_ref):
        @pl.when(pl.program_id(0) == 0)
        def _():
            dg_ref[...] = jnp.zeros_like(dg_ref)
            loss_ref[...] = jnp.zeros_like(loss_ref)

        dnv = dn_ref[...]
        g = g_ref[...]
        r = _rstd(dnv)
        err = x1_ref[...] + dnv * r * g - t_ref[...]
        loss_ref[...] += 0.5 * jnp.sum(jnp.mean(err * err, axis=-1, keepdims=True), axis=0, keepdims=True)
        dx2 = err * (1.0 / D)
        dx2_ref[...] = dx2
        ddn, dgr = _norm_bwd(dnv, r, g, dx2)
        ddn_ref[...] = ddn.astype(BF16)
        dg_ref[...] += jnp.sum(dgr, axis=0, keepdims=True)

    return _row_call(body, [x1, dn, target], [g_post], [F32, BF16], [(1, D), (1, 1)],
                     rows=L, width=D, name="loss_bwd")


def _norm_bwd_pair(x1, dh2, dx2, mixed, g_pre, g_post):
    L, D = x1.shape

    def body(x1_ref, dh_ref, dx2_ref, m_ref, gq_ref, gp_ref, dx1_ref, dm_ref, dgq_ref, dgp_ref):
        @pl.when(pl.program_id(0) == 0)
        def _():
            dgq_ref[...] = jnp.zeros_like(dgq_ref)
            dgp_ref[...] = jnp.zeros_like(dgp_ref)

        x1v = x1_ref[...]
        d1, dgq = _norm_bwd(x1v, _rstd(x1v), gq_ref[...], dh_ref[...].astype(F32))
        dx1 = dx2_ref[...] + d1
        dx1_ref[...] = dx1
        mv = m_ref[...]
        dm, dgp = _norm_bwd(mv, _rstd(mv), gp_ref[...], dx1)
        dm_ref[...] = dm.astype(BF16)
        dgq_ref[...] += jnp.sum(dgq, axis=0, keepdims=True)
        dgp_ref[...] += jnp.sum(dgp, axis=0, keepdims=True)

    return _row_call(body, [x1, dh2, dx2, mixed], [g_pre, g_post], [F32, BF16], [(1, D), (1, D)],
                     rows=L, width=D, name="norm_bwd_pair")


def _final_bwd(x, dh, dx1, g_pre, comm):
    L, D = x.shape

    def body(x_ref, dh_ref, dx1_ref, g_ref, gx_ref, dg_ref):
        @pl.when(pl.program_id(0) == 0)
        def _():
            dg_ref[...] = jnp.zeros_like(dg_ref)

        xv = x_ref[...]
        d0, dg = _norm_bwd(xv, _rstd(xv), g_ref[...], dh_ref[...].astype(F32))
        gx_ref[...] = dx1_ref[...] + d0
        dg_ref[...] += jnp.sum(dg, axis=0, keepdims=True)

    return _row_call(body, [x, dh, dx1], [g_pre], [F32], [(1, D)], rows=L, width=D, name="final_bwd", comm=comm)


def _rope_tables(L):
    half = ROT_DIM // 2
    inv = ROPE_THETA ** (-jnp.arange(half, dtype=F32) * 2.0 / ROT_DIM)
    ang = jnp.arange(L, dtype=F32)[:, None] * inv[None, :]
    d = jnp.arange(LANES) % HEAD_DIM
    a = ang[:, d % half]
    cos_t = jnp.where(d[None, :] < ROT_DIM, jnp.cos(a), 1.0)
    sin_t = jnp.where(d[None, :] < half, -jnp.sin(a), jnp.where(d[None, :] < ROT_DIM, jnp.sin(a), 0.0))
    return cos_t.astype(F32), sin_t.astype(F32)


def _lane_lo(shape):
    return lax.broadcasted_iota(jnp.int32, shape, 1) < HEAD_DIM


def _rope(x, cos_t, sin_t):
    d = lax.broadcasted_iota(jnp.int32, x.shape, 1) % HEAD_DIM
    partner = jnp.where(d < ROT_DIM // 2, pltpu.roll(x, LANES - ROT_DIM // 2, 1), pltpu.roll(x, ROT_DIM // 2, 1))
    return x * cos_t + partner * sin_t


def _dup(kv, g):
    sw = pltpu.roll(kv, HEAD_DIM, 1)
    lo = _lane_lo(kv.shape)
    return jnp.where(lo, kv, sw) if g == 0 else jnp.where(lo, sw, kv)


def _attn_mask(n):
    qi = lax.broadcasted_iota(jnp.int32, (BLOCK, 2 * BLOCK), 0)
    kj = lax.broadcasted_iota(jnp.int32, (BLOCK, 2 * BLOCK), 1)
    rel = qi + BLOCK - kj
    return (rel >= 0) & (rel < BLOCK) & ((kj >= BLOCK) | (n > 0))


def _softmax_sink(s, mask, sink):
    s = jnp.where(mask, s, -1e30)
    m = jnp.maximum(jnp.max(s, axis=-1, keepdims=True), sink)
    e = jnp.where(mask, jnp.exp(s - m), 0.0)
    es = jnp.exp(sink - m)
    inv = 1.0 / (jnp.sum(e, axis=-1, keepdims=True) + es)
    return e * inv, es * inv


_NT = (((1,), (1,)), ((), ()))
_TN = (((0,), (0,)), ((), ()))


def _dot(a, b):
    return jnp.dot(a.astype(BF16), b.astype(BF16), preferred_element_type=F32)


def _dot_nt(a, b):
    return lax.dot_general(a.astype(BF16), b.astype(BF16), _NT, preferred_element_type=F32)


def _dot_tn(a, b):
    return lax.dot_general(a.astype(BF16), b.astype(BF16), _TN, preferred_element_type=F32)


def _attn_specs(nb):
    kcol, vcol = Q_W // LANES, Q_W // LANES + 1
    prev = lambda n: jnp.maximum(n - 1, 0)
    return [
        pl.BlockSpec((BLOCK, Q_W), lambda n: (n, 0)),
        pl.BlockSpec((BLOCK, LANES), lambda n: (n, kcol)),
        pl.BlockSpec((BLOCK, LANES), lambda n: (prev(n), kcol)),
        pl.BlockSpec((BLOCK, LANES), lambda n: (n, vcol)),
        pl.BlockSpec((BLOCK, LANES), lambda n: (prev(n), vcol)),
        pl.BlockSpec((BLOCK, LANES), lambda n: (n, 0)),
        pl.BlockSpec((BLOCK, LANES), lambda n: (prev(n), 0)),
        pl.BlockSpec((BLOCK, LANES), lambda n: (n, 0)),
        pl.BlockSpec((BLOCK, LANES), lambda n: (prev(n), 0)),
        pl.BlockSpec(memory_space=pltpu.SMEM),
    ]


def _attn_prep(refs):
    q_ref, kc_ref, kp_ref, vc_ref, vp_ref, cc_ref, cp_ref, sc_ref, sp_ref = refs
    cos_c, sin_c, cos_p, sin_p = cc_ref[...], sc_ref[...], cp_ref[...], sp_ref[...]
    k2 = jnp.concatenate([_rope(kp_ref[...], cos_p, sin_p), _rope(kc_ref[...], cos_c, sin_c)], axis=0)
    v2 = jnp.concatenate([vp_ref[...], vc_ref[...]], axis=0)
    kd = [_dup(k2, g).astype(BF16) for g in range(N_KV_HEADS)]
    vd = [_dup(v2, g).astype(BF16) for g in range(N_KV_HEADS)]
    return cos_c, sin_c, cos_p, sin_p, kd, vd


def _stack_heads(x):
    lo = _lane_lo(x.shape)
    return jnp.concatenate([jnp.where(lo, x, 0.0), jnp.where(lo, 0.0, x)], axis=0)


def _unstack_heads(x2):
    return jnp.where(_lane_lo((BLOCK, LANES)), x2[:BLOCK], x2[BLOCK:])


def _pair_sinks(sink_ref, t):
    first = lax.broadcasted_iota(jnp.int32, (2 * BLOCK, 1), 0) < BLOCK
    return jnp.where(first, sink_ref[0, 2 * t], sink_ref[0, 2 * t + 1])


_SCALE = 1.0 / math.sqrt(HEAD_DIM)
_TILES = Q_W // LANES
_TILES_PER_KV = _TILES // N_KV_HEADS


def _attn_fwd(z, cos_t, sin_t, sinks, comm=None):
    L = z.shape[0]
    nb = L // BLOCK

    def body(*refs):
        ins, c_ins, outs, c_outs, _, sems = _split_refs(refs, 10, 1, comm)
        q_ref, kc_ref, kp_ref, vc_ref, vp_ref, cc_ref, cp_ref, sc_ref, sp_ref, sink_ref = ins
        o_ref = outs[0]
        n = pl.program_id(0)
        if comm:
            comm.run(c_ins, c_outs, sems, n == 0, n == nb // 2, n == nb - 1)
        cos_c, sin_c, _, _, kd, vd = _attn_prep((q_ref, kc_ref, kp_ref, vc_ref, vp_ref, cc_ref, cp_ref, sc_ref, sp_ref))
        mask = _attn_mask(n)
        mask2 = jnp.concatenate([mask, mask], axis=0)
        for t in range(_TILES):
            g = t // _TILES_PER_KV
            q2 = _stack_heads(_rope(q_ref[:, t * LANES:(t + 1) * LANES], cos_c, sin_c) * _SCALE)
            p2, _ = _softmax_sink(_dot_nt(q2, kd[g]), mask2, _pair_sinks(sink_ref, t))
            o_ref[:, t * LANES:(t + 1) * LANES] = _unstack_heads(_dot(p2, vd[g])).astype(BF16)

    c_ins = comm.ins if comm else []
    c_shapes = comm.out_shapes if comm else []
    res = pl.pallas_call(
        body,
        grid=(nb,),
        in_specs=_attn_specs(nb) + [_ANY] * len(c_ins),
        out_specs=tuple([pl.BlockSpec((BLOCK, Q_W), lambda n: (n, 0))] + [_ANY] * len(c_shapes)),
        out_shape=tuple([jax.ShapeDtypeStruct((L, Q_W), BF16)] + c_shapes),
        scratch_shapes=comm.scratch() if comm else [],
        compiler_params=_host_params(("parallel",), comm),
        name="attn_fwd",
    )(z, z, z, z, z, cos_t, cos_t, sin_t, sin_t, sinks, *c_ins)
    return (res[0], list(res[1:])) if comm else res[0]


def _attn_bwd(z, d_o, cos_t, sin_t, sinks, dz, comm=None):
    L = z.shape[0]
    nb = L // BLOCK

    def body(*refs):
        ins, c_ins, outs, c_outs, _, sems = _split_refs(refs, 12, 6, comm)
        q_ref, kc_ref, kp_ref, vc_ref, vp_ref, cc_ref, cp_ref, sc_ref, sp_ref, sink_ref, do_ref, _ = ins
        dq_ref, dkc_ref, dkp_ref, dvc_ref, dvp_ref, ds_ref = outs
        n = pl.program_id(0)
        if comm:
            comm.run(c_ins, c_outs, sems, n == 0, n == nb // 2, n == nb - 1)

        @pl.when(n == 0)
        def _():
            ds_ref[...] = jnp.zeros_like(ds_ref)

        cos_c, sin_c, cos_p, sin_p, kd, vd = _attn_prep(
            (q_ref, kc_ref, kp_ref, vc_ref, vp_ref, cc_ref, cp_ref, sc_ref, sp_ref))
        mask = _attn_mask(n)
        mask2 = jnp.concatenate([mask, mask], axis=0)
        lo2 = _lane_lo((2 * BLOCK, LANES))
        acc_k = [jnp.zeros((2 * BLOCK, LANES), F32) for _ in range(N_KV_HEADS)]
        acc_v = [jnp.zeros((2 * BLOCK, LANES), F32) for _ in range(N_KV_HEADS)]
        sink_rows = []
        for t in range(_TILES):
            g = t // _TILES_PER_KV
            sl = slice(t * LANES, (t + 1) * LANES)
            q2 = _stack_heads(_rope(q_ref[:, sl], cos_c, sin_c) * _SCALE)
            p2, ps2 = _softmax_sink(_dot_nt(q2, kd[g]), mask2, _pair_sinks(sink_ref, t))
            do2 = _stack_heads(do_ref[:, sl])
            d2 = jnp.sum(do2 * _dot(p2, vd[g]), axis=-1, keepdims=True)
            ds2 = p2 * (_dot_nt(do2, vd[g]) - d2)
            dqt = _unstack_heads(_dot(ds2, kd[g])) * _SCALE
            dq_ref[:, sl] = _rope(dqt, cos_c, -sin_c).astype(BF16)
            acc_k[g] = acc_k[g] + _dot_tn(ds2, q2)
            acc_v[g] = acc_v[g] + _dot_tn(p2, do2)
            sd = ps2 * d2
            sink_rows.append(jnp.where(_lane_lo((1, LANES)), -jnp.sum(sd[:BLOCK], axis=0, keepdims=True),
                                       -jnp.sum(sd[BLOCK:], axis=0, keepdims=True)))
        ds_ref[...] += jnp.concatenate(sink_rows, axis=0)
        fk = [a + pltpu.roll(a, HEAD_DIM, 1) for a in acc_k]
        fv = [a + pltpu.roll(a, HEAD_DIM, 1) for a in acc_v]
        dk2 = jnp.where(lo2, fk[0], fk[1])
        dv2 = jnp.where(lo2, fv[0], fv[1])
        dkp_ref[...] = _rope(dk2[:BLOCK], cos_p, -sin_p)
        dkc_ref[...] = _rope(dk2[BLOCK:], cos_c, -sin_c)
        dvp_ref[...] = dv2[:BLOCK]
        dvc_ref[...] = dv2[BLOCK:]

    blk = pl.BlockSpec((BLOCK, LANES), lambda n: (n, 0))
    kv = jax.ShapeDtypeStruct((L, LANES), F32)
    c_ins = comm.ins if comm else []
    c_shapes = comm.out_shapes if comm else []
    res = pl.pallas_call(
        body,
        grid=(nb,),
        in_specs=_attn_specs(nb) + [pl.BlockSpec((BLOCK, Q_W), lambda n: (n, 0)), _ANY] + [_ANY] * len(c_ins),
        out_specs=tuple([pl.BlockSpec((BLOCK, Q_W), lambda n: (n, 0)), blk, blk, blk, blk,
                         pl.BlockSpec((_TILES, LANES), lambda n: (0, 0))] + [_ANY] * len(c_shapes)),
        out_shape=tuple([jax.ShapeDtypeStruct(dz.shape, dz.dtype), kv, kv, kv, kv,
                         jax.ShapeDtypeStruct((_TILES, LANES), F32)] + c_shapes),
        scratch_shapes=comm.scratch() if comm else [],
        input_output_aliases={11: 0},
        compiler_params=_host_params(("arbitrary",), comm),
        name="attn_bwd",
    )(z, z, z, z, z, cos_t, cos_t, sin_t, sin_t, sinks, d_o, dz, *c_ins)
    return (res[:6], list(res[6:])) if comm else res


def _kv_combine(dkc, dkp, dvc, dvp, dz):
    L = dkc.shape[0]
    nb = L // BLOCK

    def body(kc_ref, kp_ref, vc_ref, vp_ref, dz_in, o_ref):
        live = jnp.where(pl.program_id(0) + 1 < nb, 1.0, 0.0)
        o_ref[:, :LANES] = (kc_ref[...] + live * kp_ref[...]).astype(BF16)
        o_ref[:, LANES:] = (vc_ref[...] + live * vp_ref[...]).astype(BF16)

    cur = pl.BlockSpec((BLOCK, LANES), lambda n: (n, 0))
    nxt = pl.BlockSpec((BLOCK, LANES), lambda n: (jnp.minimum(n + 1, nb - 1), 0))
    kv_block = Q_W // (2 * KV_W)
    return pl.pallas_call(body, grid=(nb,), in_specs=[cur, nxt, cur, nxt, _ANY],
                          out_specs=pl.BlockSpec((BLOCK, 2 * KV_W), lambda n: (n, kv_block)),
                          out_shape=jax.ShapeDtypeStruct(dz.shape, dz.dtype), input_output_aliases={4: 0},
                          compiler_params=_params(("parallel",)), name="kv_combine")(dkc, dkp, dvc, dvp, dz)


def _gate_bwd(d_mix, z, o_ga, y_attn, y_ssm):
    L, D = d_mix.shape
    tc = _pick(math.gcd(D, o_ga), (512, 256, 128))
    tr = _pick(L, (1024, 512, 256, 128))
    nd, gb = D // tc, o_ga // tc

    def body(dm_ref, g_ref, ya_ref, ys_ref, dy_ref, dz_ref):
        dm = dm_ref[...].astype(F32)
        s = _sigmoid(g_ref[...])
        y = jnp.where(pl.program_id(1) < nd, ya_ref[...], ys_ref[...]).astype(F32)
        dy_ref[...] = (dm * s).astype(BF16)
        dz_ref[...] = (dm * y * s * (1.0 - s)).astype(BF16)

    blk = lambda f: pl.BlockSpec((tr, tc), f)
    return pl.pallas_call(
        body,
        grid=(L // tr, 2 * nd),
        in_specs=[blk(lambda i, j: (i, j % nd)), blk(lambda i, j: (i, j + gb)),
                  blk(lambda i, j: (i, jnp.minimum(j, nd - 1))), blk(lambda i, j: (i, jnp.maximum(j - nd, 0)))],
        out_specs=(blk(lambda i, j: (i, j)), blk(lambda i, j: (i, j + gb))),
        out_shape=(jax.ShapeDtypeStruct((L, 2 * D), BF16), jax.ShapeDtypeStruct(z.shape, BF16)),
        compiler_params=_params(("parallel", "arbitrary")),
        name="gate_bwd",
    )(d_mix, z, y_attn, y_ssm)


def _discretise(lr, li, ldt, br, bi):
    dt = jnp.exp(ldt)
    mag = jnp.exp(lr * dt)
    a_re, a_im = mag * jnp.cos(li * dt), mag * jnp.sin(li * dt)
    den = lr * lr + li * li
    nr, ni = a_re - 1.0, a_im
    coef_re = (nr * lr + ni * li) / den
    coef_im = (ni * lr - nr * li) / den
    return a_re, a_im, coef_re * br - coef_im * bi, coef_re * bi + coef_im * br


def _disc_specs(n):
    tr = _pick(n, (512,))
    cs = pl.BlockSpec((tr, 1), lambda i: (i, 0))
    ms = pl.BlockSpec((tr, SSM_GC), lambda i: (i, 0))
    return tr, cs, ms


def _disc_fwd(lr, li, ldt, br, bi):
    n = lr.shape[0]
    tr, cs, ms = _disc_specs(n)

    def body(lr_ref, li_ref, dt_ref, br_ref, bi_ref, o1, o2, o3, o4):
        r = _discretise(lr_ref[...], li_ref[...], dt_ref[...], br_ref[...], bi_ref[...])
        o1[...], o2[...], o3[...], o4[...] = r

    col = jax.ShapeDtypeStruct((n, 1), F32)
    mat = jax.ShapeDtypeStruct((n, SSM_GC), F32)
    return pl.pallas_call(body, grid=(n // tr,), in_specs=[cs, cs, cs, ms, ms], out_specs=(cs, cs, ms, ms),
                          out_shape=(col, col, mat, mat), compiler_params=_params(("parallel",)), name="disc_fwd")(
        lr, li, ldt, br, bi)


def _disc_bwd(lr, li, ldt, br, bi, gar, gai, gbr, gbi):
    n = lr.shape[0]
    tr, cs, ms = _disc_specs(n)

    def body(lr_ref, li_ref, dt_ref, br_ref, bi_ref, gar_ref, gai_ref, gbr_ref, gbi_ref, o_lr, o_li, o_dt, o_br, o_bi):
        _, vjp = jax.vjp(_discretise, lr_ref[...], li_ref[...], dt_ref[...], br_ref[...], bi_ref[...])
        g = vjp((gar_ref[...], gai_ref[...], gbr_ref[...], gbi_ref[...]))
        o_lr[...] = g[0]
        o_li[...] = g[1]
        o_dt[...] = jnp.sum(g[2].reshape(tr // SSM_P, SSM_P, 1), axis=1)
        o_br[...] = g[3]
        o_bi[...] = g[4]

    col = jax.ShapeDtypeStruct((n, 1), F32)
    mat = jax.ShapeDtypeStruct((n, SSM_GC), F32)
    return pl.pallas_call(
        body, grid=(n // tr,), in_specs=[cs, cs, cs, ms, ms, cs, cs, ms, ms],
        out_specs=(cs, cs, pl.BlockSpec((tr // SSM_P, 1), lambda i: (i, 0)), ms, ms),
        out_shape=(col, col, jax.ShapeDtypeStruct((n // SSM_P, 1), F32), mat, mat),
        compiler_params=_params(("parallel",)), name="disc_bwd")(lr, li, ldt, br, bi, gar, gai, gbr, gbi)


def _cpow(ar, ai, nsq):
    for _ in range(nsq):
        ar, ai = ar * ar - ai * ai, 2.0 * ar * ai
    return ar, ai


def _ssm_dims(L):
    tc = min(1024, L)
    seg = tc // SUBLANES
    assert seg & (seg - 1) == 0
    return tc, seg, L // tc, seg.bit_length() - 1


def _tile_rows(i):
    return pl.ds(pl.multiple_of(i * SUBLANES, SUBLANES), SUBLANES)


def _rows_to_segments(src_ref, dst_ref, seg):
    def body(i, _):
        dst_ref[_tile_rows(i), :] = src_ref[pl.ds(i, SUBLANES, stride=seg), :]
        return 0
    lax.fori_loop(0, seg, body, 0, unroll=8)


def _segments_to_rows(src_ref, dst_ref, seg):
    def body(i, _):
        dst_ref[pl.ds(i, SUBLANES, stride=seg), :] = src_ref[_tile_rows(i), :]
        return 0
    lax.fori_loop(0, seg, body, 0, unroll=8)


def _ssm_fwd(z, u_off, br_m, bi_m, cr_m, ci_m, a_re3, a_im3, d_row, comm=None):
    L = z.shape[0]
    ngb = br_m.shape[0]
    tc, seg, nc, nsq = _ssm_dims(L)
    ucol = u_off // LANES

    def body(*refs):
        ins, c_ins, outs, c_outs, scratch, sems = _split_refs(refs, 8, 3, comm)
        u_ref, br_ref, bi_ref, cr_ref, ci_ref, ar_ref, ai_ref, d_ref = ins
        y_ref, xr_ref, xi_ref = outs
        bur, bui, car_r, car_i, ini_r, ini_i, up, ys = scratch
        if comm:
            s = pl.program_id(0) * nc + pl.program_id(1)
            comm.run(c_ins, c_outs, sems, s == 0, s == (ngb * nc) // 2, s == ngb * nc - 1)

        @pl.when(pl.program_id(1) == 0)
        def _():
            car_r[...] = jnp.zeros_like(car_r)
            car_i[...] = jnp.zeros_like(car_i)

        _rows_to_segments(u_ref, up, seg)
        u = up[...]
        pr = _dot(u, br_ref[0])
        pi = _dot(u, bi_ref[0])
        for w in range(NW):
            bur[w] = pr[:, w * LANES:(w + 1) * LANES]
            bui[w] = pi[:, w * LANES:(w + 1) * LANES]
        ar = [jnp.broadcast_to(ar_ref[w], (SUBLANES, LANES)) for w in range(NW)]
        ai = [jnp.broadcast_to(ai_ref[w], (SUBLANES, LANES)) for w in range(NW)]

        def step(i, carry, store):
            xr, xi = carry
            rows = _tile_rows(i)
            nr, ni = [], []
            for w in range(NW):
                r = ar[w] * xr[w] - ai[w] * xi[w] + bur[w, rows, :]
                m = ar[w] * xi[w] + ai[w] * xr[w] + bui[w, rows, :]
                if store:
                    xr_ref[w, rows, :] = r
                    xi_ref[w, rows, :] = m
                nr.append(r)
                ni.append(m)
            return tuple(nr), tuple(ni)

        zero = tuple(jnp.zeros((SUBLANES, LANES), F32) for _ in range(NW))
        er, ei = lax.fori_loop(0, seg, functools.partial(step, store=False), (zero, zero), unroll=2)
        for w in range(NW):
            pr_, pi_ = _cpow(ar[w][0:1], ai[w][0:1], nsq)
            sr, si = car_r[w, 0:1, :], car_i[w, 0:1, :]
            for j in range(SUBLANES):
                ini_r[w, j:j + 1, :] = sr
                ini_i[w, j:j + 1, :] = si
                sr, si = (pr_ * sr - pi_ * si + er[w][j:j + 1], pr_ * si + pi_ * sr + ei[w][j:j + 1])
            car_r[w, 0:1, :] = sr
            car_i[w, 0:1, :] = si
        init = (tuple(ini_r[w] for w in range(NW)), tuple(ini_i[w] for w in range(NW)))
        lax.fori_loop(0, seg, functools.partial(step, store=True), init, unroll=2)
        acc = d_ref[...] * u
        for w in range(NW):
            sl = slice(w * LANES, (w + 1) * LANES)
            acc = acc + _dot(xr_ref[w], cr_ref[0, sl, :]) - _dot(xi_ref[w], ci_ref[0, sl, :])
        ys[...] = acc
        _segments_to_rows(ys, y_ref, seg)

    mat_b = pl.BlockSpec((1, LANES, NW * LANES), lambda b, k: (b, 0, 0))
    mat_c = pl.BlockSpec((1, NW * LANES, LANES), lambda b, k: (b, 0, 0))
    a_spec = pl.BlockSpec((NW, 1, LANES), lambda b, k: (b, 0, 0))
    x_spec = pl.BlockSpec((NW, tc, LANES), lambda b, k: (b, k, 0))
    xs = jax.ShapeDtypeStruct((ngb * NW, L, LANES), F32)
    st = pltpu.VMEM((NW, SUBLANES, LANES), F32)
    c_ins = comm.ins if comm else []
    c_shapes = comm.out_shapes if comm else []
    res = pl.pallas_call(
        body,
        grid=(ngb, nc),
        in_specs=[pl.BlockSpec((tc, LANES), lambda b, k: (k, b + ucol)), mat_b, mat_b, mat_c, mat_c, a_spec, a_spec,
                  pl.BlockSpec((1, LANES), lambda b, k: (0, b))] + [_ANY] * len(c_ins),
        out_specs=tuple([pl.BlockSpec((tc, LANES), lambda b, k: (k, b)), x_spec, x_spec] + [_ANY] * len(c_shapes)),
        out_shape=tuple([jax.ShapeDtypeStruct((L, ngb * LANES), F32), xs, xs] + c_shapes),
        scratch_shapes=[pltpu.VMEM((NW, tc, LANES), F32), pltpu.VMEM((NW, tc, LANES), F32), st, st, st, st,
                        pltpu.VMEM((tc, LANES), F32), pltpu.VMEM((tc, LANES), F32)]
        + (comm.scratch() if comm else []),
        compiler_params=_host_params(("arbitrary", "arbitrary"), comm),
        name="ssm_fwd",
    )(z, br_m, bi_m, cr_m, ci_m, a_re3, a_im3, d_row, *c_ins)
    return (res[:3], list(res[3:])) if comm else res


def _ssm_bwd(dy, z, u_off, xs_r, xs_i, br_m, bi_m, cr_m, ci_m, a_re3, a_im3, d_row, dz):
    L = z.shape[0]
    ngb = br_m.shape[0]
    tc, seg, nc, nsq = _ssm_dims(L)
    ucol = u_off // LANES

    def body(dy_ref, u_ref, xr_ref, xi_ref, br_ref, bi_ref, cr_ref, ci_ref, ar_ref, ai_ref, d_ref, dz_in,
             du_ref, gd_ref, gar_ref, gai_ref, gbr_ref, gbi_ref, gcr_ref, gci_ref,
             gr_s, gi_s, car_r, car_i, ini_r, ini_i, acc_r, acc_i, dyp, up, dus, dun):
        k = pl.program_id(1)

        @pl.when(k == 0)
        def _():
            for ref in (car_r, car_i, acc_r, acc_i, gd_ref, gbr_ref, gbi_ref, gcr_ref, gci_ref):
                ref[...] = jnp.zeros_like(ref)

        _rows_to_segments(dy_ref, dyp, seg)
        _rows_to_segments(u_ref, up, seg)
        dy_v = dyp[...]
        u = up[...]
        g_re = _dot_nt(dy_v, cr_ref[0])
        g_im = -_dot_nt(dy_v, ci_ref[0])
        for w in range(NW):
            gr_s[w] = g_re[:, w * LANES:(w + 1) * LANES]
            gi_s[w] = g_im[:, w * LANES:(w + 1) * LANES]
        ar = [jnp.broadcast_to(ar_ref[w], (SUBLANES, LANES)) for w in range(NW)]
        ai = [jnp.broadcast_to(ai_ref[w], (SUBLANES, LANES)) for w in range(NW)]

        def step1(ii, carry):
            xr, xi = carry
            rows = _tile_rows(seg - 1 - ii)
            nr = tuple(ar[w] * xr[w] + ai[w] * xi[w] + gr_s[w, rows, :] for w in range(NW))
            ni = tuple(ar[w] * xi[w] - ai[w] * xr[w] + gi_s[w, rows, :] for w in range(NW))
            return nr, ni

        zero = tuple(jnp.zeros((SUBLANES, LANES), F32) for _ in range(NW))
        er, ei = lax.fori_loop(0, seg, step1, (zero, zero), unroll=2)
        for w in range(NW):
            pr_, pi_ = _cpow(ar[w][0:1], ai[w][0:1], nsq)
            sr, si = car_r[w, 0:1, :], car_i[w, 0:1, :]
            for j in reversed(range(SUBLANES)):
                ini_r[w, j:j + 1, :] = sr
                ini_i[w, j:j + 1, :] = si
                sr, si = (pr_ * sr + pi_ * si + er[w][j:j + 1], pr_ * si - pi_ * sr + ei[w][j:j + 1])
            car_r[w, 0:1, :] = sr
            car_i[w, 0:1, :] = si

        def step2(ii, carry):
            gxr, gxi, acr, aci = carry
            rows = _tile_rows(seg - 1 - ii)
            nr, ni, nar, nai = [], [], [], []
            for w in range(NW):
                xr_t, xi_t = xr_ref[w, rows, :], xi_ref[w, rows, :]
                nar.append(acr[w] + gxr[w] * xr_t + gxi[w] * xi_t)
                nai.append(aci[w] + gxi[w] * xr_t - gxr[w] * xi_t)
                r = ar[w] * gxr[w] + ai[w] * gxi[w] + gr_s[w, rows, :]
                m = ar[w] * gxi[w] - ai[w] * gxr[w] + gi_s[w, rows, :]
                gr_s[w, rows, :] = r
                gi_s[w, rows, :] = m
                nr.append(r)
                ni.append(m)
            return tuple(nr), tuple(ni), tuple(nar), tuple(nai)

        init = (tuple(ini_r[w] for w in range(NW)), tuple(ini_i[w] for w in range(NW)),
                tuple(acc_r[w] for w in range(NW)), tuple(acc_i[w] for w in range(NW)))
        _, _, acr, aci = lax.fori_loop(0, seg, step2, init, unroll=2)
        du = d_ref[...] * dy_v
        for w in range(NW):
            sl = slice(w * LANES, (w + 1) * LANES)
            acc_r[w] = acr[w]
            acc_i[w] = aci[w]
            gxr_w, gxi_w = gr_s[w], gi_s[w]
            du = du + _dot_nt(gxr_w, br_ref[0, :, sl]) + _dot_nt(gxi_w, bi_ref[0, :, sl])
            gbr_ref[0, :, sl] += _dot_tn(u, gxr_w)
            gbi_ref[0, :, sl] += _dot_tn(u, gxi_w)
            gcr_ref[0, sl, :] += _dot_tn(xr_ref[w], dy_v)
            gci_ref[0, sl, :] += _dot_tn(-xi_ref[w], dy_v)
        dus[...] = du
        _segments_to_rows(dus, dun, seg)
        du_ref[...] = dun[...].astype(BF16)
        gd_ref[...] += jnp.sum(dy_v * u, axis=0, keepdims=True)

        @pl.when(k == nc - 1)
        def _():
            for w in range(NW):
                gar_ref[w] = jnp.sum(acc_r[w], axis=0, keepdims=True)
                gai_ref[w] = jnp.sum(acc_i[w], axis=0, keepdims=True)

    rk = lambda k: nc - 1 - k
    mat_b = pl.BlockSpec((1, LANES, NW * LANES), lambda b, k: (b, 0, 0))
    mat_c = pl.BlockSpec((1, NW * LANES, LANES), lambda b, k: (b, 0, 0))
    a_spec = pl.BlockSpec((NW, 1, LANES), lambda b, k: (b, 0, 0))
    x_spec = pl.BlockSpec((NW, tc, LANES), lambda b, k: (b, rk(k), 0))
    st = pltpu.VMEM((NW, SUBLANES, LANES), F32)
    big = pltpu.VMEM((NW, tc, LANES), F32)
    return pl.pallas_call(
        body,
        grid=(ngb, nc),
        in_specs=[pl.BlockSpec((tc, LANES), lambda b, k: (rk(k), b)),
                  pl.BlockSpec((tc, LANES), lambda b, k: (rk(k), b + ucol)),
                  x_spec, x_spec, mat_b, mat_b, mat_c, mat_c, a_spec, a_spec,
                  pl.BlockSpec((1, LANES), lambda b, k: (0, b)), _ANY],
        out_specs=(pl.BlockSpec((tc, LANES), lambda b, k: (rk(k), b + ucol)),
                   pl.BlockSpec((1, LANES), lambda b, k: (0, b)), a_spec, a_spec, mat_b, mat_b, mat_c, mat_c),
        out_shape=(jax.ShapeDtypeStruct(dz.shape, dz.dtype),
                   jax.ShapeDtypeStruct((1, ngb * LANES), F32),
                   jax.ShapeDtypeStruct((ngb * NW, 1, LANES), F32), jax.ShapeDtypeStruct((ngb * NW, 1, LANES), F32),
                   jax.ShapeDtypeStruct(br_m.shape, F32), jax.ShapeDtypeStruct(br_m.shape, F32),
                   jax.ShapeDtypeStruct(cr_m.shape, F32), jax.ShapeDtypeStruct(cr_m.shape, F32)),
        scratch_shapes=[big, big, st, st, st, st, st, st] + [pltpu.VMEM((tc, LANES), F32)] * 4,
        input_output_aliases={11: 0},
        compiler_params=_params(("arbitrary", "arbitrary")),
        name="ssm_bwd",
    )(dy, z, xs_r, xs_i, br_m, bi_m, cr_m, ci_m, a_re3, a_im3, d_row, dz)


def _block_diag_in(bb, ngb):
    t = bb.reshape(ngb, GROUPS_PER_BLOCK, SSM_P, SSM_GC).transpose(0, 1, 3, 2)
    eye = jnp.eye(GROUPS_PER_BLOCK, dtype=F32)
    m = t[:, :, :, None, :] * eye[None, :, None, :, None]
    return m.reshape(ngb, GROUPS_PER_BLOCK * SSM_GC, GROUPS_PER_BLOCK * SSM_P)


def _block_diag_out(c, ngb):
    t = c.reshape(ngb, GROUPS_PER_BLOCK, SSM_GC, SSM_P).transpose(0, 1, 3, 2)
    eye = jnp.eye(GROUPS_PER_BLOCK, dtype=F32)
    m = t[:, :, :, None, :] * eye[None, :, None, :, None]
    return m.reshape(ngb, GROUPS_PER_BLOCK * SSM_P, GROUPS_PER_BLOCK * SSM_GC)


def _diag_in(m, ngb):
    m5 = m.reshape(ngb, GROUPS_PER_BLOCK, SSM_GC, GROUPS_PER_BLOCK, SSM_P)
    d = jnp.diagonal(m5, axis1=1, axis2=3)
    return d.transpose(0, 3, 2, 1).reshape(ngb * GROUPS_PER_BLOCK * SSM_P, SSM_GC)


def _diag_out(m, ngb):
    m5 = m.reshape(ngb, GROUPS_PER_BLOCK, SSM_P, GROUPS_PER_BLOCK, SSM_GC)
    d = jnp.diagonal(m5, axis1=1, axis2=3)
    return d.transpose(0, 3, 2, 1).reshape(ngb * GROUPS_PER_BLOCK, SSM_GC, SSM_P)


_ANY = pl.BlockSpec(memory_space=pl.ANY)


class _Comm:
    def __init__(self, ins, out_shapes, n_sem, start, mid, finish):
        self.ins, self.out_shapes, self.n_sem = list(ins), list(out_shapes), n_sem
        self.start, self.mid, self.finish = start, mid, finish

    def scratch(self):
        return [pltpu.SemaphoreType.DMA((self.n_sem,)), pltpu.SemaphoreType.DMA((self.n_sem,)),
                pltpu.SemaphoreType.DMA((len(self.ins),))]

    def run(self, in_refs, out_refs, sems, first, mid, last):
        send, recv, local = sems

        @pl.when(first)
        def _():
            self.start(in_refs, out_refs, send, recv, local)

        @pl.when(mid)
        def _():
            self.mid(in_refs, out_refs, send, recv, local)

        @pl.when(last)
        def _():
            self.finish(in_refs, out_refs, send, recv, local)


def _split_refs(refs, n_in, n_out, comm):
    ci = len(comm.ins) if comm else 0
    co = len(comm.out_shapes) if comm else 0
    ins = refs[:n_in]
    c_ins = refs[n_in:n_in + ci]
    outs = refs[n_in + ci:n_in + ci + n_out]
    c_outs = refs[n_in + ci + n_out:n_in + ci + n_out + co]
    rest = refs[n_in + ci + n_out + co:]
    if comm:
        return ins, c_ins, outs, c_outs, rest[:-3], rest[-3:]
    return ins, c_ins, outs, c_outs, rest, ()


def _run_comm(comm, name):
    ni, no = len(comm.ins), len(comm.out_shapes)

    def body(*refs):
        args = (refs[:ni], refs[ni:ni + no]) + tuple(refs[ni + no:])
        comm.start(*args)
        comm.mid(*args)
        comm.finish(*args)

    return pl.pallas_call(
        body,
        in_specs=[_ANY] * ni,
        out_specs=tuple([_ANY] * no),
        out_shape=tuple(comm.out_shapes),
        scratch_shapes=comm.scratch(),
        compiler_params=pltpu.CompilerParams(has_side_effects=True),
        name=name,
    )(*comm.ins)


def _ag_comm(shards):
    n = len(shards)

    def env(ins, outs, send_sems, recv_sems):
        x, y, c = lax.axis_index("x"), lax.axis_index("y"), lax.axis_index("c")
        me, sibling = (x, y, c), (x, y, 1 - c)
        chips = [(1 - x, y), (x, 1 - y), (1 - x, 1 - y)]

        def copy(a, k, block, to, src=None):
            s = 4 * block[0] + 2 * block[1] + block[2]
            return pltpu.make_async_remote_copy(
                src_ref=outs[a].at[s] if src is None else src, dst_ref=outs[a].at[s],
                send_sem=send_sems.at[7 * a + k], recv_sem=recv_sems.at[7 * a + k],
                device_id=to, device_id_type=MESH)

        return c, me, sibling, chips, copy

    def own(ins, outs, local_sems, a):
        x, y, c = lax.axis_index("x"), lax.axis_index("y"), lax.axis_index("c")
        return pltpu.make_async_copy(ins[a], outs[a].at[4 * x + 2 * y + c], local_sems.at[a])

    def first_sends(ins, copy, me, sibling, chips, c, a):
        return [copy(a, 0, me, sibling, src=ins[a])] + [
            copy(a, 1 + j, me, (*chip, c), src=ins[a]) for j, chip in enumerate(chips)]

    def start(ins, outs, send_sems, recv_sems, local_sems):
        c, me, sibling, chips, copy = env(ins, outs, send_sems, recv_sems)
        for a in range(n):
            own(ins, outs, local_sems, a).start()
        for a in range(n):
            for cp in first_sends(ins, copy, me, sibling, chips, c, a):
                cp.start()

    def mid(ins, outs, send_sems, recv_sems, local_sems):
        c, me, sibling, chips, copy = env(ins, outs, send_sems, recv_sems)
        for a in range(n):
            for j, chip in enumerate(chips):
                copy(a, 1 + j, (*chip, c), me).wait_recv()
                copy(a, 4 + j, (*chip, c), sibling).start()

    def finish(ins, outs, send_sems, recv_sems, local_sems):
        c, me, sibling, chips, copy = env(ins, outs, send_sems, recv_sems)
        for a in range(n):
            copy(a, 0, sibling, me).wait_recv()
            for j, chip in enumerate(chips):
                copy(a, 4 + j, (*chip, 1 - c), me).wait_recv()
        for a in range(n):
            for cp in first_sends(ins, copy, me, sibling, chips, c, a):
                cp.wait_send()
            for j, chip in enumerate(chips):
                copy(a, 4 + j, (*chip, c), sibling).wait_send()
            own(ins, outs, local_sems, a).wait()

    return _Comm(shards, [jax.ShapeDtypeStruct((N_DEV,) + s.shape, s.dtype) for s in shards], 7 * n,
                 start, mid, finish)


def _sibling_comm(parts):
    n = len(parts)

    def copies(ins, outs, send_sems, recv_sems):
        x, y, c = lax.axis_index("x"), lax.axis_index("y"), lax.axis_index("c")
        return [pltpu.make_async_remote_copy(
            src_ref=ins[a].at[2 * q + (1 - c)], dst_ref=outs[a].at[q],
            send_sem=send_sems.at[4 * a + q], recv_sem=recv_sems.at[4 * a + q],
            device_id=(x, y, 1 - c), device_id_type=MESH) for a in range(n) for q in range(4)]

    def start(ins, outs, send_sems, recv_sems, local_sems):
        for cp in copies(ins, outs, send_sems, recv_sems):
            cp.start()

    def mid(ins, outs, send_sems, recv_sems, local_sems):
        pass

    def finish(ins, outs, send_sems, recv_sems, local_sems):
        for cp in copies(ins, outs, send_sems, recv_sems):
            cp.wait()

    return _Comm(parts, [jax.ShapeDtypeStruct((4,) + p.shape[1:], p.dtype) for p in parts], 4 * n,
                 start, mid, finish)


def _chips_comm(parts):
    n = len(parts)

    def copies(ins, outs, send_sems, recv_sems):
        x, y, c = lax.axis_index("x"), lax.axis_index("y"), lax.axis_index("c")
        chips = [(1 - x, y), (x, 1 - y), (1 - x, 1 - y)]
        return [pltpu.make_async_remote_copy(
            src_ref=ins[a].at[2 * px + py], dst_ref=outs[a].at[j],
            send_sem=send_sems.at[3 * a + j], recv_sem=recv_sems.at[3 * a + j],
            device_id=(px, py, c), device_id_type=MESH) for a in range(n) for j, (px, py) in enumerate(chips)]

    def start(ins, outs, send_sems, recv_sems, local_sems):
        for cp in copies(ins, outs, send_sems, recv_sems):
            cp.start()

    def mid(ins, outs, send_sems, recv_sems, local_sems):
        pass

    def finish(ins, outs, send_sems, recv_sems, local_sems):
        for cp in copies(ins, outs, send_sems, recv_sems):
            cp.wait()

    return _Comm(parts, [jax.ShapeDtypeStruct((3,) + p.shape[1:], p.dtype) for p in parts], 3 * n,
                 start, mid, finish)


def _sibling_add(part, recv, name):
    _, R, C = part.shape
    tr = _pick(R, (256, 128, 80))
    c = lax.axis_index("c")

    def body(c_ref, p_ref, r_ref, o_ref, o16_ref):
        t = p_ref[...] + r_ref[...]
        o_ref[...] = t
        o16_ref[...] = t.astype(BF16)

    blk = pl.BlockSpec((1, tr, C), lambda q, i, c_ref: (q, i, 0))
    return pl.pallas_call(
        body,
        grid_spec=pltpu.PrefetchScalarGridSpec(
            num_scalar_prefetch=1,
            grid=(4, R // tr),
            in_specs=[pl.BlockSpec((1, tr, C), lambda q, i, c_ref: (2 * q + c_ref[0], i, 0)), blk],
            out_specs=(blk, blk),
        ),
        out_shape=(jax.ShapeDtypeStruct((4, R, C), F32), jax.ShapeDtypeStruct((4, R, C), BF16)),
        compiler_params=_params(("parallel", "parallel")),
        name=name,
    )(jnp.reshape(c, (1,)).astype(jnp.int32), part, recv)


def _adamw(w, g, m, v):
    m = ADAM_B1 * m + (1.0 - ADAM_B1) * g
    v = ADAM_B2 * v + (1.0 - ADAM_B2) * (g * g)
    m_hat = m / (1.0 - ADAM_B1 ** ADAM_STEP)
    v_hat = v / (1.0 - ADAM_B2 ** ADAM_STEP)
    delta = -ADAM_LR * (m_hat / (jnp.sqrt(v_hat) + ADAM_EPS) + ADAM_WD * w)
    return delta, m, v


def _adam_big(t, recv, w, m, v, name):
    _, R, C = t.shape
    tr = _pick(R, (256, 128))
    chip = 2 * lax.axis_index("x") + lax.axis_index("y")

    def body(q_ref, t_ref, r_ref, w_ref, m_ref, v_ref, g_ref, d_ref, nm_ref, nv_ref):
        g = t_ref[0] + r_ref[0].astype(F32) + r_ref[1].astype(F32) + r_ref[2].astype(F32)
        g_ref[...] = g
        d_ref[...], nm_ref[...], nv_ref[...] = _adamw(w_ref[...], g, m_ref[...], v_ref[...])

    blk = pl.BlockSpec((tr, C), lambda i, q_ref: (i, 0))
    o = jax.ShapeDtypeStruct((R, C), F32)
    return pl.pallas_call(
        body,
        grid_spec=pltpu.PrefetchScalarGridSpec(
            num_scalar_prefetch=1,
            grid=(R // tr,),
            in_specs=[pl.BlockSpec((1, tr, C), lambda i, q_ref: (q_ref[0], i, 0)),
                      pl.BlockSpec((3, tr, C), lambda i, q_ref: (0, i, 0)), blk, blk, blk],
            out_specs=(blk, blk, blk, blk),
        ),
        out_shape=(o, o, o, o),
        compiler_params=_params(("parallel",)),
        name=name,
    )(jnp.reshape(chip, (1,)).astype(jnp.int32), t, recv, w, m, v)


def _reduce_big(t, recv, name):
    _, R, C = t.shape
    tr = _pick(R, (256, 128, 80))
    chip = 2 * lax.axis_index("x") + lax.axis_index("y")

    def body(q_ref, t_ref, r_ref, g_ref):
        g_ref[...] = t_ref[0] + r_ref[0].astype(F32) + r_ref[1].astype(F32) + r_ref[2].astype(F32)

    return pl.pallas_call(
        body,
        grid_spec=pltpu.PrefetchScalarGridSpec(
            num_scalar_prefetch=1,
            grid=(R // tr,),
            in_specs=[pl.BlockSpec((1, tr, C), lambda i, q_ref: (q_ref[0], i, 0)),
                      pl.BlockSpec((3, tr, C), lambda i, q_ref: (0, i, 0))],
            out_specs=pl.BlockSpec((tr, C), lambda i, q_ref: (i, 0)),
        ),
        out_shape=jax.ShapeDtypeStruct((R, C), F32),
        compiler_params=_params(("parallel",)),
        name=name,
    )(jnp.reshape(chip, (1,)).astype(jnp.int32), t, recv)


def _adam_only(g, w, m, v, name):
    R, C = g.shape
    tr = _pick(R, (256, 128))

    def body(g_ref, w_ref, m_ref, v_ref, d_ref, nm_ref, nv_ref):
        d_ref[...], nm_ref[...], nv_ref[...] = _adamw(w_ref[...], g_ref[...], m_ref[...], v_ref[...])

    blk = pl.BlockSpec((tr, C), lambda i: (i, 0))
    o = jax.ShapeDtypeStruct((R, C), F32)
    return pl.pallas_call(body, grid=(R // tr,), in_specs=[blk] * 4, out_specs=(blk, blk, blk),
                          out_shape=(o, o, o), compiler_params=_params(("parallel",)), name=name)(g, w, m, v)


def _gather8_comm(gbuf):
    def copies(ins, outs, send_sems, recv_sems):
        x, y, c = lax.axis_index("x"), lax.axis_index("y"), lax.axis_index("c")
        me = 4 * x + 2 * y + c
        out = []
        for k in range(1, N_DEV):
            fx, fy, fc = (k >> 2) & 1, (k >> 1) & 1, k & 1
            px, py, pc = x + fx - 2 * x * fx, y + fy - 2 * y * fy, c + fc - 2 * c * fc
            send = pltpu.make_async_remote_copy(
                src_ref=ins[0], dst_ref=outs[0].at[me], send_sem=send_sems.at[k - 1], recv_sem=recv_sems.at[k - 1],
                device_id=(px, py, pc), device_id_type=MESH)
            recv = pltpu.make_async_remote_copy(
                src_ref=ins[0], dst_ref=outs[0].at[4 * px + 2 * py + pc], send_sem=send_sems.at[k - 1],
                recv_sem=recv_sems.at[k - 1], device_id=(px, py, pc), device_id_type=MESH)
            out.append((send, recv))
        return me, out

    def start(ins, outs, send_sems, recv_sems, local_sems):
        me, cps = copies(ins, outs, send_sems, recv_sems)
        pltpu.make_async_copy(ins[0], outs[0].at[me], local_sems.at[0]).start()
        for send, _ in cps:
            send.start()

    def mid(ins, outs, send_sems, recv_sems, local_sems):
        pass

    def finish(ins, outs, send_sems, recv_sems, local_sems):
        me, cps = copies(ins, outs, send_sems, recv_sems)
        for send, recv in cps:
            recv.wait_recv()
            send.wait_send()
        pltpu.make_async_copy(ins[0], outs[0].at[me], local_sems.at[0]).wait()

    return _Comm([gbuf], [jax.ShapeDtypeStruct((N_DEV,) + gbuf.shape, gbuf.dtype)], N_DEV - 1, start, mid, finish)


def _sum8_adam(slots, wbuf, mbuf, vbuf, name):
    R = wbuf.shape[0]

    def body(s_ref, w_ref, m_ref, v_ref, gs_ref, d_ref, nm_ref, nv_ref):
        g = s_ref[0]
        for s in range(1, N_DEV):
            g = g + s_ref[s]
        gs_ref[...] = g
        d_ref[...], nm_ref[...], nv_ref[...] = _adamw(w_ref[...], g, m_ref[...], v_ref[...])

    o = jax.ShapeDtypeStruct((R, LANES), F32)
    return pl.pallas_call(body, out_shape=(o, o, o, o),
                          compiler_params=pltpu.CompilerParams(vmem_limit_bytes=VMEM_LIMIT), name=name)(
        slots, wbuf, mbuf, vbuf)


def _pack(items):
    rows, spans, r0 = [], [], 0
    for a in items:
        n = a.size
        nr = -(-n // LANES)
        rows.append(jnp.pad(a.reshape(-1).astype(F32), (0, nr * LANES - n)).reshape(nr, LANES))
        spans.append((r0, nr, a.shape))
        r0 += nr
    pad = -r0 % SUBLANES
    if pad:
        rows.append(jnp.zeros((pad, LANES), F32))
    return jnp.concatenate(rows, axis=0), spans


def _unpack(buf, spans):
    return [buf[r0:r0 + nr].reshape(-1)[:math.prod(shape)].reshape(shape) for r0, nr, shape in spans]


def kernel(x, norm_mix_pre, norm_mix_post, norm_mlp_pre, norm_mlp_post, w_in, sinks, lam_re, lam_im, log_dt, b_re, b_im, c_re, c_im, d_skip, w_glu, w_branch, w_out, w_up, w_down, loss_target, m_norm_mix_pre, m_norm_mix_post, m_norm_mlp_pre, m_norm_mlp_post, m_w_in, m_sinks, m_lam_re, m_lam_im, m_log_dt, m_b_re, m_b_im, m_c_re, m_c_im, m_d_skip, m_w_glu, m_w_branch, m_w_out, m_w_up, m_w_down, v_norm_mix_pre, v_norm_mix_post, v_norm_mlp_pre, v_norm_mlp_post, v_w_in, v_sinks, v_lam_re, v_lam_im, v_log_dt, v_b_re, v_b_im, v_c_re, v_c_im, v_d_skip, v_w_glu, v_w_branch, v_w_out, v_w_up, v_w_down):
    _, L, D = x.shape
    xs = x[0]
    tgt = loss_target[0]
    ssm_w = D // 2
    n_groups = ssm_w // SSM_GC
    ngb = n_groups // GROUPS_PER_BLOCK
    n_state = n_groups * SSM_P
    d_ff = w_up.shape[2] * N_DEV
    o_k, o_v, o_u = Q_W, Q_W + KV_W, Q_W + 2 * KV_W
    o_ga = o_u + ssm_w
    o_gs = o_ga + D

    big = {"w_in": w_in[0], "w_glu": w_glu[0], "w_branch": w_branch[0], "w_out": w_out[0],
           "w_up": w_up[0], "w_down": w_down[0]}
    col_sharded = ("w_in", "w_glu", "w_up")
    names = list(big)
    shard16 = {k: (big[k].T if k in col_sharded else big[k]).astype(BF16) for k in names}
    full = {}

    def gathered(keys, arrays):
        for k, g in zip(keys, arrays):
            _, r, c = g.shape
            full[k] = g.reshape(N_DEV * r, c)

    def by_owner(g):
        return g.reshape(N_DEV, g.shape[0] // N_DEV, g.shape[1])

    col = lambda a: a.reshape(n_state, 1)
    lr_c, li_c = col(lam_re[0]), col(lam_im[0])
    ldt_c = jnp.repeat(log_dt[0], SSM_P).reshape(n_state, 1)
    b_re_c, b_im_c = b_re[0].reshape(n_state, SSM_GC), b_im[0].reshape(n_state, SSM_GC)
    a_re, a_im, bb_re, bb_im = _disc_fwd(lr_c, li_c, ldt_c, b_re_c, b_im_c)
    a_re3 = a_re.reshape(n_state // LANES, 1, LANES)
    a_im3 = a_im.reshape(n_state // LANES, 1, LANES)
    br_m = _block_diag_in(bb_re, ngb).astype(BF16)
    bi_m = _block_diag_in(bb_im, ngb).astype(BF16)
    cr_m = _block_diag_out(c_re[0], ngb).astype(BF16)
    ci_m = _block_diag_out(c_im[0], ngb).astype(BF16)
    d_row = d_skip[0].reshape(1, ssm_w)
    cos_t, sin_t = _rope_tables(L)

    h, g1 = _rms_pre(xs, norm_mix_pre, _ag_comm([shard16["w_in"]]))
    gathered(["w_in"], g1)
    z, g3 = _mm(h, full["w_in"], mode="nt", name="mm_z",
                comm=_ag_comm([shard16[k] for k in ("w_glu", "w_branch", "w_out")]))
    gathered(["w_glu", "w_branch", "w_out"], g3)
    wb_a, wb_s = full["w_branch"][:Q_W], full["w_branch"][Q_W:]
    o_attn = _attn_fwd(z, cos_t, sin_t, sinks)
    (y_pre, xs_r, xs_i), g1 = _ssm_fwd(z, o_u, br_m, bi_m, cr_m, ci_m, a_re3, a_im3, d_row,
                                       comm=_ag_comm([shard16["w_up"]]))
    gathered(["w_up"], g1)
    gy = _ew(lambda y: (_gelu(y),), [(y_pre, 0)], (BF16,), rows=L, ncols=ssm_w, name="gelu")
    zg = _mm(gy, full["w_glu"], mode="nt", name="mm_zg")
    o_ssm = _ew(lambda a, b: (a * _sigmoid(b),), [(zg, 0), (zg, ssm_w)], (BF16,), rows=L, ncols=ssm_w, name="glu")
    y_attn = _mm(o_attn, wb_a, mode="nn", name="mm_y_attn", out_dtypes=(BF16,))
    y_ssm = _mm(o_ssm, wb_s, mode="nn", name="mm_y_ssm", out_dtypes=(BF16,))
    mix = _ew(lambda ga, gs, ya, ys: (_sigmoid(ga) * ya + _sigmoid(gs) * ys,),
              [(z, o_ga), (z, o_gs), (y_attn, 0), (y_ssm, 0)], (BF16,), rows=L, ncols=D, name="mix")
    mixed = _mm(mix, full["w_out"], mode="nn", name="mm_mixed")
    x1, h2 = _post_pre(xs, mixed, norm_mix_post, norm_mlp_pre)

    def relu_sq(acc):
        a = jnp.maximum(acc, 0.0)
        return a, a * a

    (act, act2), g1 = _mm(h2, full["w_up"], mode="nt", name="mm_up", out_dtypes=(BF16, BF16), epi=relu_sq,
                          comm=_ag_comm([shard16["w_down"]]))
    gathered(["w_down"], g1)
    dn = _mm(act2, full["w_down"], mode="nn", name="mm_down")
    dx2, d_dn, dg_mlp_post, loss_part = _loss_bwd(x1, dn, norm_mlp_post, tgt)

    d_pre = _mm(d_dn, full["w_down"], mode="nt", name="mm_d_act", out_dtypes=(BF16,),
                epi=lambda acc, a: (acc * (2.0 * a.astype(F32)),), extras=(act,))
    gw_down = _mm(act2, d_dn, mode="tn", name="mm_gw_down")
    p_down = by_owner(gw_down)
    dh2, (sib_down,) = _mm(d_pre, full["w_up"], mode="nn", name="mm_dh2", out_dtypes=(BF16,),
                           comm=_sibling_comm([p_down]))
    t_down, t16_down = _sibling_add(p_down, sib_down, "rs_add_w_down")
    gw_up, (chips_down,) = _mm(d_pre, h2, mode="tn", name="mm_gw_up", comm=_chips_comm([t16_down]))
    p_up = by_owner(gw_up)
    dx1, d_mixed, dg_mlp_pre, dg_mix_post = _norm_bwd_pair(x1, dh2, dx2, mixed, norm_mlp_pre, norm_mix_post)
    d_mix, (sib_up,) = _mm(d_mixed, full["w_out"], mode="nt", name="mm_d_mix", out_dtypes=(BF16,),
                           comm=_sibling_comm([p_up]))
    t_up, t16_up = _sibling_add(p_up, sib_up, "rs_add_w_up")
    gw_out = _mm(mix, d_mixed, mode="tn", name="mm_gw_out")
    d_y2, dz = _gate_bwd(d_mix, z, o_ga, y_attn, y_ssm)
    d_o_attn = _mm(d_y2, wb_a, mode="nt", name="mm_d_o_attn", a_win=(0, D))
    gwb_a = _mm(o_attn, d_y2, mode="tn", name="mm_gwb_a", b_win=(0, D))
    d_o_ssm = _mm(d_y2, wb_s, mode="nt", name="mm_d_o_ssm", a_win=(D, D))
    gwb_s = _mm(o_ssm, d_y2, mode="tn", name="mm_gwb_s", b_win=(D, D))

    def glu_bwd(do, a, b):
        s = _sigmoid(b)
        return do * s, do * a * s * (1.0 - s)

    d_zg_a, d_zg_b = _ew(glu_bwd, [(d_o_ssm, 0), (zg, 0), (zg, ssm_w)], (BF16, BF16), rows=L, ncols=ssm_w, name="glu_bwd")
    d_zg = jnp.concatenate([d_zg_a, d_zg_b], axis=1)
    dy_pre = _mm(d_zg, full["w_glu"], mode="nn", name="mm_d_gy",
                 epi=lambda acc, y: (acc * _gelu_grad(y),), extras=(y_pre,))
    gw_glu = _mm(d_zg, gy, mode="tn", name="mm_gw_glu")
    mids = ["w_glu", "w_branch", "w_out"]
    p_mid = [by_owner(gw_glu), by_owner(jnp.concatenate([gwb_a, gwb_s], axis=0)), by_owner(gw_out)]
    sib_mid = _run_comm(_sibling_comm(p_mid), "rs_sibling_mid")
    t_mid = [_sibling_add(p, r, "rs_add_" + k) for k, p, r in zip(mids, p_mid, sib_mid)]
    (dz, g_dskip, g_ar3, g_ai3, g_br_m, g_bi_m, g_cr_m, g_ci_m) = _ssm_bwd(
        dy_pre, z, o_u, xs_r, xs_i, br_m, bi_m, cr_m, ci_m, a_re3, a_im3, d_row, dz)
    g_lr, g_li, g_ldt, g_b_re, g_b_im = _disc_bwd(
        lr_c, li_c, ldt_c, b_re_c, b_im_c, g_ar3.reshape(n_state, 1), g_ai3.reshape(n_state, 1),
        _diag_in(g_br_m, ngb), _diag_in(g_bi_m, ngb))
    (dz, dkc, dkp, dvc, dvp, dsink_rows), chips_a = _attn_bwd(
        z, d_o_attn, cos_t, sin_t, sinks, dz, comm=_chips_comm([t16_up] + [t16 for _, t16 in t_mid]))
    chips_up, chips_mid = chips_a[0], chips_a[1:]
    dz = _kv_combine(dkc, dkp, dvc, dvp, dz)
    dsink = jnp.stack([dsink_rows[:, 0], dsink_rows[:, HEAD_DIM]], axis=1).reshape(1, N_Q_HEADS)
    small_names = ["norm_mix_post", "norm_mlp_pre", "norm_mlp_post", "sinks", "lam_re", "lam_im",
                   "log_dt", "b_re", "b_im", "c_re", "c_im", "d_skip"]
    small_g = [dg_mix_post, dg_mlp_pre, dg_mlp_post, dsink,
               g_lr.reshape(lam_re.shape), g_li.reshape(lam_im.shape), g_ldt.reshape(log_dt.shape),
               g_b_re.reshape(b_re.shape), g_b_im.reshape(b_im.shape),
               _diag_out(g_cr_m, ngb).reshape(c_re.shape), _diag_out(g_ci_m, ngb).reshape(c_im.shape),
               g_dskip.reshape(d_skip.shape)]
    gbuf, spans = _pack(small_g + [loss_part])
    gw_in, (small_slots,) = _mm(dz, h, mode="tn", name="mm_gw_in", comm=_gather8_comm(gbuf))
    p_in = by_owner(gw_in)
    (sib_in,) = _run_comm(_sibling_comm([p_in]), "rs_sibling_w_in")
    t_in, t16_in = _sibling_add(p_in, sib_in, "rs_add_w_in")
    dh, (chips_in,) = _mm(dz, full["w_in"], mode="nn", name="mm_dh", out_dtypes=(BF16,),
                          comm=_chips_comm([t16_in]))
    grad_x, dg_mix_pre = _final_bwd(xs, dh, dx1, norm_mix_pre, None)

    reduced = {"w_in": (t_in, chips_in), "w_up": (t_up, chips_up), "w_down": (t_down, chips_down)}
    for k, (t32, _), r in zip(mids, t_mid, chips_mid):
        reduced[k] = (t32, r)
    moments = {"w_in": (m_w_in, v_w_in), "w_glu": (m_w_glu, v_w_glu), "w_branch": (m_w_branch, v_w_branch),
               "w_out": (m_w_out, v_w_out), "w_up": (m_w_up, v_w_up), "w_down": (m_w_down, v_w_down)}
    big_out = {}
    for k in names:
        t, r = reduced[k]
        mm_, vv_ = moments[k]
        if k in col_sharded:
            g = _reduce_big(t, r, "reduce_" + k).T
            big_out[k] = [o[None] for o in (g,) + tuple(_adam_only(g, big[k], mm_[0], vv_[0], "adam_" + k))]
        else:
            big_out[k] = [o[None] for o in _adam_big(t, r, big[k], mm_[0], vv_[0], "adam_" + k)]

    small_w = [norm_mix_post, norm_mlp_pre, norm_mlp_post, sinks, lam_re, lam_im, log_dt,
               b_re, b_im, c_re, c_im, d_skip]
    small_m = [m_norm_mix_post, m_norm_mlp_pre, m_norm_mlp_post, m_sinks, m_lam_re, m_lam_im,
               m_log_dt, m_b_re, m_b_im, m_c_re, m_c_im, m_d_skip]
    small_v = [v_norm_mix_post, v_norm_mlp_pre, v_norm_mlp_post, v_sinks, v_lam_re, v_lam_im,
               v_log_dt, v_b_re, v_b_im, v_c_re, v_c_im, v_d_skip]
    zero1 = jnp.zeros((1, 1), F32)
    wbuf, _ = _pack(small_w + [zero1])
    mbuf, _ = _pack(small_m + [zero1])
    vbuf, _ = _pack(small_v + [zero1])
    gs, ds, nms, nvs = [_unpack(b, spans) for b in _sum8_adam(small_slots, wbuf, mbuf, vbuf, "small_adam")]
    loss = gs[-1].reshape(())
    tbuf, tspans = _pack([dg_mix_pre])
    (tail_slots,) = _run_comm(_gather8_comm(tbuf), "gather_tail")
    tail = _sum8_adam(tail_slots, _pack([norm_mix_pre])[0], _pack([m_norm_mix_pre])[0],
                      _pack([v_norm_mix_pre])[0], "small_adam_tail")
    small_names = ["norm_mix_pre"] + small_names
    gs, ds, nms, nvs = [_unpack(t, tspans) + src for t, src in zip(tail, (gs, ds, nms, nvs))]

    order = ["norm_mix_pre", "norm_mix_post", "norm_mlp_pre", "norm_mlp_post", "w_in", "sinks", "lam_re", "lam_im",
             "log_dt", "b_re", "b_im", "c_re", "c_im", "d_skip", "w_glu", "w_branch", "w_out", "w_up", "w_down"]
    outs = [loss, grad_x[None]]
    for idx, src in enumerate((gs, ds, nms, nvs)):
        for k in order:
            outs.append(big_out[k][idx] if k in big_out else src[small_names.index(k)])
    return tuple(outs)
```

```python
import functools
import math

import jax
import jax.numpy as jnp
from jax import lax
from jax.experimental import pallas as pl
from jax.experimental.pallas import tpu as pltpu

F32 = jnp.float32
BF16 = jnp.bfloat16
MESH = pl.DeviceIdType.MESH

LANES = 128
SUBLANES = 8
VMEM_LIMIT = 56 * 1024 * 1024

HEAD_DIM = 64
N_Q_HEADS = 16
N_KV_HEADS = 2
Q_W = N_Q_HEADS * HEAD_DIM
KV_W = N_KV_HEADS * HEAD_DIM
BLOCK = 128
ROT_DIM = HEAD_DIM // 4
ROPE_THETA = 500000.0
SSM_GC = 16
SSM_P = 64
GROUPS_PER_BLOCK = 8
NW = GROUPS_PER_BLOCK * SSM_P // LANES
EPS = 1e-6
N_DEV = 8

ADAM_LR = 0.001
ADAM_B1 = 0.9
ADAM_B2 = 0.999
ADAM_EPS = 1e-08
ADAM_WD = 0.01
ADAM_STEP = 10


def _params(sem=None):
    return pltpu.CompilerParams(dimension_semantics=sem, vmem_limit_bytes=VMEM_LIMIT)


def _host_params(sem, comm):
    if comm:
        return pltpu.CompilerParams(dimension_semantics=("arbitrary",) * len(sem), vmem_limit_bytes=VMEM_LIMIT,
                                    has_side_effects=True)
    return _params(sem)


def _pick(dim, prefs):
    for p in prefs:
        if dim % p == 0:
            return p
    return dim


def _sigmoid(x):
    return 1.0 / (1.0 + jnp.exp(-x))


_GELU_C = math.sqrt(2.0 / math.pi)


def _gelu(x):
    return 0.5 * x * (1.0 + jnp.tanh(_GELU_C * (x + 0.044715 * x * x * x)))


def _gelu_grad(x):
    t = jnp.tanh(_GELU_C * (x + 0.044715 * x * x * x))
    return 0.5 * (1.0 + t) + 0.5 * x * (1.0 - t * t) * _GELU_C * (1.0 + 3.0 * 0.044715 * x * x)


_DIMS = {"nn": (((1,), (0,)), ((), ())), "nt": (((1,), (1,)), ((), ())), "tn": (((0,), (0,)), ((), ()))}


def _mm(a, b, *, mode, name, out_dtypes=(F32,), epi=None, extras=(), comm=None, a_win=None, b_win=None):
    ar, ac = a.shape[0], (a_win[1] if a_win else a.shape[1])
    br, bc = b.shape[0], (b_win[1] if b_win else b.shape[1])
    if mode == "nn":
        (M, K), (K2, N) = (ar, ac), (br, bc)
    elif mode == "nt":
        (M, K), (N, K2) = (ar, ac), (br, bc)
    else:
        (K, M), (K2, N) = (ar, ac), (br, bc)
    assert K == K2, (a.shape, b.shape, mode)
    tm = _pick(M, (1024, 1280, 640, 512, 256, 128))
    tn = _pick(N, ((2048,) if K <= 2048 else ()) + (1024, 1280, 640, 512, 384, 256, 128))
    tk = K if K <= 2048 else _pick(K, (2048, 1280, 1024, 640, 512, 256, 128))
    nk = K // tk
    a_col_tile = tm if mode == "tn" else tk
    b_col_tile = tk if mode == "nt" else tn
    ao = a_win[0] // a_col_tile if a_win else 0
    bo = b_win[0] // b_col_tile if b_win else 0
    assert (not a_win or a_win[0] % a_col_tile == 0) and (not b_win or b_win[0] % b_col_tile == 0)
    n_ex = len(extras)
    n_out = len(out_dtypes)
    gi, gj = M // tm, N // tn
    steps = gi * gj * nk

    def body(*refs):
        ins, c_ins, o_refs, c_outs, scratch, sems = _split_refs(refs, 2 + n_ex, n_out, comm)
        a_ref, b_ref = ins[0], ins[1]
        ex_refs = ins[2:]
        if comm:
            s = (pl.program_id(0) * gj + pl.program_id(1)) * nk + pl.program_id(2)
            comm.run(c_ins, c_outs, sems, s == 0, s == steps // 2, s == steps - 1)

        def finish(r):
            outs = (r,) if epi is None else epi(r, *[e[...] for e in ex_refs])
            for o_ref, o in zip(o_refs, outs):
                o_ref[...] = o.astype(o_ref.dtype)

        part = lax.dot_general(a_ref[...].astype(BF16), b_ref[...].astype(BF16), _DIMS[mode],
                               preferred_element_type=F32)
        if nk == 1:
            finish(part)
            return
        acc = scratch[0]
        k = pl.program_id(2)

        @pl.when(k == 0)
        def _():
            acc[...] = part

        @pl.when((k > 0) & (k < nk - 1))
        def _():
            acc[...] += part

        @pl.when(k == nk - 1)
        def _():
            finish(acc[...] + part)

    if mode == "nn":
        a_spec = pl.BlockSpec((tm, tk), lambda i, j, k: (i, k + ao))
        b_spec = pl.BlockSpec((tk, tn), lambda i, j, k: (k, j + bo))
    elif mode == "nt":
        a_spec = pl.BlockSpec((tm, tk), lambda i, j, k: (i, k + ao))
        b_spec = pl.BlockSpec((tn, tk), lambda i, j, k: (j, k + bo))
    else:
        a_spec = pl.BlockSpec((tk, tm), lambda i, j, k: (k, i + ao))
        b_spec = pl.BlockSpec((tk, tn), lambda i, j, k: (k, j + bo))
    o_spec = pl.BlockSpec((tm, tn), lambda i, j, k: (i, j))
    c_ins = comm.ins if comm else []
    c_shapes = comm.out_shapes if comm else []
    res = pl.pallas_call(
        body,
        grid=(gi, gj, nk),
        in_specs=[a_spec, b_spec] + [o_spec] * n_ex + [_ANY] * len(c_ins),
        out_specs=tuple([o_spec] * n_out + [_ANY] * len(c_shapes)),
        out_shape=tuple([jax.ShapeDtypeStruct((M, N), d) for d in out_dtypes] + c_shapes),
        scratch_shapes=([pltpu.VMEM((tm, tn), F32)] if nk > 1 else []) + (comm.scratch() if comm else []),
        compiler_params=_host_params(("parallel", "parallel", "arbitrary"), comm),
        name=name,
    )(a, b, *extras, *c_ins)
    if comm:
        return (res[0] if n_out == 1 else res[:n_out]), list(res[n_out:])
    return res[0] if n_out == 1 else res


def _ew(fn, ins, out_dtypes, *, rows, ncols, name):
    g = ncols
    for _, off in ins:
        g = math.gcd(g, off)
    tc = _pick(g, (512, 256, 128))
    tr = _pick(rows, (2048, 1024, 512, 256, 128))
    n_in = len(ins)

    def body(*refs):
        outs = fn(*[r[...] for r in refs[:n_in]])
        for o_ref, o in zip(refs[n_in:], outs):
            o_ref[...] = o.astype(o_ref.dtype)

    def in_spec(off):
        ob = off // tc
        return pl.BlockSpec((tr, tc), lambda i, j: (i, j + ob))

    o_spec = pl.BlockSpec((tr, tc), lambda i, j: (i, j))
    res = pl.pallas_call(
        body,
        grid=(rows // tr, ncols // tc),
        in_specs=[in_spec(off) for _, off in ins],
        out_specs=tuple([o_spec] * len(out_dtypes)),
        out_shape=tuple(jax.ShapeDtypeStruct((rows, ncols), d) for d in out_dtypes),
        compiler_params=_params(("parallel", "parallel")),
        name=name,
    )(*[arr for arr, _ in ins])
    return res[0] if len(out_dtypes) == 1 else res


def _rstd(x):
    return lax.rsqrt(jnp.mean(x * x, axis=-1, keepdims=True) + EPS)


def _norm_bwd(x, r, g, dy):
    t = dy * g
    dx = r * t - x * (r * r * r) * jnp.mean(t * x, axis=-1, keepdims=True)
    return dx, dy * x * r


def _row_call(body, ins, row_ins, outs, acc_outs, *, rows, width, name, comm=None):
    tr = _pick(rows, (256, 128))
    steps = rows // tr
    t_spec = pl.BlockSpec((tr, width), lambda i: (i, 0))
    r_spec = pl.BlockSpec((1, width), lambda i: (0, 0))
    n_in, n_out = len(ins) + len(row_ins), len(outs) + len(acc_outs)

    def hosted(*refs):
        h_ins, c_ins, h_outs, c_outs, _, sems = _split_refs(refs, n_in, n_out, comm)
        i = pl.program_id(0)
        comm.run(c_ins, c_outs, sems, i == 0, i == steps // 2, i == steps - 1)
        body(*h_ins, *h_outs)

    c_ins = comm.ins if comm else []
    c_shapes = comm.out_shapes if comm else []
    res = pl.pallas_call(
        hosted if comm else body,
        grid=(steps,),
        in_specs=[t_spec] * len(ins) + [r_spec] * len(row_ins) + [_ANY] * len(c_ins),
        out_specs=tuple([t_spec] * len(outs) + [pl.BlockSpec(s, lambda i: (0, 0)) for s in acc_outs]
                        + [_ANY] * len(c_shapes)),
        out_shape=tuple([jax.ShapeDtypeStruct((rows, width), d) for d in outs]
                        + [jax.ShapeDtypeStruct(s, F32) for s in acc_outs] + c_shapes),
        scratch_shapes=comm.scratch() if comm else [],
        compiler_params=_host_params(("arbitrary",), comm),
        name=name,
    )(*ins, *row_ins, *c_ins)
    return (res[:n_out], list(res[n_out:])) if comm else res


def _rms_pre(x, g, comm):
    L, D = x.shape

    def body(x_ref, g_ref, h_ref):
        xv = x_ref[...]
        h_ref[...] = (xv * _rstd(xv) * g_ref[...]).astype(BF16)

    (h,), c_outs = _row_call(body, [x], [g], [BF16], [], rows=L, width=D, name="rms_pre", comm=comm)
    return h, c_outs


def _post_pre(x, mixed, g_post, g_pre):
    L, D = x.shape

    def body(x_ref, m_ref, gp_ref, gq_ref, x1_ref, h2_ref):
        mv = m_ref[...]
        x1 = x_ref[...] + mv * _rstd(mv) * gp_ref[...]
        x1_ref[...] = x1
        h2_ref[...] = (x1 * _rstd(x1) * gq_ref[...]).astype(BF16)

    return _row_call(body, [x, mixed], [g_post, g_pre], [F32, BF16], [], rows=L, width=D, name="post_pre")


def _loss_bwd(x1, dn, g_post, target):
    L, D = x1.shape

    def body(x1_ref, dn_ref, t_ref, g_ref, dx2_ref, ddn_ref, dg_ref, loss_ref):
        @pl.when(pl.program_id(0) == 0)
        def _():
            dg_ref[...] = jnp.zeros_like(dg_ref)
            loss_ref[...] = jnp.zeros_like(loss_ref)

        dnv = dn_ref[...]
        g = g_ref[...]
        r = _rstd(dnv)
        err = x1_ref[...] + dnv * r * g - t_ref[...]
        loss_ref[...] += 0.5 * jnp.sum(jnp.mean(err * err, axis=-1, keepdims=True), axis=0, keepdims=True)
        dx2 = err * (1.0 / D)
        dx2_ref[...] = dx2
        ddn, dgr = _norm_bwd(dnv, r, g, dx2)
        ddn_ref[...] = ddn.astype(BF16)
        dg_ref[...] += jnp.sum(dgr, axis=0, keepdims=True)

    return _row_call(body, [x1, dn, target], [g_post], [F32, BF16], [(1, D), (1, 1)],
                     rows=L, width=D, name="loss_bwd")


def _norm_bwd_pair(x1, dh2, dx2, mixed, g_pre, g_post):
    L, D = x1.shape

    def body(x1_ref, dh_ref, dx2_ref, m_ref, gq_ref, gp_ref, dx1_ref, dm_ref, dgq_ref, dgp_ref):
        @pl.when(pl.program_id(0) == 0)
        def _():
            dgq_ref[...] = jnp.zeros_like(dgq_ref)
            dgp_ref[...] = jnp.zeros_like(dgp_ref)

        x1v = x1_ref[...]
        d1, dgq = _norm_bwd(x1v, _rstd(x1v), gq_ref[...], dh_ref[...].astype(F32))
        dx1 = dx2_ref[...] + d1
        dx1_ref[...] = dx1
        mv = m_ref[...]
        dm, dgp = _norm_bwd(mv, _rstd(mv), gp_ref[...], dx1)
        dm_ref[...] = dm.astype(BF16)
        dgq_ref[...] += jnp.sum(dgq, axis=0, keepdims=True)
        dgp_ref[...] += jnp.sum(dgp, axis=0, keepdims=True)

    return _row_call(body, [x1, dh2, dx2, mixed], [g_pre, g_post], [F32, BF16], [(1, D), (1, D)],
                     rows=L, width=D, name="norm_bwd_pair")


def _final_bwd(x, dh, dx1, g_pre, comm):
    L, D = x.shape

    def body(x_ref, dh_ref, dx1_ref, g_ref, gx_ref, dg_ref):
        @pl.when(pl.program_id(0) == 0)
        def _():
            dg_ref[...] = jnp.zeros_like(dg_ref)

        xv = x_ref[...]
        d0, dg = _norm_bwd(xv, _rstd(xv), g_ref[...], dh_ref[...].astype(F32))
        gx_ref[...] = dx1_ref[...] + d0
        dg_ref[...] += jnp.sum(dg, axis=0, keepdims=True)

    return _row_call(body, [x, dh, dx1], [g_pre], [F32], [(1, D)], rows=L, width=D, name="final_bwd", comm=comm)


def _rope_tables(L):
    half = ROT_DIM // 2
    inv = ROPE_THETA ** (-jnp.arange(half, dtype=F32) * 2.0 / ROT_DIM)
    ang = jnp.arange(L, dtype=F32)[:, None] * inv[None, :]
    d = jnp.arange(LANES) % HEAD_DIM
    a = ang[:, d % half]
    cos_t = jnp.where(d[None, :] < ROT_DIM, jnp.cos(a), 1.0)
    sin_t = jnp.where(d[None, :] < half, -jnp.sin(a), jnp.where(d[None, :] < ROT_DIM, jnp.sin(a), 0.0))
    return cos_t.astype(F32), sin_t.astype(F32)


def _lane_lo(shape):
    return lax.broadcasted_iota(jnp.int32, shape, 1) < HEAD_DIM


def _rope(x, cos_t, sin_t):
    d = lax.broadcasted_iota(jnp.int32, x.shape, 1) % HEAD_DIM
    partner = jnp.where(d < ROT_DIM // 2, pltpu.roll(x, LANES - ROT_DIM // 2, 1), pltpu.roll(x, ROT_DIM // 2, 1))
    return x * cos_t + partner * sin_t


def _dup(kv, g):
    sw = pltpu.roll(kv, HEAD_DIM, 1)
    lo = _lane_lo(kv.shape)
    return jnp.where(lo, kv, sw) if g == 0 else jnp.where(lo, sw, kv)


def _attn_mask(n):
    qi = lax.broadcasted_iota(jnp.int32, (BLOCK, 2 * BLOCK), 0)
    kj = lax.broadcasted_iota(jnp.int32, (BLOCK, 2 * BLOCK), 1)
    rel = qi + BLOCK - kj
    return (rel >= 0) & (rel < BLOCK) & ((kj >= BLOCK) | (n > 0))


def _softmax_sink(s, mask, sink):
    s = jnp.where(mask, s, -1e30)
    m = jnp.maximum(jnp.max(s, axis=-1, keepdims=True), sink)
    e = jnp.where(mask, jnp.exp(s - m), 0.0)
    es = jnp.exp(sink - m)
    inv = 1.0 / (jnp.sum(e, axis=-1, keepdims=True) + es)
    return e * inv, es * inv


_NT = (((1,), (1,)), ((), ()))
_TN = (((0,), (0,)), ((), ()))


def _dot(a, b):
    return jnp.dot(a.astype(BF16), b.astype(BF16), preferred_element_type=F32)


def _dot_nt(a, b):
    return lax.dot_general(a.astype(BF16), b.astype(BF16), _NT, preferred_element_type=F32)


def _dot_tn(a, b):
    return lax.dot_general(a.astype(BF16), b.astype(BF16), _TN, preferred_element_type=F32)


def _attn_specs(nb):
    kcol, vcol = Q_W // LANES, Q_W // LANES + 1
    prev = lambda n: jnp.maximum(n - 1, 0)
    return [
        pl.BlockSpec((BLOCK, Q_W), lambda n: (n, 0)),
        pl.BlockSpec((BLOCK, LANES), lambda n: (n, kcol)),
        pl.BlockSpec((BLOCK, LANES), lambda n: (prev(n), kcol)),
        pl.BlockSpec((BLOCK, LANES), lambda n: (n, vcol)),
        pl.BlockSpec((BLOCK, LANES), lambda n: (prev(n), vcol)),
        pl.BlockSpec((BLOCK, LANES), lambda n: (n, 0)),
        pl.BlockSpec((BLOCK, LANES), lambda n: (prev(n), 0)),
        pl.BlockSpec((BLOCK, LANES), lambda n: (n, 0)),
        pl.BlockSpec((BLOCK, LANES), lambda n: (prev(n), 0)),
        pl.BlockSpec(memory_space=pltpu.SMEM),
    ]


def _attn_prep(refs):
    q_ref, kc_ref, kp_ref, vc_ref, vp_ref, cc_ref, cp_ref, sc_ref, sp_ref = refs
    cos_c, sin_c, cos_p, sin_p = cc_ref[...], sc_ref[...], cp_ref[...], sp_ref[...]
    k2 = jnp.concatenate([_rope(kp_ref[...], cos_p, sin_p), _rope(kc_ref[...], cos_c, sin_c)], axis=0)
    v2 = jnp.concatenate([vp_ref[...], vc_ref[...]], axis=0)
    kd = [_dup(k2, g).astype(BF16) for g in range(N_KV_HEADS)]
    vd = [_dup(v2, g).astype(BF16) for g in range(N_KV_HEADS)]
    return cos_c, sin_c, cos_p, sin_p, kd, vd


def _stack_heads(x):
    lo = _lane_lo(x.shape)
    return jnp.concatenate([jnp.where(lo, x, 0.0), jnp.where(lo, 0.0, x)], axis=0)


def _unstack_heads(x2):
    return jnp.where(_lane_lo((BLOCK, LANES)), x2[:BLOCK], x2[BLOCK:])


def _pair_sinks(sink_ref, t):
    first = lax.broadcasted_iota(jnp.int32, (2 * BLOCK, 1), 0) < BLOCK
    return jnp.where(first, sink_ref[0, 2 * t], sink_ref[0, 2 * t + 1])


_SCALE = 1.0 / math.sqrt(HEAD_DIM)
_TILES = Q_W // LANES
_TILES_PER_KV = _TILES // N_KV_HEADS


def _attn_fwd(z, cos_t, sin_t, sinks, comm=None):
    L = z.shape[0]
    nb = L // BLOCK

    def body(*refs):
        ins, c_ins, outs, c_outs, _, sems = _split_refs(refs, 10, 1, comm)
        q_ref, kc_ref, kp_ref, vc_ref, vp_ref, cc_ref, cp_ref, sc_ref, sp_ref, sink_ref = ins
        o_ref = outs[0]
        n = pl.program_id(0)
        if comm:
            comm.run(c_ins, c_outs, sems, n == 0, n == nb // 2, n == nb - 1)
        cos_c, sin_c, _, _, kd, vd = _attn_prep((q_ref, kc_ref, kp_ref, vc_ref, vp_ref, cc_ref, cp_ref, sc_ref, sp_ref))
        mask = _attn_mask(n)
        mask2 = jnp.concatenate([mask, mask], axis=0)
        for t in range(_TILES):
            g = t // _TILES_PER_KV
            q2 = _stack_heads(_rope(q_ref[:, t * LANES:(t + 1) * LANES], cos_c, sin_c) * _SCALE)
            p2, _ = _softmax_sink(_dot_nt(q2, kd[g]), mask2, _pair_sinks(sink_ref, t))
            o_ref[:, t * LANES:(t + 1) * LANES] = _unstack_heads(_dot(p2, vd[g])).astype(BF16)

    c_ins = comm.ins if comm else []
    c_shapes = comm.out_shapes if comm else []
    res = pl.pallas_call(
        body,
        grid=(nb,),
        in_specs=_attn_specs(nb) + [_ANY] * len(c_ins),
        out_specs=tuple([pl.BlockSpec((BLOCK, Q_W), lambda n: (n, 0))] + [_ANY] * len(c_shapes)),
        out_shape=tuple([jax.ShapeDtypeStruct((L, Q_W), BF16)] + c_shapes),
        scratch_shapes=comm.scratch() if comm else [],
        compiler_params=_host_params(("parallel",), comm),
        name="attn_fwd",
    )(z, z, z, z, z, cos_t, cos_t, sin_t, sin_t, sinks, *c_ins)
    return (res[0], list(res[1:])) if comm else res[0]


def _attn_bwd(z, d_o, cos_t, sin_t, sinks, dz, comm=None):
    L = z.shape[0]
    nb = L // BLOCK

    def body(*refs):
        ins, c_ins, outs, c_outs, _, sems = _split_refs(refs, 12, 6, comm)
        q_ref, kc_ref, kp_ref, vc_ref, vp_ref, cc_ref, cp_ref, sc_ref, sp_ref, sink_ref, do_ref, _ = ins
        dq_ref, dkc_ref, dkp_ref, dvc_ref, dvp_ref, ds_ref = outs
        n = pl.program_id(0)
        if comm:
            comm.run(c_ins, c_outs, sems, n == 0, n == nb // 2, n == nb - 1)

        @pl.when(n == 0)
        def _():
            ds_ref[...] = jnp.zeros_like(ds_ref)

        cos_c, sin_c, cos_p, sin_p, kd, vd = _attn_prep(
            (q_ref, kc_ref, kp_ref, vc_ref, vp_ref, cc_ref, cp_ref, sc_ref, sp_ref))
        mask = _attn_mask(n)
        mask2 = jnp.concatenate([mask, mask], axis=0)
        lo2 = _lane_lo((2 * BLOCK, LANES))
        acc_k = [jnp.zeros((2 * BLOCK, LANES), F32) for _ in range(N_KV_HEADS)]
        acc_v = [jnp.zeros((2 * BLOCK, LANES), F32) for _ in range(N_KV_HEADS)]
        sink_rows = []
        for t in range(_TILES):
            g = t // _TILES_PER_KV
            sl = slice(t * LANES, (t + 1) * LANES)
            q2 = _stack_heads(_rope(q_ref[:, sl], cos_c, sin_c) * _SCALE)
            p2, ps2 = _softmax_sink(_dot_nt(q2, kd[g]), mask2, _pair_sinks(sink_ref, t))
            do2 = _stack_heads(do_ref[:, sl])
            d2 = jnp.sum(do2 * _dot(p2, vd[g]), axis=-1, keepdims=True)
            ds2 = p2 * (_dot_nt(do2, vd[g]) - d2)
            dqt = _unstack_heads(_dot(ds2, kd[g])) * _SCALE
            dq_ref[:, sl] = _rope(dqt, cos_c, -sin_c).astype(BF16)
            acc_k[g] = acc_k[g] + _dot_tn(ds2, q2)
            acc_v[g] = acc_v[g] + _dot_tn(p2, do2)
            sd = ps2 * d2
            sink_rows.append(jnp.where(_lane_lo((1, LANES)), -jnp.sum(sd[:BLOCK], axis=0, keepdims=True),
                                       -jnp.sum(sd[BLOCK:], axis=0, keepdims=True)))
        ds_ref[...] += jnp.concatenate(sink_rows, axis=0)
        fk = [a + pltpu.roll(a, HEAD_DIM, 1) for a in acc_k]
        fv = [a + pltpu.roll(a, HEAD_DIM, 1) for a in acc_v]
        dk2 = jnp.where(lo2, fk[0], fk[1])
        dv2 = jnp.where(lo2, fv[0], fv[1])
        dkp_ref[...] = _rope(dk2[:BLOCK], cos_p, -sin_p)
        dkc_ref[...] = _rope(dk2[BLOCK:], cos_c, -sin_c)
        dvp_ref[...] = dv2[:BLOCK]
        dvc_ref[...] = dv2[BLOCK:]

    blk = pl.BlockSpec((BLOCK, LANES), lambda n: (n, 0))
    kv = jax.ShapeDtypeStruct((L, LANES), F32)
    c_ins = comm.ins if comm else []
    c_shapes = comm.out_shapes if comm else []
    res = pl.pallas_call(
        body,
        grid=(nb,),
        in_specs=_attn_specs(nb) + [pl.BlockSpec((BLOCK, Q_W), lambda n: (n, 0)), _ANY] + [_ANY] * len(c_ins),
        out_specs=tuple([pl.BlockSpec((BLOCK, Q_W), lambda n: (n, 0)), blk, blk, blk, blk,
                         pl.BlockSpec((_TILES, LANES), lambda n: (0, 0))] + [_ANY] * len(c_shapes)),
        out_shape=tuple([jax.ShapeDtypeStruct(dz.shape, dz.dtype), kv, kv, kv, kv,
                         jax.ShapeDtypeStruct((_TILES, LANES), F32)] + c_shapes),
        scratch_shapes=comm.scratch() if comm else [],
        input_output_aliases={11: 0},
        compiler_params=_host_params(("arbitrary",), comm),
        name="attn_bwd",
    )(z, z, z, z, z, cos_t, cos_t, sin_t, sin_t, sinks, d_o, dz, *c_ins)
    return (res[:6], list(res[6:])) if comm else res


def _kv_combine(dkc, dkp, dvc, dvp, dz):
    L = dkc.shape[0]
    tr = _pick(L, (1024, 512, 256, 128))
    per = tr // BLOCK
    nt, nb = L // tr, L // BLOCK

    def body(kc_ref, kp_ref, kn_ref, vc_ref, vp_ref, vn_ref, dz_in, o_ref):
        live = jnp.where(pl.program_id(0) + 1 < nt, 1.0, 0.0)

        def shifted(p_ref, n_ref):
            tail = live * n_ref[...]
            return tail if per == 1 else jnp.concatenate([p_ref[BLOCK:, :], tail], axis=0)

        o_ref[:, :LANES] = (kc_ref[...] + shifted(kp_ref, kn_ref)).astype(BF16)
        o_ref[:, LANES:] = (vc_ref[...] + shifted(vp_ref, vn_ref)).astype(BF16)

    cur = pl.BlockSpec((tr, LANES), lambda n: (n, 0))
    nxt = pl.BlockSpec((BLOCK, LANES), lambda n: (jnp.minimum((n + 1) * per, nb - 1), 0))
    kv_block = Q_W // (2 * KV_W)
    return pl.pallas_call(body, grid=(nt,), in_specs=[cur, cur, nxt, cur, cur, nxt, _ANY],
                          out_specs=pl.BlockSpec((tr, 2 * KV_W), lambda n: (n, kv_block)),
                          out_shape=jax.ShapeDtypeStruct(dz.shape, dz.dtype), input_output_aliases={6: 0},
                          compiler_params=_params(("parallel",)), name="kv_combine")(
        dkc, dkp, dkp, dvc, dvp, dvp, dz)


def _gate_bwd(d_mix, z, o_ga, y_attn, y_ssm):
    L, D = d_mix.shape
    tc = _pick(math.gcd(D, o_ga), (512, 256, 128))
    tr = _pick(L, (2048, 1024, 512, 256, 128))
    nd, gb = D // tc, o_ga // tc

    def body(dm_ref, g_ref, ya_ref, ys_ref, dy_ref, dz_ref):
        dm = dm_ref[...].astype(F32)
        s = _sigmoid(g_ref[...])
        y = jnp.where(pl.program_id(1) < nd, ya_ref[...], ys_ref[...]).astype(F32)
        dy_ref[...] = (dm * s).astype(BF16)
        dz_ref[...] = (dm * y * s * (1.0 - s)).astype(BF16)

    blk = lambda f: pl.BlockSpec((tr, tc), f)
    return pl.pallas_call(
        body,
        grid=(L // tr, 2 * nd),
        in_specs=[blk(lambda i, j: (i, j % nd)), blk(lambda i, j: (i, j + gb)),
                  blk(lambda i, j: (i, jnp.minimum(j, nd - 1))), blk(lambda i, j: (i, jnp.maximum(j - nd, 0)))],
        out_specs=(blk(lambda i, j: (i, j)), blk(lambda i, j: (i, j + gb))),
        out_shape=(jax.ShapeDtypeStruct((L, 2 * D), BF16), jax.ShapeDtypeStruct(z.shape, BF16)),
        compiler_params=_params(("parallel", "arbitrary")),
        name="gate_bwd",
    )(d_mix, z, y_attn, y_ssm)


def _discretise(lr, li, ldt, br, bi):
    dt = jnp.exp(ldt)
    mag = jnp.exp(lr * dt)
    a_re, a_im = mag * jnp.cos(li * dt), mag * jnp.sin(li * dt)
    den = lr * lr + li * li
    nr, ni = a_re - 1.0, a_im
    coef_re = (nr * lr + ni * li) / den
    coef_im = (ni * lr - nr * li) / den
    return a_re, a_im, coef_re * br - coef_im * bi, coef_re * bi + coef_im * br


def _disc_specs(n):
    tr = _pick(n, (512,))
    cs = pl.BlockSpec((tr, 1), lambda i: (i, 0))
    ms = pl.BlockSpec((tr, SSM_GC), lambda i: (i, 0))
    return tr, cs, ms


def _disc_fwd(lr, li, ldt, br, bi):
    n = lr.shape[0]
    tr, cs, ms = _disc_specs(n)

    def body(lr_ref, li_ref, dt_ref, br_ref, bi_ref, o1, o2, o3, o4):
        r = _discretise(lr_ref[...], li_ref[...], dt_ref[...], br_ref[...], bi_ref[...])
        o1[...], o2[...], o3[...], o4[...] = r

    col = jax.ShapeDtypeStruct((n, 1), F32)
    mat = jax.ShapeDtypeStruct((n, SSM_GC), F32)
    return pl.pallas_call(body, grid=(n // tr,), in_specs=[cs, cs, cs, ms, ms], out_specs=(cs, cs, ms, ms),
                          out_shape=(col, col, mat, mat), compiler_params=_params(("parallel",)), name="disc_fwd")(
        lr, li, ldt, br, bi)


def _disc_bwd(lr, li, ldt, br, bi, gar, gai, gbr, gbi):
    n = lr.shape[0]
    tr, cs, ms = _disc_specs(n)

    def body(lr_ref, li_ref, dt_ref, br_ref, bi_ref, gar_ref, gai_ref, gbr_ref, gbi_ref, o_lr, o_li, o_dt, o_br, o_bi):
        _, vjp = jax.vjp(_discretise, lr_ref[...], li_ref[...], dt_ref[...], br_ref[...], bi_ref[...])
        g = vjp((gar_ref[...], gai_ref[...], gbr_ref[...], gbi_ref[...]))
        o_lr[...] = g[0]
        o_li[...] = g[1]
        o_dt[...] = jnp.sum(g[2].reshape(tr // SSM_P, SSM_P, 1), axis=1)
        o_br[...] = g[3]
        o_bi[...] = g[4]

    col = jax.ShapeDtypeStruct((n, 1), F32)
    mat = jax.ShapeDtypeStruct((n, SSM_GC), F32)
    return pl.pallas_call(
        body, grid=(n // tr,), in_specs=[cs, cs, cs, ms, ms, cs, cs, ms, ms],
        out_specs=(cs, cs, pl.BlockSpec((tr // SSM_P, 1), lambda i: (i, 0)), ms, ms),
        out_shape=(col, col, jax.ShapeDtypeStruct((n // SSM_P, 1), F32), mat, mat),
        compiler_params=_params(("parallel",)), name="disc_bwd")(lr, li, ldt, br, bi, gar, gai, gbr, gbi)


def _cpow(ar, ai, nsq):
    for _ in range(nsq):
        ar, ai = ar * ar - ai * ai, 2.0 * ar * ai
    return ar, ai


def _ssm_dims(L):
    tc = min(1024, L)
    seg = tc // SUBLANES
    assert seg & (seg - 1) == 0
    return tc, seg, L // tc, seg.bit_length() - 1


def _tile_rows(i):
    return pl.ds(pl.multiple_of(i * SUBLANES, SUBLANES), SUBLANES)


def _rows_to_segments(src_ref, dst_ref, seg):
    def body(i, _):
        dst_ref[_tile_rows(i), :] = src_ref[pl.ds(i, SUBLANES, stride=seg), :]
        return 0
    lax.fori_loop(0, seg, body, 0, unroll=8)


def _segments_to_rows(src_ref, dst_ref, seg):
    def body(i, _):
        dst_ref[pl.ds(i, SUBLANES, stride=seg), :] = src_ref[_tile_rows(i), :]
        return 0
    lax.fori_loop(0, seg, body, 0, unroll=8)


def _ssm_fwd(z, u_off, br_m, bi_m, cr_m, ci_m, a_re3, a_im3, d_row, comm=None):
    L = z.shape[0]
    ngb = br_m.shape[0]
    tc, seg, nc, nsq = _ssm_dims(L)
    ucol = u_off // LANES

    def body(*refs):
        ins, c_ins, outs, c_outs, scratch, sems = _split_refs(refs, 8, 3, comm)
        u_ref, br_ref, bi_ref, cr_ref, ci_ref, ar_ref, ai_ref, d_ref = ins
        y_ref, xr_ref, xi_ref = outs
        bur, bui, car_r, car_i, ini_r, ini_i, up, ys = scratch
        if comm:
            s = pl.program_id(0) * nc + pl.program_id(1)
            comm.run(c_ins, c_outs, sems, s == 0, s == (ngb * nc) // 2, s == ngb * nc - 1)

        @pl.when(pl.program_id(1) == 0)
        def _():
            car_r[...] = jnp.zeros_like(car_r)
            car_i[...] = jnp.zeros_like(car_i)

        _rows_to_segments(u_ref, up, seg)
        u = up[...]
        pr = _dot(u, br_ref[0])
        pi = _dot(u, bi_ref[0])
        for w in range(NW):
            bur[w] = pr[:, w * LANES:(w + 1) * LANES]
            bui[w] = pi[:, w * LANES:(w + 1) * LANES]
        ar = [jnp.broadcast_to(ar_ref[w], (SUBLANES, LANES)) for w in range(NW)]
        ai = [jnp.broadcast_to(ai_ref[w], (SUBLANES, LANES)) for w in range(NW)]

        def step(i, carry, store):
            xr, xi = carry
            rows = _tile_rows(i)
            nr, ni = [], []
            for w in range(NW):
                r = ar[w] * xr[w] - ai[w] * xi[w] + bur[w, rows, :]
                m = ar[w] * xi[w] + ai[w] * xr[w] + bui[w, rows, :]
                if store:
                    xr_ref[w, rows, :] = r
                    xi_ref[w, rows, :] = m
                nr.append(r)
                ni.append(m)
            return tuple(nr), tuple(ni)

        zero = tuple(jnp.zeros((SUBLANES, LANES), F32) for _ in range(NW))
        er, ei = lax.fori_loop(0, seg, functools.partial(step, store=False), (zero, zero), unroll=2)
        for w in range(NW):
            pr_, pi_ = _cpow(ar[w][0:1], ai[w][0:1], nsq)
            sr, si = car_r[w, 0:1, :], car_i[w, 0:1, :]
            for j in range(SUBLANES):
                ini_r[w, j:j + 1, :] = sr
                ini_i[w, j:j + 1, :] = si
                sr, si = (pr_ * sr - pi_ * si + er[w][j:j + 1], pr_ * si + pi_ * sr + ei[w][j:j + 1])
            car_r[w, 0:1, :] = sr
            car_i[w, 0:1, :] = si
        init = (tuple(ini_r[w] for w in range(NW)), tuple(ini_i[w] for w in range(NW)))
        lax.fori_loop(0, seg, functools.partial(step, store=True), init, unroll=2)
        acc = d_ref[...] * u
        for w in range(NW):
            sl = slice(w * LANES, (w + 1) * LANES)
            acc = acc + _dot(xr_ref[w], cr_ref[0, sl, :]) - _dot(xi_ref[w], ci_ref[0, sl, :])
        ys[...] = acc
        _segments_to_rows(ys, y_ref, seg)

    mat_b = pl.BlockSpec((1, LANES, NW * LANES), lambda b, k: (b, 0, 0))
    mat_c = pl.BlockSpec((1, NW * LANES, LANES), lambda b, k: (b, 0, 0))
    a_spec = pl.BlockSpec((NW, 1, LANES), lambda b, k: (b, 0, 0))
    x_spec = pl.BlockSpec((NW, tc, LANES), lambda b, k: (b, k, 0))
    xs = jax.ShapeDtypeStruct((ngb * NW, L, LANES), F32)
    st = pltpu.VMEM((NW, SUBLANES, LANES), F32)
    c_ins = comm.ins if comm else []
    c_shapes = comm.out_shapes if comm else []
    res = pl.pallas_call(
        body,
        grid=(ngb, nc),
        in_specs=[pl.BlockSpec((tc, LANES), lambda b, k: (k, b + ucol)), mat_b, mat_b, mat_c, mat_c, a_spec, a_spec,
                  pl.BlockSpec((1, LANES), lambda b, k: (0, b))] + [_ANY] * len(c_ins),
        out_specs=tuple([pl.BlockSpec((tc, LANES), lambda b, k: (k, b)), x_spec, x_spec] + [_ANY] * len(c_shapes)),
        out_shape=tuple([jax.ShapeDtypeStruct((L, ngb * LANES), F32), xs, xs] + c_shapes),
        scratch_shapes=[pltpu.VMEM((NW, tc, LANES), F32), pltpu.VMEM((NW, tc, LANES), F32), st, st, st, st,
                        pltpu.VMEM((tc, LANES), F32), pltpu.VMEM((tc, LANES), F32)]
        + (comm.scratch() if comm else []),
        compiler_params=_host_params(("arbitrary", "arbitrary"), comm),
        name="ssm_fwd",
    )(z, br_m, bi_m, cr_m, ci_m, a_re3, a_im3, d_row, *c_ins)
    return (res[:3], list(res[3:])) if comm else res


def _ssm_bwd(dy, z, u_off, xs_r, xs_i, br_m, bi_m, cr_m, ci_m, a_re3, a_im3, d_row, dz):
    L = z.shape[0]
    ngb = br_m.shape[0]
    tc, seg, nc, nsq = _ssm_dims(L)
    ucol = u_off // LANES

    def body(dy_ref, u_ref, xr_ref, xi_ref, br_ref, bi_ref, cr_ref, ci_ref, ar_ref, ai_ref, d_ref, dz_in,
             du_ref, gd_ref, gar_ref, gai_ref, gbr_ref, gbi_ref, gcr_ref, gci_ref,
             gr_s, gi_s, car_r, car_i, ini_r, ini_i, acc_r, acc_i, dyp, up, dus, dun):
        k = pl.program_id(1)

        @pl.when(k == 0)
        def _():
            for ref in (car_r, car_i, acc_r, acc_i, gd_ref, gbr_ref, gbi_ref, gcr_ref, gci_ref):
                ref[...] = jnp.zeros_like(ref)

        _rows_to_segments(dy_ref, dyp, seg)
        _rows_to_segments(u_ref, up, seg)
        dy_v = dyp[...]
        u = up[...]
        g_re = _dot_nt(dy_v, cr_ref[0])
        g_im = -_dot_nt(dy_v, ci_ref[0])
        for w in range(NW):
            gr_s[w] = g_re[:, w * LANES:(w + 1) * LANES]
            gi_s[w] = g_im[:, w * LANES:(w + 1) * LANES]
        ar = [jnp.broadcast_to(ar_ref[w], (SUBLANES, LANES)) for w in range(NW)]
        ai = [jnp.broadcast_to(ai_ref[w], (SUBLANES, LANES)) for w in range(NW)]

        def step1(ii, carry):
            xr, xi = carry
            rows = _tile_rows(seg - 1 - ii)
            nr = tuple(ar[w] * xr[w] + ai[w] * xi[w] + gr_s[w, rows, :] for w in range(NW))
            ni = tuple(ar[w] * xi[w] - ai[w] * xr[w] + gi_s[w, rows, :] for w in range(NW))
            return nr, ni

        zero = tuple(jnp.zeros((SUBLANES, LANES), F32) for _ in range(NW))
        er, ei = lax.fori_loop(0, seg, step1, (zero, zero), unroll=2)
        for w in range(NW):
            pr_, pi_ = _cpow(ar[w][0:1], ai[w][0:1], nsq)
            sr, si = car_r[w, 0:1, :], car_i[w, 0:1, :]
            for j in reversed(range(SUBLANES)):
                ini_r[w, j:j + 1, :] = sr
                ini_i[w, j:j + 1, :] = si
                sr, si = (pr_ * sr + pi_ * si + er[w][j:j + 1], pr_ * si - pi_ * sr + ei[w][j:j + 1])
            car_r[w, 0:1, :] = sr
            car_i[w, 0:1, :] = si

        def step2(ii, carry):
            gxr, gxi, acr, aci = carry
            rows = _tile_rows(seg - 1 - ii)
            nr, ni, nar, nai = [], [], [], []
            for w in range(NW):
                xr_t, xi_t = xr_ref[w, rows, :], xi_ref[w, rows, :]
                nar.append(acr[w] + gxr[w] * xr_t + gxi[w] * xi_t)
                nai.append(aci[w] + gxi[w] * xr_t - gxr[w] * xi_t)
                r = ar[w] * gxr[w] + ai[w] * gxi[w] + gr_s[w, rows, :]
                m = ar[w] * gxi[w] - ai[w] * gxr[w] + gi_s[w, rows, :]
                gr_s[w, rows, :] = r
                gi_s[w, rows, :] = m
                nr.append(r)
                ni.append(m)
            return tuple(nr), tuple(ni), tuple(nar), tuple(nai)

        init = (tuple(ini_r[w] for w in range(NW)), tuple(ini_i[w] for w in range(NW)),
                tuple(acc_r[w] for w in range(NW)), tuple(acc_i[w] for w in range(NW)))
        _, _, acr, aci = lax.fori_loop(0, seg, step2, init, unroll=2)
        du = d_ref[...] * dy_v
        for w in range(NW):
            sl = slice(w * LANES, (w + 1) * LANES)
            acc_r[w] = acr[w]
            acc_i[w] = aci[w]
            gxr_w, gxi_w = gr_s[w], gi_s[w]
            du = du + _dot_nt(gxr_w, br_ref[0, :, sl]) + _dot_nt(gxi_w, bi_ref[0, :, sl])
            gbr_ref[0, :, sl] += _dot_tn(u, gxr_w)
            gbi_ref[0, :, sl] += _dot_tn(u, gxi_w)
            gcr_ref[0, sl, :] += _dot_tn(xr_ref[w], dy_v)
            gci_ref[0, sl, :] += _dot_tn(-xi_ref[w], dy_v)
        dus[...] = du
        _segments_to_rows(dus, dun, seg)
        du_ref[...] = dun[...].astype(BF16)
        gd_ref[...] += jnp.sum(dy_v * u, axis=0, keepdims=True)

        @pl.when(k == nc - 1)
        def _():
            for w in range(NW):
                gar_ref[w] = jnp.sum(acc_r[w], axis=0, keepdims=True)
                gai_ref[w] = jnp.sum(acc_i[w], axis=0, keepdims=True)

    rk = lambda k: nc - 1 - k
    mat_b = pl.BlockSpec((1, LANES, NW * LANES), lambda b, k: (b, 0, 0))
    mat_c = pl.BlockSpec((1, NW * LANES, LANES), lambda b, k: (b, 0, 0))
    a_spec = pl.BlockSpec((NW, 1, LANES), lambda b, k: (b, 0, 0))
    x_spec = pl.BlockSpec((NW, tc, LANES), lambda b, k: (b, rk(k), 0))
    st = pltpu.VMEM((NW, SUBLANES, LANES), F32)
    big = pltpu.VMEM((NW, tc, LANES), F32)
    return pl.pallas_call(
        body,
        grid=(ngb, nc),
        in_specs=[pl.BlockSpec((tc, LANES), lambda b, k: (rk(k), b)),
                  pl.BlockSpec((tc, LANES), lambda b, k: (rk(k), b + ucol)),
                  x_spec, x_spec, mat_b, mat_b, mat_c, mat_c, a_spec, a_spec,
                  pl.BlockSpec((1, LANES), lambda b, k: (0, b)), _ANY],
        out_specs=(pl.BlockSpec((tc, LANES), lambda b, k: (rk(k), b + ucol)),
                   pl.BlockSpec((1, LANES), lambda b, k: (0, b)), a_spec, a_spec, mat_b, mat_b, mat_c, mat_c),
        out_shape=(jax.ShapeDtypeStruct(dz.shape, dz.dtype),
                   jax.ShapeDtypeStruct((1, ngb * LANES), F32),
                   jax.ShapeDtypeStruct((ngb * NW, 1, LANES), F32), jax.ShapeDtypeStruct((ngb * NW, 1, LANES), F32),
                   jax.ShapeDtypeStruct(br_m.shape, F32), jax.ShapeDtypeStruct(br_m.shape, F32),
                   jax.ShapeDtypeStruct(cr_m.shape, F32), jax.ShapeDtypeStruct(cr_m.shape, F32)),
        scratch_shapes=[big, big, st, st, st, st, st, st] + [pltpu.VMEM((tc, LANES), F32)] * 4,
        input_output_aliases={11: 0},
        compiler_params=_params(("arbitrary", "arbitrary")),
        name="ssm_bwd",
    )(dy, z, xs_r, xs_i, br_m, bi_m, cr_m, ci_m, a_re3, a_im3, d_row, dz)


def _block_diag_in(bb, ngb):
    t = bb.reshape(ngb, GROUPS_PER_BLOCK, SSM_P, SSM_GC).transpose(0, 1, 3, 2)
    eye = jnp.eye(GROUPS_PER_BLOCK, dtype=F32)
    m = t[:, :, :, None, :] * eye[None, :, None, :, None]
    return m.reshape(ngb, GROUPS_PER_BLOCK * SSM_GC, GROUPS_PER_BLOCK * SSM_P)


def _block_diag_out(c, ngb):
    t = c.reshape(ngb, GROUPS_PER_BLOCK, SSM_GC, SSM_P).transpose(0, 1, 3, 2)
    eye = jnp.eye(GROUPS_PER_BLOCK, dtype=F32)
    m = t[:, :, :, None, :] * eye[None, :, None, :, None]
    return m.reshape(ngb, GROUPS_PER_BLOCK * SSM_P, GROUPS_PER_BLOCK * SSM_GC)


def _diag_in(m, ngb):
    m5 = m.reshape(ngb, GROUPS_PER_BLOCK, SSM_GC, GROUPS_PER_BLOCK, SSM_P)
    d = jnp.diagonal(m5, axis1=1, axis2=3)
    return d.transpose(0, 3, 2, 1).reshape(ngb * GROUPS_PER_BLOCK * SSM_P, SSM_GC)


def _diag_out(m, ngb):
    m5 = m.reshape(ngb, GROUPS_PER_BLOCK, SSM_P, GROUPS_PER_BLOCK, SSM_GC)
    d = jnp.diagonal(m5, axis1=1, axis2=3)
    return d.transpose(0, 3, 2, 1).reshape(ngb * GROUPS_PER_BLOCK, SSM_GC, SSM_P)


_ANY = pl.BlockSpec(memory_space=pl.ANY)


class _Comm:
    def __init__(self, ins, out_shapes, n_sem, start, mid, finish):
        self.ins, self.out_shapes, self.n_sem = list(ins), list(out_shapes), n_sem
        self.start, self.mid, self.finish = start, mid, finish

    def scratch(self):
        return [pltpu.SemaphoreType.DMA((self.n_sem,)), pltpu.SemaphoreType.DMA((self.n_sem,)),
                pltpu.SemaphoreType.DMA((len(self.ins),))]

    def run(self, in_refs, out_refs, sems, first, mid, last):
        send, recv, local = sems

        @pl.when(first)
        def _():
            self.start(in_refs, out_refs, send, recv, local)

        @pl.when(mid)
        def _():
            self.mid(in_refs, out_refs, send, recv, local)

        @pl.when(last)
        def _():
            self.finish(in_refs, out_refs, send, recv, local)


def _split_refs(refs, n_in, n_out, comm):
    ci = len(comm.ins) if comm else 0
    co = len(comm.out_shapes) if comm else 0
    ins = refs[:n_in]
    c_ins = refs[n_in:n_in + ci]
    outs = refs[n_in + ci:n_in + ci + n_out]
    c_outs = refs[n_in + ci + n_out:n_in + ci + n_out + co]
    rest = refs[n_in + ci + n_out + co:]
    if comm:
        return ins, c_ins, outs, c_outs, rest[:-3], rest[-3:]
    return ins, c_ins, outs, c_outs, rest, ()


def _run_comm(comm, name):
    ni, no = len(comm.ins), len(comm.out_shapes)

    def body(*refs):
        args = (refs[:ni], refs[ni:ni + no]) + tuple(refs[ni + no:])
        comm.start(*args)
        comm.mid(*args)
        comm.finish(*args)

    return pl.pallas_call(
        body,
        in_specs=[_ANY] * ni,
        out_specs=tuple([_ANY] * no),
        out_shape=tuple(comm.out_shapes),
        scratch_shapes=comm.scratch(),
        compiler_params=pltpu.CompilerParams(has_side_effects=True),
        name=name,
    )(*comm.ins)


def _ag_comm(shards):
    n = len(shards)

    def env(ins, outs, send_sems, recv_sems):
        x, y, c = lax.axis_index("x"), lax.axis_index("y"), lax.axis_index("c")
        me, sibling = (x, y, c), (x, y, 1 - c)
        chips = [(1 - x, y), (x, 1 - y), (1 - x, 1 - y)]

        def copy(a, k, block, to, src=None):
            s = 4 * block[0] + 2 * block[1] + block[2]
            return pltpu.make_async_remote_copy(
                src_ref=outs[a].at[s] if src is None else src, dst_ref=outs[a].at[s],
                send_sem=send_sems.at[7 * a + k], recv_sem=recv_sems.at[7 * a + k],
                device_id=to, device_id_type=MESH)

        return c, me, sibling, chips, copy

    def own(ins, outs, local_sems, a):
        x, y, c = lax.axis_index("x"), lax.axis_index("y"), lax.axis_index("c")
        return pltpu.make_async_copy(ins[a], outs[a].at[4 * x + 2 * y + c], local_sems.at[a])

    def first_sends(ins, copy, me, sibling, chips, c, a):
        return [copy(a, 0, me, sibling, src=ins[a])] + [
            copy(a, 1 + j, me, (*chip, c), src=ins[a]) for j, chip in enumerate(chips)]

    def start(ins, outs, send_sems, recv_sems, local_sems):
        c, me, sibling, chips, copy = env(ins, outs, send_sems, recv_sems)
        for a in range(n):
            own(ins, outs, local_sems, a).start()
        for a in range(n):
            for cp in first_sends(ins, copy, me, sibling, chips, c, a):
                cp.start()

    def mid(ins, outs, send_sems, recv_sems, local_sems):
        c, me, sibling, chips, copy = env(ins, outs, send_sems, recv_sems)
        for a in range(n):
            for j, chip in enumerate(chips):
                copy(a, 1 + j, (*chip, c), me).wait_recv()
                copy(a, 4 + j, (*chip, c), sibling).start()

    def finish(ins, outs, send_sems, recv_sems, local_sems):
        c, me, sibling, chips, copy = env(ins, outs, send_sems, recv_sems)
        for a in range(n):
            copy(a, 0, sibling, me).wait_recv()
            for j, chip in enumerate(chips):
                copy(a, 4 + j, (*chip, 1 - c), me).wait_recv()
        for a in range(n):
            for cp in first_sends(ins, copy, me, sibling, chips, c, a):
                cp.wait_send()
            for j, chip in enumerate(chips):
                copy(a, 4 + j, (*chip, c), sibling).wait_send()
            own(ins, outs, local_sems, a).wait()

    return _Comm(shards, [jax.ShapeDtypeStruct((N_DEV,) + s.shape, s.dtype) for s in shards], 7 * n,
                 start, mid, finish)


def _sibling_comm(parts):
    n = len(parts)

    def copies(ins, outs, send_sems, recv_sems):
        x, y, c = lax.axis_index("x"), lax.axis_index("y"), lax.axis_index("c")
        return [pltpu.make_async_remote_copy(
            src_ref=ins[a].at[2 * q + (1 - c)], dst_ref=outs[a].at[q],
            send_sem=send_sems.at[4 * a + q], recv_sem=recv_sems.at[4 * a + q],
            device_id=(x, y, 1 - c), device_id_type=MESH) for a in range(n) for q in range(4)]

    def start(ins, outs, send_sems, recv_sems, local_sems):
        for cp in copies(ins, outs, send_sems, recv_sems):
            cp.start()

    def mid(ins, outs, send_sems, recv_sems, local_sems):
        pass

    def finish(ins, outs, send_sems, recv_sems, local_sems):
        for cp in copies(ins, outs, send_sems, recv_sems):
            cp.wait()

    return _Comm(parts, [jax.ShapeDtypeStruct((4,) + p.shape[1:], p.dtype) for p in parts], 4 * n,
                 start, mid, finish)


def _chips_comm(parts):
    n = len(parts)

    def copies(ins, outs, send_sems, recv_sems):
        x, y, c = lax.axis_index("x"), lax.axis_index("y"), lax.axis_index("c")
        chips = [(1 - x, y), (x, 1 - y), (1 - x, 1 - y)]
        return [pltpu.make_async_remote_copy(
            src_ref=ins[a].at[2 * px + py], dst_ref=outs[a].at[j],
            send_sem=send_sems.at[3 * a + j], recv_sem=recv_sems.at[3 * a + j],
            device_id=(px, py, c), device_id_type=MESH) for a in range(n) for j, (px, py) in enumerate(chips)]

    def start(ins, outs, send_sems, recv_sems, local_sems):
        for cp in copies(ins, outs, send_sems, recv_sems):
            cp.start()

    def mid(ins, outs, send_sems, recv_sems, local_sems):
        pass

    def finish(ins, outs, send_sems, recv_sems, local_sems):
        for cp in copies(ins, outs, send_sems, recv_sems):
            cp.wait()

    return _Comm(parts, [jax.ShapeDtypeStruct((3,) + p.shape[1:], p.dtype) for p in parts], 3 * n,
                 start, mid, finish)


def _sibling_add(part, recv, name):
    _, R, C = part.shape
    tr = _pick(R, (256, 128, 80))
    c = lax.axis_index("c")

    def body(c_ref, p_ref, r_ref, o_ref, o16_ref):
        t = p_ref[...] + r_ref[...]
        o_ref[...] = t
        o16_ref[...] = t.astype(BF16)

    blk = pl.BlockSpec((1, tr, C), lambda q, i, c_ref: (q, i, 0))
    return pl.pallas_call(
        body,
        grid_spec=pltpu.PrefetchScalarGridSpec(
            num_scalar_prefetch=1,
            grid=(4, R // tr),
            in_specs=[pl.BlockSpec((1, tr, C), lambda q, i, c_ref: (2 * q + c_ref[0], i, 0)), blk],
            out_specs=(blk, blk),
        ),
        out_shape=(jax.ShapeDtypeStruct((4, R, C), F32), jax.ShapeDtypeStruct((4, R, C), BF16)),
        compiler_params=_params(("parallel", "parallel")),
        name=name,
    )(jnp.reshape(c, (1,)).astype(jnp.int32), part, recv)


def _adamw(w, g, m, v):
    m = ADAM_B1 * m + (1.0 - ADAM_B1) * g
    v = ADAM_B2 * v + (1.0 - ADAM_B2) * (g * g)
    m_hat = m / (1.0 - ADAM_B1 ** ADAM_STEP)
    v_hat = v / (1.0 - ADAM_B2 ** ADAM_STEP)
    delta = -ADAM_LR * (m_hat / (jnp.sqrt(v_hat) + ADAM_EPS) + ADAM_WD * w)
    return delta, m, v


def _adam_big(t, recv, w, m, v, name):
    _, R, C = t.shape
    tr = _pick(R, (256, 128))
    chip = 2 * lax.axis_index("x") + lax.axis_index("y")

    def body(q_ref, t_ref, r_ref, w_ref, m_ref, v_ref, g_ref, d_ref, nm_ref, nv_ref):
        g = t_ref[0] + r_ref[0].astype(F32) + r_ref[1].astype(F32) + r_ref[2].astype(F32)
        g_ref[...] = g
        d_ref[...], nm_ref[...], nv_ref[...] = _adamw(w_ref[...], g, m_ref[...], v_ref[...])

    blk = pl.BlockSpec((tr, C), lambda i, q_ref: (i, 0))
    o = jax.ShapeDtypeStruct((R, C), F32)
    return pl.pallas_call(
        body,
        grid_spec=pltpu.PrefetchScalarGridSpec(
            num_scalar_prefetch=1,
            grid=(R // tr,),
            in_specs=[pl.BlockSpec((1, tr, C), lambda i, q_ref: (q_ref[0], i, 0)),
                      pl.BlockSpec((3, tr, C), lambda i, q_ref: (0, i, 0)), blk, blk, blk],
            out_specs=(blk, blk, blk, blk),
        ),
        out_shape=(o, o, o, o),
        compiler_params=_params(("parallel",)),
        name=name,
    )(jnp.reshape(chip, (1,)).astype(jnp.int32), t, recv, w, m, v)


def _reduce_big(t, recv, name):
    _, R, C = t.shape
    tr = _pick(R, (256, 128, 80))
    chip = 2 * lax.axis_index("x") + lax.axis_index("y")

    def body(q_ref, t_ref, r_ref, g_ref):
        g_ref[...] = t_ref[0] + r_ref[0].astype(F32) + r_ref[1].astype(F32) + r_ref[2].astype(F32)

    return pl.pallas_call(
        body,
        grid_spec=pltpu.PrefetchScalarGridSpec(
            num_scalar_prefetch=1,
            grid=(R // tr,),
            in_specs=[pl.BlockSpec((1, tr, C), lambda i, q_ref: (q_ref[0], i, 0)),
                      pl.BlockSpec((3, tr, C), lambda i, q_ref: (0, i, 0))],
            out_specs=pl.BlockSpec((tr, C), lambda i, q_ref: (i, 0)),
        ),
        out_shape=jax.ShapeDtypeStruct((R, C), F32),
        compiler_params=_params(("parallel",)),
        name=name,
    )(jnp.reshape(chip, (1,)).astype(jnp.int32), t, recv)


def _adam_only(g, w, m, v, name):
    R, C = g.shape
    tr = _pick(R, (256, 128))

    def body(g_ref, w_ref, m_ref, v_ref, d_ref, nm_ref, nv_ref):
        d_ref[...], nm_ref[...], nv_ref[...] = _adamw(w_ref[...], g_ref[...], m_ref[...], v_ref[...])

    blk = pl.BlockSpec((tr, C), lambda i: (i, 0))
    o = jax.ShapeDtypeStruct((R, C), F32)
    return pl.pallas_call(body, grid=(R // tr,), in_specs=[blk] * 4, out_specs=(blk, blk, blk),
                          out_shape=(o, o, o), compiler_params=_params(("parallel",)), name=name)(g, w, m, v)


def _gather8_comm(gbuf):
    def copies(ins, outs, send_sems, recv_sems):
        x, y, c = lax.axis_index("x"), lax.axis_index("y"), lax.axis_index("c")
        me = 4 * x + 2 * y + c
        out = []
        for k in range(1, N_DEV):
            fx, fy, fc = (k >> 2) & 1, (k >> 1) & 1, k & 1
            px, py, pc = x + fx - 2 * x * fx, y + fy - 2 * y * fy, c + fc - 2 * c * fc
            send = pltpu.make_async_remote_copy(
                src_ref=ins[0], dst_ref=outs[0].at[me], send_sem=send_sems.at[k - 1], recv_sem=recv_sems.at[k - 1],
                device_id=(px, py, pc), device_id_type=MESH)
            recv = pltpu.make_async_remote_copy(
                src_ref=ins[0], dst_ref=outs[0].at[4 * px + 2 * py + pc], send_sem=send_sems.at[k - 1],
                recv_sem=recv_sems.at[k - 1], device_id=(px, py, pc), device_id_type=MESH)
            out.append((send, recv))
        return me, out

    def start(ins, outs, send_sems, recv_sems, local_sems):
        me, cps = copies(ins, outs, send_sems, recv_sems)
        pltpu.make_async_copy(ins[0], outs[0].at[me], local_sems.at[0]).start()
        for send, _ in cps:
            send.start()

    def mid(ins, outs, send_sems, recv_sems, local_sems):
        pass

    def finish(ins, outs, send_sems, recv_sems, local_sems):
        me, cps = copies(ins, outs, send_sems, recv_sems)
        for send, recv in cps:
            recv.wait_recv()
            send.wait_send()
        pltpu.make_async_copy(ins[0], outs[0].at[me], local_sems.at[0]).wait()

    return _Comm([gbuf], [jax.ShapeDtypeStruct((N_DEV,) + gbuf.shape, gbuf.dtype)], N_DEV - 1, start, mid, finish)


def _sum8_adam(slots, wbuf, mbuf, vbuf, name):
    R = wbuf.shape[0]

    def body(s_ref, w_ref, m_ref, v_ref, gs_ref, d_ref, nm_ref, nv_ref):
        g = s_ref[0]
        for s in range(1, N_DEV):
            g = g + s_ref[s]
        gs_ref[...] = g
        d_ref[...], nm_ref[...], nv_ref[...] = _adamw(w_ref[...], g, m_ref[...], v_ref[...])

    o = jax.ShapeDtypeStruct((R, LANES), F32)
    return pl.pallas_call(body, out_shape=(o, o, o, o),
                          compiler_params=pltpu.CompilerParams(vmem_limit_bytes=VMEM_LIMIT), name=name)(
        slots, wbuf, mbuf, vbuf)


def _pack(items):
    rows, spans, r0 = [], [], 0
    for a in items:
        n = a.size
        nr = -(-n // LANES)
        rows.append(jnp.pad(a.reshape(-1).astype(F32), (0, nr * LANES - n)).reshape(nr, LANES))
        spans.append((r0, nr, a.shape))
        r0 += nr
    pad = -r0 % SUBLANES
    if pad:
        rows.append(jnp.zeros((pad, LANES), F32))
    return jnp.concatenate(rows, axis=0), spans


def _unpack(buf, spans):
    return [buf[r0:r0 + nr].reshape(-1)[:math.prod(shape)].reshape(shape) for r0, nr, shape in spans]


def kernel(x, norm_mix_pre, norm_mix_post, norm_mlp_pre, norm_mlp_post, w_in, sinks, lam_re, lam_im, log_dt, b_re, b_im, c_re, c_im, d_skip, w_glu, w_branch, w_out, w_up, w_down, loss_target, m_norm_mix_pre, m_norm_mix_post, m_norm_mlp_pre, m_norm_mlp_post, m_w_in, m_sinks, m_lam_re, m_lam_im, m_log_dt, m_b_re, m_b_im, m_c_re, m_c_im, m_d_skip, m_w_glu, m_w_branch, m_w_out, m_w_up, m_w_down, v_norm_mix_pre, v_norm_mix_post, v_norm_mlp_pre, v_norm_mlp_post, v_w_in, v_sinks, v_lam_re, v_lam_im, v_log_dt, v_b_re, v_b_im, v_c_re, v_c_im, v_d_skip, v_w_glu, v_w_branch, v_w_out, v_w_up, v_w_down):
    _, L, D = x.shape
    xs = x[0]
    tgt = loss_target[0]
    ssm_w = D // 2
    n_groups = ssm_w // SSM_GC
    ngb = n_groups // GROUPS_PER_BLOCK
    n_state = n_groups * SSM_P
    d_ff = w_up.shape[2] * N_DEV
    o_k, o_v, o_u = Q_W, Q_W + KV_W, Q_W + 2 * KV_W
    o_ga = o_u + ssm_w
    o_gs = o_ga + D

    big = {"w_in": w_in[0], "w_glu": w_glu[0], "w_branch": w_branch[0], "w_out": w_out[0],
           "w_up": w_up[0], "w_down": w_down[0]}
    col_sharded = ("w_in", "w_glu", "w_up")
    names = list(big)
    shard16 = {k: (big[k].T if k in col_sharded else big[k]).astype(BF16) for k in names}
    full = {}

    def gathered(keys, arrays):
        for k, g in zip(keys, arrays):
            _, r, c = g.shape
            full[k] = g.reshape(N_DEV * r, c)

    def by_owner(g):
        return g.reshape(N_DEV, g.shape[0] // N_DEV, g.shape[1])

    col = lambda a: a.reshape(n_state, 1)
    lr_c, li_c = col(lam_re[0]), col(lam_im[0])
    ldt_c = jnp.repeat(log_dt[0], SSM_P).reshape(n_state, 1)
    b_re_c, b_im_c = b_re[0].reshape(n_state, SSM_GC), b_im[0].reshape(n_state, SSM_GC)
    a_re, a_im, bb_re, bb_im = _disc_fwd(lr_c, li_c, ldt_c, b_re_c, b_im_c)
    a_re3 = a_re.reshape(n_state // LANES, 1, LANES)
    a_im3 = a_im.reshape(n_state // LANES, 1, LANES)
    br_m = _block_diag_in(bb_re, ngb).astype(BF16)
    bi_m = _block_diag_in(bb_im, ngb).astype(BF16)
    cr_m = _block_diag_out(c_re[0], ngb).astype(BF16)
    ci_m = _block_diag_out(c_im[0], ngb).astype(BF16)
    d_row = d_skip[0].reshape(1, ssm_w)
    cos_t, sin_t = _rope_tables(L)

    h, g1 = _rms_pre(xs, norm_mix_pre, _ag_comm([shard16["w_in"]]))
    gathered(["w_in"], g1)
    z, g3 = _mm(h, full["w_in"], mode="nt", name="mm_z",
                comm=_ag_comm([shard16[k] for k in ("w_glu", "w_branch", "w_out")]))
    gathered(["w_glu", "w_branch", "w_out"], g3)
    wb_a, wb_s = full["w_branch"][:Q_W], full["w_branch"][Q_W:]
    o_attn = _attn_fwd(z, cos_t, sin_t, sinks)
    (y_pre, xs_r, xs_i), g1 = _ssm_fwd(z, o_u, br_m, bi_m, cr_m, ci_m, a_re3, a_im3, d_row,
                                       comm=_ag_comm([shard16["w_up"]]))
    gathered(["w_up"], g1)
    gy = _ew(lambda y: (_gelu(y),), [(y_pre, 0)], (BF16,), rows=L, ncols=ssm_w, name="gelu")
    zg = _mm(gy, full["w_glu"], mode="nt", name="mm_zg")
    o_ssm = _ew(lambda a, b: (a * _sigmoid(b),), [(zg, 0), (zg, ssm_w)], (BF16,), rows=L, ncols=ssm_w, name="glu")
    y_attn = _mm(o_attn, wb_a, mode="nn", name="mm_y_attn", out_dtypes=(BF16,))
    y_ssm = _mm(o_ssm, wb_s, mode="nn", name="mm_y_ssm", out_dtypes=(BF16,))
    mix = _ew(lambda ga, gs, ya, ys: (_sigmoid(ga) * ya + _sigmoid(gs) * ys,),
              [(z, o_ga), (z, o_gs), (y_attn, 0), (y_ssm, 0)], (BF16,), rows=L, ncols=D, name="mix")
    mixed = _mm(mix, full["w_out"], mode="nn", name="mm_mixed")
    x1, h2 = _post_pre(xs, mixed, norm_mix_post, norm_mlp_pre)

    def relu_sq(acc):
        a = jnp.maximum(acc, 0.0)
        return a, a * a

    (act, act2), g1 = _mm(h2, full["w_up"], mode="nt", name="mm_up", out_dtypes=(BF16, BF16), epi=relu_sq,
                          comm=_ag_comm([shard16["w_down"]]))
    gathered(["w_down"], g1)
    dn = _mm(act2, full["w_down"], mode="nn", name="mm_down")
    dx2, d_dn, dg_mlp_post, loss_part = _loss_bwd(x1, dn, norm_mlp_post, tgt)

    d_pre = _mm(d_dn, full["w_down"], mode="nt", name="mm_d_act", out_dtypes=(BF16,),
                epi=lambda acc, a: (acc * (2.0 * a.astype(F32)),), extras=(act,))
    gw_down = _mm(act2, d_dn, mode="tn", name="mm_gw_down")
    p_down = by_owner(gw_down)
    dh2, (sib_down,) = _mm(d_pre, full["w_up"], mode="nn", name="mm_dh2", out_dtypes=(BF16,),
                           comm=_sibling_comm([p_down]))
    t_down, t16_down = _sibling_add(p_down, sib_down, "rs_add_w_down")
    gw_up, (chips_down,) = _mm(d_pre, h2, mode="tn", name="mm_gw_up", comm=_chips_comm([t16_down]))
    p_up = by_owner(gw_up)
    dx1, d_mixed, dg_mlp_pre, dg_mix_post = _norm_bwd_pair(x1, dh2, dx2, mixed, norm_mlp_pre, norm_mix_post)
    d_mix, (sib_up,) = _mm(d_mixed, full["w_out"], mode="nt", name="mm_d_mix", out_dtypes=(BF16,),
                           comm=_sibling_comm([p_up]))
    t_up, t16_up = _sibling_add(p_up, sib_up, "rs_add_w_up")
    gw_out = _mm(mix, d_mixed, mode="tn", name="mm_gw_out")
    d_y2, dz = _gate_bwd(d_mix, z, o_ga, y_attn, y_ssm)
    d_o_attn = _mm(d_y2, wb_a, mode="nt", name="mm_d_o_attn", a_win=(0, D))
    gwb_a = _mm(o_attn, d_y2, mode="tn", name="mm_gwb_a", b_win=(0, D))
    d_o_ssm = _mm(d_y2, wb_s, mode="nt", name="mm_d_o_ssm", a_win=(D, D))
    gwb_s = _mm(o_ssm, d_y2, mode="tn", name="mm_gwb_s", b_win=(D, D))

    def glu_bwd(do, a, b):
        s = _sigmoid(b)
        return do * s, do * a * s * (1.0 - s)

    d_zg_a, d_zg_b = _ew(glu_bwd, [(d_o_ssm, 0), (zg, 0), (zg, ssm_w)], (BF16, BF16), rows=L, ncols=ssm_w, name="glu_bwd")
    d_zg = jnp.concatenate([d_zg_a, d_zg_b], axis=1)
    dy_pre = _mm(d_zg, full["w_glu"], mode="nn", name="mm_d_gy",
                 epi=lambda acc, y: (acc * _gelu_grad(y),), extras=(y_pre,))
    gw_glu = _mm(d_zg, gy, mode="tn", name="mm_gw_glu")
    mids = ["w_glu", "w_branch", "w_out"]
    p_mid = [by_owner(gw_glu), by_owner(jnp.concatenate([gwb_a, gwb_s], axis=0)), by_owner(gw_out)]
    sib_mid = _run_comm(_sibling_comm(p_mid), "rs_sibling_mid")
    t_mid = [_sibling_add(p, r, "rs_add_" + k) for k, p, r in zip(mids, p_mid, sib_mid)]
    (dz, g_dskip, g_ar3, g_ai3, g_br_m, g_bi_m, g_cr_m, g_ci_m) = _ssm_bwd(
        dy_pre, z, o_u, xs_r, xs_i, br_m, bi_m, cr_m, ci_m, a_re3, a_im3, d_row, dz)
    g_lr, g_li, g_ldt, g_b_re, g_b_im = _disc_bwd(
        lr_c, li_c, ldt_c, b_re_c, b_im_c, g_ar3.reshape(n_state, 1), g_ai3.reshape(n_state, 1),
        _diag_in(g_br_m, ngb), _diag_in(g_bi_m, ngb))
    (dz, dkc, dkp, dvc, dvp, dsink_rows), chips_a = _attn_bwd(
        z, d_o_attn, cos_t, sin_t, sinks, dz, comm=_chips_comm([t16_up] + [t16 for _, t16 in t_mid]))
    chips_up, chips_mid = chips_a[0], chips_a[1:]
    dz = _kv_combine(dkc, dkp, dvc, dvp, dz)
    dsink = jnp.stack([dsink_rows[:, 0], dsink_rows[:, HEAD_DIM]], axis=1).reshape(1, N_Q_HEADS)
    small_names = ["norm_mix_post", "norm_mlp_pre", "norm_mlp_post", "sinks", "lam_re", "lam_im",
                   "log_dt", "b_re", "b_im", "c_re", "c_im", "d_skip"]
    small_g = [dg_mix_post, dg_mlp_pre, dg_mlp_post, dsink,
               g_lr.reshape(lam_re.shape), g_li.reshape(lam_im.shape), g_ldt.reshape(log_dt.shape),
               g_b_re.reshape(b_re.shape), g_b_im.reshape(b_im.shape),
               _diag_out(g_cr_m, ngb).reshape(c_re.shape), _diag_out(g_ci_m, ngb).reshape(c_im.shape),
               g_dskip.reshape(d_skip.shape)]
    gbuf, spans = _pack(small_g + [loss_part])
    gw_in, (small_slots,) = _mm(dz, h, mode="tn", name="mm_gw_in", comm=_gather8_comm(gbuf))
    p_in = by_owner(gw_in)
    (sib_in,) = _run_comm(_sibling_comm([p_in]), "rs_sibling_w_in")
    t_in, t16_in = _sibling_add(p_in, sib_in, "rs_add_w_in")
    dh, (chips_in,) = _mm(dz, full["w_in"], mode="nn", name="mm_dh", out_dtypes=(BF16,),
                          comm=_chips_comm([t16_in]))
    grad_x, dg_mix_pre = _final_bwd(xs, dh, dx1, norm_mix_pre, None)

    reduced = {"w_in": (t_in, chips_in), "w_up": (t_up, chips_up), "w_down": (t_down, chips_down)}
    for k, (t32, _), r in zip(mids, t_mid, chips_mid):
        reduced[k] = (t32, r)
    moments = {"w_in": (m_w_in, v_w_in), "w_glu": (m_w_glu, v_w_glu), "w_branch": (m_w_branch, v_w_branch),
               "w_out": (m_w_out, v_w_out), "w_up": (m_w_up, v_w_up), "w_down": (m_w_down, v_w_down)}
    big_out = {}
    for k in names:
        t, r = reduced[k]
        mm_, vv_ = moments[k]
        if k in col_sharded:
            g = _reduce_big(t, r, "reduce_" + k).T
            big_out[k] = [o[None] for o in (g,) + tuple(_adam_only(g, big[k], mm_[0], vv_[0], "adam_" + k))]
        else:
            big_out[k] = [o[None] for o in _adam_big(t, r, big[k], mm_[0], vv_[0], "adam_" + k)]

    small_w = [norm_mix_post, norm_mlp_pre, norm_mlp_post, sinks, lam_re, lam_im, log_dt,
               b_re, b_im, c_re, c_im, d_skip]
    small_m = [m_norm_mix_post, m_norm_mlp_pre, m_norm_mlp_post, m_sinks, m_lam_re, m_lam_im,
               m_log_dt, m_b_re, m_b_im, m_c_re, m_c_im, m_d_skip]
    small_v = [v_norm_mix_post, v_norm_mlp_pre, v_norm_mlp_post, v_sinks, v_lam_re, v_lam_im,
               v_log_dt, v_b_re, v_b_im, v_c_re, v_c_im, v_d_skip]
    zero1 = jnp.zeros((1, 1), F32)
    wbuf, _ = _pack(small_w + [zero1])
    mbuf, _ = _pack(small_m + [zero1])
    vbuf, _ = _pack(small_v + [zero1])
    gs, ds, nms, nvs = [_unpack(b, spans) for b in _sum8_adam(small_slots, wbuf, mbuf, vbuf, "small_adam")]
    loss = gs[-1].reshape(())
    tbuf, tspans = _pack([dg_mix_pre])
    (tail_slots,) = _run_comm(_gather8_comm(tbuf), "gather_tail")
    tail = _sum8_adam(tail_slots, _pack([norm_mix_pre])[0], _pack([m_norm_mix_pre])[0],
                      _pack([v_norm_mix_pre])[0], "small_adam_tail")
    small_names = ["norm_mix_pre"] + small_names
    gs, ds, nms, nvs = [_unpack(t, tspans) + src for t, src in zip(tail, (gs, ds, nms, nvs))]

    order = ["norm_mix_pre", "norm_mix_post", "norm_mlp_pre", "norm_mlp_post", "w_in", "sinks", "lam_re", "lam_im",
             "log_dt", "b_re", "b_im", "c_re", "c_im", "d_skip", "w_glu", "w_branch", "w_out", "w_up", "w_down"]
    outs = [loss, grad_x[None]]
    for idx, src in enumerate((gs, ds, nms, nvs)):
        for k in order:
            outs.append(big_out[k][idx] if k in big_out else src[small_names.index(k)])
    return tuple(outs)
```

```python
import functools
import math

import jax
import jax.numpy as jnp
from jax import lax
from jax.experimental import pallas as pl
from jax.experimental.pallas import tpu as pltpu

F32 = jnp.float32
BF16 = jnp.bfloat16
MESH = pl.DeviceIdType.MESH

LANES = 128
SUBLANES = 8
VMEM_LIMIT = 56 * 1024 * 1024

HEAD_DIM = 64
N_Q_HEADS = 16
N_KV_HEADS = 2
Q_W = N_Q_HEADS * HEAD_DIM
KV_W = N_KV_HEADS * HEAD_DIM
BLOCK = 128
ROT_DIM = HEAD_DIM // 4
ROPE_THETA = 500000.0
SSM_GC = 16
SSM_P = 64
GROUPS_PER_BLOCK = 8
NW = GROUPS_PER_BLOCK * SSM_P // LANES
EPS = 1e-6
N_DEV = 8

ADAM_LR = 0.001
ADAM_B1 = 0.9
ADAM_B2 = 0.999
ADAM_EPS = 1e-08
ADAM_WD = 0.01
ADAM_STEP = 10


def _params(sem=None):
    return pltpu.CompilerParams(dimension_semantics=sem, vmem_limit_bytes=VMEM_LIMIT)


def _mid_step(steps):
    return (3 * steps) // 4


def _host_params(sem, comm):
    if comm:
        return pltpu.CompilerParams(dimension_semantics=("arbitrary",) * len(sem), vmem_limit_bytes=VMEM_LIMIT,
                                    has_side_effects=True)
    return _params(sem)


def _pick(dim, prefs):
    for p in prefs:
        if dim % p == 0:
            return p
    return dim


def _sigmoid(x):
    return 1.0 / (1.0 + jnp.exp(-x))


_GELU_C = math.sqrt(2.0 / math.pi)


def _gelu(x):
    return 0.5 * x * (1.0 + jnp.tanh(_GELU_C * (x + 0.044715 * x * x * x)))


def _gelu_grad(x):
    t = jnp.tanh(_GELU_C * (x + 0.044715 * x * x * x))
    return 0.5 * (1.0 + t) + 0.5 * x * (1.0 - t * t) * _GELU_C * (1.0 + 3.0 * 0.044715 * x * x)


_DIMS = {"nn": (((1,), (0,)), ((), ())), "nt": (((1,), (1,)), ((), ())), "tn": (((0,), (0,)), ((), ()))}


def _mm(a, b, *, mode, name, out_dtypes=(F32,), epi=None, extras=(), comm=None, a_win=None, b_win=None):
    ar, ac = a.shape[0], (a_win[1] if a_win else a.shape[1])
    br, bc = b.shape[0], (b_win[1] if b_win else b.shape[1])
    if mode == "nn":
        (M, K), (K2, N) = (ar, ac), (br, bc)
    elif mode == "nt":
        (M, K), (N, K2) = (ar, ac), (br, bc)
    else:
        (K, M), (K2, N) = (ar, ac), (br, bc)
    assert K == K2, (a.shape, b.shape, mode)
    tm = _pick(M, (1024, 1280, 640, 512, 256, 128))
    tn = _pick(N, ((2048,) if K <= 2048 else ()) + (1024, 1280, 640, 512, 384, 256, 128))
    tk = K if K <= 2048 else _pick(K, (2048, 1280, 1024, 640, 512, 256, 128))
    nk = K // tk
    a_col_tile = tm if mode == "tn" else tk
    b_col_tile = tk if mode == "nt" else tn
    ao = a_win[0] // a_col_tile if a_win else 0
    bo = b_win[0] // b_col_tile if b_win else 0
    assert (not a_win or a_win[0] % a_col_tile == 0) and (not b_win or b_win[0] % b_col_tile == 0)
    n_ex = len(extras)
    n_out = len(out_dtypes)
    gi, gj = M // tm, N // tn
    steps = gi * gj * nk

    def body(*refs):
        ins, c_ins, o_refs, c_outs, scratch, sems = _split_refs(refs, 2 + n_ex, n_out, comm)
        a_ref, b_ref = ins[0], ins[1]
        ex_refs = ins[2:]
        if comm:
            s = (pl.program_id(0) * gj + pl.program_id(1)) * nk + pl.program_id(2)
            comm.run(c_ins, c_outs, sems, s == 0, s == _mid_step(steps), s == steps - 1)

        def finish(r):
            outs = (r,) if epi is None else epi(r, *[e[...] for e in ex_refs])
            for o_ref, o in zip(o_refs, outs):
                o_ref[...] = o.astype(o_ref.dtype)

        part = lax.dot_general(a_ref[...].astype(BF16), b_ref[...].astype(BF16), _DIMS[mode],
                               preferred_element_type=F32)
        if nk == 1:
            finish(part)
            return
        acc = scratch[0]
        k = pl.program_id(2)

        @pl.when(k == 0)
        def _():
            acc[...] = part

        @pl.when((k > 0) & (k < nk - 1))
        def _():
            acc[...] += part

        @pl.when(k == nk - 1)
        def _():
            finish(acc[...] + part)

    if mode == "nn":
        a_spec = pl.BlockSpec((tm, tk), lambda i, j, k: (i, k + ao))
        b_spec = pl.BlockSpec((tk, tn), lambda i, j, k: (k, j + bo))
    elif mode == "nt":
        a_spec = pl.BlockSpec((tm, tk), lambda i, j, k: (i, k + ao))
        b_spec = pl.BlockSpec((tn, tk), lambda i, j, k: (j, k + bo))
    else:
        a_spec = pl.BlockSpec((tk, tm), lambda i, j, k: (k, i + ao))
        b_spec = pl.BlockSpec((tk, tn), lambda i, j, k: (k, j + bo))
    o_spec = pl.BlockSpec((tm, tn), lambda i, j, k: (i, j))
    c_ins = comm.ins if comm else []
    c_shapes = comm.out_shapes if comm else []
    res = pl.pallas_call(
        body,
        grid=(gi, gj, nk),
        in_specs=[a_spec, b_spec] + [o_spec] * n_ex + [_ANY] * len(c_ins),
        out_specs=tuple([o_spec] * n_out + [_ANY] * len(c_shapes)),
        out_shape=tuple([jax.ShapeDtypeStruct((M, N), d) for d in out_dtypes] + c_shapes),
        scratch_shapes=([pltpu.VMEM((tm, tn), F32)] if nk > 1 else []) + (comm.scratch() if comm else []),
        compiler_params=_host_params(("parallel", "parallel", "arbitrary"), comm),
        name=name,
    )(a, b, *extras, *c_ins)
    if comm:
        return (res[0] if n_out == 1 else res[:n_out]), list(res[n_out:])
    return res[0] if n_out == 1 else res


def _ew(fn, ins, out_dtypes, *, rows, ncols, name):
    g = ncols
    for _, off in ins:
        g = math.gcd(g, off)
    tc = _pick(g, (512, 256, 128))
    tr = _pick(rows, (2048, 1024, 512, 256, 128))
    n_in = len(ins)

    def body(*refs):
        outs = fn(*[r[...] for r in refs[:n_in]])
        for o_ref, o in zip(refs[n_in:], outs):
            o_ref[...] = o.astype(o_ref.dtype)

    def in_spec(off):
        ob = off // tc
        return pl.BlockSpec((tr, tc), lambda i, j: (i, j + ob))

    o_spec = pl.BlockSpec((tr, tc), lambda i, j: (i, j))
    res = pl.pallas_call(
        body,
        grid=(rows // tr, ncols // tc),
        in_specs=[in_spec(off) for _, off in ins],
        out_specs=tuple([o_spec] * len(out_dtypes)),
        out_shape=tuple(jax.ShapeDtypeStruct((rows, ncols), d) for d in out_dtypes),
        compiler_params=_params(("parallel", "parallel")),
        name=name,
    )(*[arr for arr, _ in ins])
    return res[0] if len(out_dtypes) == 1 else res


def _rstd(x):
    return lax.rsqrt(jnp.mean(x * x, axis=-1, keepdims=True) + EPS)


def _norm_bwd(x, r, g, dy):
    t = dy * g
    dx = r * t - x * (r * r * r) * jnp.mean(t * x, axis=-1, keepdims=True)
    return dx, dy * x * r


def _row_call(body, ins, row_ins, outs, acc_outs, *, rows, width, name, comm=None):
    tr = _pick(rows, (512, 256, 128))
    steps = rows // tr
    t_spec = pl.BlockSpec((tr, width), lambda i: (i, 0))
    r_spec = pl.BlockSpec((1, width), lambda i: (0, 0))
    n_in, n_out = len(ins) + len(row_ins), len(outs) + len(acc_outs)

    def hosted(*refs):
        h_ins, c_ins, h_outs, c_outs, _, sems = _split_refs(refs, n_in, n_out, comm)
        i = pl.program_id(0)
        comm.run(c_ins, c_outs, sems, i == 0, i == _mid_step(steps), i == steps - 1)
        body(*h_ins, *h_outs)

    c_ins = comm.ins if comm else []
    c_shapes = comm.out_shapes if comm else []
    res = pl.pallas_call(
        hosted if comm else body,
        grid=(steps,),
        in_specs=[t_spec] * len(ins) + [r_spec] * len(row_ins) + [_ANY] * len(c_ins),
        out_specs=tuple([t_spec] * len(outs) + [pl.BlockSpec(s, lambda i: (0, 0)) for s in acc_outs]
                        + [_ANY] * len(c_shapes)),
        out_shape=tuple([jax.ShapeDtypeStruct((rows, width), d) for d in outs]
                        + [jax.ShapeDtypeStruct(s, F32) for s in acc_outs] + c_shapes),
        scratch_shapes=comm.scratch() if comm else [],
        compiler_params=_host_params(("arbitrary",), comm),
        name=name,
    )(*ins, *row_ins, *c_ins)
    return (res[:n_out], list(res[n_out:])) if comm else res


def _rms_pre(x, g, comm):
    L, D = x.shape

    def body(x_ref, g_ref, h_ref):
        xv = x_ref[...]
        h_ref[...] = (xv * _rstd(xv) * g_ref[...]).astype(BF16)

    (h,), c_outs = _row_call(body, [x], [g], [BF16], [], rows=L, width=D, name="rms_pre", comm=comm)
    return h, c_outs


def _post_pre(x, mixed, g_post, g_pre):
    L, D = x.shape

    def body(x_ref, m_ref, gp_ref, gq_ref, x1_ref, h2_ref):
        mv = m_ref[...]
        x1 = x_ref[...] + mv * _rstd(mv) * gp_ref[...]
        x1_ref[...] = x1
        h2_ref[...] = (x1 * _rstd(x1) * gq_ref[...]).astype(BF16)

    return _row_call(body, [x, mixed], [g_post, g_pre], [F32, BF16], [], rows=L, width=D, name="post_pre")


def _loss_bwd(x1, dn, g_post, target):
    L, D = x1.shape

    def body(x1_ref, dn_ref, t_ref, g_ref, dx2_ref, ddn_ref, dg_ref, loss_ref):
        @pl.when(pl.program_id(0) == 0)
        def _():
            dg_ref[...] = jnp.zeros_like(dg_ref)
            loss_ref[...] = jnp.zeros_like(loss_ref)

        dnv = dn_ref[...]
        g = g_ref[...]
        r = _rstd(dnv)
        err = x1_ref[...] + dnv * r * g - t_ref[...]
        loss_ref[...] += 0.5 * jnp.sum(jnp.mean(err * err, axis=-1, keepdims=True), axis=0, keepdims=True)
        dx2 = err * (1.0 / D)
        dx2_ref[...] = dx2
        ddn, dgr = _norm_bwd(dnv, r, g, dx2)
        ddn_ref[...] = ddn.astype(BF16)
        dg_ref[...] += jnp.sum(dgr, axis=0, keepdims=True)

    return _row_call(body, [x1, dn, target], [g_post], [F32, BF16], [(1, D), (1, 1)],
                     rows=L, width=D, name="loss_bwd")


def _norm_bwd_pair(x1, dh2, dx2, mixed, g_pre, g_post):
    L, D = x1.shape

    def body(x1_ref, dh_ref, dx2_ref, m_ref, gq_ref, gp_ref, dx1_ref, dm_ref, dgq_ref, dgp_ref):
        @pl.when(pl.program_id(0) == 0)
        def _():
            dgq_ref[...] = jnp.zeros_like(dgq_ref)
            dgp_ref[...] = jnp.zeros_like(dgp_ref)

        x1v = x1_ref[...]
        d1, dgq = _norm_bwd(x1v, _rstd(x1v), gq_ref[...], dh_ref[...].astype(F32))
        dx1 = dx2_ref[...] + d1
        dx1_ref[...] = dx1
        mv = m_ref[...]
        dm, dgp = _norm_bwd(mv, _rstd(mv), gp_ref[...], dx1)
        dm_ref[...] = dm.astype(BF16)
        dgq_ref[...] += jnp.sum(dgq, axis=0, keepdims=True)
        dgp_ref[...] += jnp.sum(dgp, axis=0, keepdims=True)

    return _row_call(body, [x1, dh2, dx2, mixed], [g_pre, g_post], [F32, BF16], [(1, D), (1, D)],
                     rows=L, width=D, name="norm_bwd_pair")


def _final_bwd(x, dh, dx1, g_pre, comm):
    L, D = x.shape

    def body(x_ref, dh_ref, dx1_ref, g_ref, gx_ref, dg_ref):
        @pl.when(pl.program_id(0) == 0)
        def _():
            dg_ref[...] = jnp.zeros_like(dg_ref)

        xv = x_ref[...]
        d0, dg = _norm_bwd(xv, _rstd(xv), g_ref[...], dh_ref[...].astype(F32))
        gx_ref[...] = dx1_ref[...] + d0
        dg_ref[...] += jnp.sum(dg, axis=0, keepdims=True)

    return _row_call(body, [x, dh, dx1], [g_pre], [F32], [(1, D)], rows=L, width=D, name="final_bwd", comm=comm)


def _rope_tables(L):
    half = ROT_DIM // 2
    inv = ROPE_THETA ** (-jnp.arange(half, dtype=F32) * 2.0 / ROT_DIM)
    ang = jnp.arange(L, dtype=F32)[:, None] * inv[None, :]
    d = jnp.arange(LANES) % HEAD_DIM
    a = ang[:, d % half]
    cos_t = jnp.where(d[None, :] < ROT_DIM, jnp.cos(a), 1.0)
    sin_t = jnp.where(d[None, :] < half, -jnp.sin(a), jnp.where(d[None, :] < ROT_DIM, jnp.sin(a), 0.0))
    return cos_t.astype(F32), sin_t.astype(F32)


def _lane_lo(shape):
    return lax.broadcasted_iota(jnp.int32, shape, 1) < HEAD_DIM


def _rope(x, cos_t, sin_t):
    d = lax.broadcasted_iota(jnp.int32, x.shape, 1) % HEAD_DIM
    partner = jnp.where(d < ROT_DIM // 2, pltpu.roll(x, LANES - ROT_DIM // 2, 1), pltpu.roll(x, ROT_DIM // 2, 1))
    return x * cos_t + partner * sin_t


def _dup(kv, g):
    sw = pltpu.roll(kv, HEAD_DIM, 1)
    lo = _lane_lo(kv.shape)
    return jnp.where(lo, kv, sw) if g == 0 else jnp.where(lo, sw, kv)


def _attn_mask(n):
    qi = lax.broadcasted_iota(jnp.int32, (BLOCK, 2 * BLOCK), 0)
    kj = lax.broadcasted_iota(jnp.int32, (BLOCK, 2 * BLOCK), 1)
    rel = qi + BLOCK - kj
    return (rel >= 0) & (rel < BLOCK) & ((kj >= BLOCK) | (n > 0))


def _softmax_sink(s, mask, sink):
    s = jnp.where(mask, s, -1e30)
    m = jnp.maximum(jnp.max(s, axis=-1, keepdims=True), sink)
    e = jnp.where(mask, jnp.exp(s - m), 0.0)
    es = jnp.exp(sink - m)
    inv = 1.0 / (jnp.sum(e, axis=-1, keepdims=True) + es)
    return e * inv, es * inv


_NT = (((1,), (1,)), ((), ()))
_TN = (((0,), (0,)), ((), ()))


def _dot(a, b):
    return jnp.dot(a.astype(BF16), b.astype(BF16), preferred_element_type=F32)


def _dot_nt(a, b):
    return lax.dot_general(a.astype(BF16), b.astype(BF16), _NT, preferred_element_type=F32)


def _dot_tn(a, b):
    return lax.dot_general(a.astype(BF16), b.astype(BF16), _TN, preferred_element_type=F32)


def _attn_specs(nb):
    kcol, vcol = Q_W // LANES, Q_W // LANES + 1
    prev = lambda n: jnp.maximum(n - 1, 0)
    return [
        pl.BlockSpec((BLOCK, Q_W), lambda n: (n, 0)),
        pl.BlockSpec((BLOCK, LANES), lambda n: (n, kcol)),
        pl.BlockSpec((BLOCK, LANES), lambda n: (prev(n), kcol)),
        pl.BlockSpec((BLOCK, LANES), lambda n: (n, vcol)),
        pl.BlockSpec((BLOCK, LANES), lambda n: (prev(n), vcol)),
        pl.BlockSpec((BLOCK, LANES), lambda n: (n, 0)),
        pl.BlockSpec((BLOCK, LANES), lambda n: (prev(n), 0)),
        pl.BlockSpec((BLOCK, LANES), lambda n: (n, 0)),
        pl.BlockSpec((BLOCK, LANES), lambda n: (prev(n), 0)),
        pl.BlockSpec(memory_space=pltpu.SMEM),
    ]


def _attn_prep(refs):
    q_ref, kc_ref, kp_ref, vc_ref, vp_ref, cc_ref, cp_ref, sc_ref, sp_ref = refs
    cos_c, sin_c, cos_p, sin_p = cc_ref[...], sc_ref[...], cp_ref[...], sp_ref[...]
    k2 = jnp.concatenate([_rope(kp_ref[...], cos_p, sin_p), _rope(kc_ref[...], cos_c, sin_c)], axis=0)
    v2 = jnp.concatenate([vp_ref[...], vc_ref[...]], axis=0)
    kd = [_dup(k2, g).astype(BF16) for g in range(N_KV_HEADS)]
    vd = [_dup(v2, g).astype(BF16) for g in range(N_KV_HEADS)]
    return cos_c, sin_c, cos_p, sin_p, kd, vd


def _stack_heads(x):
    lo = _lane_lo(x.shape)
    return jnp.concatenate([jnp.where(lo, x, 0.0), jnp.where(lo, 0.0, x)], axis=0)


def _unstack_heads(x2):
    return jnp.where(_lane_lo((BLOCK, LANES)), x2[:BLOCK], x2[BLOCK:])


def _pair_sinks(sink_ref, t):
    first = lax.broadcasted_iota(jnp.int32, (2 * BLOCK, 1), 0) < BLOCK
    return jnp.where(first, sink_ref[0, 2 * t], sink_ref[0, 2 * t + 1])


_SCALE = 1.0 / math.sqrt(HEAD_DIM)
_TILES = Q_W // LANES
_TILES_PER_KV = _TILES // N_KV_HEADS


def _attn_fwd(z, cos_t, sin_t, sinks, comm=None):
    L = z.shape[0]
    nb = L // BLOCK

    def body(*refs):
        ins, c_ins, outs, c_outs, _, sems = _split_refs(refs, 10, 1, comm)
        q_ref, kc_ref, kp_ref, vc_ref, vp_ref, cc_ref, cp_ref, sc_ref, sp_ref, sink_ref = ins
        o_ref = outs[0]
        n = pl.program_id(0)
        if comm:
            comm.run(c_ins, c_outs, sems, n == 0, n == _mid_step(nb), n == nb - 1)
        cos_c, sin_c, _, _, kd, vd = _attn_prep((q_ref, kc_ref, kp_ref, vc_ref, vp_ref, cc_ref, cp_ref, sc_ref, sp_ref))
        mask = _attn_mask(n)
        mask2 = jnp.concatenate([mask, mask], axis=0)
        for t in range(_TILES):
            g = t // _TILES_PER_KV
            q2 = _stack_heads(_rope(q_ref[:, t * LANES:(t + 1) * LANES], cos_c, sin_c) * _SCALE)
            p2, _ = _softmax_sink(_dot_nt(q2, kd[g]), mask2, _pair_sinks(sink_ref, t))
            o_ref[:, t * LANES:(t + 1) * LANES] = _unstack_heads(_dot(p2, vd[g])).astype(BF16)

    c_ins = comm.ins if comm else []
    c_shapes = comm.out_shapes if comm else []
    res = pl.pallas_call(
        body,
        grid=(nb,),
        in_specs=_attn_specs(nb) + [_ANY] * len(c_ins),
        out_specs=tuple([pl.BlockSpec((BLOCK, Q_W), lambda n: (n, 0))] + [_ANY] * len(c_shapes)),
        out_shape=tuple([jax.ShapeDtypeStruct((L, Q_W), BF16)] + c_shapes),
        scratch_shapes=comm.scratch() if comm else [],
        compiler_params=_host_params(("parallel",), comm),
        name="attn_fwd",
    )(z, z, z, z, z, cos_t, cos_t, sin_t, sin_t, sinks, *c_ins)
    return (res[0], list(res[1:])) if comm else res[0]


def _attn_bwd(z, d_o, cos_t, sin_t, sinks, dz, comm=None):
    L = z.shape[0]
    nb = L // BLOCK

    def body(*refs):
        ins, c_ins, outs, c_outs, _, sems = _split_refs(refs, 12, 6, comm)
        q_ref, kc_ref, kp_ref, vc_ref, vp_ref, cc_ref, cp_ref, sc_ref, sp_ref, sink_ref, do_ref, _ = ins
        dq_ref, dkc_ref, dkp_ref, dvc_ref, dvp_ref, ds_ref = outs
        n = pl.program_id(0)
        if comm:
            comm.run(c_ins, c_outs, sems, n == 0, n == _mid_step(nb), n == nb - 1)

        @pl.when(n == 0)
        def _():
            ds_ref[...] = jnp.zeros_like(ds_ref)

        cos_c, sin_c, cos_p, sin_p, kd, vd = _attn_prep(
            (q_ref, kc_ref, kp_ref, vc_ref, vp_ref, cc_ref, cp_ref, sc_ref, sp_ref))
        mask = _attn_mask(n)
        mask2 = jnp.concatenate([mask, mask], axis=0)
        lo2 = _lane_lo((2 * BLOCK, LANES))
        acc_k = [jnp.zeros((2 * BLOCK, LANES), F32) for _ in range(N_KV_HEADS)]
        acc_v = [jnp.zeros((2 * BLOCK, LANES), F32) for _ in range(N_KV_HEADS)]
        sink_rows = []
        for t in range(_TILES):
            g = t // _TILES_PER_KV
            sl = slice(t * LANES, (t + 1) * LANES)
            q2 = _stack_heads(_rope(q_ref[:, sl], cos_c, sin_c) * _SCALE)
            p2, ps2 = _softmax_sink(_dot_nt(q2, kd[g]), mask2, _pair_sinks(sink_ref, t))
            do2 = _stack_heads(do_ref[:, sl])
            d2 = jnp.sum(do2 * _dot(p2, vd[g]), axis=-1, keepdims=True)
            ds2 = p2 * (_dot_nt(do2, vd[g]) - d2)
            dqt = _unstack_heads(_dot(ds2, kd[g])) * _SCALE
            dq_ref[:, sl] = _rope(dqt, cos_c, -sin_c).astype(BF16)
            acc_k[g] = acc_k[g] + _dot_tn(ds2, q2)
            acc_v[g] = acc_v[g] + _dot_tn(p2, do2)
            sd = ps2 * d2
            sink_rows.append(jnp.where(_lane_lo((1, LANES)), -jnp.sum(sd[:BLOCK], axis=0, keepdims=True),
                                       -jnp.sum(sd[BLOCK:], axis=0, keepdims=True)))
        ds_ref[...] += jnp.concatenate(sink_rows, axis=0)
        fk = [a + pltpu.roll(a, HEAD_DIM, 1) for a in acc_k]
        fv = [a + pltpu.roll(a, HEAD_DIM, 1) for a in acc_v]
        dk2 = jnp.where(lo2, fk[0], fk[1])
        dv2 = jnp.where(lo2, fv[0], fv[1])
        dkp_ref[...] = _rope(dk2[:BLOCK], cos_p, -sin_p)
        dkc_ref[...] = _rope(dk2[BLOCK:], cos_c, -sin_c)
        dvp_ref[...] = dv2[:BLOCK]
        dvc_ref[...] = dv2[BLOCK:]

    blk = pl.BlockSpec((BLOCK, LANES), lambda n: (n, 0))
    kv = jax.ShapeDtypeStruct((L, LANES), F32)
    c_ins = comm.ins if comm else []
    c_shapes = comm.out_shapes if comm else []
    res = pl.pallas_call(
        body,
        grid=(nb,),
        in_specs=_attn_specs(nb) + [pl.BlockSpec((BLOCK, Q_W), lambda n: (n, 0)), _ANY] + [_ANY] * len(c_ins),
        out_specs=tuple([pl.BlockSpec((BLOCK, Q_W), lambda n: (n, 0)), blk, blk, blk, blk,
                         pl.BlockSpec((_TILES, LANES), lambda n: (0, 0))] + [_ANY] * len(c_shapes)),
        out_shape=tuple([jax.ShapeDtypeStruct(dz.shape, dz.dtype), kv, kv, kv, kv,
                         jax.ShapeDtypeStruct((_TILES, LANES), F32)] + c_shapes),
        scratch_shapes=comm.scratch() if comm else [],
        input_output_aliases={11: 0},
        compiler_params=_host_params(("arbitrary",), comm),
        name="attn_bwd",
    )(z, z, z, z, z, cos_t, cos_t, sin_t, sin_t, sinks, d_o, dz, *c_ins)
    return (res[:6], list(res[6:])) if comm else res


def _kv_combine(dkc, dkp, dvc, dvp, dz):
    L = dkc.shape[0]
    tr = _pick(L, (1024, 512, 256, 128))
    per = tr // BLOCK
    nt, nb = L // tr, L // BLOCK

    def body(kc_ref, kp_ref, kn_ref, vc_ref, vp_ref, vn_ref, dz_in, o_ref):
        live = jnp.where(pl.program_id(0) + 1 < nt, 1.0, 0.0)

        def shifted(p_ref, n_ref):
            tail = live * n_ref[...]
            return tail if per == 1 else jnp.concatenate([p_ref[BLOCK:, :], tail], axis=0)

        o_ref[:, :LANES] = (kc_ref[...] + shifted(kp_ref, kn_ref)).astype(BF16)
        o_ref[:, LANES:] = (vc_ref[...] + shifted(vp_ref, vn_ref)).astype(BF16)

    cur = pl.BlockSpec((tr, LANES), lambda n: (n, 0))
    nxt = pl.BlockSpec((BLOCK, LANES), lambda n: (jnp.minimum((n + 1) * per, nb - 1), 0))
    kv_block = Q_W // (2 * KV_W)
    return pl.pallas_call(body, grid=(nt,), in_specs=[cur, cur, nxt, cur, cur, nxt, _ANY],
                          out_specs=pl.BlockSpec((tr, 2 * KV_W), lambda n: (n, kv_block)),
                          out_shape=jax.ShapeDtypeStruct(dz.shape, dz.dtype), input_output_aliases={6: 0},
                          compiler_params=_params(("parallel",)), name="kv_combine")(
        dkc, dkp, dkp, dvc, dvp, dvp, dz)


def _gate_bwd(d_mix, z, o_ga, y_attn, y_ssm):
    L, D = d_mix.shape
    tc = _pick(math.gcd(D, o_ga), (512, 256, 128))
    tr = _pick(L, (2048, 1024, 512, 256, 128))
    nd, gb = D // tc, o_ga // tc

    def body(dm_ref, g_ref, ya_ref, ys_ref, dy_ref, dz_ref):
        dm = dm_ref[...].astype(F32)
        s = _sigmoid(g_ref[...])
        y = jnp.where(pl.program_id(1) < nd, ya_ref[...], ys_ref[...]).astype(F32)
        dy_ref[...] = (dm * s).astype(BF16)
        dz_ref[...] = (dm * y * s * (1.0 - s)).astype(BF16)

    blk = lambda f: pl.BlockSpec((tr, tc), f)
    return pl.pallas_call(
        body,
        grid=(L // tr, 2 * nd),
        in_specs=[blk(lambda i, j: (i, j % nd)), blk(lambda i, j: (i, j + gb)),
                  blk(lambda i, j: (i, jnp.minimum(j, nd - 1))), blk(lambda i, j: (i, jnp.maximum(j - nd, 0)))],
        out_specs=(blk(lambda i, j: (i, j)), blk(lambda i, j: (i, j + gb))),
        out_shape=(jax.ShapeDtypeStruct((L, 2 * D), BF16), jax.ShapeDtypeStruct(z.shape, BF16)),
        compiler_params=_params(("parallel", "arbitrary")),
        name="gate_bwd",
    )(d_mix, z, y_attn, y_ssm)


def _discretise(lr, li, ldt, br, bi):
    dt = jnp.exp(ldt)
    mag = jnp.exp(lr * dt)
    a_re, a_im = mag * jnp.cos(li * dt), mag * jnp.sin(li * dt)
    den = lr * lr + li * li
    nr, ni = a_re - 1.0, a_im
    coef_re = (nr * lr + ni * li) / den
    coef_im = (ni * lr - nr * li) / den
    return a_re, a_im, coef_re * br - coef_im * bi, coef_re * bi + coef_im * br


def _disc_specs(n):
    tr = _pick(n, (512,))
    cs = pl.BlockSpec((tr, 1), lambda i: (i, 0))
    ms = pl.BlockSpec((tr, SSM_GC), lambda i: (i, 0))
    return tr, cs, ms


def _disc_fwd(lr, li, ldt, br, bi):
    n = lr.shape[0]
    tr, cs, ms = _disc_specs(n)

    def body(lr_ref, li_ref, dt_ref, br_ref, bi_ref, o1, o2, o3, o4):
        r = _discretise(lr_ref[...], li_ref[...], dt_ref[...], br_ref[...], bi_ref[...])
        o1[...], o2[...], o3[...], o4[...] = r

    col = jax.ShapeDtypeStruct((n, 1), F32)
    mat = jax.ShapeDtypeStruct((n, SSM_GC), F32)
    return pl.pallas_call(body, grid=(n // tr,), in_specs=[cs, cs, cs, ms, ms], out_specs=(cs, cs, ms, ms),
                          out_shape=(col, col, mat, mat), compiler_params=_params(("parallel",)), name="disc_fwd")(
        lr, li, ldt, br, bi)


def _disc_bwd(lr, li, ldt, br, bi, gar, gai, gbr, gbi):
    n = lr.shape[0]
    tr, cs, ms = _disc_specs(n)

    def body(lr_ref, li_ref, dt_ref, br_ref, bi_ref, gar_ref, gai_ref, gbr_ref, gbi_ref, o_lr, o_li, o_dt, o_br, o_bi):
        _, vjp = jax.vjp(_discretise, lr_ref[...], li_ref[...], dt_ref[...], br_ref[...], bi_ref[...])
        g = vjp((gar_ref[...], gai_ref[...], gbr_ref[...], gbi_ref[...]))
        o_lr[...] = g[0]
        o_li[...] = g[1]
        o_dt[...] = jnp.sum(g[2].reshape(tr // SSM_P, SSM_P, 1), axis=1)
        o_br[...] = g[3]
        o_bi[...] = g[4]

    col = jax.ShapeDtypeStruct((n, 1), F32)
    mat = jax.ShapeDtypeStruct((n, SSM_GC), F32)
    return pl.pallas_call(
        body, grid=(n // tr,), in_specs=[cs, cs, cs, ms, ms, cs, cs, ms, ms],
        out_specs=(cs, cs, pl.BlockSpec((tr // SSM_P, 1), lambda i: (i, 0)), ms, ms),
        out_shape=(col, col, jax.ShapeDtypeStruct((n // SSM_P, 1), F32), mat, mat),
        compiler_params=_params(("parallel",)), name="disc_bwd")(lr, li, ldt, br, bi, gar, gai, gbr, gbi)


def _cpow(ar, ai, nsq):
    for _ in range(nsq):
        ar, ai = ar * ar - ai * ai, 2.0 * ar * ai
    return ar, ai


def _ssm_dims(L):
    tc = min(1024, L)
    seg = tc // SUBLANES
    assert seg & (seg - 1) == 0
    return tc, seg, L // tc, seg.bit_length() - 1


def _tile_rows(i):
    return pl.ds(pl.multiple_of(i * SUBLANES, SUBLANES), SUBLANES)


def _rows_to_segments(src_ref, dst_ref, seg):
    def body(i, _):
        dst_ref[_tile_rows(i), :] = src_ref[pl.ds(i, SUBLANES, stride=seg), :]
        return 0
    lax.fori_loop(0, seg, body, 0, unroll=8)


def _segments_to_rows(src_ref, dst_ref, seg):
    def body(i, _):
        dst_ref[pl.ds(i, SUBLANES, stride=seg), :] = src_ref[_tile_rows(i), :]
        return 0
    lax.fori_loop(0, seg, body, 0, unroll=8)


def _ssm_fwd(z, u_off, br_m, bi_m, cr_m, ci_m, a_re3, a_im3, d_row, comm=None):
    L = z.shape[0]
    ngb = br_m.shape[0]
    tc, seg, nc, nsq = _ssm_dims(L)
    ucol = u_off // LANES

    def body(*refs):
        ins, c_ins, outs, c_outs, scratch, sems = _split_refs(refs, 8, 3, comm)
        u_ref, br_ref, bi_ref, cr_ref, ci_ref, ar_ref, ai_ref, d_ref = ins
        y_ref, xr_ref, xi_ref = outs
        bur, bui, car_r, car_i, ini_r, ini_i, up, ys = scratch
        if comm:
            s = pl.program_id(0) * nc + pl.program_id(1)
            comm.run(c_ins, c_outs, sems, s == 0, s == _mid_step(ngb * nc), s == ngb * nc - 1)

        @pl.when(pl.program_id(1) == 0)
        def _():
            car_r[...] = jnp.zeros_like(car_r)
            car_i[...] = jnp.zeros_like(car_i)

        _rows_to_segments(u_ref, up, seg)
        u = up[...]
        pr = _dot(u, br_ref[0])
        pi = _dot(u, bi_ref[0])
        for w in range(NW):
            bur[w] = pr[:, w * LANES:(w + 1) * LANES]
            bui[w] = pi[:, w * LANES:(w + 1) * LANES]
        ar = [jnp.broadcast_to(ar_ref[w], (SUBLANES, LANES)) for w in range(NW)]
        ai = [jnp.broadcast_to(ai_ref[w], (SUBLANES, LANES)) for w in range(NW)]

        def step(i, carry, store):
            xr, xi = carry
            rows = _tile_rows(i)
            nr, ni = [], []
            for w in range(NW):
                r = ar[w] * xr[w] - ai[w] * xi[w] + bur[w, rows, :]
                m = ar[w] * xi[w] + ai[w] * xr[w] + bui[w, rows, :]
                if store:
                    xr_ref[w, rows, :] = r
                    xi_ref[w, rows, :] = m
                nr.append(r)
                ni.append(m)
            return tuple(nr), tuple(ni)

        zero = tuple(jnp.zeros((SUBLANES, LANES), F32) for _ in range(NW))
        er, ei = lax.fori_loop(0, seg, functools.partial(step, store=False), (zero, zero), unroll=2)
        for w in range(NW):
            pr_, pi_ = _cpow(ar[w][0:1], ai[w][0:1], nsq)
            sr, si = car_r[w, 0:1, :], car_i[w, 0:1, :]
            for j in range(SUBLANES):
                ini_r[w, j:j + 1, :] = sr
                ini_i[w, j:j + 1, :] = si
                sr, si = (pr_ * sr - pi_ * si + er[w][j:j + 1], pr_ * si + pi_ * sr + ei[w][j:j + 1])
            car_r[w, 0:1, :] = sr
            car_i[w, 0:1, :] = si
        init = (tuple(ini_r[w] for w in range(NW)), tuple(ini_i[w] for w in range(NW)))
        lax.fori_loop(0, seg, functools.partial(step, store=True), init, unroll=2)
        acc = d_ref[...] * u
        for w in range(NW):
            sl = slice(w * LANES, (w + 1) * LANES)
            acc = acc + _dot(xr_ref[w], cr_ref[0, sl, :]) - _dot(xi_ref[w], ci_ref[0, sl, :])
        ys[...] = acc
        _segments_to_rows(ys, y_ref, seg)

    mat_b = pl.BlockSpec((1, LANES, NW * LANES), lambda b, k: (b, 0, 0))
    mat_c = pl.BlockSpec((1, NW * LANES, LANES), lambda b, k: (b, 0, 0))
    a_spec = pl.BlockSpec((NW, 1, LANES), lambda b, k: (b, 0, 0))
    x_spec = pl.BlockSpec((NW, tc, LANES), lambda b, k: (b, k, 0))
    xs = jax.ShapeDtypeStruct((ngb * NW, L, LANES), F32)
    st = pltpu.VMEM((NW, SUBLANES, LANES), F32)
    c_ins = comm.ins if comm else []
    c_shapes = comm.out_shapes if comm else []
    res = pl.pallas_call(
        body,
        grid=(ngb, nc),
        in_specs=[pl.BlockSpec((tc, LANES), lambda b, k: (k, b + ucol)), mat_b, mat_b, mat_c, mat_c, a_spec, a_spec,
                  pl.BlockSpec((1, LANES), lambda b, k: (0, b))] + [_ANY] * len(c_ins),
        out_specs=tuple([pl.BlockSpec((tc, LANES), lambda b, k: (k, b)), x_spec, x_spec] + [_ANY] * len(c_shapes)),
        out_shape=tuple([jax.ShapeDtypeStruct((L, ngb * LANES), F32), xs, xs] + c_shapes),
        scratch_shapes=[pltpu.VMEM((NW, tc, LANES), F32), pltpu.VMEM((NW, tc, LANES), F32), st, st, st, st,
                        pltpu.VMEM((tc, LANES), F32), pltpu.VMEM((tc, LANES), F32)]
        + (comm.scratch() if comm else []),
        compiler_params=_host_params(("arbitrary", "arbitrary"), comm),
        name="ssm_fwd",
    )(z, br_m, bi_m, cr_m, ci_m, a_re3, a_im3, d_row, *c_ins)
    return (res[:3], list(res[3:])) if comm else res


def _ssm_bwd(dy, z, u_off, xs_r, xs_i, br_m, bi_m, cr_m, ci_m, a_re3, a_im3, d_row, dz):
    L = z.shape[0]
    ngb = br_m.shape[0]
    tc, seg, nc, nsq = _ssm_dims(L)
    ucol = u_off // LANES

    def body(dy_ref, u_ref, xr_ref, xi_ref, br_ref, bi_ref, cr_ref, ci_ref, ar_ref, ai_ref, d_ref, dz_in,
             du_ref, gd_ref, gar_ref, gai_ref, gbr_ref, gbi_ref, gcr_ref, gci_ref,
             gr_s, gi_s, car_r, car_i, ini_r, ini_i, acc_r, acc_i, dyp, up, dus, dun):
        k = pl.program_id(1)

        @pl.when(k == 0)
        def _():
            for ref in (car_r, car_i, acc_r, acc_i, gd_ref, gbr_ref, gbi_ref, gcr_ref, gci_ref):
                ref[...] = jnp.zeros_like(ref)

        _rows_to_segments(dy_ref, dyp, seg)
        _rows_to_segments(u_ref, up, seg)
        dy_v = dyp[...]
        u = up[...]
        g_re = _dot_nt(dy_v, cr_ref[0])
        g_im = -_dot_nt(dy_v, ci_ref[0])
        for w in range(NW):
            gr_s[w] = g_re[:, w * LANES:(w + 1) * LANES]
            gi_s[w] = g_im[:, w * LANES:(w + 1) * LANES]
        ar = [jnp.broadcast_to(ar_ref[w], (SUBLANES, LANES)) for w in range(NW)]
        ai = [jnp.broadcast_to(ai_ref[w], (SUBLANES, LANES)) for w in range(NW)]

        def step1(ii, carry):
            xr, xi = carry
            rows = _tile_rows(seg - 1 - ii)
            nr = tuple(ar[w] * xr[w] + ai[w] * xi[w] + gr_s[w, rows, :] for w in range(NW))
            ni = tuple(ar[w] * xi[w] - ai[w] * xr[w] + gi_s[w, rows, :] for w in range(NW))
            return nr, ni

        zero = tuple(jnp.zeros((SUBLANES, LANES), F32) for _ in range(NW))
        er, ei = lax.fori_loop(0, seg, step1, (zero, zero), unroll=2)
        for w in range(NW):
            pr_, pi_ = _cpow(ar[w][0:1], ai[w][0:1], nsq)
            sr, si = car_r[w, 0:1, :], car_i[w, 0:1, :]
            for j in reversed(range(SUBLANES)):
                ini_r[w, j:j + 1, :] = sr
                ini_i[w, j:j + 1, :] = si
                sr, si = (pr_ * sr + pi_ * si + er[w][j:j + 1], pr_ * si - pi_ * sr + ei[w][j:j + 1])
            car_r[w, 0:1, :] = sr
            car_i[w, 0:1, :] = si

        def step2(ii, carry):
            gxr, gxi, acr, aci = carry
            rows = _tile_rows(seg - 1 - ii)
            nr, ni, nar, nai = [], [], [], []
            for w in range(NW):
                xr_t, xi_t = xr_ref[w, rows, :], xi_ref[w, rows, :]
                nar.append(acr[w] + gxr[w] * xr_t + gxi[w] * xi_t)
                nai.append(aci[w] + gxi[w] * xr_t - gxr[w] * xi_t)
                r = ar[w] * gxr[w] + ai[w] * gxi[w] + gr_s[w, rows, :]
                m = ar[w] * gxi[w] - ai[w] * gxr[w] + gi_s[w, rows, :]
                gr_s[w, rows, :] = r
                gi_s[w, rows, :] = m
                nr.append(r)
                ni.append(m)
            return tuple(nr), tuple(ni), tuple(nar), tuple(nai)

        init = (tuple(ini_r[w] for w in range(NW)), tuple(ini_i[w] for w in range(NW)),
                tuple(acc_r[w] for w in range(NW)), tuple(acc_i[w] for w in range(NW)))
        _, _, acr, aci = lax.fori_loop(0, seg, step2, init, unroll=2)
        du = d_ref[...] * dy_v
        for w in range(NW):
            sl = slice(w * LANES, (w + 1) * LANES)
            acc_r[w] = acr[w]
            acc_i[w] = aci[w]
            gxr_w, gxi_w = gr_s[w], gi_s[w]
            du = du + _dot_nt(gxr_w, br_ref[0, :, sl]) + _dot_nt(gxi_w, bi_ref[0, :, sl])
            gbr_ref[0, :, sl] += _dot_tn(u, gxr_w)
            gbi_ref[0, :, sl] += _dot_tn(u, gxi_w)
            gcr_ref[0, sl, :] += _dot_tn(xr_ref[w], dy_v)
            gci_ref[0, sl, :] += _dot_tn(-xi_ref[w], dy_v)
        dus[...] = du
        _segments_to_rows(dus, dun, seg)
        du_ref[...] = dun[...].astype(BF16)
        gd_ref[...] += jnp.sum(dy_v * u, axis=0, keepdims=True)

        @pl.when(k == nc - 1)
        def _():
            for w in range(NW):
                gar_ref[w] = jnp.sum(acc_r[w], axis=0, keepdims=True)
                gai_ref[w] = jnp.sum(acc_i[w], axis=0, keepdims=True)

    rk = lambda k: nc - 1 - k
    mat_b = pl.BlockSpec((1, LANES, NW * LANES), lambda b, k: (b, 0, 0))
    mat_c = pl.BlockSpec((1, NW * LANES, LANES), lambda b, k: (b, 0, 0))
    a_spec = pl.BlockSpec((NW, 1, LANES), lambda b, k: (b, 0, 0))
    x_spec = pl.BlockSpec((NW, tc, LANES), lambda b, k: (b, rk(k), 0))
    st = pltpu.VMEM((NW, SUBLANES, LANES), F32)
    big = pltpu.VMEM((NW, tc, LANES), F32)
    return pl.pallas_call(
        body,
        grid=(ngb, nc),
        in_specs=[pl.BlockSpec((tc, LANES), lambda b, k: (rk(k), b)),
                  pl.BlockSpec((tc, LANES), lambda b, k: (rk(k), b + ucol)),
                  x_spec, x_spec, mat_b, mat_b, mat_c, mat_c, a_spec, a_spec,
                  pl.BlockSpec((1, LANES), lambda b, k: (0, b)), _ANY],
        out_specs=(pl.BlockSpec((tc, LANES), lambda b, k: (rk(k), b + ucol)),
                   pl.BlockSpec((1, LANES), lambda b, k: (0, b)), a_spec, a_spec, mat_b, mat_b, mat_c, mat_c),
        out_shape=(jax.ShapeDtypeStruct(dz.shape, dz.dtype),
                   jax.ShapeDtypeStruct((1, ngb * LANES), F32),
                   jax.ShapeDtypeStruct((ngb * NW, 1, LANES), F32), jax.ShapeDtypeStruct((ngb * NW, 1, LANES), F32),
                   jax.ShapeDtypeStruct(br_m.shape, F32), jax.ShapeDtypeStruct(br_m.shape, F32),
                   jax.ShapeDtypeStruct(cr_m.shape, F32), jax.ShapeDtypeStruct(cr_m.shape, F32)),
        scratch_shapes=[big, big, st, st, st, st, st, st] + [pltpu.VMEM((tc, LANES), F32)] * 4,
        input_output_aliases={11: 0},
        compiler_params=_params(("arbitrary", "arbitrary")),
        name="ssm_bwd",
    )(dy, z, xs_r, xs_i, br_m, bi_m, cr_m, ci_m, a_re3, a_im3, d_row, dz)


def _block_diag_in(bb, ngb):
    t = bb.reshape(ngb, GROUPS_PER_BLOCK, SSM_P, SSM_GC).transpose(0, 1, 3, 2)
    eye = jnp.eye(GROUPS_PER_BLOCK, dtype=F32)
    m = t[:, :, :, None, :] * eye[None, :, None, :, None]
    return m.reshape(ngb, GROUPS_PER_BLOCK * SSM_GC, GROUPS_PER_BLOCK * SSM_P)


def _block_diag_out(c, ngb):
    t = c.reshape(ngb, GROUPS_PER_BLOCK, SSM_GC, SSM_P).transpose(0, 1, 3, 2)
    eye = jnp.eye(GROUPS_PER_BLOCK, dtype=F32)
    m = t[:, :, :, None, :] * eye[None, :, None, :, None]
    return m.reshape(ngb, GROUPS_PER_BLOCK * SSM_P, GROUPS_PER_BLOCK * SSM_GC)


def _diag_in(m, ngb):
    m5 = m.reshape(ngb, GROUPS_PER_BLOCK, SSM_GC, GROUPS_PER_BLOCK, SSM_P)
    d = jnp.diagonal(m5, axis1=1, axis2=3)
    return d.transpose(0, 3, 2, 1).reshape(ngb * GROUPS_PER_BLOCK * SSM_P, SSM_GC)


def _diag_out(m, ngb):
    m5 = m.reshape(ngb, GROUPS_PER_BLOCK, SSM_P, GROUPS_PER_BLOCK, SSM_GC)
    d = jnp.diagonal(m5, axis1=1, axis2=3)
    return d.transpose(0, 3, 2, 1).reshape(ngb * GROUPS_PER_BLOCK, SSM_GC, SSM_P)


_ANY = pl.BlockSpec(memory_space=pl.ANY)


class _Comm:
    def __init__(self, ins, out_shapes, n_sem, start, mid, finish):
        self.ins, self.out_shapes, self.n_sem = list(ins), list(out_shapes), n_sem
        self.start, self.mid, self.finish = start, mid, finish

    def scratch(self):
        return [pltpu.SemaphoreType.DMA((self.n_sem,)), pltpu.SemaphoreType.DMA((self.n_sem,)),
                pltpu.SemaphoreType.DMA((len(self.ins),))]

    def run(self, in_refs, out_refs, sems, first, mid, last):
        send, recv, local = sems

        @pl.when(first)
        def _():
            self.start(in_refs, out_refs, send, recv, local)

        @pl.when(mid)
        def _():
            self.mid(in_refs, out_refs, send, recv, local)

        @pl.when(last)
        def _():
            self.finish(in_refs, out_refs, send, recv, local)


def _split_refs(refs, n_in, n_out, comm):
    ci = len(comm.ins) if comm else 0
    co = len(comm.out_shapes) if comm else 0
    ins = refs[:n_in]
    c_ins = refs[n_in:n_in + ci]
    outs = refs[n_in + ci:n_in + ci + n_out]
    c_outs = refs[n_in + ci + n_out:n_in + ci + n_out + co]
    rest = refs[n_in + ci + n_out + co:]
    if comm:
        return ins, c_ins, outs, c_outs, rest[:-3], rest[-3:]
    return ins, c_ins, outs, c_outs, rest, ()


def _run_comm(comm, name):
    ni, no = len(comm.ins), len(comm.out_shapes)

    def body(*refs):
        args = (refs[:ni], refs[ni:ni + no]) + tuple(refs[ni + no:])
        comm.start(*args)
        comm.mid(*args)
        comm.finish(*args)

    return pl.pallas_call(
        body,
        in_specs=[_ANY] * ni,
        out_specs=tuple([_ANY] * no),
        out_shape=tuple(comm.out_shapes),
        scratch_shapes=comm.scratch(),
        compiler_params=pltpu.CompilerParams(has_side_effects=True),
        name=name,
    )(*comm.ins)


def _ag_comm(shards):
    n = len(shards)

    def env(ins, outs, send_sems, recv_sems):
        x, y, c = lax.axis_index("x"), lax.axis_index("y"), lax.axis_index("c")
        me, sibling = (x, y, c), (x, y, 1 - c)
        chips = [(1 - x, y), (x, 1 - y), (1 - x, 1 - y)]

        def copy(a, k, block, to, src=None):
            s = 4 * block[0] + 2 * block[1] + block[2]
            return pltpu.make_async_remote_copy(
                src_ref=outs[a].at[s] if src is None else src, dst_ref=outs[a].at[s],
                send_sem=send_sems.at[7 * a + k], recv_sem=recv_sems.at[7 * a + k],
                device_id=to, device_id_type=MESH)

        return c, me, sibling, chips, copy

    def own(ins, outs, local_sems, a):
        x, y, c = lax.axis_index("x"), lax.axis_index("y"), lax.axis_index("c")
        return pltpu.make_async_copy(ins[a], outs[a].at[4 * x + 2 * y + c], local_sems.at[a])

    def first_sends(ins, copy, me, sibling, chips, c, a):
        return [copy(a, 0, me, sibling, src=ins[a])] + [
            copy(a, 1 + j, me, (*chip, c), src=ins[a]) for j, chip in enumerate(chips)]

    def start(ins, outs, send_sems, recv_sems, local_sems):
        c, me, sibling, chips, copy = env(ins, outs, send_sems, recv_sems)
        for a in range(n):
            own(ins, outs, local_sems, a).start()
        for a in range(n):
            for cp in first_sends(ins, copy, me, sibling, chips, c, a):
                cp.start()

    def mid(ins, outs, send_sems, recv_sems, local_sems):
        c, me, sibling, chips, copy = env(ins, outs, send_sems, recv_sems)
        for a in range(n):
            for j, chip in enumerate(chips):
                copy(a, 1 + j, (*chip, c), me).wait_recv()
                copy(a, 4 + j, (*chip, c), sibling).start()

    def finish(ins, outs, send_sems, recv_sems, local_sems):
        c, me, sibling, chips, copy = env(ins, outs, send_sems, recv_sems)
        for a in range(n):
            copy(a, 0, sibling, me).wait_recv()
            for j, chip in enumerate(chips):
                copy(a, 4 + j, (*chip, 1 - c), me).wait_recv()
        for a in range(n):
            for cp in first_sends(ins, copy, me, sibling, chips, c, a):
                cp.wait_send()
            for j, chip in enumerate(chips):
                copy(a, 4 + j, (*chip, c), sibling).wait_send()
            own(ins, outs, local_sems, a).wait()

    return _Comm(shards, [jax.ShapeDtypeStruct((N_DEV,) + s.shape, s.dtype) for s in shards], 7 * n,
                 start, mid, finish)


def _sibling_comm(parts):
    n = len(parts)

    def copies(ins, outs, send_sems, recv_sems):
        x, y, c = lax.axis_index("x"), lax.axis_index("y"), lax.axis_index("c")
        return [pltpu.make_async_remote_copy(
            src_ref=ins[a].at[2 * q + (1 - c)], dst_ref=outs[a].at[q],
            send_sem=send_sems.at[4 * a + q], recv_sem=recv_sems.at[4 * a + q],
            device_id=(x, y, 1 - c), device_id_type=MESH) for a in range(n) for q in range(4)]

    def start(ins, outs, send_sems, recv_sems, local_sems):
        for cp in copies(ins, outs, send_sems, recv_sems):
            cp.start()

    def mid(ins, outs, send_sems, recv_sems, local_sems):
        pass

    def finish(ins, outs, send_sems, recv_sems, local_sems):
        for cp in copies(ins, outs, send_sems, recv_sems):
            cp.wait()

    return _Comm(parts, [jax.ShapeDtypeStruct((4,) + p.shape[1:], p.dtype) for p in parts], 4 * n,
                 start, mid, finish)


def _chips_comm(parts):
    n = len(parts)

    def copies(ins, outs, send_sems, recv_sems):
        x, y, c = lax.axis_index("x"), lax.axis_index("y"), lax.axis_index("c")
        chips = [(1 - x, y), (x, 1 - y), (1 - x, 1 - y)]
        return [pltpu.make_async_remote_copy(
            src_ref=ins[a].at[2 * px + py], dst_ref=outs[a].at[j],
            send_sem=send_sems.at[3 * a + j], recv_sem=recv_sems.at[3 * a + j],
            device_id=(px, py, c), device_id_type=MESH) for a in range(n) for j, (px, py) in enumerate(chips)]

    def start(ins, outs, send_sems, recv_sems, local_sems):
        for cp in copies(ins, outs, send_sems, recv_sems):
            cp.start()

    def mid(ins, outs, send_sems, recv_sems, local_sems):
        pass

    def finish(ins, outs, send_sems, recv_sems, local_sems):
        for cp in copies(ins, outs, send_sems, recv_sems):
            cp.wait()

    return _Comm(parts, [jax.ShapeDtypeStruct((3,) + p.shape[1:], p.dtype) for p in parts], 3 * n,
                 start, mid, finish)


def _sibling_add(part, recv, name):
    _, R, C = part.shape
    tr = _pick(R, (256, 128, 80))
    c = lax.axis_index("c")

    def body(c_ref, p_ref, r_ref, o_ref, o16_ref):
        t = p_ref[...] + r_ref[...]
        o_ref[...] = t
        o16_ref[...] = t.astype(BF16)

    blk = pl.BlockSpec((1, tr, C), lambda q, i, c_ref: (q, i, 0))
    return pl.pallas_call(
        body,
        grid_spec=pltpu.PrefetchScalarGridSpec(
            num_scalar_prefetch=1,
            grid=(4, R // tr),
            in_specs=[pl.BlockSpec((1, tr, C), lambda q, i, c_ref: (2 * q + c_ref[0], i, 0)), blk],
            out_specs=(blk, blk),
        ),
        out_shape=(jax.ShapeDtypeStruct((4, R, C), F32), jax.ShapeDtypeStruct((4, R, C), BF16)),
        compiler_params=_params(("parallel", "parallel")),
        name=name,
    )(jnp.reshape(c, (1,)).astype(jnp.int32), part, recv)


def _adamw(w, g, m, v):
    m = ADAM_B1 * m + (1.0 - ADAM_B1) * g
    v = ADAM_B2 * v + (1.0 - ADAM_B2) * (g * g)
    m_hat = m / (1.0 - ADAM_B1 ** ADAM_STEP)
    v_hat = v / (1.0 - ADAM_B2 ** ADAM_STEP)
    delta = -ADAM_LR * (m_hat / (jnp.sqrt(v_hat) + ADAM_EPS) + ADAM_WD * w)
    return delta, m, v


def _adam_big(t, recv, w, m, v, name):
    _, R, C = t.shape
    tr = _pick(R, (256, 128))
    chip = 2 * lax.axis_index("x") + lax.axis_index("y")

    def body(q_ref, t_ref, r_ref, w_ref, m_ref, v_ref, g_ref, d_ref, nm_ref, nv_ref):
        g = t_ref[0] + r_ref[0].astype(F32) + r_ref[1].astype(F32) + r_ref[2].astype(F32)
        g_ref[...] = g
        d_ref[...], nm_ref[...], nv_ref[...] = _adamw(w_ref[...], g, m_ref[...], v_ref[...])

    blk = pl.BlockSpec((tr, C), lambda i, q_ref: (i, 0))
    o = jax.ShapeDtypeStruct((R, C), F32)
    return pl.pallas_call(
        body,
        grid_spec=pltpu.PrefetchScalarGridSpec(
            num_scalar_prefetch=1,
            grid=(R // tr,),
            in_specs=[pl.BlockSpec((1, tr, C), lambda i, q_ref: (q_ref[0], i, 0)),
                      pl.BlockSpec((3, tr, C), lambda i, q_ref: (0, i, 0)), blk, blk, blk],
            out_specs=(blk, blk, blk, blk),
        ),
        out_shape=(o, o, o, o),
        compiler_params=_params(("parallel",)),
        name=name,
    )(jnp.reshape(chip, (1,)).astype(jnp.int32), t, recv, w, m, v)


def _reduce_big(t, recv, name):
    _, R, C = t.shape
    tr = _pick(R, (256, 128, 80))
    chip = 2 * lax.axis_index("x") + lax.axis_index("y")

    def body(q_ref, t_ref, r_ref, g_ref):
        g_ref[...] = t_ref[0] + r_ref[0].astype(F32) + r_ref[1].astype(F32) + r_ref[2].astype(F32)

    return pl.pallas_call(
        body,
        grid_spec=pltpu.PrefetchScalarGridSpec(
            num_scalar_prefetch=1,
            grid=(R // tr,),
            in_specs=[pl.BlockSpec((1, tr, C), lambda i, q_ref: (q_ref[0], i, 0)),
                      pl.BlockSpec((3, tr, C), lambda i, q_ref: (0, i, 0))],
            out_specs=pl.BlockSpec((tr, C), lambda i, q_ref: (i, 0)),
        ),
        out_shape=jax.ShapeDtypeStruct((R, C), F32),
        compiler_params=_params(("parallel",)),
        name=name,
    )(jnp.reshape(chip, (1,)).astype(jnp.int32), t, recv)


def _adam_only(g, w, m, v, name):
    R, C = g.shape
    tr = _pick(R, (256, 128))

    def body(g_ref, w_ref, m_ref, v_ref, d_ref, nm_ref, nv_ref):
        d_ref[...], nm_ref[...], nv_ref[...] = _adamw(w_ref[...], g_ref[...], m_ref[...], v_ref[...])

    blk = pl.BlockSpec((tr, C), lambda i: (i, 0))
    o = jax.ShapeDtypeStruct((R, C), F32)
    return pl.pallas_call(body, grid=(R // tr,), in_specs=[blk] * 4, out_specs=(blk, blk, blk),
                          out_shape=(o, o, o), compiler_params=_params(("parallel",)), name=name)(g, w, m, v)


def _gather8_comm(gbuf):
    def copies(ins, outs, send_sems, recv_sems):
        x, y, c = lax.axis_index("x"), lax.axis_index("y"), lax.axis_index("c")
        me = 4 * x + 2 * y + c
        out = []
        for k in range(1, N_DEV):
            fx, fy, fc = (k >> 2) & 1, (k >> 1) & 1, k & 1
            px, py, pc = x + fx - 2 * x * fx, y + fy - 2 * y * fy, c + fc - 2 * c * fc
            send = pltpu.make_async_remote_copy(
                src_ref=ins[0], dst_ref=outs[0].at[me], send_sem=send_sems.at[k - 1], recv_sem=recv_sems.at[k - 1],
                device_id=(px, py, pc), device_id_type=MESH)
            recv = pltpu.make_async_remote_copy(
                src_ref=ins[0], dst_ref=outs[0].at[4 * px + 2 * py + pc], send_sem=send_sems.at[k - 1],
                recv_sem=recv_sems.at[k - 1], device_id=(px, py, pc), device_id_type=MESH)
            out.append((send, recv))
        return me, out

    def start(ins, outs, send_sems, recv_sems, local_sems):
        me, cps = copies(ins, outs, send_sems, recv_sems)
        pltpu.make_async_copy(ins[0], outs[0].at[me], local_sems.at[0]).start()
        for send, _ in cps:
            send.start()

    def mid(ins, outs, send_sems, recv_sems, local_sems):
        pass

    def finish(ins, outs, send_sems, recv_sems, local_sems):
        me, cps = copies(ins, outs, send_sems, recv_sems)
        for send, recv in cps:
            recv.wait_recv()
            send.wait_send()
        pltpu.make_async_copy(ins[0], outs[0].at[me], local_sems.at[0]).wait()

    return _Comm([gbuf], [jax.ShapeDtypeStruct((N_DEV,) + gbuf.shape, gbuf.dtype)], N_DEV - 1, start, mid, finish)


def _sum8_adam(slots, wbuf, mbuf, vbuf, name):
    R = wbuf.shape[0]

    def body(s_ref, w_ref, m_ref, v_ref, gs_ref, d_ref, nm_ref, nv_ref):
        g = s_ref[0]
        for s in range(1, N_DEV):
            g = g + s_ref[s]
        gs_ref[...] = g
        d_ref[...], nm_ref[...], nv_ref[...] = _adamw(w_ref[...], g, m_ref[...], v_ref[...])

    o = jax.ShapeDtypeStruct((R, LANES), F32)
    return pl.pallas_call(body, out_shape=(o, o, o, o),
                          compiler_params=pltpu.CompilerParams(vmem_limit_bytes=VMEM_LIMIT), name=name)(
        slots, wbuf, mbuf, vbuf)


def _pack(items):
    rows, spans, r0 = [], [], 0
    for a in items:
        n = a.size
        nr = -(-n // LANES)
        rows.append(jnp.pad(a.reshape(-1).astype(F32), (0, nr * LANES - n)).reshape(nr, LANES))
        spans.append((r0, nr, a.shape))
        r0 += nr
    pad = -r0 % SUBLANES
    if pad:
        rows.append(jnp.zeros((pad, LANES), F32))
    return jnp.concatenate(rows, axis=0), spans


def _unpack(buf, spans):
    return [buf[r0:r0 + nr].reshape(-1)[:math.prod(shape)].reshape(shape) for r0, nr, shape in spans]


def kernel(x, norm_mix_pre, norm_mix_post, norm_mlp_pre, norm_mlp_post, w_in, sinks, lam_re, lam_im, log_dt, b_re, b_im, c_re, c_im, d_skip, w_glu, w_branch, w_out, w_up, w_down, loss_target, m_norm_mix_pre, m_norm_mix_post, m_norm_mlp_pre, m_norm_mlp_post, m_w_in, m_sinks, m_lam_re, m_lam_im, m_log_dt, m_b_re, m_b_im, m_c_re, m_c_im, m_d_skip, m_w_glu, m_w_branch, m_w_out, m_w_up, m_w_down, v_norm_mix_pre, v_norm_mix_post, v_norm_mlp_pre, v_norm_mlp_post, v_w_in, v_sinks, v_lam_re, v_lam_im, v_log_dt, v_b_re, v_b_im, v_c_re, v_c_im, v_d_skip, v_w_glu, v_w_branch, v_w_out, v_w_up, v_w_down):
    _, L, D = x.shape
    xs = x[0]
    tgt = loss_target[0]
    ssm_w = D // 2
    n_groups = ssm_w // SSM_GC
    ngb = n_groups // GROUPS_PER_BLOCK
    n_state = n_groups * SSM_P
    d_ff = w_up.shape[2] * N_DEV
    o_k, o_v, o_u = Q_W, Q_W + KV_W, Q_W + 2 * KV_W
    o_ga = o_u + ssm_w
    o_gs = o_ga + D

    big = {"w_in": w_in[0], "w_glu": w_glu[0], "w_branch": w_branch[0], "w_out": w_out[0],
           "w_up": w_up[0], "w_down": w_down[0]}
    col_sharded = ("w_in", "w_glu", "w_up")
    names = list(big)
    shard16 = {k: (big[k].T if k in col_sharded else big[k]).astype(BF16) for k in names}
    full = {}

    def gathered(keys, arrays):
        for k, g in zip(keys, arrays):
            _, r, c = g.shape
            full[k] = g.reshape(N_DEV * r, c)

    def by_owner(g):
        return g.reshape(N_DEV, g.shape[0] // N_DEV, g.shape[1])

    col = lambda a: a.reshape(n_state, 1)
    lr_c, li_c = col(lam_re[0]), col(lam_im[0])
    ldt_c = jnp.repeat(log_dt[0], SSM_P).reshape(n_state, 1)
    b_re_c, b_im_c = b_re[0].reshape(n_state, SSM_GC), b_im[0].reshape(n_state, SSM_GC)
    a_re, a_im, bb_re, bb_im = _disc_fwd(lr_c, li_c, ldt_c, b_re_c, b_im_c)
    a_re3 = a_re.reshape(n_state // LANES, 1, LANES)
    a_im3 = a_im.reshape(n_state // LANES, 1, LANES)
    br_m = _block_diag_in(bb_re, ngb).astype(BF16)
    bi_m = _block_diag_in(bb_im, ngb).astype(BF16)
    cr_m = _block_diag_out(c_re[0], ngb).astype(BF16)
    ci_m = _block_diag_out(c_im[0], ngb).astype(BF16)
    d_row = d_skip[0].reshape(1, ssm_w)
    cos_t, sin_t = _rope_tables(L)

    h, g1 = _rms_pre(xs, norm_mix_pre, _ag_comm([shard16["w_in"]]))
    gathered(["w_in"], g1)
    z, g3 = _mm(h, full["w_in"], mode="nt", name="mm_z",
                comm=_ag_comm([shard16[k] for k in ("w_glu", "w_branch", "w_out")]))
    gathered(["w_glu", "w_branch", "w_out"], g3)
    wb_a, wb_s = full["w_branch"][:Q_W], full["w_branch"][Q_W:]
    o_attn = _attn_fwd(z, cos_t, sin_t, sinks)
    (y_pre, xs_r, xs_i), g1 = _ssm_fwd(z, o_u, br_m, bi_m, cr_m, ci_m, a_re3, a_im3, d_row,
                                       comm=_ag_comm([shard16["w_up"]]))
    gathered(["w_up"], g1)
    gy = _ew(lambda y: (_gelu(y),), [(y_pre, 0)], (BF16,), rows=L, ncols=ssm_w, name="gelu")
    zg = _mm(gy, full["w_glu"], mode="nt", name="mm_zg")
    o_ssm = _ew(lambda a, b: (a * _sigmoid(b),), [(zg, 0), (zg, ssm_w)], (BF16,), rows=L, ncols=ssm_w, name="glu")
    y_attn = _mm(o_attn, wb_a, mode="nn", name="mm_y_attn", out_dtypes=(BF16,))
    y_ssm = _mm(o_ssm, wb_s, mode="nn", name="mm_y_ssm", out_dtypes=(BF16,))
    mix = _ew(lambda ga, gs, ya, ys: (_sigmoid(ga) * ya + _sigmoid(gs) * ys,),
              [(z, o_ga), (z, o_gs), (y_attn, 0), (y_ssm, 0)], (BF16,), rows=L, ncols=D, name="mix")
    mixed = _mm(mix, full["w_out"], mode="nn", name="mm_mixed")
    x1, h2 = _post_pre(xs, mixed, norm_mix_post, norm_mlp_pre)

    def relu_sq(acc):
        a = jnp.maximum(acc, 0.0)
        return a, a * a

    (act, act2), g1 = _mm(h2, full["w_up"], mode="nt", name="mm_up", out_dtypes=(BF16, BF16), epi=relu_sq,
                          comm=_ag_comm([shard16["w_down"]]))
    gathered(["w_down"], g1)
    dn = _mm(act2, full["w_down"], mode="nn", name="mm_down")
    dx2, d_dn, dg_mlp_post, loss_part = _loss_bwd(x1, dn, norm_mlp_post, tgt)

    d_pre = _mm(d_dn, full["w_down"], mode="nt", name="mm_d_act", out_dtypes=(BF16,),
                epi=lambda acc, a: (acc * (2.0 * a.astype(F32)),), extras=(act,))
    gw_down = _mm(act2, d_dn, mode="tn", name="mm_gw_down")
    p_down = by_owner(gw_down)
    dh2, (sib_down,) = _mm(d_pre, full["w_up"], mode="nn", name="mm_dh2", out_dtypes=(BF16,),
                           comm=_sibling_comm([p_down]))
    t_down, t16_down = _sibling_add(p_down, sib_down, "rs_add_w_down")
    gw_up, (chips_down,) = _mm(d_pre, h2, mode="tn", name="mm_gw_up", comm=_chips_comm([t16_down]))
    p_up = by_owner(gw_up)
    dx1, d_mixed, dg_mlp_pre, dg_mix_post = _norm_bwd_pair(x1, dh2, dx2, mixed, norm_mlp_pre, norm_mix_post)
    d_mix, (sib_up,) = _mm(d_mixed, full["w_out"], mode="nt", name="mm_d_mix", out_dtypes=(BF16,),
                           comm=_sibling_comm([p_up]))
    t_up, t16_up = _sibling_add(p_up, sib_up, "rs_add_w_up")
    gw_out = _mm(mix, d_mixed, mode="tn", name="mm_gw_out")
    d_y2, dz = _gate_bwd(d_mix, z, o_ga, y_attn, y_ssm)
    d_o_attn = _mm(d_y2, wb_a, mode="nt", name="mm_d_o_attn", a_win=(0, D))
    gwb_a = _mm(o_attn, d_y2, mode="tn", name="mm_gwb_a", b_win=(0, D))
    d_o_ssm = _mm(d_y2, wb_s, mode="nt", name="mm_d_o_ssm", a_win=(D, D))
    gwb_s = _mm(o_ssm, d_y2, mode="tn", name="mm_gwb_s", b_win=(D, D))

    def glu_bwd(do, a, b):
        s = _sigmoid(b)
        return do * s, do * a * s * (1.0 - s)

    d_zg_a, d_zg_b = _ew(glu_bwd, [(d_o_ssm, 0), (zg, 0), (zg, ssm_w)], (BF16, BF16), rows=L, ncols=ssm_w, name="glu_bwd")
    d_zg = jnp.concatenate([d_zg_a, d_zg_b], axis=1)
    dy_pre = _mm(d_zg, full["w_glu"], mode="nn", name="mm_d_gy",
                 epi=lambda acc, y: (acc * _gelu_grad(y),), extras=(y_pre,))
    gw_glu = _mm(d_zg, gy, mode="tn", name="mm_gw_glu")
    mids = ["w_glu", "w_branch", "w_out"]
    p_mid = [by_owner(gw_glu), by_owner(jnp.concatenate([gwb_a, gwb_s], axis=0)), by_owner(gw_out)]
    sib_mid = _run_comm(_sibling_comm(p_mid), "rs_sibling_mid")
    t_mid = [_sibling_add(p, r, "rs_add_" + k) for k, p, r in zip(mids, p_mid, sib_mid)]
    (dz, g_dskip, g_ar3, g_ai3, g_br_m, g_bi_m, g_cr_m, g_ci_m) = _ssm_bwd(
        dy_pre, z, o_u, xs_r, xs_i, br_m, bi_m, cr_m, ci_m, a_re3, a_im3, d_row, dz)
    g_lr, g_li, g_ldt, g_b_re, g_b_im = _disc_bwd(
        lr_c, li_c, ldt_c, b_re_c, b_im_c, g_ar3.reshape(n_state, 1), g_ai3.reshape(n_state, 1),
        _diag_in(g_br_m, ngb), _diag_in(g_bi_m, ngb))
    (dz, dkc, dkp, dvc, dvp, dsink_rows), chips_a = _attn_bwd(
        z, d_o_attn, cos_t, sin_t, sinks, dz, comm=_chips_comm([t16_up] + [t16 for _, t16 in t_mid]))
    chips_up, chips_mid = chips_a[0], chips_a[1:]
    dz = _kv_combine(dkc, dkp, dvc, dvp, dz)
    dsink = jnp.stack([dsink_rows[:, 0], dsink_rows[:, HEAD_DIM]], axis=1).reshape(1, N_Q_HEADS)
    small_names = ["norm_mix_post", "norm_mlp_pre", "norm_mlp_post", "sinks", "lam_re", "lam_im",
                   "log_dt", "b_re", "b_im", "c_re", "c_im", "d_skip"]
    small_g = [dg_mix_post, dg_mlp_pre, dg_mlp_post, dsink,
               g_lr.reshape(lam_re.shape), g_li.reshape(lam_im.shape), g_ldt.reshape(log_dt.shape),
               g_b_re.reshape(b_re.shape), g_b_im.reshape(b_im.shape),
               _diag_out(g_cr_m, ngb).reshape(c_re.shape), _diag_out(g_ci_m, ngb).reshape(c_im.shape),
               g_dskip.reshape(d_skip.shape)]
    gbuf, spans = _pack(small_g + [loss_part])
    gw_in, (small_slots,) = _mm(dz, h, mode="tn", name="mm_gw_in", comm=_gather8_comm(gbuf))
    p_in = by_owner(gw_in)
    (sib_in,) = _run_comm(_sibling_comm([p_in]), "rs_sibling_w_in")
    t_in, t16_in = _sibling_add(p_in, sib_in, "rs_add_w_in")
    dh, (chips_in,) = _mm(dz, full["w_in"], mode="nn", name="mm_dh", out_dtypes=(BF16,),
                          comm=_chips_comm([t16_in]))
    grad_x, dg_mix_pre = _final_bwd(xs, dh, dx1, norm_mix_pre, None)

    reduced = {"w_in": (t_in, chips_in), "w_up": (t_up, chips_up), "w_down": (t_down, chips_down)}
    for k, (t32, _), r in zip(mids, t_mid, chips_mid):
        reduced[k] = (t32, r)
    moments = {"w_in": (m_w_in, v_w_in), "w_glu": (m_w_glu, v_w_glu), "w_branch": (m_w_branch, v_w_branch),
               "w_out": (m_w_out, v_w_out), "w_up": (m_w_up, v_w_up), "w_down": (m_w_down, v_w_down)}
    big_out = {}
    for k in names:
        t, r = reduced[k]
        mm_, vv_ = moments[k]
        if k in col_sharded:
            g = _reduce_big(t, r, "reduce_" + k).T
            big_out[k] = [o[None] for o in (g,) + tuple(_adam_only(g, big[k], mm_[0], vv_[0], "adam_" + k))]
        else:
            big_out[k] = [o[None] for o in _adam_big(t, r, big[k], mm_[0], vv_[0], "adam_" + k)]

    small_w = [norm_mix_post, norm_mlp_pre, norm_mlp_post, sinks, lam_re, lam_im, log_dt,
               b_re, b_im, c_re, c_im, d_skip]
    small_m = [m_norm_mix_post, m_norm_mlp_pre, m_norm_mlp_post, m_sinks, m_lam_re, m_lam_im,
               m_log_dt, m_b_re, m_b_im, m_c_re, m_c_im, m_d_skip]
    small_v = [v_norm_mix_post, v_norm_mlp_pre, v_norm_mlp_post, v_sinks, v_lam_re, v_lam_im,
               v_log_dt, v_b_re, v_b_im, v_c_re, v_c_im, v_d_skip]
    zero1 = jnp.zeros((1, 1), F32)
    wbuf, _ = _pack(small_w + [zero1])
    mbuf, _ = _pack(small_m + [zero1])
    vbuf, _ = _pack(small_v + [zero1])
    gs, ds, nms, nvs = [_unpack(b, spans) for b in _sum8_adam(small_slots, wbuf, mbuf, vbuf, "small_adam")]
    loss = gs[-1].reshape(())
    tbuf, tspans = _pack([dg_mix_pre])
    (tail_slots,) = _run_comm(_gather8_comm(tbuf), "gather_tail")
    tail = _sum8_adam(tail_slots, _pack([norm_mix_pre])[0], _pack([m_norm_mix_pre])[0],
                      _pack([v_norm_mix_pre])[0], "small_adam_tail")
    small_names = ["norm_mix_pre"] + small_names
    gs, ds, nms, nvs = [_unpack(t, tspans) + src for t, src in zip(tail, (gs, ds, nms, nvs))]

    order = ["norm_mix_pre", "norm_mix_post", "norm_mlp_pre", "norm_mlp_post", "w_in", "sinks", "lam_re", "lam_im",
             "log_dt", "b_re", "b_im", "c_re", "c_im", "d_skip", "w_glu", "w_branch", "w_out", "w_up", "w_down"]
    outs = [loss, grad_x[None]]
    for idx, src in enumerate((gs, ds, nms, nvs)):
        for k in order:
            outs.append(big_out[k][idx] if k in big_out else src[small_names.index(k)])
    return tuple(outs)
```

```python
import functools
import math

import jax
import jax.numpy as jnp
from jax import lax
from jax.experimental import pallas as pl
from jax.experimental.pallas import tpu as pltpu

F32 = jnp.float32
BF16 = jnp.bfloat16
MESH = pl.DeviceIdType.MESH

LANES = 128
SUBLANES = 8
VMEM_LIMIT = 56 * 1024 * 1024
MM_TILE_BUDGET = 46 * 1024 * 1024

HEAD_DIM = 64
N_Q_HEADS = 16
N_KV_HEADS = 2
Q_W = N_Q_HEADS * HEAD_DIM
KV_W = N_KV_HEADS * HEAD_DIM
BLOCK = 128
ROT_DIM = HEAD_DIM // 4
ROPE_THETA = 500000.0
SSM_GC = 16
SSM_P = 64
GROUPS_PER_BLOCK = 8
NW = GROUPS_PER_BLOCK * SSM_P // LANES
EPS = 1e-6
N_DEV = 8

ADAM_LR = 0.001
ADAM_B1 = 0.9
ADAM_B2 = 0.999
ADAM_EPS = 1e-08
ADAM_WD = 0.01
ADAM_STEP = 10


def _params(sem=None):
    return pltpu.CompilerParams(dimension_semantics=sem, vmem_limit_bytes=VMEM_LIMIT)


def _mid_step(steps):
    return (3 * steps) // 4


def _host_params(sem, comm):
    if comm:
        return pltpu.CompilerParams(dimension_semantics=("arbitrary",) * len(sem), vmem_limit_bytes=VMEM_LIMIT,
                                    has_side_effects=True)
    return _params(sem)


def _pick(dim, prefs):
    for p in prefs:
        if dim % p == 0:
            return p
    return dim


def _sigmoid(x):
    return 1.0 / (1.0 + jnp.exp(-x))


_GELU_C = math.sqrt(2.0 / math.pi)


def _gelu(x):
    return 0.5 * x * (1.0 + jnp.tanh(_GELU_C * (x + 0.044715 * x * x * x)))


def _gelu_grad(x):
    t = jnp.tanh(_GELU_C * (x + 0.044715 * x * x * x))
    return 0.5 * (1.0 + t) + 0.5 * x * (1.0 - t * t) * _GELU_C * (1.0 + 3.0 * 0.044715 * x * x)


_DIMS = {"nn": (((1,), (0,)), ((), ())), "nt": (((1,), (1,)), ((), ())), "tn": (((0,), (0,)), ((), ()))}


def _mm(a, b, *, mode, name, out_dtypes=(F32,), epi=None, extras=(), comm=None, a_win=None, b_win=None):
    ar, ac = a.shape[0], (a_win[1] if a_win else a.shape[1])
    br, bc = b.shape[0], (b_win[1] if b_win else b.shape[1])
    if mode == "nn":
        (M, K), (K2, N) = (ar, ac), (br, bc)
    elif mode == "nt":
        (M, K), (N, K2) = (ar, ac), (br, bc)
    else:
        (K, M), (K2, N) = (ar, ac), (br, bc)
    assert K == K2, (a.shape, b.shape, mode)
    tm = _pick(M, (1024, 1280, 640, 512, 256, 128))
    tn = _pick(N, ((2048,) if K <= 2048 else ()) + (1024, 1280, 640, 512, 384, 256, 128))
    tk = K if K <= 2048 else _pick(K, (2048, 1280, 1024, 640, 512, 256, 128))
    if K % 4096 == 0 and K > 4096:
        tile_bytes = 2 * 2 * 4096 * (tm + tn) + 4 * tm * tn + 2 * tm * tn * (
            sum(jnp.dtype(d).itemsize for d in out_dtypes) + sum(e.dtype.itemsize for e in extras))
        if tile_bytes <= MM_TILE_BUDGET:
            tk = 4096
    nk = K // tk
    a_col_tile = tm if mode == "tn" else tk
    b_col_tile = tk if mode == "nt" else tn
    ao = a_win[0] // a_col_tile if a_win else 0
    bo = b_win[0] // b_col_tile if b_win else 0
    assert (not a_win or a_win[0] % a_col_tile == 0) and (not b_win or b_win[0] % b_col_tile == 0)
    n_ex = len(extras)
    n_out = len(out_dtypes)
    gi, gj = M // tm, N // tn
    steps = gi * gj * nk

    def body(*refs):
        ins, c_ins, o_refs, c_outs, scratch, sems = _split_refs(refs, 2 + n_ex, n_out, comm)
        a_ref, b_ref = ins[0], ins[1]
        ex_refs = ins[2:]
        if comm:
            s = (pl.program_id(0) * gj + pl.program_id(1)) * nk + pl.program_id(2)
            comm.run(c_ins, c_outs, sems, s == 0, s == _mid_step(steps), s == steps - 1)

        def finish(r):
            outs = (r,) if epi is None else epi(r, *[e[...] for e in ex_refs])
            for o_ref, o in zip(o_refs, outs):
                o_ref[...] = o.astype(o_ref.dtype)

        part = lax.dot_general(a_ref[...].astype(BF16), b_ref[...].astype(BF16), _DIMS[mode],
                               preferred_element_type=F32)
        if nk == 1:
            finish(part)
            return
        acc = scratch[0]
        k = pl.program_id(2)

        @pl.when(k == 0)
        def _():
            acc[...] = part

        @pl.when((k > 0) & (k < nk - 1))
        def _():
            acc[...] += part

        @pl.when(k == nk - 1)
        def _():
            finish(acc[...] + part)

    if mode == "nn":
        a_spec = pl.BlockSpec((tm, tk), lambda i, j, k: (i, k + ao))
        b_spec = pl.BlockSpec((tk, tn), lambda i, j, k: (k, j + bo))
    elif mode == "nt":
        a_spec = pl.BlockSpec((tm, tk), lambda i, j, k: (i, k + ao))
        b_spec = pl.BlockSpec((tn, tk), lambda i, j, k: (j, k + bo))
    else:
        a_spec = pl.BlockSpec((tk, tm), lambda i, j, k: (k, i + ao))
        b_spec = pl.BlockSpec((tk, tn), lambda i, j, k: (k, j + bo))
    o_spec = pl.BlockSpec((tm, tn), lambda i, j, k: (i, j))
    c_ins = comm.ins if comm else []
    c_shapes = comm.out_shapes if comm else []
    res = pl.pallas_call(
        body,
        grid=(gi, gj, nk),
        in_specs=[a_spec, b_spec] + [o_spec] * n_ex + [_ANY] * len(c_ins),
        out_specs=tuple([o_spec] * n_out + [_ANY] * len(c_shapes)),
        out_shape=tuple([jax.ShapeDtypeStruct((M, N), d) for d in out_dtypes] + c_shapes),
        scratch_shapes=([pltpu.VMEM((tm, tn), F32)] if nk > 1 else []) + (comm.scratch() if comm else []),
        compiler_params=_host_params(("parallel", "parallel", "arbitrary"), comm),
        name=name,
    )(a, b, *extras, *c_ins)
    if comm:
        return (res[0] if n_out == 1 else res[:n_out]), list(res[n_out:])
    return res[0] if n_out == 1 else res


def _ew(fn, ins, out_dtypes, *, rows, ncols, name):
    g = ncols
    for _, off in ins:
        g = math.gcd(g, off)
    tc = _pick(g, (512, 256, 128))
    tr = _pick(rows, (2048, 1024, 512, 256, 128))
    n_in = len(ins)

    def body(*refs):
        outs = fn(*[r[...] for r in refs[:n_in]])
        for o_ref, o in zip(refs[n_in:], outs):
            o_ref[...] = o.astype(o_ref.dtype)

    def in_spec(off):
        ob = off // tc
        return pl.BlockSpec((tr, tc), lambda i, j: (i, j + ob))

    o_spec = pl.BlockSpec((tr, tc), lambda i, j: (i, j))
    res = pl.pallas_call(
        body,
        grid=(rows // tr, ncols // tc),
        in_specs=[in_spec(off) for _, off in ins],
        out_specs=tuple([o_spec] * len(out_dtypes)),
        out_shape=tuple(jax.ShapeDtypeStruct((rows, ncols), d) for d in out_dtypes),
        compiler_params=_params(("parallel", "parallel")),
        name=name,
    )(*[arr for arr, _ in ins])
    return res[0] if len(out_dtypes) == 1 else res


def _rstd(x):
    return lax.rsqrt(jnp.mean(x * x, axis=-1, keepdims=True) + EPS)


def _norm_bwd(x, r, g, dy):
    t = dy * g
    dx = r * t - x * (r * r * r) * jnp.mean(t * x, axis=-1, keepdims=True)
    return dx, dy * x * r


def _row_call(body, ins, row_ins, outs, acc_outs, *, rows, width, name, comm=None):
    tr = _pick(rows, (512, 256, 128))
    steps = rows // tr
    t_spec = pl.BlockSpec((tr, width), lambda i: (i, 0))
    r_spec = pl.BlockSpec((1, width), lambda i: (0, 0))
    n_in, n_out = len(ins) + len(row_ins), len(outs) + len(acc_outs)

    def hosted(*refs):
        h_ins, c_ins, h_outs, c_outs, _, sems = _split_refs(refs, n_in, n_out, comm)
        i = pl.program_id(0)
        comm.run(c_ins, c_outs, sems, i == 0, i == _mid_step(steps), i == steps - 1)
        body(*h_ins, *h_outs)

    c_ins = comm.ins if comm else []
    c_shapes = comm.out_shapes if comm else []
    res = pl.pallas_call(
        hosted if comm else body,
        grid=(steps,),
        in_specs=[t_spec] * len(ins) + [r_spec] * len(row_ins) + [_ANY] * len(c_ins),
        out_specs=tuple([t_spec] * len(outs) + [pl.BlockSpec(s, lambda i: (0, 0)) for s in acc_outs]
                        + [_ANY] * len(c_shapes)),
        out_shape=tuple([jax.ShapeDtypeStruct((rows, width), d) for d in outs]
                        + [jax.ShapeDtypeStruct(s, F32) for s in acc_outs] + c_shapes),
        scratch_shapes=comm.scratch() if comm else [],
        compiler_params=_host_params(("arbitrary",), comm),
        name=name,
    )(*ins, *row_ins, *c_ins)
    return (res[:n_out], list(res[n_out:])) if comm else res


def _rms_pre(x, g, comm):
    L, D = x.shape

    def body(x_ref, g_ref, h_ref):
        xv = x_ref[...]
        h_ref[...] = (xv * _rstd(xv) * g_ref[...]).astype(BF16)

    (h,), c_outs = _row_call(body, [x], [g], [BF16], [], rows=L, width=D, name="rms_pre", comm=comm)
    return h, c_outs


def _post_pre(x, mixed, g_post, g_pre):
    L, D = x.shape

    def body(x_ref, m_ref, gp_ref, gq_ref, x1_ref, h2_ref):
        mv = m_ref[...]
        x1 = x_ref[...] + mv * _rstd(mv) * gp_ref[...]
        x1_ref[...] = x1
        h2_ref[...] = (x1 * _rstd(x1) * gq_ref[...]).astype(BF16)

    return _row_call(body, [x, mixed], [g_post, g_pre], [F32, BF16], [], rows=L, width=D, name="post_pre")


def _loss_bwd(x1, dn, g_post, target):
    L, D = x1.shape

    def body(x1_ref, dn_ref, t_ref, g_ref, dx2_ref, ddn_ref, dg_ref, loss_ref):
        @pl.when(pl.program_id(0) == 0)
        def _():
            dg_ref[...] = jnp.zeros_like(dg_ref)
            loss_ref[...] = jnp.zeros_like(loss_ref)

        dnv = dn_ref[...]
        g = g_ref[...]
        r = _rstd(dnv)
        err = x1_ref[...] + dnv * r * g - t_ref[...]
        loss_ref[...] += 0.5 * jnp.sum(jnp.mean(err * err, axis=-1, keepdims=True), axis=0, keepdims=True)
        dx2 = err * (1.0 / D)
        dx2_ref[...] = dx2
        ddn, dgr = _norm_bwd(dnv, r, g, dx2)
        ddn_ref[...] = ddn.astype(BF16)
        dg_ref[...] += jnp.sum(dgr, axis=0, keepdims=True)

    return _row_call(body, [x1, dn, target], [g_post], [F32, BF16], [(1, D), (1, 1)],
                     rows=L, width=D, name="loss_bwd")


def _norm_bwd_pair(x1, dh2, dx2, mixed, g_pre, g_post):
    L, D = x1.shape

    def body(x1_ref, dh_ref, dx2_ref, m_ref, gq_ref, gp_ref, dx1_ref, dm_ref, dgq_ref, dgp_ref):
        @pl.when(pl.program_id(0) == 0)
        def _():
            dgq_ref[...] = jnp.zeros_like(dgq_ref)
            dgp_ref[...] = jnp.zeros_like(dgp_ref)

        x1v = x1_ref[...]
        d1, dgq = _norm_bwd(x1v, _rstd(x1v), gq_ref[...], dh_ref[...].astype(F32))
        dx1 = dx2_ref[...] + d1
        dx1_ref[...] = dx1
        mv = m_ref[...]
        dm, dgp = _norm_bwd(mv, _rstd(mv), gp_ref[...], dx1)
        dm_ref[...] = dm.astype(BF16)
        dgq_ref[...] += jnp.sum(dgq, axis=0, keepdims=True)
        dgp_ref[...] += jnp.sum(dgp, axis=0, keepdims=True)

    return _row_call(body, [x1, dh2, dx2, mixed], [g_pre, g_post], [F32, BF16], [(1, D), (1, D)],
                     rows=L, width=D, name="norm_bwd_pair")


def _final_bwd(x, dh, dx1, g_pre, comm):
    L, D = x.shape

    def body(x_ref, dh_ref, dx1_ref, g_ref, gx_ref, dg_ref):
        @pl.when(pl.program_id(0) == 0)
        def _():
            dg_ref[...] = jnp.zeros_like(dg_ref)

        xv = x_ref[...]
        d0, dg = _norm_bwd(xv, _rstd(xv), g_ref[...], dh_ref[...].astype(F32))
        gx_ref[...] = dx1_ref[...] + d0
        dg_ref[...] += jnp.sum(dg, axis=0, keepdims=True)

    return _row_call(body, [x, dh, dx1], [g_pre], [F32], [(1, D)], rows=L, width=D, name="final_bwd", comm=comm)


def _rope_tables(L):
    half = ROT_DIM // 2
    inv = ROPE_THETA ** (-jnp.arange(half, dtype=F32) * 2.0 / ROT_DIM)
    ang = jnp.arange(L, dtype=F32)[:, None] * inv[None, :]
    d = jnp.arange(LANES) % HEAD_DIM
    a = ang[:, d % half]
    cos_t = jnp.where(d[None, :] < ROT_DIM, jnp.cos(a), 1.0)
    sin_t = jnp.where(d[None, :] < half, -jnp.sin(a), jnp.where(d[None, :] < ROT_DIM, jnp.sin(a), 0.0))
    return cos_t.astype(F32), sin_t.astype(F32)


def _lane_lo(shape):
    return lax.broadcasted_iota(jnp.int32, shape, 1) < HEAD_DIM


def _rope(x, cos_t, sin_t):
    d = lax.broadcasted_iota(jnp.int32, x.shape, 1) % HEAD_DIM
    partner = jnp.where(d < ROT_DIM // 2, pltpu.roll(x, LANES - ROT_DIM // 2, 1), pltpu.roll(x, ROT_DIM // 2, 1))
    return x * cos_t + partner * sin_t


def _dup(kv, g):
    sw = pltpu.roll(kv, HEAD_DIM, 1)
    lo = _lane_lo(kv.shape)
    return jnp.where(lo, kv, sw) if g == 0 else jnp.where(lo, sw, kv)


def _attn_mask(n):
    qi = lax.broadcasted_iota(jnp.int32, (BLOCK, 2 * BLOCK), 0)
    kj = lax.broadcasted_iota(jnp.int32, (BLOCK, 2 * BLOCK), 1)
    rel = qi + BLOCK - kj
    return (rel >= 0) & (rel < BLOCK) & ((kj >= BLOCK) | (n > 0))


def _softmax_sink(s, mask, sink):
    s = jnp.where(mask, s, -1e30)
    m = jnp.maximum(jnp.max(s, axis=-1, keepdims=True), sink)
    e = jnp.where(mask, jnp.exp(s - m), 0.0)
    es = jnp.exp(sink - m)
    inv = 1.0 / (jnp.sum(e, axis=-1, keepdims=True) + es)
    return e * inv, es * inv


_NT = (((1,), (1,)), ((), ()))
_TN = (((0,), (0,)), ((), ()))


def _dot(a, b):
    return jnp.dot(a.astype(BF16), b.astype(BF16), preferred_element_type=F32)


def _dot_nt(a, b):
    return lax.dot_general(a.astype(BF16), b.astype(BF16), _NT, preferred_element_type=F32)


def _dot_tn(a, b):
    return lax.dot_general(a.astype(BF16), b.astype(BF16), _TN, preferred_element_type=F32)


def _attn_specs(nb):
    kcol, vcol = Q_W // LANES, Q_W // LANES + 1
    prev = lambda n: jnp.maximum(n - 1, 0)
    return [
        pl.BlockSpec((BLOCK, Q_W), lambda n: (n, 0)),
        pl.BlockSpec((BLOCK, LANES), lambda n: (n, kcol)),
        pl.BlockSpec((BLOCK, LANES), lambda n: (prev(n), kcol)),
        pl.BlockSpec((BLOCK, LANES), lambda n: (n, vcol)),
        pl.BlockSpec((BLOCK, LANES), lambda n: (prev(n), vcol)),
        pl.BlockSpec((BLOCK, LANES), lambda n: (n, 0)),
        pl.BlockSpec((BLOCK, LANES), lambda n: (prev(n), 0)),
        pl.BlockSpec((BLOCK, LANES), lambda n: (n, 0)),
        pl.BlockSpec((BLOCK, LANES), lambda n: (prev(n), 0)),
        pl.BlockSpec(memory_space=pltpu.SMEM),
    ]


def _attn_prep(refs):
    q_ref, kc_ref, kp_ref, vc_ref, vp_ref, cc_ref, cp_ref, sc_ref, sp_ref = refs
    cos_c, sin_c, cos_p, sin_p = cc_ref[...], sc_ref[...], cp_ref[...], sp_ref[...]
    k2 = jnp.concatenate([_rope(kp_ref[...], cos_p, sin_p), _rope(kc_ref[...], cos_c, sin_c)], axis=0)
    v2 = jnp.concatenate([vp_ref[...], vc_ref[...]], axis=0)
    kd = [_dup(k2, g).astype(BF16) for g in range(N_KV_HEADS)]
    vd = [_dup(v2, g).astype(BF16) for g in range(N_KV_HEADS)]
    return cos_c, sin_c, cos_p, sin_p, kd, vd


def _stack_heads(x):
    lo = _lane_lo(x.shape)
    return jnp.concatenate([jnp.where(lo, x, 0.0), jnp.where(lo, 0.0, x)], axis=0)


def _unstack_heads(x2):
    return jnp.where(_lane_lo((BLOCK, LANES)), x2[:BLOCK], x2[BLOCK:])


def _pair_sinks(sink_ref, t):
    first = lax.broadcasted_iota(jnp.int32, (2 * BLOCK, 1), 0) < BLOCK
    return jnp.where(first, sink_ref[0, 2 * t], sink_ref[0, 2 * t + 1])


_SCALE = 1.0 / math.sqrt(HEAD_DIM)
_TILES = Q_W // LANES
_TILES_PER_KV = _TILES // N_KV_HEADS


def _attn_fwd(z, cos_t, sin_t, sinks, comm=None):
    L = z.shape[0]
    nb = L // BLOCK

    def body(*refs):
        ins, c_ins, outs, c_outs, _, sems = _split_refs(refs, 10, 1, comm)
        q_ref, kc_ref, kp_ref, vc_ref, vp_ref, cc_ref, cp_ref, sc_ref, sp_ref, sink_ref = ins
        o_ref = outs[0]
        n = pl.program_id(0)
        if comm:
            comm.run(c_ins, c_outs, sems, n == 0, n == _mid_step(nb), n == nb - 1)
        cos_c, sin_c, _, _, kd, vd = _attn_prep((q_ref, kc_ref, kp_ref, vc_ref, vp_ref, cc_ref, cp_ref, sc_ref, sp_ref))
        mask = _attn_mask(n)
        mask2 = jnp.concatenate([mask, mask], axis=0)
        for t in range(_TILES):
            g = t // _TILES_PER_KV
            q2 = _stack_heads(_rope(q_ref[:, t * LANES:(t + 1) * LANES], cos_c, sin_c) * _SCALE)
            p2, _ = _softmax_sink(_dot_nt(q2, kd[g]), mask2, _pair_sinks(sink_ref, t))
            o_ref[:, t * LANES:(t + 1) * LANES] = _unstack_heads(_dot(p2, vd[g])).astype(BF16)

    c_ins = comm.ins if comm else []
    c_shapes = comm.out_shapes if comm else []
    res = pl.pallas_call(
        body,
        grid=(nb,),
        in_specs=_attn_specs(nb) + [_ANY] * len(c_ins),
        out_specs=tuple([pl.BlockSpec((BLOCK, Q_W), lambda n: (n, 0))] + [_ANY] * len(c_shapes)),
        out_shape=tuple([jax.ShapeDtypeStruct((L, Q_W), BF16)] + c_shapes),
        scratch_shapes=comm.scratch() if comm else [],
        compiler_params=_host_params(("parallel",), comm),
        name="attn_fwd",
    )(z, z, z, z, z, cos_t, cos_t, sin_t, sin_t, sinks, *c_ins)
    return (res[0], list(res[1:])) if comm else res[0]


def _attn_bwd(z, d_o, cos_t, sin_t, sinks, dz, comm=None):
    L = z.shape[0]
    nb = L // BLOCK

    def body(*refs):
        ins, c_ins, outs, c_outs, _, sems = _split_refs(refs, 12, 6, comm)
        q_ref, kc_ref, kp_ref, vc_ref, vp_ref, cc_ref, cp_ref, sc_ref, sp_ref, sink_ref, do_ref, _ = ins
        dq_ref, dkc_ref, dkp_ref, dvc_ref, dvp_ref, ds_ref = outs
        n = pl.program_id(0)
        if comm:
            comm.run(c_ins, c_outs, sems, n == 0, n == _mid_step(nb), n == nb - 1)

        @pl.when(n == 0)
        def _():
            ds_ref[...] = jnp.zeros_like(ds_ref)

        cos_c, sin_c, cos_p, sin_p, kd, vd = _attn_prep(
            (q_ref, kc_ref, kp_ref, vc_ref, vp_ref, cc_ref, cp_ref, sc_ref, sp_ref))
        mask = _attn_mask(n)
        mask2 = jnp.concatenate([mask, mask], axis=0)
        lo2 = _lane_lo((2 * BLOCK, LANES))
        acc_k = [jnp.zeros((2 * BLOCK, LANES), F32) for _ in range(N_KV_HEADS)]
        acc_v = [jnp.zeros((2 * BLOCK, LANES), F32) for _ in range(N_KV_HEADS)]
        sink_rows = []
        for t in range(_TILES):
            g = t // _TILES_PER_KV
            sl = slice(t * LANES, (t + 1) * LANES)
            q2 = _stack_heads(_rope(q_ref[:, sl], cos_c, sin_c) * _SCALE)
            p2, ps2 = _softmax_sink(_dot_nt(q2, kd[g]), mask2, _pair_sinks(sink_ref, t))
            do2 = _stack_heads(do_ref[:, sl])
            d2 = jnp.sum(do2 * _dot(p2, vd[g]), axis=-1, keepdims=True)
            ds2 = p2 * (_dot_nt(do2, vd[g]) - d2)
            dqt = _unstack_heads(_dot(ds2, kd[g])) * _SCALE
            dq_ref[:, sl] = _rope(dqt, cos_c, -sin_c).astype(BF16)
            acc_k[g] = acc_k[g] + _dot_tn(ds2, q2)
            acc_v[g] = acc_v[g] + _dot_tn(p2, do2)
            sd = ps2 * d2
            sink_rows.append(jnp.where(_lane_lo((1, LANES)), -jnp.sum(sd[:BLOCK], axis=0, keepdims=True),
                                       -jnp.sum(sd[BLOCK:], axis=0, keepdims=True)))
        ds_ref[...] += jnp.concatenate(sink_rows, axis=0)
        fk = [a + pltpu.roll(a, HEAD_DIM, 1) for a in acc_k]
        fv = [a + pltpu.roll(a, HEAD_DIM, 1) for a in acc_v]
        dk2 = jnp.where(lo2, fk[0], fk[1])
        dv2 = jnp.where(lo2, fv[0], fv[1])
        dkp_ref[...] = _rope(dk2[:BLOCK], cos_p, -sin_p)
        dkc_ref[...] = _rope(dk2[BLOCK:], cos_c, -sin_c)
        dvp_ref[...] = dv2[:BLOCK]
        dvc_ref[...] = dv2[BLOCK:]

    blk = pl.BlockSpec((BLOCK, LANES), lambda n: (n, 0))
    kv = jax.ShapeDtypeStruct((L, LANES), F32)
    c_ins = comm.ins if comm else []
    c_shapes = comm.out_shapes if comm else []
    res = pl.pallas_call(
        body,
        grid=(nb,),
        in_specs=_attn_specs(nb) + [pl.BlockSpec((BLOCK, Q_W), lambda n: (n, 0)), _ANY] + [_ANY] * len(c_ins),
        out_specs=tuple([pl.BlockSpec((BLOCK, Q_W), lambda n: (n, 0)), blk, blk, blk, blk,
                         pl.BlockSpec((_TILES, LANES), lambda n: (0, 0))] + [_ANY] * len(c_shapes)),
        out_shape=tuple([jax.ShapeDtypeStruct(dz.shape, dz.dtype), kv, kv, kv, kv,
                         jax.ShapeDtypeStruct((_TILES, LANES), F32)] + c_shapes),
        scratch_shapes=comm.scratch() if comm else [],
        input_output_aliases={11: 0},
        compiler_params=_host_params(("arbitrary",), comm),
        name="attn_bwd",
    )(z, z, z, z, z, cos_t, cos_t, sin_t, sin_t, sinks, d_o, dz, *c_ins)
    return (res[:6], list(res[6:])) if comm else res


def _kv_combine(dkc, dkp, dvc, dvp, dz):
    L = dkc.shape[0]
    tr = _pick(L, (1024, 512, 256, 128))
    per = tr // BLOCK
    nt, nb = L // tr, L // BLOCK

    def body(kc_ref, kp_ref, kn_ref, vc_ref, vp_ref, vn_ref, dz_in, o_ref):
        live = jnp.where(pl.program_id(0) + 1 < nt, 1.0, 0.0)

        def shifted(p_ref, n_ref):
            tail = live * n_ref[...]
            return tail if per == 1 else jnp.concatenate([p_ref[BLOCK:, :], tail], axis=0)

        o_ref[:, :LANES] = (kc_ref[...] + shifted(kp_ref, kn_ref)).astype(BF16)
        o_ref[:, LANES:] = (vc_ref[...] + shifted(vp_ref, vn_ref)).astype(BF16)

    cur = pl.BlockSpec((tr, LANES), lambda n: (n, 0))
    nxt = pl.BlockSpec((BLOCK, LANES), lambda n: (jnp.minimum((n + 1) * per, nb - 1), 0))
    kv_block = Q_W // (2 * KV_W)
    return pl.pallas_call(body, grid=(nt,), in_specs=[cur, cur, nxt, cur, cur, nxt, _ANY],
                          out_specs=pl.BlockSpec((tr, 2 * KV_W), lambda n: (n, kv_block)),
                          out_shape=jax.ShapeDtypeStruct(dz.shape, dz.dtype), input_output_aliases={6: 0},
                          compiler_params=_params(("parallel",)), name="kv_combine")(
        dkc, dkp, dkp, dvc, dvp, dvp, dz)


def _gate_bwd(d_mix, z, o_ga, y_attn, y_ssm):
    L, D = d_mix.shape
    tc = _pick(math.gcd(D, o_ga), (512, 256, 128))
    tr = _pick(L, (2048, 1024, 512, 256, 128))
    nd, gb = D // tc, o_ga // tc

    def body(dm_ref, g_ref, ya_ref, ys_ref, dy_ref, dz_ref):
        dm = dm_ref[...].astype(F32)
        s = _sigmoid(g_ref[...])
        y = jnp.where(pl.program_id(1) < nd, ya_ref[...], ys_ref[...]).astype(F32)
        dy_ref[...] = (dm * s).astype(BF16)
        dz_ref[...] = (dm * y * s * (1.0 - s)).astype(BF16)

    blk = lambda f: pl.BlockSpec((tr, tc), f)
    return pl.pallas_call(
        body,
        grid=(L // tr, 2 * nd),
        in_specs=[blk(lambda i, j: (i, j % nd)), blk(lambda i, j: (i, j + gb)),
                  blk(lambda i, j: (i, jnp.minimum(j, nd - 1))), blk(lambda i, j: (i, jnp.maximum(j - nd, 0)))],
        out_specs=(blk(lambda i, j: (i, j)), blk(lambda i, j: (i, j + gb))),
        out_shape=(jax.ShapeDtypeStruct((L, 2 * D), BF16), jax.ShapeDtypeStruct(z.shape, BF16)),
        compiler_params=_params(("parallel", "arbitrary")),
        name="gate_bwd",
    )(d_mix, z, y_attn, y_ssm)


def _discretise(lr, li, ldt, br, bi):
    dt = jnp.exp(ldt)
    mag = jnp.exp(lr * dt)
    a_re, a_im = mag * jnp.cos(li * dt), mag * jnp.sin(li * dt)
    den = lr * lr + li * li
    nr, ni = a_re - 1.0, a_im
    coef_re = (nr * lr + ni * li) / den
    coef_im = (ni * lr - nr * li) / den
    return a_re, a_im, coef_re * br - coef_im * bi, coef_re * bi + coef_im * br


def _disc_specs(n):
    tr = _pick(n, (512,))
    cs = pl.BlockSpec((tr, 1), lambda i: (i, 0))
    ms = pl.BlockSpec((tr, SSM_GC), lambda i: (i, 0))
    return tr, cs, ms


def _disc_fwd(lr, li, ldt, br, bi):
    n = lr.shape[0]
    tr, cs, ms = _disc_specs(n)

    def body(lr_ref, li_ref, dt_ref, br_ref, bi_ref, o1, o2, o3, o4):
        r = _discretise(lr_ref[...], li_ref[...], dt_ref[...], br_ref[...], bi_ref[...])
        o1[...], o2[...], o3[...], o4[...] = r

    col = jax.ShapeDtypeStruct((n, 1), F32)
    mat = jax.ShapeDtypeStruct((n, SSM_GC), F32)
    return pl.pallas_call(body, grid=(n // tr,), in_specs=[cs, cs, cs, ms, ms], out_specs=(cs, cs, ms, ms),
                          out_shape=(col, col, mat, mat), compiler_params=_params(("parallel",)), name="disc_fwd")(
        lr, li, ldt, br, bi)


def _disc_bwd(lr, li, ldt, br, bi, gar, gai, gbr, gbi):
    n = lr.shape[0]
    tr, cs, ms = _disc_specs(n)

    def body(lr_ref, li_ref, dt_ref, br_ref, bi_ref, gar_ref, gai_ref, gbr_ref, gbi_ref, o_lr, o_li, o_dt, o_br, o_bi):
        _, vjp = jax.vjp(_discretise, lr_ref[...], li_ref[...], dt_ref[...], br_ref[...], bi_ref[...])
        g = vjp((gar_ref[...], gai_ref[...], gbr_ref[...], gbi_ref[...]))
        o_lr[...] = g[0]
        o_li[...] = g[1]
        o_dt[...] = jnp.sum(g[2].reshape(tr // SSM_P, SSM_P, 1), axis=1)
        o_br[...] = g[3]
        o_bi[...] = g[4]

    col = jax.ShapeDtypeStruct((n, 1), F32)
    mat = jax.ShapeDtypeStruct((n, SSM_GC), F32)
    return pl.pallas_call(
        body, grid=(n // tr,), in_specs=[cs, cs, cs, ms, ms, cs, cs, ms, ms],
        out_specs=(cs, cs, pl.BlockSpec((tr // SSM_P, 1), lambda i: (i, 0)), ms, ms),
        out_shape=(col, col, jax.ShapeDtypeStruct((n // SSM_P, 1), F32), mat, mat),
        compiler_params=_params(("parallel",)), name="disc_bwd")(lr, li, ldt, br, bi, gar, gai, gbr, gbi)


def _cpow(ar, ai, nsq):
    for _ in range(nsq):
        ar, ai = ar * ar - ai * ai, 2.0 * ar * ai
    return ar, ai


def _ssm_dims(L):
    tc = min(1024, L)
    seg = tc // SUBLANES
    assert seg & (seg - 1) == 0
    return tc, seg, L // tc, seg.bit_length() - 1


def _tile_rows(i):
    return pl.ds(pl.multiple_of(i * SUBLANES, SUBLANES), SUBLANES)


def _rows_to_segments(src_ref, dst_ref, seg):
    def body(i, _):
        dst_ref[_tile_rows(i), :] = src_ref[pl.ds(i, SUBLANES, stride=seg), :]
        return 0
    lax.fori_loop(0, seg, body, 0, unroll=8)


def _segments_to_rows(src_ref, dst_ref, seg):
    def body(i, _):
        dst_ref[pl.ds(i, SUBLANES, stride=seg), :] = src_ref[_tile_rows(i), :]
        return 0
    lax.fori_loop(0, seg, body, 0, unroll=8)


def _ssm_fwd(z, u_off, br_m, bi_m, cr_m, ci_m, a_re3, a_im3, d_row, comm=None):
    L = z.shape[0]
    ngb = br_m.shape[0]
    tc, seg, nc, nsq = _ssm_dims(L)
    ucol = u_off // LANES

    def body(*refs):
        ins, c_ins, outs, c_outs, scratch, sems = _split_refs(refs, 8, 3, comm)
        u_ref, br_ref, bi_ref, cr_ref, ci_ref, ar_ref, ai_ref, d_ref = ins
        y_ref, xr_ref, xi_ref = outs
        bur, bui, car_r, car_i, ini_r, ini_i, up, ys = scratch
        if comm:
            s = pl.program_id(0) * nc + pl.program_id(1)
            comm.run(c_ins, c_outs, sems, s == 0, s == _mid_step(ngb * nc), s == ngb * nc - 1)

        @pl.when(pl.program_id(1) == 0)
        def _():
            car_r[...] = jnp.zeros_like(car_r)
            car_i[...] = jnp.zeros_like(car_i)

        _rows_to_segments(u_ref, up, seg)
        u = up[...]
        pr = _dot(u, br_ref[0])
        pi = _dot(u, bi_ref[0])
        for w in range(NW):
            bur[w] = pr[:, w * LANES:(w + 1) * LANES]
            bui[w] = pi[:, w * LANES:(w + 1) * LANES]
        ar = [jnp.broadcast_to(ar_ref[w], (SUBLANES, LANES)) for w in range(NW)]
        ai = [jnp.broadcast_to(ai_ref[w], (SUBLANES, LANES)) for w in range(NW)]

        def step(i, carry, store):
            xr, xi = carry
            rows = _tile_rows(i)
            nr, ni = [], []
            for w in range(NW):
                r = ar[w] * xr[w] - ai[w] * xi[w] + bur[w, rows, :]
                m = ar[w] * xi[w] + ai[w] * xr[w] + bui[w, rows, :]
                if store:
                    xr_ref[w, rows, :] = r
                    xi_ref[w, rows, :] = m
                nr.append(r)
                ni.append(m)
            return tuple(nr), tuple(ni)

        zero = tuple(jnp.zeros((SUBLANES, LANES), F32) for _ in range(NW))
        er, ei = lax.fori_loop(0, seg, functools.partial(step, store=False), (zero, zero), unroll=2)
        for w in range(NW):
            pr_, pi_ = _cpow(ar[w][0:1], ai[w][0:1], nsq)
            sr, si = car_r[w, 0:1, :], car_i[w, 0:1, :]
            for j in range(SUBLANES):
                ini_r[w, j:j + 1, :] = sr
                ini_i[w, j:j + 1, :] = si
                sr, si = (pr_ * sr - pi_ * si + er[w][j:j + 1], pr_ * si + pi_ * sr + ei[w][j:j + 1])
            car_r[w, 0:1, :] = sr
            car_i[w, 0:1, :] = si
        init = (tuple(ini_r[w] for w in range(NW)), tuple(ini_i[w] for w in range(NW)))
        lax.fori_loop(0, seg, functools.partial(step, store=True), init, unroll=2)
        acc = d_ref[...] * u
        for w in range(NW):
            sl = slice(w * LANES, (w + 1) * LANES)
            acc = acc + _dot(xr_ref[w], cr_ref[0, sl, :]) - _dot(xi_ref[w], ci_ref[0, sl, :])
        ys[...] = acc
        _segments_to_rows(ys, y_ref, seg)

    mat_b = pl.BlockSpec((1, LANES, NW * LANES), lambda b, k: (b, 0, 0))
    mat_c = pl.BlockSpec((1, NW * LANES, LANES), lambda b, k: (b, 0, 0))
    a_spec = pl.BlockSpec((NW, 1, LANES), lambda b, k: (b, 0, 0))
    x_spec = pl.BlockSpec((NW, tc, LANES), lambda b, k: (b, k, 0))
    xs = jax.ShapeDtypeStruct((ngb * NW, L, LANES), F32)
    st = pltpu.VMEM((NW, SUBLANES, LANES), F32)
    c_ins = comm.ins if comm else []
    c_shapes = comm.out_shapes if comm else []
    res = pl.pallas_call(
        body,
        grid=(ngb, nc),
        in_specs=[pl.BlockSpec((tc, LANES), lambda b, k: (k, b + ucol)), mat_b, mat_b, mat_c, mat_c, a_spec, a_spec,
                  pl.BlockSpec((1, LANES), lambda b, k: (0, b))] + [_ANY] * len(c_ins),
        out_specs=tuple([pl.BlockSpec((tc, LANES), lambda b, k: (k, b)), x_spec, x_spec] + [_ANY] * len(c_shapes)),
        out_shape=tuple([jax.ShapeDtypeStruct((L, ngb * LANES), F32), xs, xs] + c_shapes),
        scratch_shapes=[pltpu.VMEM((NW, tc, LANES), F32), pltpu.VMEM((NW, tc, LANES), F32), st, st, st, st,
                        pltpu.VMEM((tc, LANES), F32), pltpu.VMEM((tc, LANES), F32)]
        + (comm.scratch() if comm else []),
        compiler_params=_host_params(("arbitrary", "arbitrary"), comm),
        name="ssm_fwd",
    )(z, br_m, bi_m, cr_m, ci_m, a_re3, a_im3, d_row, *c_ins)
    return (res[:3], list(res[3:])) if comm else res


def _ssm_bwd(dy, z, u_off, xs_r, xs_i, br_m, bi_m, cr_m, ci_m, a_re3, a_im3, d_row, dz):
    L = z.shape[0]
    ngb = br_m.shape[0]
    tc, seg, nc, nsq = _ssm_dims(L)
    ucol = u_off // LANES

    def body(dy_ref, u_ref, xr_ref, xi_ref, br_ref, bi_ref, cr_ref, ci_ref, ar_ref, ai_ref, d_ref, dz_in,
             du_ref, gd_ref, gar_ref, gai_ref, gbr_ref, gbi_ref, gcr_ref, gci_ref,
             gr_s, gi_s, car_r, car_i, ini_r, ini_i, acc_r, acc_i, dyp, up, dus, dun):
        k = pl.program_id(1)

        @pl.when(k == 0)
        def _():
            for ref in (car_r, car_i, acc_r, acc_i, gd_ref, gbr_ref, gbi_ref, gcr_ref, gci_ref):
                ref[...] = jnp.zeros_like(ref)

        _rows_to_segments(dy_ref, dyp, seg)
        _rows_to_segments(u_ref, up, seg)
        dy_v = dyp[...]
        u = up[...]
        g_re = _dot_nt(dy_v, cr_ref[0])
        g_im = -_dot_nt(dy_v, ci_ref[0])
        for w in range(NW):
            gr_s[w] = g_re[:, w * LANES:(w + 1) * LANES]
            gi_s[w] = g_im[:, w * LANES:(w + 1) * LANES]
        ar = [jnp.broadcast_to(ar_ref[w], (SUBLANES, LANES)) for w in range(NW)]
        ai = [jnp.broadcast_to(ai_ref[w], (SUBLANES, LANES)) for w in range(NW)]

        def step1(ii, carry):
            xr, xi = carry
            rows = _tile_rows(seg - 1 - ii)
            nr = tuple(ar[w] * xr[w] + ai[w] * xi[w] + gr_s[w, rows, :] for w in range(NW))
            ni = tuple(ar[w] * xi[w] - ai[w] * xr[w] + gi_s[w, rows, :] for w in range(NW))
            return nr, ni

        zero = tuple(jnp.zeros((SUBLANES, LANES), F32) for _ in range(NW))
        er, ei = lax.fori_loop(0, seg, step1, (zero, zero), unroll=2)
        for w in range(NW):
            pr_, pi_ = _cpow(ar[w][0:1], ai[w][0:1], nsq)
            sr, si = car_r[w, 0:1, :], car_i[w, 0:1, :]
            for j in reversed(range(SUBLANES)):
                ini_r[w, j:j + 1, :] = sr
                ini_i[w, j:j + 1, :] = si
                sr, si = (pr_ * sr + pi_ * si + er[w][j:j + 1], pr_ * si - pi_ * sr + ei[w][j:j + 1])
            car_r[w, 0:1, :] = sr
            car_i[w, 0:1, :] = si

        def step2(ii, carry):
            gxr, gxi, acr, aci = carry
            rows = _tile_rows(seg - 1 - ii)
            nr, ni, nar, nai = [], [], [], []
            for w in range(NW):
                xr_t, xi_t = xr_ref[w, rows, :], xi_ref[w, rows, :]
                nar.append(acr[w] + gxr[w] * xr_t + gxi[w] * xi_t)
                nai.append(aci[w] + gxi[w] * xr_t - gxr[w] * xi_t)
                r = ar[w] * gxr[w] + ai[w] * gxi[w] + gr_s[w, rows, :]
                m = ar[w] * gxi[w] - ai[w] * gxr[w] + gi_s[w, rows, :]
                gr_s[w, rows, :] = r
                gi_s[w, rows, :] = m
                nr.append(r)
                ni.append(m)
            return tuple(nr), tuple(ni), tuple(nar), tuple(nai)

        init = (tuple(ini_r[w] for w in range(NW)), tuple(ini_i[w] for w in range(NW)),
                tuple(acc_r[w] for w in range(NW)), tuple(acc_i[w] for w in range(NW)))
        _, _, acr, aci = lax.fori_loop(0, seg, step2, init, unroll=2)
        du = d_ref[...] * dy_v
        for w in range(NW):
            sl = slice(w * LANES, (w + 1) * LANES)
            acc_r[w] = acr[w]
            acc_i[w] = aci[w]
            gxr_w, gxi_w = gr_s[w], gi_s[w]
            du = du + _dot_nt(gxr_w, br_ref[0, :, sl]) + _dot_nt(gxi_w, bi_ref[0, :, sl])
            gbr_ref[0, :, sl] += _dot_tn(u, gxr_w)
            gbi_ref[0, :, sl] += _dot_tn(u, gxi_w)
            gcr_ref[0, sl, :] += _dot_tn(xr_ref[w], dy_v)
            gci_ref[0, sl, :] += _dot_tn(-xi_ref[w], dy_v)
        dus[...] = du
        _segments_to_rows(dus, dun, seg)
        du_ref[...] = dun[...].astype(BF16)
        gd_ref[...] += jnp.sum(dy_v * u, axis=0, keepdims=True)

        @pl.when(k == nc - 1)
        def _():
            for w in range(NW):
                gar_ref[w] = jnp.sum(acc_r[w], axis=0, keepdims=True)
                gai_ref[w] = jnp.sum(acc_i[w], axis=0, keepdims=True)

    rk = lambda k: nc - 1 - k
    mat_b = pl.BlockSpec((1, LANES, NW * LANES), lambda b, k: (b, 0, 0))
    mat_c = pl.BlockSpec((1, NW * LANES, LANES), lambda b, k: (b, 0, 0))
    a_spec = pl.BlockSpec((NW, 1, LANES), lambda b, k: (b, 0, 0))
    x_spec = pl.BlockSpec((NW, tc, LANES), lambda b, k: (b, rk(k), 0))
    st = pltpu.VMEM((NW, SUBLANES, LANES), F32)
    big = pltpu.VMEM((NW, tc, LANES), F32)
    return pl.pallas_call(
        body,
        grid=(ngb, nc),
        in_specs=[pl.BlockSpec((tc, LANES), lambda b, k: (rk(k), b)),
                  pl.BlockSpec((tc, LANES), lambda b, k: (rk(k), b + ucol)),
                  x_spec, x_spec, mat_b, mat_b, mat_c, mat_c, a_spec, a_spec,
                  pl.BlockSpec((1, LANES), lambda b, k: (0, b)), _ANY],
        out_specs=(pl.BlockSpec((tc, LANES), lambda b, k: (rk(k), b + ucol)),
                   pl.BlockSpec((1, LANES), lambda b, k: (0, b)), a_spec, a_spec, mat_b, mat_b, mat_c, mat_c),
        out_shape=(jax.ShapeDtypeStruct(dz.shape, dz.dtype),
                   jax.ShapeDtypeStruct((1, ngb * LANES), F32),
                   jax.ShapeDtypeStruct((ngb * NW, 1, LANES), F32), jax.ShapeDtypeStruct((ngb * NW, 1, LANES), F32),
                   jax.ShapeDtypeStruct(br_m.shape, F32), jax.ShapeDtypeStruct(br_m.shape, F32),
                   jax.ShapeDtypeStruct(cr_m.shape, F32), jax.ShapeDtypeStruct(cr_m.shape, F32)),
        scratch_shapes=[big, big, st, st, st, st, st, st] + [pltpu.VMEM((tc, LANES), F32)] * 4,
        input_output_aliases={11: 0},
        compiler_params=_params(("arbitrary", "arbitrary")),
        name="ssm_bwd",
    )(dy, z, xs_r, xs_i, br_m, bi_m, cr_m, ci_m, a_re3, a_im3, d_row, dz)


def _block_diag_in(bb, ngb):
    t = bb.reshape(ngb, GROUPS_PER_BLOCK, SSM_P, SSM_GC).transpose(0, 1, 3, 2)
    eye = jnp.eye(GROUPS_PER_BLOCK, dtype=F32)
    m = t[:, :, :, None, :] * eye[None, :, None, :, None]
    return m.reshape(ngb, GROUPS_PER_BLOCK * SSM_GC, GROUPS_PER_BLOCK * SSM_P)


def _block_diag_out(c, ngb):
    t = c.reshape(ngb, GROUPS_PER_BLOCK, SSM_GC, SSM_P).transpose(0, 1, 3, 2)
    eye = jnp.eye(GROUPS_PER_BLOCK, dtype=F32)
    m = t[:, :, :, None, :] * eye[None, :, None, :, None]
    return m.reshape(ngb, GROUPS_PER_BLOCK * SSM_P, GROUPS_PER_BLOCK * SSM_GC)


def _diag_in(m, ngb):
    m5 = m.reshape(ngb, GROUPS_PER_BLOCK, SSM_GC, GROUPS_PER_BLOCK, SSM_P)
    d = jnp.diagonal(m5, axis1=1, axis2=3)
    return d.transpose(0, 3, 2, 1).reshape(ngb * GROUPS_PER_BLOCK * SSM_P, SSM_GC)


def _diag_out(m, ngb):
    m5 = m.reshape(ngb, GROUPS_PER_BLOCK, SSM_P, GROUPS_PER_BLOCK, SSM_GC)
    d = jnp.diagonal(m5, axis1=1, axis2=3)
    return d.transpose(0, 3, 2, 1).reshape(ngb * GROUPS_PER_BLOCK, SSM_GC, SSM_P)


_ANY = pl.BlockSpec(memory_space=pl.ANY)


class _Comm:
    def __init__(self, ins, out_shapes, n_sem, start, mid, finish):
        self.ins, self.out_shapes, self.n_sem = list(ins), list(out_shapes), n_sem
        self.start, self.mid, self.finish = start, mid, finish

    def scratch(self):
        return [pltpu.SemaphoreType.DMA((self.n_sem,)), pltpu.SemaphoreType.DMA((self.n_sem,)),
                pltpu.SemaphoreType.DMA((len(self.ins),))]

    def run(self, in_refs, out_refs, sems, first, mid, last):
        send, recv, local = sems

        @pl.when(first)
        def _():
            self.start(in_refs, out_refs, send, recv, local)

        @pl.when(mid)
        def _():
            self.mid(in_refs, out_refs, send, recv, local)

        @pl.when(last)
        def _():
            self.finish(in_refs, out_refs, send, recv, local)


def _split_refs(refs, n_in, n_out, comm):
    ci = len(comm.ins) if comm else 0
    co = len(comm.out_shapes) if comm else 0
    ins = refs[:n_in]
    c_ins = refs[n_in:n_in + ci]
    outs = refs[n_in + ci:n_in + ci + n_out]
    c_outs = refs[n_in + ci + n_out:n_in + ci + n_out + co]
    rest = refs[n_in + ci + n_out + co:]
    if comm:
        return ins, c_ins, outs, c_outs, rest[:-3], rest[-3:]
    return ins, c_ins, outs, c_outs, rest, ()


def _run_comm(comm, name):
    ni, no = len(comm.ins), len(comm.out_shapes)

    def body(*refs):
        args = (refs[:ni], refs[ni:ni + no]) + tuple(refs[ni + no:])
        comm.start(*args)
        comm.mid(*args)
        comm.finish(*args)

    return pl.pallas_call(
        body,
        in_specs=[_ANY] * ni,
        out_specs=tuple([_ANY] * no),
        out_shape=tuple(comm.out_shapes),
        scratch_shapes=comm.scratch(),
        compiler_params=pltpu.CompilerParams(has_side_effects=True),
        name=name,
    )(*comm.ins)


def _ag_comm(shards):
    n = len(shards)

    def env(ins, outs, send_sems, recv_sems):
        x, y, c = lax.axis_index("x"), lax.axis_index("y"), lax.axis_index("c")
        me, sibling = (x, y, c), (x, y, 1 - c)
        chips = [(1 - x, y), (x, 1 - y), (1 - x, 1 - y)]

        def copy(a, k, block, to, src=None):
            s = 4 * block[0] + 2 * block[1] + block[2]
            return pltpu.make_async_remote_copy(
                src_ref=outs[a].at[s] if src is None else src, dst_ref=outs[a].at[s],
                send_sem=send_sems.at[7 * a + k], recv_sem=recv_sems.at[7 * a + k],
                device_id=to, device_id_type=MESH)

        return c, me, sibling, chips, copy

    def own(ins, outs, local_sems, a):
        x, y, c = lax.axis_index("x"), lax.axis_index("y"), lax.axis_index("c")
        return pltpu.make_async_copy(ins[a], outs[a].at[4 * x + 2 * y + c], local_sems.at[a])

    def first_sends(ins, copy, me, sibling, chips, c, a):
        return [copy(a, 0, me, sibling, src=ins[a])] + [
            copy(a, 1 + j, me, (*chip, c), src=ins[a]) for j, chip in enumerate(chips)]

    def start(ins, outs, send_sems, recv_sems, local_sems):
        c, me, sibling, chips, copy = env(ins, outs, send_sems, recv_sems)
        for a in range(n):
            own(ins, outs, local_sems, a).start()
        for a in range(n):
            for cp in first_sends(ins, copy, me, sibling, chips, c, a):
                cp.start()

    def mid(ins, outs, send_sems, recv_sems, local_sems):
        c, me, sibling, chips, copy = env(ins, outs, send_sems, recv_sems)
        for a in range(n):
            for j, chip in enumerate(chips):
                copy(a, 1 + j, (*chip, c), me).wait_recv()
                copy(a, 4 + j, (*chip, c), sibling).start()

    def finish(ins, outs, send_sems, recv_sems, local_sems):
        c, me, sibling, chips, copy = env(ins, outs, send_sems, recv_sems)
        for a in range(n):
            copy(a, 0, sibling, me).wait_recv()
            for j, chip in enumerate(chips):
                copy(a, 4 + j, (*chip, 1 - c), me).wait_recv()
        for a in range(n):
            for cp in first_sends(ins, copy, me, sibling, chips, c, a):
                cp.wait_send()
            for j, chip in enumerate(chips):
                copy(a, 4 + j, (*chip, c), sibling).wait_send()
            own(ins, outs, local_sems, a).wait()

    return _Comm(shards, [jax.ShapeDtypeStruct((N_DEV,) + s.shape, s.dtype) for s in shards], 7 * n,
                 start, mid, finish)


def _sibling_comm(parts):
    n = len(parts)

    def copies(ins, outs, send_sems, recv_sems):
        x, y, c = lax.axis_index("x"), lax.axis_index("y"), lax.axis_index("c")
        return [pltpu.make_async_remote_copy(
            src_ref=ins[a].at[2 * q + (1 - c)], dst_ref=outs[a].at[q],
            send_sem=send_sems.at[4 * a + q], recv_sem=recv_sems.at[4 * a + q],
            device_id=(x, y, 1 - c), device_id_type=MESH) for a in range(n) for q in range(4)]

    def start(ins, outs, send_sems, recv_sems, local_sems):
        for cp in copies(ins, outs, send_sems, recv_sems):
            cp.start()

    def mid(ins, outs, send_sems, recv_sems, local_sems):
        pass

    def finish(ins, outs, send_sems, recv_sems, local_sems):
        for cp in copies(ins, outs, send_sems, recv_sems):
            cp.wait()

    return _Comm(parts, [jax.ShapeDtypeStruct((4,) + p.shape[1:], p.dtype) for p in parts], 4 * n,
                 start, mid, finish)


def _chips_comm(parts):
    n = len(parts)

    def copies(ins, outs, send_sems, recv_sems):
        x, y, c = lax.axis_index("x"), lax.axis_index("y"), lax.axis_index("c")
        chips = [(1 - x, y), (x, 1 - y), (1 - x, 1 - y)]
        return [pltpu.make_async_remote_copy(
            src_ref=ins[a].at[2 * px + py], dst_ref=outs[a].at[j],
            send_sem=send_sems.at[3 * a + j], recv_sem=recv_sems.at[3 * a + j],
            device_id=(px, py, c), device_id_type=MESH) for a in range(n) for j, (px, py) in enumerate(chips)]

    def start(ins, outs, send_sems, recv_sems, local_sems):
        for cp in copies(ins, outs, send_sems, recv_sems):
            cp.start()

    def mid(ins, outs, send_sems, recv_sems, local_sems):
        pass

    def finish(ins, outs, send_sems, recv_sems, local_sems):
        for cp in copies(ins, outs, send_sems, recv_sems):
            cp.wait()

    return _Comm(parts, [jax.ShapeDtypeStruct((3,) + p.shape[1:], p.dtype) for p in parts], 3 * n,
                 start, mid, finish)


def _sibling_add(part, recv, name):
    _, R, C = part.shape
    tr = _pick(R, (256, 128, 80))
    c = lax.axis_index("c")

    def body(c_ref, p_ref, r_ref, o_ref, o16_ref):
        t = p_ref[...] + r_ref[...]
        o_ref[...] = t
        o16_ref[...] = t.astype(BF16)

    blk = pl.BlockSpec((1, tr, C), lambda q, i, c_ref: (q, i, 0))
    return pl.pallas_call(
        body,
        grid_spec=pltpu.PrefetchScalarGridSpec(
            num_scalar_prefetch=1,
            grid=(4, R // tr),
            in_specs=[pl.BlockSpec((1, tr, C), lambda q, i, c_ref: (2 * q + c_ref[0], i, 0)), blk],
            out_specs=(blk, blk),
        ),
        out_shape=(jax.ShapeDtypeStruct((4, R, C), F32), jax.ShapeDtypeStruct((4, R, C), BF16)),
        compiler_params=_params(("parallel", "parallel")),
        name=name,
    )(jnp.reshape(c, (1,)).astype(jnp.int32), part, recv)


def _adamw(w, g, m, v):
    m = ADAM_B1 * m + (1.0 - ADAM_B1) * g
    v = ADAM_B2 * v + (1.0 - ADAM_B2) * (g * g)
    m_hat = m / (1.0 - ADAM_B1 ** ADAM_STEP)
    v_hat = v / (1.0 - ADAM_B2 ** ADAM_STEP)
    delta = -ADAM_LR * (m_hat / (jnp.sqrt(v_hat) + ADAM_EPS) + ADAM_WD * w)
    return delta, m, v


def _adam_big(t, recv, w, m, v, name):
    _, R, C = t.shape
    tr = _pick(R, (256, 128))
    chip = 2 * lax.axis_index("x") + lax.axis_index("y")

    def body(q_ref, t_ref, r_ref, w_ref, m_ref, v_ref, g_ref, d_ref, nm_ref, nv_ref):
        g = t_ref[0] + r_ref[0].astype(F32) + r_ref[1].astype(F32) + r_ref[2].astype(F32)
        g_ref[...] = g
        d_ref[...], nm_ref[...], nv_ref[...] = _adamw(w_ref[...], g, m_ref[...], v_ref[...])

    blk = pl.BlockSpec((tr, C), lambda i, q_ref: (i, 0))
    o = jax.ShapeDtypeStruct((R, C), F32)
    return pl.pallas_call(
        body,
        grid_spec=pltpu.PrefetchScalarGridSpec(
            num_scalar_prefetch=1,
            grid=(R // tr,),
            in_specs=[pl.BlockSpec((1, tr, C), lambda i, q_ref: (q_ref[0], i, 0)),
                      pl.BlockSpec((3, tr, C), lambda i, q_ref: (0, i, 0)), blk, blk, blk],
            out_specs=(blk, blk, blk, blk),
        ),
        out_shape=(o, o, o, o),
        compiler_params=_params(("parallel",)),
        name=name,
    )(jnp.reshape(chip, (1,)).astype(jnp.int32), t, recv, w, m, v)


def _reduce_big(t, recv, name):
    _, R, C = t.shape
    tr = _pick(R, (256, 128, 80))
    chip = 2 * lax.axis_index("x") + lax.axis_index("y")

    def body(q_ref, t_ref, r_ref, g_ref):
        g_ref[...] = t_ref[0] + r_ref[0].astype(F32) + r_ref[1].astype(F32) + r_ref[2].astype(F32)

    return pl.pallas_call(
        body,
        grid_spec=pltpu.PrefetchScalarGridSpec(
            num_scalar_prefetch=1,
            grid=(R // tr,),
            in_specs=[pl.BlockSpec((1, tr, C), lambda i, q_ref: (q_ref[0], i, 0)),
                      pl.BlockSpec((3, tr, C), lambda i, q_ref: (0, i, 0))],
            out_specs=pl.BlockSpec((tr, C), lambda i, q_ref: (i, 0)),
        ),
        out_shape=jax.ShapeDtypeStruct((R, C), F32),
        compiler_params=_params(("parallel",)),
        name=name,
    )(jnp.reshape(chip, (1,)).astype(jnp.int32), t, recv)


def _adam_only(g, w, m, v, name):
    R, C = g.shape
    tr = _pick(R, (256, 128))

    def body(g_ref, w_ref, m_ref, v_ref, d_ref, nm_ref, nv_ref):
        d_ref[...], nm_ref[...], nv_ref[...] = _adamw(w_ref[...], g_ref[...], m_ref[...], v_ref[...])

    blk = pl.BlockSpec((tr, C), lambda i: (i, 0))
    o = jax.ShapeDtypeStruct((R, C), F32)
    return pl.pallas_call(body, grid=(R // tr,), in_specs=[blk] * 4, out_specs=(blk, blk, blk),
                          out_shape=(o, o, o), compiler_params=_params(("parallel",)), name=name)(g, w, m, v)


def _gather8_comm(gbuf):
    def copies(ins, outs, send_sems, recv_sems):
        x, y, c = lax.axis_index("x"), lax.axis_index("y"), lax.axis_index("c")
        me = 4 * x + 2 * y + c
        out = []
        for k in range(1, N_DEV):
            fx, fy, fc = (k >> 2) & 1, (k >> 1) & 1, k & 1
            px, py, pc = x + fx - 2 * x * fx, y + fy - 2 * y * fy, c + fc - 2 * c * fc
            send = pltpu.make_async_remote_copy(
                src_ref=ins[0], dst_ref=outs[0].at[me], send_sem=send_sems.at[k - 1], recv_sem=recv_sems.at[k - 1],
                device_id=(px, py, pc), device_id_type=MESH)
            recv = pltpu.make_async_remote_copy(
                src_ref=ins[0], dst_ref=outs[0].at[4 * px + 2 * py + pc], send_sem=send_sems.at[k - 1],
                recv_sem=recv_sems.at[k - 1], device_id=(px, py, pc), device_id_type=MESH)
            out.append((send, recv))
        return me, out

    def start(ins, outs, send_sems, recv_sems, local_sems):
        me, cps = copies(ins, outs, send_sems, recv_sems)
        pltpu.make_async_copy(ins[0], outs[0].at[me], local_sems.at[0]).start()
        for send, _ in cps:
            send.start()

    def mid(ins, outs, send_sems, recv_sems, local_sems):
        pass

    def finish(ins, outs, send_sems, recv_sems, local_sems):
        me, cps = copies(ins, outs, send_sems, recv_sems)
        for send, recv in cps:
            recv.wait_recv()
            send.wait_send()
        pltpu.make_async_copy(ins[0], outs[0].at[me], local_sems.at[0]).wait()

    return _Comm([gbuf], [jax.ShapeDtypeStruct((N_DEV,) + gbuf.shape, gbuf.dtype)], N_DEV - 1, start, mid, finish)


def _sum8_adam(slots, wbuf, mbuf, vbuf, name):
    R = wbuf.shape[0]

    def body(s_ref, w_ref, m_ref, v_ref, gs_ref, d_ref, nm_ref, nv_ref):
        g = s_ref[0]
        for s in range(1, N_DEV):
            g = g + s_ref[s]
        gs_ref[...] = g
        d_ref[...], nm_ref[...], nv_ref[...] = _adamw(w_ref[...], g, m_ref[...], v_ref[...])

    o = jax.ShapeDtypeStruct((R, LANES), F32)
    return pl.pallas_call(body, out_shape=(o, o, o, o),
                          compiler_params=pltpu.CompilerParams(vmem_limit_bytes=VMEM_LIMIT), name=name)(
        slots, wbuf, mbuf, vbuf)


def _pack(items):
    rows, spans, r0 = [], [], 0
    for a in items:
        n = a.size
        nr = -(-n // LANES)
        rows.append(jnp.pad(a.reshape(-1).astype(F32), (0, nr * LANES - n)).reshape(nr, LANES))
        spans.append((r0, nr, a.shape))
        r0 += nr
    pad = -r0 % SUBLANES
    if pad:
        rows.append(jnp.zeros((pad, LANES), F32))
    return jnp.concatenate(rows, axis=0), spans


def _unpack(buf, spans):
    return [buf[r0:r0 + nr].reshape(-1)[:math.prod(shape)].reshape(shape) for r0, nr, shape in spans]


def kernel(x, norm_mix_pre, norm_mix_post, norm_mlp_pre, norm_mlp_post, w_in, sinks, lam_re, lam_im, log_dt, b_re, b_im, c_re, c_im, d_skip, w_glu, w_branch, w_out, w_up, w_down, loss_target, m_norm_mix_pre, m_norm_mix_post, m_norm_mlp_pre, m_norm_mlp_post, m_w_in, m_sinks, m_lam_re, m_lam_im, m_log_dt, m_b_re, m_b_im, m_c_re, m_c_im, m_d_skip, m_w_glu, m_w_branch, m_w_out, m_w_up, m_w_down, v_norm_mix_pre, v_norm_mix_post, v_norm_mlp_pre, v_norm_mlp_post, v_w_in, v_sinks, v_lam_re, v_lam_im, v_log_dt, v_b_re, v_b_im, v_c_re, v_c_im, v_d_skip, v_w_glu, v_w_branch, v_w_out, v_w_up, v_w_down):
    _, L, D = x.shape
    xs = x[0]
    tgt = loss_target[0]
    ssm_w = D // 2
    n_groups = ssm_w // SSM_GC
    ngb = n_groups // GROUPS_PER_BLOCK
    n_state = n_groups * SSM_P
    d_ff = w_up.shape[2] * N_DEV
    o_k, o_v, o_u = Q_W, Q_W + KV_W, Q_W + 2 * KV_W
    o_ga = o_u + ssm_w
    o_gs = o_ga + D

    big = {"w_in": w_in[0], "w_glu": w_glu[0], "w_branch": w_branch[0], "w_out": w_out[0],
           "w_up": w_up[0], "w_down": w_down[0]}
    col_sharded = ("w_in", "w_glu", "w_up")
    names = list(big)
    shard16 = {k: (big[k].T if k in col_sharded else big[k]).astype(BF16) for k in names}
    full = {}

    def gathered(keys, arrays):
        for k, g in zip(keys, arrays):
            _, r, c = g.shape
            full[k] = g.reshape(N_DEV * r, c)

    def by_owner(g):
        return g.reshape(N_DEV, g.shape[0] // N_DEV, g.shape[1])

    col = lambda a: a.reshape(n_state, 1)
    lr_c, li_c = col(lam_re[0]), col(lam_im[0])
    ldt_c = jnp.repeat(log_dt[0], SSM_P).reshape(n_state, 1)
    b_re_c, b_im_c = b_re[0].reshape(n_state, SSM_GC), b_im[0].reshape(n_state, SSM_GC)
    a_re, a_im, bb_re, bb_im = _disc_fwd(lr_c, li_c, ldt_c, b_re_c, b_im_c)
    a_re3 = a_re.reshape(n_state // LANES, 1, LANES)
    a_im3 = a_im.reshape(n_state // LANES, 1, LANES)
    br_m = _block_diag_in(bb_re, ngb).astype(BF16)
    bi_m = _block_diag_in(bb_im, ngb).astype(BF16)
    cr_m = _block_diag_out(c_re[0], ngb).astype(BF16)
    ci_m = _block_diag_out(c_im[0], ngb).astype(BF16)
    d_row = d_skip[0].reshape(1, ssm_w)
    cos_t, sin_t = _rope_tables(L)

    h, g1 = _rms_pre(xs, norm_mix_pre, _ag_comm([shard16["w_in"]]))
    gathered(["w_in"], g1)
    z, g3 = _mm(h, full["w_in"], mode="nt", name="mm_z",
                comm=_ag_comm([shard16[k] for k in ("w_glu", "w_branch", "w_out")]))
    gathered(["w_glu", "w_branch", "w_out"], g3)
    wb_a, wb_s = full["w_branch"][:Q_W], full["w_branch"][Q_W:]
    o_attn = _attn_fwd(z, cos_t, sin_t, sinks)
    (y_pre, xs_r, xs_i), g1 = _ssm_fwd(z, o_u, br_m, bi_m, cr_m, ci_m, a_re3, a_im3, d_row,
                                       comm=_ag_comm([shard16["w_up"]]))
    gathered(["w_up"], g1)
    gy = _ew(lambda y: (_gelu(y),), [(y_pre, 0)], (BF16,), rows=L, ncols=ssm_w, name="gelu")
    zg = _mm(gy, full["w_glu"], mode="nt", name="mm_zg")
    o_ssm = _ew(lambda a, b: (a * _sigmoid(b),), [(zg, 0), (zg, ssm_w)], (BF16,), rows=L, ncols=ssm_w, name="glu")
    y_attn = _mm(o_attn, wb_a, mode="nn", name="mm_y_attn", out_dtypes=(BF16,))
    y_ssm = _mm(o_ssm, wb_s, mode="nn", name="mm_y_ssm", out_dtypes=(BF16,))
    mix = _ew(lambda ga, gs, ya, ys: (_sigmoid(ga) * ya + _sigmoid(gs) * ys,),
              [(z, o_ga), (z, o_gs), (y_attn, 0), (y_ssm, 0)], (BF16,), rows=L, ncols=D, name="mix")
    mixed = _mm(mix, full["w_out"], mode="nn", name="mm_mixed")
    x1, h2 = _post_pre(xs, mixed, norm_mix_post, norm_mlp_pre)

    def relu_sq(acc):
        a = jnp.maximum(acc, 0.0)
        return a, a * a

    (act, act2), g1 = _mm(h2, full["w_up"], mode="nt", name="mm_up", out_dtypes=(BF16, BF16), epi=relu_sq,
                          comm=_ag_comm([shard16["w_down"]]))
    gathered(["w_down"], g1)
    dn = _mm(act2, full["w_down"], mode="nn", name="mm_down")
    dx2, d_dn, dg_mlp_post, loss_part = _loss_bwd(x1, dn, norm_mlp_post, tgt)

    d_pre = _mm(d_dn, full["w_down"], mode="nt", name="mm_d_act", out_dtypes=(BF16,),
                epi=lambda acc, a: (acc * (2.0 * a.astype(F32)),), extras=(act,))
    gw_down = _mm(act2, d_dn, mode="tn", name="mm_gw_down")
    p_down = by_owner(gw_down)
    dh2, (sib_down,) = _mm(d_pre, full["w_up"], mode="nn", name="mm_dh2", out_dtypes=(BF16,),
                           comm=_sibling_comm([p_down]))
    t_down, t16_down = _sibling_add(p_down, sib_down, "rs_add_w_down")
    gw_up, (chips_down,) = _mm(d_pre, h2, mode="tn", name="mm_gw_up", comm=_chips_comm([t16_down]))
    p_up = by_owner(gw_up)
    dx1, d_mixed, dg_mlp_pre, dg_mix_post = _norm_bwd_pair(x1, dh2, dx2, mixed, norm_mlp_pre, norm_mix_post)
    d_mix, (sib_up,) = _mm(d_mixed, full["w_out"], mode="nt", name="mm_d_mix", out_dtypes=(BF16,),
                           comm=_sibling_comm([p_up]))
    t_up, t16_up = _sibling_add(p_up, sib_up, "rs_add_w_up")
    gw_out = _mm(mix, d_mixed, mode="tn", name="mm_gw_out")
    d_y2, dz = _gate_bwd(d_mix, z, o_ga, y_attn, y_ssm)
    d_o_attn = _mm(d_y2, wb_a, mode="nt", name="mm_d_o_attn", a_win=(0, D))
    gwb_a = _mm(o_attn, d_y2, mode="tn", name="mm_gwb_a", b_win=(0, D))
    d_o_ssm = _mm(d_y2, wb_s, mode="nt", name="mm_d_o_ssm", a_win=(D, D))
    gwb_s = _mm(o_ssm, d_y2, mode="tn", name="mm_gwb_s", b_win=(D, D))

    def glu_bwd(do, a, b):
        s = _sigmoid(b)
        return do * s, do * a * s * (1.0 - s)

    d_zg_a, d_zg_b = _ew(glu_bwd, [(d_o_ssm, 0), (zg, 0), (zg, ssm_w)], (BF16, BF16), rows=L, ncols=ssm_w, name="glu_bwd")
    d_zg = jnp.concatenate([d_zg_a, d_zg_b], axis=1)
    dy_pre = _mm(d_zg, full["w_glu"], mode="nn", name="mm_d_gy",
                 epi=lambda acc, y: (acc * _gelu_grad(y),), extras=(y_pre,))
    gw_glu = _mm(d_zg, gy, mode="tn", name="mm_gw_glu")
    mids = ["w_glu", "w_branch", "w_out"]
    p_mid = [by_owner(gw_glu), by_owner(jnp.concatenate([gwb_a, gwb_s], axis=0)), by_owner(gw_out)]
    sib_mid = _run_comm(_sibling_comm(p_mid), "rs_sibling_mid")
    t_mid = [_sibling_add(p, r, "rs_add_" + k) for k, p, r in zip(mids, p_mid, sib_mid)]
    (dz, g_dskip, g_ar3, g_ai3, g_br_m, g_bi_m, g_cr_m, g_ci_m) = _ssm_bwd(
        dy_pre, z, o_u, xs_r, xs_i, br_m, bi_m, cr_m, ci_m, a_re3, a_im3, d_row, dz)
    g_lr, g_li, g_ldt, g_b_re, g_b_im = _disc_bwd(
        lr_c, li_c, ldt_c, b_re_c, b_im_c, g_ar3.reshape(n_state, 1), g_ai3.reshape(n_state, 1),
        _diag_in(g_br_m, ngb), _diag_in(g_bi_m, ngb))
    (dz, dkc, dkp, dvc, dvp, dsink_rows), chips_a = _attn_bwd(
        z, d_o_attn, cos_t, sin_t, sinks, dz, comm=_chips_comm([t16_up] + [t16 for _, t16 in t_mid]))
    chips_up, chips_mid = chips_a[0], chips_a[1:]
    dz = _kv_combine(dkc, dkp, dvc, dvp, dz)
    dsink = jnp.stack([dsink_rows[:, 0], dsink_rows[:, HEAD_DIM]], axis=1).reshape(1, N_Q_HEADS)
    small_names = ["norm_mix_post", "norm_mlp_pre", "norm_mlp_post", "sinks", "lam_re", "lam_im",
                   "log_dt", "b_re", "b_im", "c_re", "c_im", "d_skip"]
    small_g = [dg_mix_post, dg_mlp_pre, dg_mlp_post, dsink,
               g_lr.reshape(lam_re.shape), g_li.reshape(lam_im.shape), g_ldt.reshape(log_dt.shape),
               g_b_re.reshape(b_re.shape), g_b_im.reshape(b_im.shape),
               _diag_out(g_cr_m, ngb).reshape(c_re.shape), _diag_out(g_ci_m, ngb).reshape(c_im.shape),
               g_dskip.reshape(d_skip.shape)]
    gbuf, spans = _pack(small_g + [loss_part])
    gw_in, (small_slots,) = _mm(dz, h, mode="tn", name="mm_gw_in", comm=_gather8_comm(gbuf))
    p_in = by_owner(gw_in)
    (sib_in,) = _run_comm(_sibling_comm([p_in]), "rs_sibling_w_in")
    t_in, t16_in = _sibling_add(p_in, sib_in, "rs_add_w_in")
    dh, (chips_in,) = _mm(dz, full["w_in"], mode="nn", name="mm_dh", out_dtypes=(BF16,),
                          comm=_chips_comm([t16_in]))
    grad_x, dg_mix_pre = _final_bwd(xs, dh, dx1, norm_mix_pre, None)

    reduced = {"w_in": (t_in, chips_in), "w_up": (t_up, chips_up), "w_down": (t_down, chips_down)}
    for k, (t32, _), r in zip(mids, t_mid, chips_mid):
        reduced[k] = (t32, r)
    moments = {"w_in": (m_w_in, v_w_in), "w_glu": (m_w_glu, v_w_glu), "w_branch": (m_w_branch, v_w_branch),
               "w_out": (m_w_out, v_w_out), "w_up": (m_w_up, v_w_up), "w_down": (m_w_down, v_w_down)}
    big_out = {}
    for k in names:
        t, r = reduced[k]
        mm_, vv_ = moments[k]
        if k in col_sharded:
            g = _reduce_big(t, r, "reduce_" + k).T
            big_out[k] = [o[None] for o in (g,) + tuple(_adam_only(g, big[k], mm_[0], vv_[0], "adam_" + k))]
        else:
            big_out[k] = [o[None] for o in _adam_big(t, r, big[k], mm_[0], vv_[0], "adam_" + k)]

    small_w = [norm_mix_post, norm_mlp_pre, norm_mlp_post, sinks, lam_re, lam_im, log_dt,
               b_re, b_im, c_re, c_im, d_skip]
    small_m = [m_norm_mix_post, m_norm_mlp_pre, m_norm_mlp_post, m_sinks, m_lam_re, m_lam_im,
               m_log_dt, m_b_re, m_b_im, m_c_re, m_c_im, m_d_skip]
    small_v = [v_norm_mix_post, v_norm_mlp_pre, v_norm_mlp_post, v_sinks, v_lam_re, v_lam_im,
               v_log_dt, v_b_re, v_b_im, v_c_re, v_c_im, v_d_skip]
    zero1 = jnp.zeros((1, 1), F32)
    wbuf, _ = _pack(small_w + [zero1])
    mbuf, _ = _pack(small_m + [zero1])
    vbuf, _ = _pack(small_v + [zero1])
    gs, ds, nms, nvs = [_unpack(b, spans) for b in _sum8_adam(small_slots, wbuf, mbuf, vbuf, "small_adam")]
    loss = gs[-1].reshape(())
    tbuf, tspans = _pack([dg_mix_pre])
    (tail_slots,) = _run_comm(_gather8_comm(tbuf), "gather_tail")
    tail = _sum8_adam(tail_slots, _pack([norm_mix_pre])[0], _pack([m_norm_mix_pre])[0],
                      _pack([v_norm_mix_pre])[0], "small_adam_tail")
    small_names = ["norm_mix_pre"] + small_names
    gs, ds, nms, nvs = [_unpack(t, tspans) + src for t, src in zip(tail, (gs, ds, nms, nvs))]

    order = ["norm_mix_pre", "norm_mix_post", "norm_mlp_pre", "norm_mlp_post", "w_in", "sinks", "lam_re", "lam_im",
             "log_dt", "b_re", "b_im", "c_re", "c_im", "d_skip", "w_glu", "w_branch", "w_out", "w_up", "w_down"]
    outs = [loss, grad_x[None]]
    for idx, src in enumerate((gs, ds, nms, nvs)):
        for k in order:
            outs.append(big_out[k][idx] if k in big_out else src[small_names.index(k)])
    return tuple(outs)
```

```python
import functools
import math

import jax
import jax.numpy as jnp
from jax import lax
from jax.experimental import pallas as pl
from jax.experimental.pallas import tpu as pltpu

F32 = jnp.float32
BF16 = jnp.bfloat16
MESH = pl.DeviceIdType.MESH

LANES = 128
SUBLANES = 8
VMEM_LIMIT = 56 * 1024 * 1024
MM_TILE_BUDGET = 46 * 1024 * 1024

HEAD_DIM = 64
N_Q_HEADS = 16
N_KV_HEADS = 2
Q_W = N_Q_HEADS * HEAD_DIM
KV_W = N_KV_HEADS * HEAD_DIM
BLOCK = 128
ROT_DIM = HEAD_DIM // 4
ROPE_THETA = 500000.0
SSM_GC = 16
SSM_P = 64
GROUPS_PER_BLOCK = 8
NW = GROUPS_PER_BLOCK * SSM_P // LANES
EPS = 1e-6
N_DEV = 8

ADAM_LR = 0.001
ADAM_B1 = 0.9
ADAM_B2 = 0.999
ADAM_EPS = 1e-08
ADAM_WD = 0.01
ADAM_STEP = 10


def _params(sem=None):
    return pltpu.CompilerParams(dimension_semantics=sem, vmem_limit_bytes=VMEM_LIMIT)


def _mid_step(steps):
    return (3 * steps) // 4


def _host_params(sem, comm):
    if comm:
        return pltpu.CompilerParams(dimension_semantics=("arbitrary",) * len(sem), vmem_limit_bytes=VMEM_LIMIT,
                                    has_side_effects=True)
    return _params(sem)


def _pick(dim, prefs):
    for p in prefs:
        if dim % p == 0:
            return p
    return dim


def _sigmoid(x):
    return 1.0 / (1.0 + jnp.exp(-x))


_GELU_C = math.sqrt(2.0 / math.pi)


def _gelu(x):
    return 0.5 * x * (1.0 + jnp.tanh(_GELU_C * (x + 0.044715 * x * x * x)))


def _gelu_grad(x):
    t = jnp.tanh(_GELU_C * (x + 0.044715 * x * x * x))
    return 0.5 * (1.0 + t) + 0.5 * x * (1.0 - t * t) * _GELU_C * (1.0 + 3.0 * 0.044715 * x * x)


_DIMS = {"nn": (((1,), (0,)), ((), ())), "nt": (((1,), (1,)), ((), ())), "tn": (((0,), (0,)), ((), ()))}


def _mm(a, b, *, mode, name, out_dtypes=(F32,), epi=None, extras=(), comm=None, a_win=None, b_win=None,
        a_pro=None, b_pro=None):
    ar, ac = a.shape[0], (a_win[1] if a_win else a.shape[1])
    br, bc = b.shape[0], (b_win[1] if b_win else b.shape[1])
    if mode == "nn":
        (M, K), (K2, N) = (ar, ac), (br, bc)
    elif mode == "nt":
        (M, K), (N, K2) = (ar, ac), (br, bc)
    else:
        (K, M), (K2, N) = (ar, ac), (br, bc)
    assert K == K2, (a.shape, b.shape, mode)
    tm = _pick(M, (1024, 1280, 640, 512, 256, 128))
    tn = _pick(N, ((2048,) if K <= 2048 else ()) + (1024, 1280, 640, 512, 384, 256, 128))
    tk = K if K <= 2048 else _pick(K, (2048, 1280, 1024, 640, 512, 256, 128))
    if K % 4096 == 0 and K > 4096:
        tile_bytes = 2 * 4096 * (tm * a.dtype.itemsize + tn * b.dtype.itemsize) + 4 * tm * tn + 2 * tm * tn * (
            sum(jnp.dtype(d).itemsize for d in out_dtypes) + sum(e.dtype.itemsize for e in extras))
        if tile_bytes <= MM_TILE_BUDGET:
            tk = 4096
    nk = K // tk
    a_col_tile = tm if mode == "tn" else tk
    b_col_tile = tk if mode == "nt" else tn
    ao = a_win[0] // a_col_tile if a_win else 0
    bo = b_win[0] // b_col_tile if b_win else 0
    assert (not a_win or a_win[0] % a_col_tile == 0) and (not b_win or b_win[0] % b_col_tile == 0)
    n_ex = len(extras)
    n_out = len(out_dtypes)
    gi, gj = M // tm, N // tn
    steps = gi * gj * nk

    def body(*refs):
        ins, c_ins, o_refs, c_outs, scratch, sems = _split_refs(refs, 2 + n_ex, n_out, comm)
        a_ref, b_ref = ins[0], ins[1]
        ex_refs = ins[2:]
        if comm:
            s = (pl.program_id(0) * gj + pl.program_id(1)) * nk + pl.program_id(2)
            comm.run(c_ins, c_outs, sems, s == 0, s == _mid_step(steps), s == steps - 1)

        def finish(r):
            outs = (r,) if epi is None else epi(r, *[e[...] for e in ex_refs])
            for o_ref, o in zip(o_refs, outs):
                o_ref[...] = o.astype(o_ref.dtype)

        a_tile = a_ref[...] if a_pro is None else a_pro(a_ref[...])
        b_tile = b_ref[...] if b_pro is None else b_pro(b_ref[...])
        part = lax.dot_general(a_tile.astype(BF16), b_tile.astype(BF16), _DIMS[mode], preferred_element_type=F32)
        if nk == 1:
            finish(part)
            return
        acc = scratch[0]
        k = pl.program_id(2)

        @pl.when(k == 0)
        def _():
            acc[...] = part

        @pl.when((k > 0) & (k < nk - 1))
        def _():
            acc[...] += part

        @pl.when(k == nk - 1)
        def _():
            finish(acc[...] + part)

    if mode == "nn":
        a_spec = pl.BlockSpec((tm, tk), lambda i, j, k: (i, k + ao))
        b_spec = pl.BlockSpec((tk, tn), lambda i, j, k: (k, j + bo))
    elif mode == "nt":
        a_spec = pl.BlockSpec((tm, tk), lambda i, j, k: (i, k + ao))
        b_spec = pl.BlockSpec((tn, tk), lambda i, j, k: (j, k + bo))
    else:
        a_spec = pl.BlockSpec((tk, tm), lambda i, j, k: (k, i + ao))
        b_spec = pl.BlockSpec((tk, tn), lambda i, j, k: (k, j + bo))
    o_spec = pl.BlockSpec((tm, tn), lambda i, j, k: (i, j))
    c_ins = comm.ins if comm else []
    c_shapes = comm.out_shapes if comm else []
    res = pl.pallas_call(
        body,
        grid=(gi, gj, nk),
        in_specs=[a_spec, b_spec] + [o_spec] * n_ex + [_ANY] * len(c_ins),
        out_specs=tuple([o_spec] * n_out + [_ANY] * len(c_shapes)),
        out_shape=tuple([jax.ShapeDtypeStruct((M, N), d) for d in out_dtypes] + c_shapes),
        scratch_shapes=([pltpu.VMEM((tm, tn), F32)] if nk > 1 else []) + (comm.scratch() if comm else []),
        compiler_params=_host_params(("parallel", "parallel", "arbitrary"), comm),
        name=name,
    )(a, b, *extras, *c_ins)
    if comm:
        return (res[0] if n_out == 1 else res[:n_out]), list(res[n_out:])
    return res[0] if n_out == 1 else res


def _ew(fn, ins, out_dtypes, *, rows, ncols, name):
    g = ncols
    for _, off in ins:
        g = math.gcd(g, off)
    tc = _pick(g, (512, 256, 128))
    tr = _pick(rows, (2048, 1024, 512, 256, 128))
    n_in = len(ins)

    def body(*refs):
        outs = fn(*[r[...] for r in refs[:n_in]])
        for o_ref, o in zip(refs[n_in:], outs):
            o_ref[...] = o.astype(o_ref.dtype)

    def in_spec(off):
        ob = off // tc
        return pl.BlockSpec((tr, tc), lambda i, j: (i, j + ob))

    o_spec = pl.BlockSpec((tr, tc), lambda i, j: (i, j))
    res = pl.pallas_call(
        body,
        grid=(rows // tr, ncols // tc),
        in_specs=[in_spec(off) for _, off in ins],
        out_specs=tuple([o_spec] * len(out_dtypes)),
        out_shape=tuple(jax.ShapeDtypeStruct((rows, ncols), d) for d in out_dtypes),
        compiler_params=_params(("parallel", "parallel")),
        name=name,
    )(*[arr for arr, _ in ins])
    return res[0] if len(out_dtypes) == 1 else res


def _rstd(x):
    return lax.rsqrt(jnp.mean(x * x, axis=-1, keepdims=True) + EPS)


def _norm_bwd(x, r, g, dy):
    t = dy * g
    dx = r * t - x * (r * r * r) * jnp.mean(t * x, axis=-1, keepdims=True)
    return dx, dy * x * r


def _row_call(body, ins, row_ins, outs, acc_outs, *, rows, width, name, comm=None):
    tr = _pick(rows, (512, 256, 128))
    steps = rows // tr
    t_spec = pl.BlockSpec((tr, width), lambda i: (i, 0))
    r_spec = pl.BlockSpec((1, width), lambda i: (0, 0))
    n_in, n_out = len(ins) + len(row_ins), len(outs) + len(acc_outs)

    def hosted(*refs):
        h_ins, c_ins, h_outs, c_outs, _, sems = _split_refs(refs, n_in, n_out, comm)
        i = pl.program_id(0)
        comm.run(c_ins, c_outs, sems, i == 0, i == _mid_step(steps), i == steps - 1)
        body(*h_ins, *h_outs)

    c_ins = comm.ins if comm else []
    c_shapes = comm.out_shapes if comm else []
    res = pl.pallas_call(
        hosted if comm else body,
        grid=(steps,),
        in_specs=[t_spec] * len(ins) + [r_spec] * len(row_ins) + [_ANY] * len(c_ins),
        out_specs=tuple([t_spec] * len(outs) + [pl.BlockSpec(s, lambda i: (0, 0)) for s in acc_outs]
                        + [_ANY] * len(c_shapes)),
        out_shape=tuple([jax.ShapeDtypeStruct((rows, width), d) for d in outs]
                        + [jax.ShapeDtypeStruct(s, F32) for s in acc_outs] + c_shapes),
        scratch_shapes=comm.scratch() if comm else [],
        compiler_params=_host_params(("arbitrary",), comm),
        name=name,
    )(*ins, *row_ins, *c_ins)
    return (res[:n_out], list(res[n_out:])) if comm else res


def _rms_pre(x, g, comm):
    L, D = x.shape

    def body(x_ref, g_ref, h_ref):
        xv = x_ref[...]
        h_ref[...] = (xv * _rstd(xv) * g_ref[...]).astype(BF16)

    (h,), c_outs = _row_call(body, [x], [g], [BF16], [], rows=L, width=D, name="rms_pre", comm=comm)
    return h, c_outs


def _post_pre(x, mixed, g_post, g_pre):
    L, D = x.shape

    def body(x_ref, m_ref, gp_ref, gq_ref, x1_ref, h2_ref):
        mv = m_ref[...]
        x1 = x_ref[...] + mv * _rstd(mv) * gp_ref[...]
        x1_ref[...] = x1
        h2_ref[...] = (x1 * _rstd(x1) * gq_ref[...]).astype(BF16)

    return _row_call(body, [x, mixed], [g_post, g_pre], [F32, BF16], [], rows=L, width=D, name="post_pre")


def _loss_bwd(x1, dn, g_post, target):
    L, D = x1.shape

    def body(x1_ref, dn_ref, t_ref, g_ref, dx2_ref, ddn_ref, dg_ref, loss_ref):
        @pl.when(pl.program_id(0) == 0)
        def _():
            dg_ref[...] = jnp.zeros_like(dg_ref)
            loss_ref[...] = jnp.zeros_like(loss_ref)

        dnv = dn_ref[...]
        g = g_ref[...]
        r = _rstd(dnv)
        err = x1_ref[...] + dnv * r * g - t_ref[...]
        loss_ref[...] += 0.5 * jnp.sum(jnp.mean(err * err, axis=-1, keepdims=True), axis=0, keepdims=True)
        dx2 = err * (1.0 / D)
        dx2_ref[...] = dx2
        ddn, dgr = _norm_bwd(dnv, r, g, dx2)
        ddn_ref[...] = ddn.astype(BF16)
        dg_ref[...] += jnp.sum(dgr, axis=0, keepdims=True)

    return _row_call(body, [x1, dn, target], [g_post], [F32, BF16], [(1, D), (1, 1)],
                     rows=L, width=D, name="loss_bwd")


def _norm_bwd_pair(x1, dh2, dx2, mixed, g_pre, g_post):
    L, D = x1.shape

    def body(x1_ref, dh_ref, dx2_ref, m_ref, gq_ref, gp_ref, dx1_ref, dm_ref, dgq_ref, dgp_ref):
        @pl.when(pl.program_id(0) == 0)
        def _():
            dgq_ref[...] = jnp.zeros_like(dgq_ref)
            dgp_ref[...] = jnp.zeros_like(dgp_ref)

        x1v = x1_ref[...]
        d1, dgq = _norm_bwd(x1v, _rstd(x1v), gq_ref[...], dh_ref[...].astype(F32))
        dx1 = dx2_ref[...] + d1
        dx1_ref[...] = dx1
        mv = m_ref[...]
        dm, dgp = _norm_bwd(mv, _rstd(mv), gp_ref[...], dx1)
        dm_ref[...] = dm.astype(BF16)
        dgq_ref[...] += jnp.sum(dgq, axis=0, keepdims=True)
        dgp_ref[...] += jnp.sum(dgp, axis=0, keepdims=True)

    return _row_call(body, [x1, dh2, dx2, mixed], [g_pre, g_post], [F32, BF16], [(1, D), (1, D)],
                     rows=L, width=D, name="norm_bwd_pair")


def _final_bwd(x, dh, dx1, g_pre, comm):
    L, D = x.shape

    def body(x_ref, dh_ref, dx1_ref, g_ref, gx_ref, dg_ref):
        @pl.when(pl.program_id(0) == 0)
        def _():
            dg_ref[...] = jnp.zeros_like(dg_ref)

        xv = x_ref[...]
        d0, dg = _norm_bwd(xv, _rstd(xv), g_ref[...], dh_ref[...].astype(F32))
        gx_ref[...] = dx1_ref[...] + d0
        dg_ref[...] += jnp.sum(dg, axis=0, keepdims=True)

    return _row_call(body, [x, dh, dx1], [g_pre], [F32], [(1, D)], rows=L, width=D, name="final_bwd", comm=comm)


def _rope_tables(L):
    half = ROT_DIM // 2
    inv = ROPE_THETA ** (-jnp.arange(half, dtype=F32) * 2.0 / ROT_DIM)
    ang = jnp.arange(L, dtype=F32)[:, None] * inv[None, :]
    d = jnp.arange(LANES) % HEAD_DIM
    a = ang[:, d % half]
    cos_t = jnp.where(d[None, :] < ROT_DIM, jnp.cos(a), 1.0)
    sin_t = jnp.where(d[None, :] < half, -jnp.sin(a), jnp.where(d[None, :] < ROT_DIM, jnp.sin(a), 0.0))
    return cos_t.astype(F32), sin_t.astype(F32)


def _lane_lo(shape):
    return lax.broadcasted_iota(jnp.int32, shape, 1) < HEAD_DIM


def _rope(x, cos_t, sin_t):
    d = lax.broadcasted_iota(jnp.int32, x.shape, 1) % HEAD_DIM
    partner = jnp.where(d < ROT_DIM // 2, pltpu.roll(x, LANES - ROT_DIM // 2, 1), pltpu.roll(x, ROT_DIM // 2, 1))
    return x * cos_t + partner * sin_t


def _dup(kv, g):
    sw = pltpu.roll(kv, HEAD_DIM, 1)
    lo = _lane_lo(kv.shape)
    return jnp.where(lo, kv, sw) if g == 0 else jnp.where(lo, sw, kv)


def _attn_mask(n):
    qi = lax.broadcasted_iota(jnp.int32, (BLOCK, 2 * BLOCK), 0)
    kj = lax.broadcasted_iota(jnp.int32, (BLOCK, 2 * BLOCK), 1)
    rel = qi + BLOCK - kj
    return (rel >= 0) & (rel < BLOCK) & ((kj >= BLOCK) | (n > 0))


def _softmax_sink(s, mask, sink):
    s = jnp.where(mask, s, -1e30)
    m = jnp.maximum(jnp.max(s, axis=-1, keepdims=True), sink)
    e = jnp.where(mask, jnp.exp(s - m), 0.0)
    es = jnp.exp(sink - m)
    inv = 1.0 / (jnp.sum(e, axis=-1, keepdims=True) + es)
    return e * inv, es * inv


_NT = (((1,), (1,)), ((), ()))
_TN = (((0,), (0,)), ((), ()))


def _dot(a, b):
    return jnp.dot(a.astype(BF16), b.astype(BF16), preferred_element_type=F32)


def _dot_nt(a, b):
    return lax.dot_general(a.astype(BF16), b.astype(BF16), _NT, preferred_element_type=F32)


def _dot_tn(a, b):
    return lax.dot_general(a.astype(BF16), b.astype(BF16), _TN, preferred_element_type=F32)


def _attn_specs(nb):
    kcol, vcol = Q_W // LANES, Q_W // LANES + 1
    prev = lambda n: jnp.maximum(n - 1, 0)
    return [
        pl.BlockSpec((BLOCK, Q_W), lambda n: (n, 0)),
        pl.BlockSpec((BLOCK, LANES), lambda n: (n, kcol)),
        pl.BlockSpec((BLOCK, LANES), lambda n: (prev(n), kcol)),
        pl.BlockSpec((BLOCK, LANES), lambda n: (n, vcol)),
        pl.BlockSpec((BLOCK, LANES), lambda n: (prev(n), vcol)),
        pl.BlockSpec((BLOCK, LANES), lambda n: (n, 0)),
        pl.BlockSpec((BLOCK, LANES), lambda n: (prev(n), 0)),
        pl.BlockSpec((BLOCK, LANES), lambda n: (n, 0)),
        pl.BlockSpec((BLOCK, LANES), lambda n: (prev(n), 0)),
        pl.BlockSpec(memory_space=pltpu.SMEM),
    ]


def _attn_prep(refs):
    q_ref, kc_ref, kp_ref, vc_ref, vp_ref, cc_ref, cp_ref, sc_ref, sp_ref = refs
    cos_c, sin_c, cos_p, sin_p = cc_ref[...], sc_ref[...], cp_ref[...], sp_ref[...]
    k2 = jnp.concatenate([_rope(kp_ref[...], cos_p, sin_p), _rope(kc_ref[...], cos_c, sin_c)], axis=0)
    v2 = jnp.concatenate([vp_ref[...], vc_ref[...]], axis=0)
    kd = [_dup(k2, g).astype(BF16) for g in range(N_KV_HEADS)]
    vd = [_dup(v2, g).astype(BF16) for g in range(N_KV_HEADS)]
    return cos_c, sin_c, cos_p, sin_p, kd, vd


def _stack_heads(x):
    lo = _lane_lo(x.shape)
    return jnp.concatenate([jnp.where(lo, x, 0.0), jnp.where(lo, 0.0, x)], axis=0)


def _unstack_heads(x2):
    return jnp.where(_lane_lo((BLOCK, LANES)), x2[:BLOCK], x2[BLOCK:])


def _pair_sinks(sink_ref, t):
    first = lax.broadcasted_iota(jnp.int32, (2 * BLOCK, 1), 0) < BLOCK
    return jnp.where(first, sink_ref[0, 2 * t], sink_ref[0, 2 * t + 1])


_SCALE = 1.0 / math.sqrt(HEAD_DIM)
_TILES = Q_W // LANES
_TILES_PER_KV = _TILES // N_KV_HEADS


def _attn_fwd(z, cos_t, sin_t, sinks, comm=None):
    L = z.shape[0]
    nb = L // BLOCK

    def body(*refs):
        ins, c_ins, outs, c_outs, _, sems = _split_refs(refs, 10, 1, comm)
        q_ref, kc_ref, kp_ref, vc_ref, vp_ref, cc_ref, cp_ref, sc_ref, sp_ref, sink_ref = ins
        o_ref = outs[0]
        n = pl.program_id(0)
        if comm:
            comm.run(c_ins, c_outs, sems, n == 0, n == _mid_step(nb), n == nb - 1)
        cos_c, sin_c, _, _, kd, vd = _attn_prep((q_ref, kc_ref, kp_ref, vc_ref, vp_ref, cc_ref, cp_ref, sc_ref, sp_ref))
        mask = _attn_mask(n)
        mask2 = jnp.concatenate([mask, mask], axis=0)
        for t in range(_TILES):
            g = t // _TILES_PER_KV
            q2 = _stack_heads(_rope(q_ref[:, t * LANES:(t + 1) * LANES], cos_c, sin_c) * _SCALE)
            p2, _ = _softmax_sink(_dot_nt(q2, kd[g]), mask2, _pair_sinks(sink_ref, t))
            o_ref[:, t * LANES:(t + 1) * LANES] = _unstack_heads(_dot(p2, vd[g])).astype(BF16)

    c_ins = comm.ins if comm else []
    c_shapes = comm.out_shapes if comm else []
    res = pl.pallas_call(
        body,
        grid=(nb,),
        in_specs=_attn_specs(nb) + [_ANY] * len(c_ins),
        out_specs=tuple([pl.BlockSpec((BLOCK, Q_W), lambda n: (n, 0))] + [_ANY] * len(c_shapes)),
        out_shape=tuple([jax.ShapeDtypeStruct((L, Q_W), BF16)] + c_shapes),
        scratch_shapes=comm.scratch() if comm else [],
        compiler_params=_host_params(("parallel",), comm),
        name="attn_fwd",
    )(z, z, z, z, z, cos_t, cos_t, sin_t, sin_t, sinks, *c_ins)
    return (res[0], list(res[1:])) if comm else res[0]


def _attn_bwd(z, d_o, cos_t, sin_t, sinks, dz, comm=None):
    L = z.shape[0]
    nb = L // BLOCK

    def body(*refs):
        ins, c_ins, outs, c_outs, _, sems = _split_refs(refs, 12, 6, comm)
        q_ref, kc_ref, kp_ref, vc_ref, vp_ref, cc_ref, cp_ref, sc_ref, sp_ref, sink_ref, do_ref, _ = ins
        dq_ref, dkc_ref, dkp_ref, dvc_ref, dvp_ref, ds_ref = outs
        n = pl.program_id(0)
        if comm:
            comm.run(c_ins, c_outs, sems, n == 0, n == _mid_step(nb), n == nb - 1)

        @pl.when(n == 0)
        def _():
            ds_ref[...] = jnp.zeros_like(ds_ref)

        cos_c, sin_c, cos_p, sin_p, kd, vd = _attn_prep(
            (q_ref, kc_ref, kp_ref, vc_ref, vp_ref, cc_ref, cp_ref, sc_ref, sp_ref))
        mask = _attn_mask(n)
        mask2 = jnp.concatenate([mask, mask], axis=0)
        lo2 = _lane_lo((2 * BLOCK, LANES))
        acc_k = [jnp.zeros((2 * BLOCK, LANES), F32) for _ in range(N_KV_HEADS)]
        acc_v = [jnp.zeros((2 * BLOCK, LANES), F32) for _ in range(N_KV_HEADS)]
        sink_rows = []
        for t in range(_TILES):
            g = t // _TILES_PER_KV
            sl = slice(t * LANES, (t + 1) * LANES)
            q2 = _stack_heads(_rope(q_ref[:, sl], cos_c, sin_c) * _SCALE)
            p2, ps2 = _softmax_sink(_dot_nt(q2, kd[g]), mask2, _pair_sinks(sink_ref, t))
            do2 = _stack_heads(do_ref[:, sl])
            d2 = jnp.sum(do2 * _dot(p2, vd[g]), axis=-1, keepdims=True)
            ds2 = p2 * (_dot_nt(do2, vd[g]) - d2)
            dqt = _unstack_heads(_dot(ds2, kd[g])) * _SCALE
            dq_ref[:, sl] = _rope(dqt, cos_c, -sin_c).astype(BF16)
            acc_k[g] = acc_k[g] + _dot_tn(ds2, q2)
            acc_v[g] = acc_v[g] + _dot_tn(p2, do2)
            sd = ps2 * d2
            sink_rows.append(jnp.where(_lane_lo((1, LANES)), -jnp.sum(sd[:BLOCK], axis=0, keepdims=True),
                                       -jnp.sum(sd[BLOCK:], axis=0, keepdims=True)))
        ds_ref[...] += jnp.concatenate(sink_rows, axis=0)
        fk = [a + pltpu.roll(a, HEAD_DIM, 1) for a in acc_k]
        fv = [a + pltpu.roll(a, HEAD_DIM, 1) for a in acc_v]
        dk2 = jnp.where(lo2, fk[0], fk[1])
        dv2 = jnp.where(lo2, fv[0], fv[1])
        dkp_ref[...] = _rope(dk2[:BLOCK], cos_p, -sin_p)
        dkc_ref[...] = _rope(dk2[BLOCK:], cos_c, -sin_c)
        dvp_ref[...] = dv2[:BLOCK]
        dvc_ref[...] = dv2[BLOCK:]

    blk = pl.BlockSpec((BLOCK, LANES), lambda n: (n, 0))
    kv = jax.ShapeDtypeStruct((L, LANES), F32)
    c_ins = comm.ins if comm else []
    c_shapes = comm.out_shapes if comm else []
    res = pl.pallas_call(
        body,
        grid=(nb,),
        in_specs=_attn_specs(nb) + [pl.BlockSpec((BLOCK, Q_W), lambda n: (n, 0)), _ANY] + [_ANY] * len(c_ins),
        out_specs=tuple([pl.BlockSpec((BLOCK, Q_W), lambda n: (n, 0)), blk, blk, blk, blk,
                         pl.BlockSpec((_TILES, LANES), lambda n: (0, 0))] + [_ANY] * len(c_shapes)),
        out_shape=tuple([jax.ShapeDtypeStruct(dz.shape, dz.dtype), kv, kv, kv, kv,
                         jax.ShapeDtypeStruct((_TILES, LANES), F32)] + c_shapes),
        scratch_shapes=comm.scratch() if comm else [],
        input_output_aliases={11: 0},
        compiler_params=_host_params(("arbitrary",), comm),
        name="attn_bwd",
    )(z, z, z, z, z, cos_t, cos_t, sin_t, sin_t, sinks, d_o, dz, *c_ins)
    return (res[:6], list(res[6:])) if comm else res


def _kv_combine(dkc, dkp, dvc, dvp, dz):
    L = dkc.shape[0]
    tr = _pick(L, (1024, 512, 256, 128))
    per = tr // BLOCK
    nt, nb = L // tr, L // BLOCK

    def body(kc_ref, kp_ref, kn_ref, vc_ref, vp_ref, vn_ref, dz_in, o_ref):
        live = jnp.where(pl.program_id(0) + 1 < nt, 1.0, 0.0)

        def shifted(p_ref, n_ref):
            tail = live * n_ref[...]
            return tail if per == 1 else jnp.concatenate([p_ref[BLOCK:, :], tail], axis=0)

        o_ref[:, :LANES] = (kc_ref[...] + shifted(kp_ref, kn_ref)).astype(BF16)
        o_ref[:, LANES:] = (vc_ref[...] + shifted(vp_ref, vn_ref)).astype(BF16)

    cur = pl.BlockSpec((tr, LANES), lambda n: (n, 0))
    nxt = pl.BlockSpec((BLOCK, LANES), lambda n: (jnp.minimum((n + 1) * per, nb - 1), 0))
    kv_block = Q_W // (2 * KV_W)
    return pl.pallas_call(body, grid=(nt,), in_specs=[cur, cur, nxt, cur, cur, nxt, _ANY],
                          out_specs=pl.BlockSpec((tr, 2 * KV_W), lambda n: (n, kv_block)),
                          out_shape=jax.ShapeDtypeStruct(dz.shape, dz.dtype), input_output_aliases={6: 0},
                          compiler_params=_params(("parallel",)), name="kv_combine")(
        dkc, dkp, dkp, dvc, dvp, dvp, dz)


def _gate_bwd(d_mix, z, o_ga, y_attn, y_ssm):
    L, D = d_mix.shape
    tc = _pick(math.gcd(D, o_ga), (512, 256, 128))
    tr = _pick(L, (2048, 1024, 512, 256, 128))
    nd, gb = D // tc, o_ga // tc

    def body(dm_ref, g_ref, ya_ref, ys_ref, dy_ref, dz_ref):
        dm = dm_ref[...].astype(F32)
        s = _sigmoid(g_ref[...])
        y = jnp.where(pl.program_id(1) < nd, ya_ref[...], ys_ref[...]).astype(F32)
        dy_ref[...] = (dm * s).astype(BF16)
        dz_ref[...] = (dm * y * s * (1.0 - s)).astype(BF16)

    blk = lambda f: pl.BlockSpec((tr, tc), f)
    return pl.pallas_call(
        body,
        grid=(L // tr, 2 * nd),
        in_specs=[blk(lambda i, j: (i, j % nd)), blk(lambda i, j: (i, j + gb)),
                  blk(lambda i, j: (i, jnp.minimum(j, nd - 1))), blk(lambda i, j: (i, jnp.maximum(j - nd, 0)))],
        out_specs=(blk(lambda i, j: (i, j)), blk(lambda i, j: (i, j + gb))),
        out_shape=(jax.ShapeDtypeStruct((L, 2 * D), BF16), jax.ShapeDtypeStruct(z.shape, BF16)),
        compiler_params=_params(("parallel", "arbitrary")),
        name="gate_bwd",
    )(d_mix, z, y_attn, y_ssm)


def _discretise(lr, li, ldt, br, bi):
    dt = jnp.exp(ldt)
    mag = jnp.exp(lr * dt)
    a_re, a_im = mag * jnp.cos(li * dt), mag * jnp.sin(li * dt)
    den = lr * lr + li * li
    nr, ni = a_re - 1.0, a_im
    coef_re = (nr * lr + ni * li) / den
    coef_im = (ni * lr - nr * li) / den
    return a_re, a_im, coef_re * br - coef_im * bi, coef_re * bi + coef_im * br


def _disc_specs(n):
    tr = _pick(n, (512,))
    cs = pl.BlockSpec((tr, 1), lambda i: (i, 0))
    ms = pl.BlockSpec((tr, SSM_GC), lambda i: (i, 0))
    return tr, cs, ms


def _disc_fwd(lr, li, ldt, br, bi):
    n = lr.shape[0]
    tr, cs, ms = _disc_specs(n)

    def body(lr_ref, li_ref, dt_ref, br_ref, bi_ref, o1, o2, o3, o4):
        r = _discretise(lr_ref[...], li_ref[...], dt_ref[...], br_ref[...], bi_ref[...])
        o1[...], o2[...], o3[...], o4[...] = r

    col = jax.ShapeDtypeStruct((n, 1), F32)
    mat = jax.ShapeDtypeStruct((n, SSM_GC), F32)
    return pl.pallas_call(body, grid=(n // tr,), in_specs=[cs, cs, cs, ms, ms], out_specs=(cs, cs, ms, ms),
                          out_shape=(col, col, mat, mat), compiler_params=_params(("parallel",)), name="disc_fwd")(
        lr, li, ldt, br, bi)


def _disc_bwd(lr, li, ldt, br, bi, gar, gai, gbr, gbi):
    n = lr.shape[0]
    tr, cs, ms = _disc_specs(n)

    def body(lr_ref, li_ref, dt_ref, br_ref, bi_ref, gar_ref, gai_ref, gbr_ref, gbi_ref, o_lr, o_li, o_dt, o_br, o_bi):
        _, vjp = jax.vjp(_discretise, lr_ref[...], li_ref[...], dt_ref[...], br_ref[...], bi_ref[...])
        g = vjp((gar_ref[...], gai_ref[...], gbr_ref[...], gbi_ref[...]))
        o_lr[...] = g[0]
        o_li[...] = g[1]
        o_dt[...] = jnp.sum(g[2].reshape(tr // SSM_P, SSM_P, 1), axis=1)
        o_br[...] = g[3]
        o_bi[...] = g[4]

    col = jax.ShapeDtypeStruct((n, 1), F32)
    mat = jax.ShapeDtypeStruct((n, SSM_GC), F32)
    return pl.pallas_call(
        body, grid=(n // tr,), in_specs=[cs, cs, cs, ms, ms, cs, cs, ms, ms],
        out_specs=(cs, cs, pl.BlockSpec((tr // SSM_P, 1), lambda i: (i, 0)), ms, ms),
        out_shape=(col, col, jax.ShapeDtypeStruct((n // SSM_P, 1), F32), mat, mat),
        compiler_params=_params(("parallel",)), name="disc_bwd")(lr, li, ldt, br, bi, gar, gai, gbr, gbi)


def _cpow(ar, ai, nsq):
    for _ in range(nsq):
        ar, ai = ar * ar - ai * ai, 2.0 * ar * ai
    return ar, ai


def _ssm_dims(L):
    tc = min(1024, L)
    seg = tc // SUBLANES
    assert seg & (seg - 1) == 0
    return tc, seg, L // tc, seg.bit_length() - 1


def _tile_rows(i):
    return pl.ds(pl.multiple_of(i * SUBLANES, SUBLANES), SUBLANES)


def _rows_to_segments(src_ref, dst_ref, seg):
    def body(i, _):
        dst_ref[_tile_rows(i), :] = src_ref[pl.ds(i, SUBLANES, stride=seg), :]
        return 0
    lax.fori_loop(0, seg, body, 0, unroll=8)


def _segments_to_rows(src_ref, dst_ref, seg):
    def body(i, _):
        dst_ref[pl.ds(i, SUBLANES, stride=seg), :] = src_ref[_tile_rows(i), :]
        return 0
    lax.fori_loop(0, seg, body, 0, unroll=8)


def _ssm_fwd(z, u_off, br_m, bi_m, cr_m, ci_m, a_re3, a_im3, d_row, comm=None):
    L = z.shape[0]
    ngb = br_m.shape[0]
    tc, seg, nc, nsq = _ssm_dims(L)
    ucol = u_off // LANES

    def body(*refs):
        ins, c_ins, outs, c_outs, scratch, sems = _split_refs(refs, 8, 3, comm)
        u_ref, br_ref, bi_ref, cr_ref, ci_ref, ar_ref, ai_ref, d_ref = ins
        y_ref, xr_ref, xi_ref = outs
        bur, bui, car_r, car_i, ini_r, ini_i, up, ys = scratch
        if comm:
            s = pl.program_id(0) * nc + pl.program_id(1)
            comm.run(c_ins, c_outs, sems, s == 0, s == _mid_step(ngb * nc), s == ngb * nc - 1)

        @pl.when(pl.program_id(1) == 0)
        def _():
            car_r[...] = jnp.zeros_like(car_r)
            car_i[...] = jnp.zeros_like(car_i)

        _rows_to_segments(u_ref, up, seg)
        u = up[...]
        pr = _dot(u, br_ref[0])
        pi = _dot(u, bi_ref[0])
        for w in range(NW):
            bur[w] = pr[:, w * LANES:(w + 1) * LANES]
            bui[w] = pi[:, w * LANES:(w + 1) * LANES]
        ar = [jnp.broadcast_to(ar_ref[w], (SUBLANES, LANES)) for w in range(NW)]
        ai = [jnp.broadcast_to(ai_ref[w], (SUBLANES, LANES)) for w in range(NW)]

        def step(i, carry, store):
            xr, xi = carry
            rows = _tile_rows(i)
            nr, ni = [], []
            for w in range(NW):
                r = ar[w] * xr[w] - ai[w] * xi[w] + bur[w, rows, :]
                m = ar[w] * xi[w] + ai[w] * xr[w] + bui[w, rows, :]
                if store:
                    xr_ref[w, rows, :] = r
                    xi_ref[w, rows, :] = m
                nr.append(r)
                ni.append(m)
            return tuple(nr), tuple(ni)

        zero = tuple(jnp.zeros((SUBLANES, LANES), F32) for _ in range(NW))
        er, ei = lax.fori_loop(0, seg, functools.partial(step, store=False), (zero, zero), unroll=2)
        for w in range(NW):
            pr_, pi_ = _cpow(ar[w][0:1], ai[w][0:1], nsq)
            sr, si = car_r[w, 0:1, :], car_i[w, 0:1, :]
            for j in range(SUBLANES):
                ini_r[w, j:j + 1, :] = sr
                ini_i[w, j:j + 1, :] = si
                sr, si = (pr_ * sr - pi_ * si + er[w][j:j + 1], pr_ * si + pi_ * sr + ei[w][j:j + 1])
            car_r[w, 0:1, :] = sr
            car_i[w, 0:1, :] = si
        init = (tuple(ini_r[w] for w in range(NW)), tuple(ini_i[w] for w in range(NW)))
        lax.fori_loop(0, seg, functools.partial(step, store=True), init, unroll=2)
        acc = d_ref[...] * u
        for w in range(NW):
            sl = slice(w * LANES, (w + 1) * LANES)
            acc = acc + _dot(xr_ref[w], cr_ref[0, sl, :]) - _dot(xi_ref[w], ci_ref[0, sl, :])
        ys[...] = acc
        _segments_to_rows(ys, y_ref, seg)

    mat_b = pl.BlockSpec((1, LANES, NW * LANES), lambda b, k: (b, 0, 0))
    mat_c = pl.BlockSpec((1, NW * LANES, LANES), lambda b, k: (b, 0, 0))
    a_spec = pl.BlockSpec((NW, 1, LANES), lambda b, k: (b, 0, 0))
    x_spec = pl.BlockSpec((NW, tc, LANES), lambda b, k: (b, k, 0))
    xs = jax.ShapeDtypeStruct((ngb * NW, L, LANES), F32)
    st = pltpu.VMEM((NW, SUBLANES, LANES), F32)
    c_ins = comm.ins if comm else []
    c_shapes = comm.out_shapes if comm else []
    res = pl.pallas_call(
        body,
        grid=(ngb, nc),
        in_specs=[pl.BlockSpec((tc, LANES), lambda b, k: (k, b + ucol)), mat_b, mat_b, mat_c, mat_c, a_spec, a_spec,
                  pl.BlockSpec((1, LANES), lambda b, k: (0, b))] + [_ANY] * len(c_ins),
        out_specs=tuple([pl.BlockSpec((tc, LANES), lambda b, k: (k, b)), x_spec, x_spec] + [_ANY] * len(c_shapes)),
        out_shape=tuple([jax.ShapeDtypeStruct((L, ngb * LANES), F32), xs, xs] + c_shapes),
        scratch_shapes=[pltpu.VMEM((NW, tc, LANES), F32), pltpu.VMEM((NW, tc, LANES), F32), st, st, st, st,
                        pltpu.VMEM((tc, LANES), F32), pltpu.VMEM((tc, LANES), F32)]
        + (comm.scratch() if comm else []),
        compiler_params=_host_params(("arbitrary", "arbitrary"), comm),
        name="ssm_fwd",
    )(z, br_m, bi_m, cr_m, ci_m, a_re3, a_im3, d_row, *c_ins)
    return (res[:3], list(res[3:])) if comm else res


def _ssm_bwd(dy, z, u_off, xs_r, xs_i, br_m, bi_m, cr_m, ci_m, a_re3, a_im3, d_row, dz):
    L = z.shape[0]
    ngb = br_m.shape[0]
    tc, seg, nc, nsq = _ssm_dims(L)
    ucol = u_off // LANES

    def body(dy_ref, u_ref, xr_ref, xi_ref, br_ref, bi_ref, cr_ref, ci_ref, ar_ref, ai_ref, d_ref, dz_in,
             du_ref, gd_ref, gar_ref, gai_ref, gbr_ref, gbi_ref, gcr_ref, gci_ref,
             gr_s, gi_s, car_r, car_i, ini_r, ini_i, acc_r, acc_i, dyp, up, dus, dun):
        k = pl.program_id(1)

        @pl.when(k == 0)
        def _():
            for ref in (car_r, car_i, acc_r, acc_i, gd_ref, gbr_ref, gbi_ref, gcr_ref, gci_ref):
                ref[...] = jnp.zeros_like(ref)

        _rows_to_segments(dy_ref, dyp, seg)
        _rows_to_segments(u_ref, up, seg)
        dy_v = dyp[...]
        u = up[...]
        g_re = _dot_nt(dy_v, cr_ref[0])
        g_im = -_dot_nt(dy_v, ci_ref[0])
        for w in range(NW):
            gr_s[w] = g_re[:, w * LANES:(w + 1) * LANES]
            gi_s[w] = g_im[:, w * LANES:(w + 1) * LANES]
        ar = [jnp.broadcast_to(ar_ref[w], (SUBLANES, LANES)) for w in range(NW)]
        ai = [jnp.broadcast_to(ai_ref[w], (SUBLANES, LANES)) for w in range(NW)]

        def step1(ii, carry):
            xr, xi = carry
            rows = _tile_rows(seg - 1 - ii)
            nr = tuple(ar[w] * xr[w] + ai[w] * xi[w] + gr_s[w, rows, :] for w in range(NW))
            ni = tuple(ar[w] * xi[w] - ai[w] * xr[w] + gi_s[w, rows, :] for w in range(NW))
            return nr, ni

        zero = tuple(jnp.zeros((SUBLANES, LANES), F32) for _ in range(NW))
        er, ei = lax.fori_loop(0, seg, step1, (zero, zero), unroll=2)
        for w in range(NW):
            pr_, pi_ = _cpow(ar[w][0:1], ai[w][0:1], nsq)
            sr, si = car_r[w, 0:1, :], car_i[w, 0:1, :]
            for j in reversed(range(SUBLANES)):
                ini_r[w, j:j + 1, :] = sr
                ini_i[w, j:j + 1, :] = si
                sr, si = (pr_ * sr + pi_ * si + er[w][j:j + 1], pr_ * si - pi_ * sr + ei[w][j:j + 1])
            car_r[w, 0:1, :] = sr
            car_i[w, 0:1, :] = si

        def step2(ii, carry):
            gxr, gxi, acr, aci = carry
            rows = _tile_rows(seg - 1 - ii)
            nr, ni, nar, nai = [], [], [], []
            for w in range(NW):
                xr_t, xi_t = xr_ref[w, rows, :], xi_ref[w, rows, :]
                nar.append(acr[w] + gxr[w] * xr_t + gxi[w] * xi_t)
                nai.append(aci[w] + gxi[w] * xr_t - gxr[w] * xi_t)
                r = ar[w] * gxr[w] + ai[w] * gxi[w] + gr_s[w, rows, :]
                m = ar[w] * gxi[w] - ai[w] * gxr[w] + gi_s[w, rows, :]
                gr_s[w, rows, :] = r
                gi_s[w, rows, :] = m
                nr.append(r)
                ni.append(m)
            return tuple(nr), tuple(ni), tuple(nar), tuple(nai)

        init = (tuple(ini_r[w] for w in range(NW)), tuple(ini_i[w] for w in range(NW)),
                tuple(acc_r[w] for w in range(NW)), tuple(acc_i[w] for w in range(NW)))
        _, _, acr, aci = lax.fori_loop(0, seg, step2, init, unroll=2)
        du = d_ref[...] * dy_v
        for w in range(NW):
            sl = slice(w * LANES, (w + 1) * LANES)
            acc_r[w] = acr[w]
            acc_i[w] = aci[w]
            gxr_w, gxi_w = gr_s[w], gi_s[w]
            du = du + _dot_nt(gxr_w, br_ref[0, :, sl]) + _dot_nt(gxi_w, bi_ref[0, :, sl])
            gbr_ref[0, :, sl] += _dot_tn(u, gxr_w)
            gbi_ref[0, :, sl] += _dot_tn(u, gxi_w)
            gcr_ref[0, sl, :] += _dot_tn(xr_ref[w], dy_v)
            gci_ref[0, sl, :] += _dot_tn(-xi_ref[w], dy_v)
        dus[...] = du
        _segments_to_rows(dus, dun, seg)
        du_ref[...] = dun[...].astype(BF16)
        gd_ref[...] += jnp.sum(dy_v * u, axis=0, keepdims=True)

        @pl.when(k == nc - 1)
        def _():
            for w in range(NW):
                gar_ref[w] = jnp.sum(acc_r[w], axis=0, keepdims=True)
                gai_ref[w] = jnp.sum(acc_i[w], axis=0, keepdims=True)

    rk = lambda k: nc - 1 - k
    mat_b = pl.BlockSpec((1, LANES, NW * LANES), lambda b, k: (b, 0, 0))
    mat_c = pl.BlockSpec((1, NW * LANES, LANES), lambda b, k: (b, 0, 0))
    a_spec = pl.BlockSpec((NW, 1, LANES), lambda b, k: (b, 0, 0))
    x_spec = pl.BlockSpec((NW, tc, LANES), lambda b, k: (b, rk(k), 0))
    st = pltpu.VMEM((NW, SUBLANES, LANES), F32)
    big = pltpu.VMEM((NW, tc, LANES), F32)
    return pl.pallas_call(
        body,
        grid=(ngb, nc),
        in_specs=[pl.BlockSpec((tc, LANES), lambda b, k: (rk(k), b)),
                  pl.BlockSpec((tc, LANES), lambda b, k: (rk(k), b + ucol)),
                  x_spec, x_spec, mat_b, mat_b, mat_c, mat_c, a_spec, a_spec,
                  pl.BlockSpec((1, LANES), lambda b, k: (0, b)), _ANY],
        out_specs=(pl.BlockSpec((tc, LANES), lambda b, k: (rk(k), b + ucol)),
                   pl.BlockSpec((1, LANES), lambda b, k: (0, b)), a_spec, a_spec, mat_b, mat_b, mat_c, mat_c),
        out_shape=(jax.ShapeDtypeStruct(dz.shape, dz.dtype),
                   jax.ShapeDtypeStruct((1, ngb * LANES), F32),
                   jax.ShapeDtypeStruct((ngb * NW, 1, LANES), F32), jax.ShapeDtypeStruct((ngb * NW, 1, LANES), F32),
                   jax.ShapeDtypeStruct(br_m.shape, F32), jax.ShapeDtypeStruct(br_m.shape, F32),
                   jax.ShapeDtypeStruct(cr_m.shape, F32), jax.ShapeDtypeStruct(cr_m.shape, F32)),
        scratch_shapes=[big, big, st, st, st, st, st, st] + [pltpu.VMEM((tc, LANES), F32)] * 4,
        input_output_aliases={11: 0},
        compiler_params=_params(("arbitrary", "arbitrary")),
        name="ssm_bwd",
    )(dy, z, xs_r, xs_i, br_m, bi_m, cr_m, ci_m, a_re3, a_im3, d_row, dz)


def _block_diag_in(bb, ngb):
    t = bb.reshape(ngb, GROUPS_PER_BLOCK, SSM_P, SSM_GC).transpose(0, 1, 3, 2)
    eye = jnp.eye(GROUPS_PER_BLOCK, dtype=F32)
    m = t[:, :, :, None, :] * eye[None, :, None, :, None]
    return m.reshape(ngb, GROUPS_PER_BLOCK * SSM_GC, GROUPS_PER_BLOCK * SSM_P)


def _block_diag_out(c, ngb):
    t = c.reshape(ngb, GROUPS_PER_BLOCK, SSM_GC, SSM_P).transpose(0, 1, 3, 2)
    eye = jnp.eye(GROUPS_PER_BLOCK, dtype=F32)
    m = t[:, :, :, None, :] * eye[None, :, None, :, None]
    return m.reshape(ngb, GROUPS_PER_BLOCK * SSM_P, GROUPS_PER_BLOCK * SSM_GC)


def _diag_in(m, ngb):
    m5 = m.reshape(ngb, GROUPS_PER_BLOCK, SSM_GC, GROUPS_PER_BLOCK, SSM_P)
    d = jnp.diagonal(m5, axis1=1, axis2=3)
    return d.transpose(0, 3, 2, 1).reshape(ngb * GROUPS_PER_BLOCK * SSM_P, SSM_GC)


def _diag_out(m, ngb):
    m5 = m.reshape(ngb, GROUPS_PER_BLOCK, SSM_P, GROUPS_PER_BLOCK, SSM_GC)
    d = jnp.diagonal(m5, axis1=1, axis2=3)
    return d.transpose(0, 3, 2, 1).reshape(ngb * GROUPS_PER_BLOCK, SSM_GC, SSM_P)


_ANY = pl.BlockSpec(memory_space=pl.ANY)


class _Comm:
    def __init__(self, ins, out_shapes, n_sem, start, mid, finish):
        self.ins, self.out_shapes, self.n_sem = list(ins), list(out_shapes), n_sem
        self.start, self.mid, self.finish = start, mid, finish

    def scratch(self):
        return [pltpu.SemaphoreType.DMA((self.n_sem,)), pltpu.SemaphoreType.DMA((self.n_sem,)),
                pltpu.SemaphoreType.DMA((len(self.ins),))]

    def run(self, in_refs, out_refs, sems, first, mid, last):
        send, recv, local = sems

        @pl.when(first)
        def _():
            self.start(in_refs, out_refs, send, recv, local)

        @pl.when(mid)
        def _():
            self.mid(in_refs, out_refs, send, recv, local)

        @pl.when(last)
        def _():
            self.finish(in_refs, out_refs, send, recv, local)


def _split_refs(refs, n_in, n_out, comm):
    ci = len(comm.ins) if comm else 0
    co = len(comm.out_shapes) if comm else 0
    ins = refs[:n_in]
    c_ins = refs[n_in:n_in + ci]
    outs = refs[n_in + ci:n_in + ci + n_out]
    c_outs = refs[n_in + ci + n_out:n_in + ci + n_out + co]
    rest = refs[n_in + ci + n_out + co:]
    if comm:
        return ins, c_ins, outs, c_outs, rest[:-3], rest[-3:]
    return ins, c_ins, outs, c_outs, rest, ()


def _run_comm(comm, name):
    ni, no = len(comm.ins), len(comm.out_shapes)

    def body(*refs):
        args = (refs[:ni], refs[ni:ni + no]) + tuple(refs[ni + no:])
        comm.start(*args)
        comm.mid(*args)
        comm.finish(*args)

    return pl.pallas_call(
        body,
        in_specs=[_ANY] * ni,
        out_specs=tuple([_ANY] * no),
        out_shape=tuple(comm.out_shapes),
        scratch_shapes=comm.scratch(),
        compiler_params=pltpu.CompilerParams(has_side_effects=True),
        name=name,
    )(*comm.ins)


def _ag_comm(shards):
    n = len(shards)

    def env(ins, outs, send_sems, recv_sems):
        x, y, c = lax.axis_index("x"), lax.axis_index("y"), lax.axis_index("c")
        me, sibling = (x, y, c), (x, y, 1 - c)
        chips = [(1 - x, y), (x, 1 - y), (1 - x, 1 - y)]

        def copy(a, k, block, to, src=None):
            s = 4 * block[0] + 2 * block[1] + block[2]
            return pltpu.make_async_remote_copy(
                src_ref=outs[a].at[s] if src is None else src, dst_ref=outs[a].at[s],
                send_sem=send_sems.at[7 * a + k], recv_sem=recv_sems.at[7 * a + k],
                device_id=to, device_id_type=MESH)

        return c, me, sibling, chips, copy

    def own(ins, outs, local_sems, a):
        x, y, c = lax.axis_index("x"), lax.axis_index("y"), lax.axis_index("c")
        return pltpu.make_async_copy(ins[a], outs[a].at[4 * x + 2 * y + c], local_sems.at[a])

    def first_sends(ins, copy, me, sibling, chips, c, a):
        return [copy(a, 0, me, sibling, src=ins[a])] + [
            copy(a, 1 + j, me, (*chip, c), src=ins[a]) for j, chip in enumerate(chips)]

    def start(ins, outs, send_sems, recv_sems, local_sems):
        c, me, sibling, chips, copy = env(ins, outs, send_sems, recv_sems)
        for a in range(n):
            own(ins, outs, local_sems, a).start()
        for a in range(n):
            for cp in first_sends(ins, copy, me, sibling, chips, c, a):
                cp.start()

    def mid(ins, outs, send_sems, recv_sems, local_sems):
        c, me, sibling, chips, copy = env(ins, outs, send_sems, recv_sems)
        for a in range(n):
            for j, chip in enumerate(chips):
                copy(a, 1 + j, (*chip, c), me).wait_recv()
                copy(a, 4 + j, (*chip, c), sibling).start()

    def finish(ins, outs, send_sems, recv_sems, local_sems):
        c, me, sibling, chips, copy = env(ins, outs, send_sems, recv_sems)
        for a in range(n):
            copy(a, 0, sibling, me).wait_recv()
            for j, chip in enumerate(chips):
                copy(a, 4 + j, (*chip, 1 - c), me).wait_recv()
        for a in range(n):
            for cp in first_sends(ins, copy, me, sibling, chips, c, a):
                cp.wait_send()
            for j, chip in enumerate(chips):
                copy(a, 4 + j, (*chip, c), sibling).wait_send()
            own(ins, outs, local_sems, a).wait()

    return _Comm(shards, [jax.ShapeDtypeStruct((N_DEV,) + s.shape, s.dtype) for s in shards], 7 * n,
                 start, mid, finish)


def _sibling_comm(parts):
    n = len(parts)

    def copies(ins, outs, send_sems, recv_sems):
        x, y, c = lax.axis_index("x"), lax.axis_index("y"), lax.axis_index("c")
        return [pltpu.make_async_remote_copy(
            src_ref=ins[a].at[2 * q + (1 - c)], dst_ref=outs[a].at[q],
            send_sem=send_sems.at[4 * a + q], recv_sem=recv_sems.at[4 * a + q],
            device_id=(x, y, 1 - c), device_id_type=MESH) for a in range(n) for q in range(4)]

    def start(ins, outs, send_sems, recv_sems, local_sems):
        for cp in copies(ins, outs, send_sems, recv_sems):
            cp.start()

    def mid(ins, outs, send_sems, recv_sems, local_sems):
        pass

    def finish(ins, outs, send_sems, recv_sems, local_sems):
        for cp in copies(ins, outs, send_sems, recv_sems):
            cp.wait()

    return _Comm(parts, [jax.ShapeDtypeStruct((4,) + p.shape[1:], p.dtype) for p in parts], 4 * n,
                 start, mid, finish)


def _chips_comm(parts):
    n = len(parts)

    def copies(ins, outs, send_sems, recv_sems):
        x, y, c = lax.axis_index("x"), lax.axis_index("y"), lax.axis_index("c")
        chips = [(1 - x, y), (x, 1 - y), (1 - x, 1 - y)]
        return [pltpu.make_async_remote_copy(
            src_ref=ins[a].at[2 * px + py], dst_ref=outs[a].at[j],
            send_sem=send_sems.at[3 * a + j], recv_sem=recv_sems.at[3 * a + j],
            device_id=(px, py, c), device_id_type=MESH) for a in range(n) for j, (px, py) in enumerate(chips)]

    def start(ins, outs, send_sems, recv_sems, local_sems):
        for cp in copies(ins, outs, send_sems, recv_sems):
            cp.start()

    def mid(ins, outs, send_sems, recv_sems, local_sems):
        pass

    def finish(ins, outs, send_sems, recv_sems, local_sems):
        for cp in copies(ins, outs, send_sems, recv_sems):
            cp.wait()

    return _Comm(parts, [jax.ShapeDtypeStruct((3,) + p.shape[1:], p.dtype) for p in parts], 3 * n,
                 start, mid, finish)


def _sibling_add(part, recv, name):
    _, R, C = part.shape
    tr = _pick(R, (256, 128, 80))
    c = lax.axis_index("c")

    def body(c_ref, p_ref, r_ref, o_ref, o16_ref):
        t = p_ref[...] + r_ref[...]
        o_ref[...] = t
        o16_ref[...] = t.astype(BF16)

    blk = pl.BlockSpec((1, tr, C), lambda q, i, c_ref: (q, i, 0))
    return pl.pallas_call(
        body,
        grid_spec=pltpu.PrefetchScalarGridSpec(
            num_scalar_prefetch=1,
            grid=(4, R // tr),
            in_specs=[pl.BlockSpec((1, tr, C), lambda q, i, c_ref: (2 * q + c_ref[0], i, 0)), blk],
            out_specs=(blk, blk),
        ),
        out_shape=(jax.ShapeDtypeStruct((4, R, C), F32), jax.ShapeDtypeStruct((4, R, C), BF16)),
        compiler_params=_params(("parallel", "parallel")),
        name=name,
    )(jnp.reshape(c, (1,)).astype(jnp.int32), part, recv)


def _adamw(w, g, m, v):
    m = ADAM_B1 * m + (1.0 - ADAM_B1) * g
    v = ADAM_B2 * v + (1.0 - ADAM_B2) * (g * g)
    m_hat = m / (1.0 - ADAM_B1 ** ADAM_STEP)
    v_hat = v / (1.0 - ADAM_B2 ** ADAM_STEP)
    delta = -ADAM_LR * (m_hat / (jnp.sqrt(v_hat) + ADAM_EPS) + ADAM_WD * w)
    return delta, m, v


def _adam_big(t, recv, w, m, v, name):
    _, R, C = t.shape
    tr = _pick(R, (256, 128))
    chip = 2 * lax.axis_index("x") + lax.axis_index("y")

    def body(q_ref, t_ref, r_ref, w_ref, m_ref, v_ref, g_ref, d_ref, nm_ref, nv_ref):
        g = t_ref[0] + r_ref[0].astype(F32) + r_ref[1].astype(F32) + r_ref[2].astype(F32)
        g_ref[...] = g
        d_ref[...], nm_ref[...], nv_ref[...] = _adamw(w_ref[...], g, m_ref[...], v_ref[...])

    blk = pl.BlockSpec((tr, C), lambda i, q_ref: (i, 0))
    o = jax.ShapeDtypeStruct((R, C), F32)
    return pl.pallas_call(
        body,
        grid_spec=pltpu.PrefetchScalarGridSpec(
            num_scalar_prefetch=1,
            grid=(R // tr,),
            in_specs=[pl.BlockSpec((1, tr, C), lambda i, q_ref: (q_ref[0], i, 0)),
                      pl.BlockSpec((3, tr, C), lambda i, q_ref: (0, i, 0)), blk, blk, blk],
            out_specs=(blk, blk, blk, blk),
        ),
        out_shape=(o, o, o, o),
        compiler_params=_params(("parallel",)),
        name=name,
    )(jnp.reshape(chip, (1,)).astype(jnp.int32), t, recv, w, m, v)


def _reduce_big(t, recv, name):
    _, R, C = t.shape
    tr = _pick(R, (256, 128, 80))
    chip = 2 * lax.axis_index("x") + lax.axis_index("y")

    def body(q_ref, t_ref, r_ref, g_ref):
        g_ref[...] = t_ref[0] + r_ref[0].astype(F32) + r_ref[1].astype(F32) + r_ref[2].astype(F32)

    return pl.pallas_call(
        body,
        grid_spec=pltpu.PrefetchScalarGridSpec(
            num_scalar_prefetch=1,
            grid=(R // tr,),
            in_specs=[pl.BlockSpec((1, tr, C), lambda i, q_ref: (q_ref[0], i, 0)),
                      pl.BlockSpec((3, tr, C), lambda i, q_ref: (0, i, 0))],
            out_specs=pl.BlockSpec((tr, C), lambda i, q_ref: (i, 0)),
        ),
        out_shape=jax.ShapeDtypeStruct((R, C), F32),
        compiler_params=_params(("parallel",)),
        name=name,
    )(jnp.reshape(chip, (1,)).astype(jnp.int32), t, recv)


def _adam_only(g, w, m, v, name):
    R, C = g.shape
    tr = _pick(R, (256, 128))

    def body(g_ref, w_ref, m_ref, v_ref, d_ref, nm_ref, nv_ref):
        d_ref[...], nm_ref[...], nv_ref[...] = _adamw(w_ref[...], g_ref[...], m_ref[...], v_ref[...])

    blk = pl.BlockSpec((tr, C), lambda i: (i, 0))
    o = jax.ShapeDtypeStruct((R, C), F32)
    return pl.pallas_call(body, grid=(R // tr,), in_specs=[blk] * 4, out_specs=(blk, blk, blk),
                          out_shape=(o, o, o), compiler_params=_params(("parallel",)), name=name)(g, w, m, v)


def _gather8_comm(gbuf):
    def copies(ins, outs, send_sems, recv_sems):
        x, y, c = lax.axis_index("x"), lax.axis_index("y"), lax.axis_index("c")
        me = 4 * x + 2 * y + c
        out = []
        for k in range(1, N_DEV):
            fx, fy, fc = (k >> 2) & 1, (k >> 1) & 1, k & 1
            px, py, pc = x + fx - 2 * x * fx, y + fy - 2 * y * fy, c + fc - 2 * c * fc
            send = pltpu.make_async_remote_copy(
                src_ref=ins[0], dst_ref=outs[0].at[me], send_sem=send_sems.at[k - 1], recv_sem=recv_sems.at[k - 1],
                device_id=(px, py, pc), device_id_type=MESH)
            recv = pltpu.make_async_remote_copy(
                src_ref=ins[0], dst_ref=outs[0].at[4 * px + 2 * py + pc], send_sem=send_sems.at[k - 1],
                recv_sem=recv_sems.at[k - 1], device_id=(px, py, pc), device_id_type=MESH)
            out.append((send, recv))
        return me, out

    def start(ins, outs, send_sems, recv_sems, local_sems):
        me, cps = copies(ins, outs, send_sems, recv_sems)
        pltpu.make_async_copy(ins[0], outs[0].at[me], local_sems.at[0]).start()
        for send, _ in cps:
            send.start()

    def mid(ins, outs, send_sems, recv_sems, local_sems):
        pass

    def finish(ins, outs, send_sems, recv_sems, local_sems):
        me, cps = copies(ins, outs, send_sems, recv_sems)
        for send, recv in cps:
            recv.wait_recv()
            send.wait_send()
        pltpu.make_async_copy(ins[0], outs[0].at[me], local_sems.at[0]).wait()

    return _Comm([gbuf], [jax.ShapeDtypeStruct((N_DEV,) + gbuf.shape, gbuf.dtype)], N_DEV - 1, start, mid, finish)


def _sum8_adam(slots, wbuf, mbuf, vbuf, name):
    R = wbuf.shape[0]

    def body(s_ref, w_ref, m_ref, v_ref, gs_ref, d_ref, nm_ref, nv_ref):
        g = s_ref[0]
        for s in range(1, N_DEV):
            g = g + s_ref[s]
        gs_ref[...] = g
        d_ref[...], nm_ref[...], nv_ref[...] = _adamw(w_ref[...], g, m_ref[...], v_ref[...])

    o = jax.ShapeDtypeStruct((R, LANES), F32)
    return pl.pallas_call(body, out_shape=(o, o, o, o),
                          compiler_params=pltpu.CompilerParams(vmem_limit_bytes=VMEM_LIMIT), name=name)(
        slots, wbuf, mbuf, vbuf)


def _pack(items):
    rows, spans, r0 = [], [], 0
    for a in items:
        n = a.size
        nr = -(-n // LANES)
        rows.append(jnp.pad(a.reshape(-1).astype(F32), (0, nr * LANES - n)).reshape(nr, LANES))
        spans.append((r0, nr, a.shape))
        r0 += nr
    pad = -r0 % SUBLANES
    if pad:
        rows.append(jnp.zeros((pad, LANES), F32))
    return jnp.concatenate(rows, axis=0), spans


def _unpack(buf, spans):
    return [buf[r0:r0 + nr].reshape(-1)[:math.prod(shape)].reshape(shape) for r0, nr, shape in spans]


def kernel(x, norm_mix_pre, norm_mix_post, norm_mlp_pre, norm_mlp_post, w_in, sinks, lam_re, lam_im, log_dt, b_re, b_im, c_re, c_im, d_skip, w_glu, w_branch, w_out, w_up, w_down, loss_target, m_norm_mix_pre, m_norm_mix_post, m_norm_mlp_pre, m_norm_mlp_post, m_w_in, m_sinks, m_lam_re, m_lam_im, m_log_dt, m_b_re, m_b_im, m_c_re, m_c_im, m_d_skip, m_w_glu, m_w_branch, m_w_out, m_w_up, m_w_down, v_norm_mix_pre, v_norm_mix_post, v_norm_mlp_pre, v_norm_mlp_post, v_w_in, v_sinks, v_lam_re, v_lam_im, v_log_dt, v_b_re, v_b_im, v_c_re, v_c_im, v_d_skip, v_w_glu, v_w_branch, v_w_out, v_w_up, v_w_down):
    _, L, D = x.shape
    xs = x[0]
    tgt = loss_target[0]
    ssm_w = D // 2
    n_groups = ssm_w // SSM_GC
    ngb = n_groups // GROUPS_PER_BLOCK
    n_state = n_groups * SSM_P
    d_ff = w_up.shape[2] * N_DEV
    o_k, o_v, o_u = Q_W, Q_W + KV_W, Q_W + 2 * KV_W
    o_ga = o_u + ssm_w
    o_gs = o_ga + D

    big = {"w_in": w_in[0], "w_glu": w_glu[0], "w_branch": w_branch[0], "w_out": w_out[0],
           "w_up": w_up[0], "w_down": w_down[0]}
    col_sharded = ("w_in", "w_glu", "w_up")
    names = list(big)
    shard16 = {k: (big[k].T if k in col_sharded else big[k]).astype(BF16) for k in names}
    full = {}

    def gathered(keys, arrays):
        for k, g in zip(keys, arrays):
            _, r, c = g.shape
            full[k] = g.reshape(N_DEV * r, c)

    def by_owner(g):
        return g.reshape(N_DEV, g.shape[0] // N_DEV, g.shape[1])

    col = lambda a: a.reshape(n_state, 1)
    lr_c, li_c = col(lam_re[0]), col(lam_im[0])
    ldt_c = jnp.repeat(log_dt[0], SSM_P).reshape(n_state, 1)
    b_re_c, b_im_c = b_re[0].reshape(n_state, SSM_GC), b_im[0].reshape(n_state, SSM_GC)
    a_re, a_im, bb_re, bb_im = _disc_fwd(lr_c, li_c, ldt_c, b_re_c, b_im_c)
    a_re3 = a_re.reshape(n_state // LANES, 1, LANES)
    a_im3 = a_im.reshape(n_state // LANES, 1, LANES)
    br_m = _block_diag_in(bb_re, ngb).astype(BF16)
    bi_m = _block_diag_in(bb_im, ngb).astype(BF16)
    cr_m = _block_diag_out(c_re[0], ngb).astype(BF16)
    ci_m = _block_diag_out(c_im[0], ngb).astype(BF16)
    d_row = d_skip[0].reshape(1, ssm_w)
    cos_t, sin_t = _rope_tables(L)

    h, g1 = _rms_pre(xs, norm_mix_pre, _ag_comm([shard16["w_in"]]))
    gathered(["w_in"], g1)
    z, g3 = _mm(h, full["w_in"], mode="nt", name="mm_z",
                comm=_ag_comm([shard16[k] for k in ("w_glu", "w_branch", "w_out")]))
    gathered(["w_glu", "w_branch", "w_out"], g3)
    wb_a, wb_s = full["w_branch"][:Q_W], full["w_branch"][Q_W:]
    o_attn = _attn_fwd(z, cos_t, sin_t, sinks)
    (y_pre, xs_r, xs_i), g1 = _ssm_fwd(z, o_u, br_m, bi_m, cr_m, ci_m, a_re3, a_im3, d_row,
                                       comm=_ag_comm([shard16["w_up"]]))
    gathered(["w_up"], g1)
    zg = _mm(y_pre, full["w_glu"], mode="nt", name="mm_zg", a_pro=_gelu)
    o_ssm = _ew(lambda a, b: (a * _sigmoid(b),), [(zg, 0), (zg, ssm_w)], (BF16,), rows=L, ncols=ssm_w, name="glu")
    y_attn = _mm(o_attn, wb_a, mode="nn", name="mm_y_attn", out_dtypes=(BF16,))
    y_ssm = _mm(o_ssm, wb_s, mode="nn", name="mm_y_ssm", out_dtypes=(BF16,))
    mix = _ew(lambda ga, gs, ya, ys: (_sigmoid(ga) * ya + _sigmoid(gs) * ys,),
              [(z, o_ga), (z, o_gs), (y_attn, 0), (y_ssm, 0)], (BF16,), rows=L, ncols=D, name="mix")
    mixed = _mm(mix, full["w_out"], mode="nn", name="mm_mixed")
    x1, h2 = _post_pre(xs, mixed, norm_mix_post, norm_mlp_pre)

    def relu_sq(acc):
        a = jnp.maximum(acc, 0.0)
        return a, a * a

    (act, act2), g1 = _mm(h2, full["w_up"], mode="nt", name="mm_up", out_dtypes=(BF16, BF16), epi=relu_sq,
                          comm=_ag_comm([shard16["w_down"]]))
    gathered(["w_down"], g1)
    dn = _mm(act2, full["w_down"], mode="nn", name="mm_down")
    dx2, d_dn, dg_mlp_post, loss_part = _loss_bwd(x1, dn, norm_mlp_post, tgt)

    d_pre = _mm(d_dn, full["w_down"], mode="nt", name="mm_d_act", out_dtypes=(BF16,),
                epi=lambda acc, a: (acc * (2.0 * a.astype(F32)),), extras=(act,))
    gw_down = _mm(act2, d_dn, mode="tn", name="mm_gw_down")
    p_down = by_owner(gw_down)
    dh2, (sib_down,) = _mm(d_pre, full["w_up"], mode="nn", name="mm_dh2", out_dtypes=(BF16,),
                           comm=_sibling_comm([p_down]))
    t_down, t16_down = _sibling_add(p_down, sib_down, "rs_add_w_down")
    gw_up, (chips_down,) = _mm(d_pre, h2, mode="tn", name="mm_gw_up", comm=_chips_comm([t16_down]))
    p_up = by_owner(gw_up)
    dx1, d_mixed, dg_mlp_pre, dg_mix_post = _norm_bwd_pair(x1, dh2, dx2, mixed, norm_mlp_pre, norm_mix_post)
    d_mix, (sib_up,) = _mm(d_mixed, full["w_out"], mode="nt", name="mm_d_mix", out_dtypes=(BF16,),
                           comm=_sibling_comm([p_up]))
    t_up, t16_up = _sibling_add(p_up, sib_up, "rs_add_w_up")
    gw_out = _mm(mix, d_mixed, mode="tn", name="mm_gw_out")
    d_y2, dz = _gate_bwd(d_mix, z, o_ga, y_attn, y_ssm)
    d_o_attn = _mm(d_y2, wb_a, mode="nt", name="mm_d_o_attn", a_win=(0, D))
    gwb_a = _mm(o_attn, d_y2, mode="tn", name="mm_gwb_a", b_win=(0, D))
    d_o_ssm = _mm(d_y2, wb_s, mode="nt", name="mm_d_o_ssm", a_win=(D, D))
    gwb_s = _mm(o_ssm, d_y2, mode="tn", name="mm_gwb_s", b_win=(D, D))

    def glu_bwd(do, a, b):
        s = _sigmoid(b)
        return do * s, do * a * s * (1.0 - s)

    d_zg_a, d_zg_b = _ew(glu_bwd, [(d_o_ssm, 0), (zg, 0), (zg, ssm_w)], (BF16, BF16), rows=L, ncols=ssm_w, name="glu_bwd")
    d_zg = jnp.concatenate([d_zg_a, d_zg_b], axis=1)
    dy_pre = _mm(d_zg, full["w_glu"], mode="nn", name="mm_d_gy",
                 epi=lambda acc, y: (acc * _gelu_grad(y),), extras=(y_pre,))
    gw_glu = _mm(d_zg, y_pre, mode="tn", name="mm_gw_glu", b_pro=_gelu)
    mids = ["w_glu", "w_branch", "w_out"]
    p_mid = [by_owner(gw_glu), by_owner(jnp.concatenate([gwb_a, gwb_s], axis=0)), by_owner(gw_out)]
    sib_mid = _run_comm(_sibling_comm(p_mid), "rs_sibling_mid")
    t_mid = [_sibling_add(p, r, "rs_add_" + k) for k, p, r in zip(mids, p_mid, sib_mid)]
    (dz, g_dskip, g_ar3, g_ai3, g_br_m, g_bi_m, g_cr_m, g_ci_m) = _ssm_bwd(
        dy_pre, z, o_u, xs_r, xs_i, br_m, bi_m, cr_m, ci_m, a_re3, a_im3, d_row, dz)
    g_lr, g_li, g_ldt, g_b_re, g_b_im = _disc_bwd(
        lr_c, li_c, ldt_c, b_re_c, b_im_c, g_ar3.reshape(n_state, 1), g_ai3.reshape(n_state, 1),
        _diag_in(g_br_m, ngb), _diag_in(g_bi_m, ngb))
    (dz, dkc, dkp, dvc, dvp, dsink_rows), chips_a = _attn_bwd(
        z, d_o_attn, cos_t, sin_t, sinks, dz, comm=_chips_comm([t16_up] + [t16 for _, t16 in t_mid]))
    chips_up, chips_mid = chips_a[0], chips_a[1:]
    dz = _kv_combine(dkc, dkp, dvc, dvp, dz)
    dsink = jnp.stack([dsink_rows[:, 0], dsink_rows[:, HEAD_DIM]], axis=1).reshape(1, N_Q_HEADS)
    small_names = ["norm_mix_post", "norm_mlp_pre", "norm_mlp_post", "sinks", "lam_re", "lam_im",
                   "log_dt", "b_re", "b_im", "c_re", "c_im", "d_skip"]
    small_g = [dg_mix_post, dg_mlp_pre, dg_mlp_post, dsink,
               g_lr.reshape(lam_re.shape), g_li.reshape(lam_im.shape), g_ldt.reshape(log_dt.shape),
               g_b_re.reshape(b_re.shape), g_b_im.reshape(b_im.shape),
               _diag_out(g_cr_m, ngb).reshape(c_re.shape), _diag_out(g_ci_m, ngb).reshape(c_im.shape),
               g_dskip.reshape(d_skip.shape)]
    gbuf, spans = _pack(small_g + [loss_part])
    gw_in, (small_slots,) = _mm(dz, h, mode="tn", name="mm_gw_in", comm=_gather8_comm(gbuf))
    p_in = by_owner(gw_in)
    (sib_in,) = _run_comm(_sibling_comm([p_in]), "rs_sibling_w_in")
    t_in, t16_in = _sibling_add(p_in, sib_in, "rs_add_w_in")
    dh, (chips_in,) = _mm(dz, full["w_in"], mode="nn", name="mm_dh", out_dtypes=(BF16,),
                          comm=_chips_comm([t16_in]))
    grad_x, dg_mix_pre = _final_bwd(xs, dh, dx1, norm_mix_pre, None)

    reduced = {"w_in": (t_in, chips_in), "w_up": (t_up, chips_up), "w_down": (t_down, chips_down)}
    for k, (t32, _), r in zip(mids, t_mid, chips_mid):
        reduced[k] = (t32, r)
    moments = {"w_in": (m_w_in, v_w_in), "w_glu": (m_w_glu, v_w_glu), "w_branch": (m_w_branch, v_w_branch),
               "w_out": (m_w_out, v_w_out), "w_up": (m_w_up, v_w_up), "w_down": (m_w_down, v_w_down)}
    big_out = {}
    for k in names:
        t, r = reduced[k]
        mm_, vv_ = moments[k]
        if k in col_sharded:
            g = _reduce_big(t, r, "reduce_" + k).T
            big_out[k] = [o[None] for o in (g,) + tuple(_adam_only(g, big[k], mm_[0], vv_[0], "adam_" + k))]
        else:
            big_out[k] = [o[None] for o in _adam_big(t, r, big[k], mm_[0], vv_[0], "adam_" + k)]

    small_w = [norm_mix_post, norm_mlp_pre, norm_mlp_post, sinks, lam_re, lam_im, log_dt,
               b_re, b_im, c_re, c_im, d_skip]
    small_m = [m_norm_mix_post, m_norm_mlp_pre, m_norm_mlp_post, m_sinks, m_lam_re, m_lam_im,
               m_log_dt, m_b_re, m_b_im, m_c_re, m_c_im, m_d_skip]
    small_v = [v_norm_mix_post, v_norm_mlp_pre, v_norm_mlp_post, v_sinks, v_lam_re, v_lam_im,
               v_log_dt, v_b_re, v_b_im, v_c_re, v_c_im, v_d_skip]
    zero1 = jnp.zeros((1, 1), F32)
    wbuf, _ = _pack(small_w + [zero1])
    mbuf, _ = _pack(small_m + [zero1])
    vbuf, _ = _pack(small_v + [zero1])
    gs, ds, nms, nvs = [_unpack(b, spans) for b in _sum8_adam(small_slots, wbuf, mbuf, vbuf, "small_adam")]
    loss = gs[-1].reshape(())
    tbuf, tspans = _pack([dg_mix_pre])
    (tail_slots,) = _run_comm(_gather8_comm(tbuf), "gather_tail")
    tail = _sum8_adam(tail_slots, _pack([norm_mix_pre])[0], _pack([m_norm_mix_pre])[0],
                      _pack([v_norm_mix_pre])[0], "small_adam_tail")
    small_names = ["norm_mix_pre"] + small_names
    gs, ds, nms, nvs = [_unpack(t, tspans) + src for t, src in zip(tail, (gs, ds, nms, nvs))]

    order = ["norm_mix_pre", "norm_mix_post", "norm_mlp_pre", "norm_mlp_post", "w_in", "sinks", "lam_re", "lam_im",
             "log_dt", "b_re", "b_im", "c_re", "c_im", "d_skip", "w_glu", "w_branch", "w_out", "w_up", "w_down"]
    outs = [loss, grad_x[None]]
    for idx, src in enumerate((gs, ds, nms, nvs)):
        for k in order:
            outs.append(big_out[k][idx] if k in big_out else src[small_names.index(k)])
    return tuple(outs)
```
